```python
import jax, jax.numpy as jnp
from jax import lax
import numpy as np

D_MODEL = 1024
BATCH = 4
SEQ = 8192
DEPTH = 1
DEC_BATCH = 128
DEC_SEQ = 1
PAST_LEN = 16384
PAGE_SIZE = 128

H_A = 8
HD_A = 64
D_A = H_A * HD_A
D_LORA_W = 64
D_LORA_A = 64
D_LORA_G = 128
GN_EPS = 64e-5
H_Q = 8
H_KV = 2
G_Q = H_Q // H_KV
HD_B = 64
D_B = H_Q * HD_B
WINDOW = 128
ATTN_BLOCK = 128
N_KEYS = 128
N_EXPERTS = N_KEYS * N_KEYS
PEER_HEADS = 8
PEER_DKEY = 256
PEER_HALF = PEER_DKEY // 2
PEER_TOPK = 16
PEER_BLOCK = 128
OFF_R = 0
OFF_K = OFF_R + D_A
OFF_V = OFF_K + D_A
OFF_WD = OFF_V + D_A
OFF_AD = OFF_WD + D_LORA_W
OFF_GD = OFF_AD + D_LORA_A
SHIFT_W = OFF_GD + D_LORA_G
OFF_Q = SHIFT_W
OFF_KB = OFF_Q + D_B
OFF_VB = OFF_KB + H_KV * HD_B
OFF_GA = OFF_VB + H_KV * HD_B
OFF_GB = OFF_GA + D_MODEL
D_IN = OFF_GB + D_MODEL
N_MOD = 6
LN_EPS = 1e-5
DEEPNORM_ALPHA = (2.0 * DEPTH) ** 0.25
DEEPNORM_BETA = (8.0 * DEPTH) ** -0.25
NEG_INF = -1e30

kernel_name = 'rwkv7_swa_sink_peer_deepnorm_step'


def _layernorm(x, g, b, eps=LN_EPS):
    xf = x.astype(jnp.float32)
    mu = jnp.mean(xf, -1, keepdims=True)
    var = jnp.mean(jnp.square(xf - mu), -1, keepdims=True)
    return ((xf - mu) * lax.rsqrt(var + eps) * g + b).astype(x.dtype)


def _alibi_slopes():
    return jnp.asarray(2.0 ** (-8.0 * np.arange(1, H_Q + 1) / H_Q), jnp.float32)


def _sink_attention(q, k, v, rel, valid, sinks):
    s = jnp.einsum('...qkgd,...skd->...kgqs', q, k).astype(jnp.float32) * (HD_B ** -0.5)
    slopes = _alibi_slopes().reshape(H_KV, G_Q, 1, 1)
    s = s - slopes * rel[..., None, None, :, :].astype(jnp.float32)
    s = jnp.where(valid[..., None, None, :, :], s, NEG_INF)
    sink = jnp.broadcast_to(sinks.astype(jnp.float32).reshape(H_KV, G_Q, 1, 1), s.shape[:-1] + (1,))
    p = jax.nn.softmax(jnp.concatenate([s, sink], axis=-1), axis=-1)[..., :-1]
    return jnp.einsum('...kgqs,...skd->...qkgd', p.astype(v.dtype), v)


def _banded_window_attention(q, k, v, sinks):
    n, t = q.shape[:2]
    nb = t // ATTN_BLOCK
    qb = q.reshape(n, nb, ATTN_BLOCK, H_KV, G_Q, HD_B)

    def band(z):
        zp = jnp.concatenate([jnp.zeros_like(z[:, :ATTN_BLOCK]), z], axis=1)
        zp = zp.reshape(n, nb + 1, ATTN_BLOCK, H_KV, HD_B)
        return jnp.concatenate([zp[:, :-1], zp[:, 1:]], axis=2)

    blk = jnp.arange(nb)[:, None, None]
    q_pos = blk * ATTN_BLOCK + jnp.arange(ATTN_BLOCK)[None, :, None]
    k_pos = (blk - 1) * ATTN_BLOCK + jnp.arange(2 * ATTN_BLOCK)[None, None, :]
    rel = q_pos - k_pos
    valid = (k_pos >= 0) & (rel >= 0) & (rel <= WINDOW)
    o = _sink_attention(qb, band(k), band(v), rel, valid, sinks)
    return o.reshape(n, t, H_KV, G_Q, HD_B)


def _buffer_window_attention(q, k, v, k_buf, v_buf, sinks):
    t, wb = q.shape[1], k_buf.shape[1]
    k_all = jnp.concatenate([k_buf, k.astype(k_buf.dtype)], axis=1)
    v_all = jnp.concatenate([v_buf, v.astype(v_buf.dtype)], axis=1)
    q_pos = PAST_LEN + jnp.arange(t)
    k_pos = PAST_LEN - wb + jnp.arange(wb + t)
    rel = q_pos[:, None] - k_pos[None, :]
    valid = (rel >= 0) & (rel <= WINDOW)
    o = _sink_attention(q, k_all, v_all, rel, valid, sinks)
    return o, k_all[:, -wb:], v_all[:, -wb:]


def _rwkv7_scan(r, w, k, v, kk, a, s0):
    def step(S, inp):
        r_t, w_t, k_t, v_t, kk_t, a_t = inp
        sa = jnp.einsum('nhij,nhj->nhi', S, -kk_t)
        S = (S * w_t[:, :, None, :] + sa[..., None] * (kk_t * a_t)[:, :, None, :]
             + v_t[..., None] * k_t[:, :, None, :])
        return S, jnp.einsum('nhij,nhj->nhi', S, r_t)

    xs = tuple(jnp.moveaxis(z, 1, 0) for z in (r, w, k, v, kk, a))
    S, ys = lax.scan(step, s0, xs)
    return jnp.moveaxis(ys, 0, 1), S


def _mixer(h, shift_prev, wkv0, k_buf, v_buf, lp):
    n, t, _ = h.shape
    f32 = jnp.float32
    p = h @ lp['w_in']
    ps = p[..., :SHIFT_W]
    prev = jnp.concatenate([shift_prev[:, None].astype(ps.dtype), ps[:, :-1]], axis=1)
    xs = ps + (prev - ps) * lp['mu_shift']
    heads = lambda z: z.reshape(n, t, H_A, HD_A)
    r = xs[..., OFF_R:OFF_K].astype(f32)
    k = xs[..., OFF_K:OFF_V].astype(f32)
    v = xs[..., OFF_V:OFF_WD].astype(f32)
    wd = xs[..., OFF_WD:OFF_AD]
    ad = xs[..., OFF_AD:OFF_GD]
    gd = xs[..., OFF_GD:SHIFT_W]
    w_log = -jax.nn.softplus(-(lp['rwkv_w0'] + jnp.tanh(wd) @ lp['rwkv_w2']).astype(f32)) - 0.5
    decay = jnp.exp(-jnp.exp(w_log))
    a = jax.nn.sigmoid((lp['rwkv_a0'] + ad @ lp['rwkv_a2']).astype(f32))
    g = jax.nn.sigmoid(gd) @ lp['rwkv_g2']
    kk = heads(k * lp['rwkv_k_k'])
    kk = kk / jnp.maximum(jnp.sqrt(jnp.sum(jnp.square(kk), -1, keepdims=True)), 1e-12)
    k = k * (1.0 + (a - 1.0) * lp['rwkv_k_a'])
    r, k, v, a, decay = heads(r), heads(k), heads(v), heads(a), heads(decay)
    y, wkv_new = _rwkv7_scan(r, decay, k, v, kk, a, wkv0.astype(f32))
    mu = jnp.mean(y, -1, keepdims=True)
    var = jnp.mean(jnp.square(y - mu), -1, keepdims=True)
    y = ((y - mu) * lax.rsqrt(var + GN_EPS)).reshape(n, t, D_A) * lp['rwkv_gn_g'] + lp['rwkv_gn_b']
    bonus = jnp.sum(r * k * lp['rwkv_r_k'], -1, keepdims=True) * v
    y_a = ((y + bonus.reshape(n, t, D_A)).astype(h.dtype) * g) @ lp['w_pa']
    q = p[..., OFF_Q:OFF_KB].reshape(n, t, H_KV, G_Q, HD_B)
    kb = p[..., OFF_KB:OFF_VB].reshape(n, t, H_KV, HD_B)
    vb = p[..., OFF_VB:OFF_GA].reshape(n, t, H_KV, HD_B)
    if k_buf is None:
        ob = _banded_window_attention(q, kb, vb, lp['attn_sinks'])
        k_new, v_new = kb[:, -WINDOW:], vb[:, -WINDOW:]
    else:
        ob, k_new, v_new = _buffer_window_attention(q, kb, vb, k_buf, v_buf, lp['attn_sinks'])
    y_b = ob.reshape(n, t, D_B) @ lp['w_pb']
    merged = jax.nn.sigmoid(p[..., OFF_GA:OFF_GB]) * y_a + jax.nn.sigmoid(p[..., OFF_GB:D_IN]) * y_b
    return merged @ lp['w_o'], ps[:, -1], wkv_new, k_new, v_new


def _peer(h, lp):
    n, t, d = h.shape
    xf = h.reshape(n * t, d)
    m = xf.shape[0]
    xp = jnp.pad(xf, ((0, (-m) % PEER_BLOCK), (0, 0)))
    blocks = xp.reshape(-1, PEER_BLOCK, d)

    def one_block(xb):
        q = (xb @ lp['peer_wq']).astype(jnp.float32).reshape(PEER_BLOCK, PEER_HEADS, 2, PEER_HALF)
        s1 = jnp.einsum('bhd,kd->bhk', q[:, :, 0], lp['peer_sub_keys'][0].astype(jnp.float32))
        s2 = jnp.einsum('bhd,kd->bhk', q[:, :, 1], lp['peer_sub_keys'][1].astype(jnp.float32))
        v1, i1 = lax.top_k(s1, PEER_TOPK)
        v2, i2 = lax.top_k(s2, PEER_TOPK)
        cand = (v1[..., :, None] + v2[..., None, :]).reshape(PEER_BLOCK, PEER_HEADS, PEER_TOPK * PEER_TOPK)
        sc, ci = lax.top_k(cand, PEER_TOPK)
        e1 = jnp.take_along_axis(i1, ci // PEER_TOPK, axis=-1)
        e2 = jnp.take_along_axis(i2, ci % PEER_TOPK, axis=-1)
        idx = (e1 * N_KEYS + e2).reshape(PEER_BLOCK, PEER_HEADS * PEER_TOPK)
        gate = jax.nn.softmax(sc, axis=-1).reshape(PEER_BLOCK, PEER_HEADS * PEER_TOPK)
        u = lp['peer_u'][idx]
        act = jax.nn.gelu(jnp.einsum('bkd,bd->bk', u, xb).astype(jnp.float32), approximate=False)
        return jnp.einsum('bk,bkd->bd', (gate * act).astype(xb.dtype), lp['peer_v'][idx])

    out = lax.map(one_block, blocks)
    return out.reshape(-1, d)[:m].reshape(n, t, d)


def _layer(x, c, shift_prev, wkv0, k_buf, v_buf, lp):
    mod = (jax.nn.silu(c) @ lp['w_ada'] + lp['b_ada'])[:, None, :]
    sh_m, sc_m, gt_m, sh_f, sc_f, gt_f = jnp.split(mod, N_MOD, axis=-1)
    mix, shift_new, wkv_new, k_new, v_new = _mixer(x * (1.0 + sc_m) + sh_m, shift_prev, wkv0, k_buf, v_buf, lp)
    x = _layernorm(DEEPNORM_ALPHA * x + gt_m * mix, lp['ln1_g'], lp['ln1_b'])
    ff = _peer(x * (1.0 + sc_f) + sh_f, lp)
    x = _layernorm(DEEPNORM_ALPHA * x + gt_f * ff, lp['ln2_g'], lp['ln2_b'])
    return x, (shift_new, wkv_new, k_new, v_new)


def setup_inputs(seed: int = 0) -> dict:
    key = jax.random.key(seed)
    ks = iter(jax.random.split(key, 40))
    nrm = lambda shape, scale: scale * jax.random.normal(next(ks), shape, jnp.float32)
    L, D = DEPTH, D_MODEL
    wb = min(WINDOW, PAST_LEN)
    return {
        'x_prompt': nrm((BATCH, SEQ, D), 1.0),
        'x_sample': nrm((DEC_BATCH, DEC_SEQ, D), 1.0),
        'state_wkv': nrm((L, DEC_BATCH, H_A, HD_A, HD_A), 0.1),
        'state_shift': nrm((L, DEC_BATCH, SHIFT_W), 1.0),
        'cache_k_win': nrm((L, DEC_BATCH, wb, H_KV, HD_B), 1.0),
        'cache_v_win': nrm((L, DEC_BATCH, wb, H_KV, HD_B), 1.0),
        'c_prompt': nrm((BATCH, D), 1.0),
        'c_sample': nrm((DEC_BATCH, D), 1.0),
        'ln_in_g': 1.0 + nrm((D,), 0.05),
        'ln_in_b': nrm((D,), 0.02),
        'w_ada': nrm((L, D, N_MOD * D), 0.5 * D ** -0.5),
        'b_ada': nrm((L, N_MOD * D), 0.02),
        'w_in': nrm((L, D, D_IN), D ** -0.5),
        'mu_shift': jax.random.uniform(next(ks), (L, SHIFT_W), jnp.float32),
        'rwkv_w0': jax.random.uniform(next(ks), (L, D_A), jnp.float32, -6.0, -1.0),
        'rwkv_w2': nrm((L, D_LORA_W, D_A), 0.5 * D_LORA_W ** -0.5),
        'rwkv_a0': nrm((L, D_A), 0.5),
        'rwkv_a2': nrm((L, D_LORA_A, D_A), D_LORA_A ** -0.5),
        'rwkv_g2': nrm((L, D_LORA_G, D_A), D_LORA_G ** -0.5),
        'rwkv_k_k': 0.85 + nrm((L, D_A), 0.1),
        'rwkv_k_a': 1.0 + nrm((L, D_A), 0.1),
        'rwkv_r_k': nrm((L, H_A, HD_A), 0.1),
        'rwkv_gn_g': 1.0 + nrm((L, D_A), 0.05),
        'rwkv_gn_b': nrm((L, D_A), 0.02),
        'attn_sinks': nrm((L, H_Q), 1.0),
        'w_pa': nrm((L, D_A, D), DEEPNORM_BETA * D_A ** -0.5),
        'w_pb': nrm((L, D_B, D), DEEPNORM_BETA * D_B ** -0.5),
        'w_o': nrm((L, D, D), DEEPNORM_BETA * D ** -0.5),
        'ln1_g': 1.0 + nrm((L, D), 0.05),
        'ln1_b': nrm((L, D), 0.02),
        'peer_wq': nrm((L, D, PEER_HEADS * PEER_DKEY), D ** -0.5),
        'peer_sub_keys': nrm((L, 2, N_KEYS, PEER_HALF), PEER_HALF ** -0.5),
        'peer_u': nrm((L, N_EXPERTS, D), D ** -0.5),
        'peer_v': nrm((L, N_EXPERTS, D), DEEPNORM_BETA),
        'ln2_g': 1.0 + nrm((L, D), 0.05),
        'ln2_b': nrm((L, D), 0.02),
    }


def reference(x_prompt, x_sample, state_wkv, state_shift, cache_k_win, cache_v_win, c_prompt, c_sample,
              ln_in_g, ln_in_b, w_ada, b_ada, w_in, mu_shift, rwkv_w0, rwkv_w2, rwkv_a0, rwkv_a2, rwkv_g2,
              rwkv_k_k, rwkv_k_a, rwkv_r_k, rwkv_gn_g, rwkv_gn_b, attn_sinks, w_pa, w_pb, w_o, ln1_g, ln1_b,
              peer_wq, peer_sub_keys, peer_u, peer_v, ln2_g, ln2_b):
    xp = _layernorm(x_prompt, ln_in_g, ln_in_b)
    xs = _layernorm(x_sample, ln_in_g, ln_in_b)
    n_p = xp.shape[0]
    st_p, st_s = [], []
    for l in range(DEPTH):
        lp = {
            'w_ada': w_ada[l], 'b_ada': b_ada[l], 'w_in': w_in[l], 'mu_shift': mu_shift[l],
            'rwkv_w0': rwkv_w0[l], 'rwkv_w2': rwkv_w2[l], 'rwkv_a0': rwkv_a0[l], 'rwkv_a2': rwkv_a2[l],
            'rwkv_g2': rwkv_g2[l], 'rwkv_k_k': rwkv_k_k[l], 'rwkv_k_a': rwkv_k_a[l], 'rwkv_r_k': rwkv_r_k[l],
            'rwkv_gn_g': rwkv_gn_g[l], 'rwkv_gn_b': rwkv_gn_b[l], 'attn_sinks': attn_sinks[l],
            'w_pa': w_pa[l], 'w_pb': w_pb[l], 'w_o': w_o[l], 'ln1_g': ln1_g[l], 'ln1_b': ln1_b[l],
            'peer_wq': peer_wq[l], 'peer_sub_keys': peer_sub_keys[l], 'peer_u': peer_u[l],
            'peer_v': peer_v[l], 'ln2_g': ln2_g[l], 'ln2_b': ln2_b[l],
        }
        xp, sp = _layer(xp, c_prompt, jnp.zeros((n_p, SHIFT_W), xp.dtype),
                        jnp.zeros((n_p, H_A, HD_A, HD_A), jnp.float32), None, None, lp)
        xs, ss = _layer(xs, c_sample, state_shift[l], state_wkv[l], cache_k_win[l], cache_v_win[l], lp)
        st_p.append(sp)
        st_s.append(ss)
    stk = lambda sts, i: jnp.stack([s[i] for s in sts])
    return (xp, xs, stk(st_p, 1), stk(st_s, 1), stk(st_p, 0), stk(st_s, 0),
            stk(st_p, 2), stk(st_s, 2), stk(st_p, 3), stk(st_s, 3))
```

```python
import functools
import math

import jax
import jax.numpy as jnp
from jax import lax
from jax.experimental import pallas as pl
from jax.experimental.pallas import tpu as pltpu
from jax.experimental.pallas import tpu_sc as plsc

F32 = jnp.float32
BF16 = jnp.bfloat16
I32 = jnp.int32
U32 = jnp.uint32

D_MODEL = 1024
H_A, HD_A = 8, 64
D_A = H_A * HD_A
D_LORA_W, D_LORA_A, D_LORA_G = 64, 64, 128
GN_EPS = 64e-5
H_Q, H_KV, HD_B = 8, 2, 64
G_Q = H_Q // H_KV
D_B = H_Q * HD_B
WINDOW = 128
N_KEYS = 128
PEER_HEADS, PEER_TOPK, PEER_HALF = 8, 16, 128
PEER_PICKS = PEER_HEADS * PEER_TOPK
N_MOD = 6
LN_EPS = 1e-5
NEG_INF = -1e30
OFF_WD = 3 * D_A
OFF_AD = OFF_WD + D_LORA_W
OFF_GD = OFF_AD + D_LORA_A
SHIFT_W = OFF_GD + D_LORA_G
ATTN_W = D_B + 2 * H_KV * HD_B
GATE_W = 2 * D_MODEL
D_IN = SHIFT_W + ATTN_W + GATE_W

VMEM_LIMIT = 48 * 1024 * 1024
RWKV_CHUNK = 64
SC_GATHER_ROWS = 64
SC_IDX_ROWS = 8


def _cparams(*sem):
    return pltpu.CompilerParams(dimension_semantics=sem, vmem_limit_bytes=VMEM_LIMIT)


def _layernorm(x, g, b):
    mu = jnp.mean(x, -1, keepdims=True)
    xc = x - mu
    var = jnp.mean(xc * xc, -1, keepdims=True)
    return xc * lax.rsqrt(var + LN_EPS) * g + b


def _split(x):
    hi = x.astype(BF16)
    lo = (x - hi.astype(F32)).astype(BF16)
    return hi, lo


_NN = (((1,), (0,)), ((), ()))
_NT = (((1,), (1,)), ((), ()))
_TN = (((0,), (0,)), ((), ()))


def _dot3(a, b, dims=_NN):
    ah, al = _split(a)
    bh, bl = _split(b)
    d = functools.partial(lax.dot_general, dimension_numbers=dims, preferred_element_type=F32)
    return d(ah, bh) + d(ah, bl) + d(al, bh)


def _dot_exact_lhs(a_bf16, b, dims=_NN):
    b1 = b.astype(BF16)
    r1 = b - b1.astype(F32)
    b2 = r1.astype(BF16)
    b3 = (r1 - b2.astype(F32)).astype(BF16)
    d = functools.partial(lax.dot_general, dimension_numbers=dims, preferred_element_type=F32)
    return d(a_bf16, b1) + d(a_bf16, b2) + d(a_bf16, b3)


def _dotb(a, b, dims=_NN):
    return lax.dot_general(a.astype(BF16), b.astype(BF16), dims, preferred_element_type=F32)


def _rows(tb, width, col=0):
    return pl.BlockSpec((None, tb, width), lambda g, i: (g, i, col))


def _mod(mod, tb, col):
    if mod.shape[1] == 1:
        return pl.BlockSpec((None, 1, D_MODEL), lambda g, i: (g, 0, col))
    return pl.BlockSpec((None, tb, D_MODEL), lambda g, i: (g, i, col))


def _const(shape):
    n = len(shape)
    return pl.BlockSpec(shape, lambda g, i: (0,) * n)


def _row2(p):
    return p.reshape(1, -1).astype(F32)


def _mod_kernel(c_ref, w_ref, b_ref, o_ref):
    c = c_ref[...]
    a = c * jax.nn.sigmoid(c)
    o_ref[...] = _dot3(a, w_ref[...]) + b_ref[...]


def _modulation(c, w_ada, b_ada):
    n = c.shape[0]
    tn = D_MODEL
    return pl.pallas_call(
        _mod_kernel,
        grid=(w_ada.shape[1] // tn,),
        in_specs=[pl.BlockSpec((n, D_MODEL), lambda j: (0, 0)),
                  pl.BlockSpec((D_MODEL, tn), lambda j: (0, j)),
                  pl.BlockSpec((1, tn), lambda j: (0, j))],
        out_specs=pl.BlockSpec((n, tn), lambda j: (0, j)),
        out_shape=jax.ShapeDtypeStruct((n, w_ada.shape[1]), F32),
        compiler_params=_cparams("arbitrary"),
        name="modulation",
    )(c, w_ada, b_ada.reshape(1, -1))


def _inproj_kernel(x_ref, sh_ref, sc_ref, g_ref, b_ref, w_ref, ps_ref, pa_ref, pg_ref):
    xn = _layernorm(x_ref[...], g_ref[...], b_ref[...])
    h = (xn * (1.0 + sc_ref[...]) + sh_ref[...]).astype(BF16)
    ps_ref[...] = jnp.dot(h, w_ref[:, :SHIFT_W], preferred_element_type=F32)
    pa_ref[...] = jnp.dot(h, w_ref[:, SHIFT_W:SHIFT_W + ATTN_W], preferred_element_type=F32)
    pg_ref[...] = jnp.dot(h, w_ref[:, SHIFT_W + ATTN_W:], preferred_element_type=F32)


def _inproj(x, mod, ln_g, ln_b, w_in_bf16, tb):
    g, r, _ = x.shape
    shp = lambda w: jax.ShapeDtypeStruct((g, r, w), F32)
    return pl.pallas_call(
        _inproj_kernel,
        grid=(g, r // tb),
        in_specs=[_rows(tb, D_MODEL), _mod(mod, tb, 0), _mod(mod, tb, 1),
                  _const((1, D_MODEL)), _const((1, D_MODEL)), _const((D_MODEL, D_IN))],
        out_specs=[_rows(tb, SHIFT_W), _rows(tb, ATTN_W), _rows(tb, GATE_W)],
        out_shape=[shp(SHIFT_W), shp(ATTN_W), shp(GATE_W)],
        compiler_params=_cparams("parallel", "parallel"),
        name="inproj",
    )(x, mod, mod, _row2(ln_g), _row2(ln_b), w_in_bf16)


def _softplus(x):
    return jnp.maximum(x, 0.0) + jnp.log1p(jnp.exp(-jnp.abs(x)))


def _rwkv_prep_kernel(ps_ref, prev_ref, mu_ref, w0_ref, w2_ref, a0_ref, a2_ref, g2_ref, kk_w_ref, ka_w_ref,
                      hsum_ref, r_ref, lw_ref, k_ref, v_ref, kk_ref, kka_ref, g_ref):
    ps = ps_ref[...]
    xs = ps + (prev_ref[...] - ps) * mu_ref[...]
    r = xs[:, 0:D_A]
    k = xs[:, D_A:2 * D_A]
    v = xs[:, 2 * D_A:3 * D_A]
    wd = xs[:, OFF_WD:OFF_AD]
    ad = xs[:, OFF_AD:OFF_GD]
    gd = xs[:, OFF_GD:SHIFT_W]
    z = w0_ref[...] + _dot3(jnp.tanh(wd), w2_ref[...])
    w_log = -_softplus(-z) - 0.5
    a = jax.nn.sigmoid(a0_ref[...] + _dot3(ad, a2_ref[...]))
    kk = k * kk_w_ref[...]
    ss = _dot3(kk * kk, hsum_ref[...])
    kk = kk / jnp.maximum(jnp.sqrt(ss), 1e-12)
    r_ref[...] = r
    lw_ref[...] = -jnp.exp(w_log)
    k_ref[...] = k * (1.0 + (a - 1.0) * ka_w_ref[...])
    v_ref[...] = v
    kk_ref[...] = kk
    kka_ref[...] = kk * a
    g_ref[...] = _dot3(jax.nn.sigmoid(gd), g2_ref[...])


def _head_sum_matrix():
    h = jnp.arange(D_A) // HD_A
    return (h[:, None] == h[None, :]).astype(F32)


def _rwkv_prep(ps, prev, p, tb):
    g, r, _ = ps.shape
    shp = jax.ShapeDtypeStruct((g, r, D_A), F32)
    return pl.pallas_call(
        _rwkv_prep_kernel,
        grid=(g, r // tb),
        in_specs=[_rows(tb, SHIFT_W), _rows(tb, SHIFT_W), _const((1, SHIFT_W)),
                  _const((1, D_A)), _const((D_LORA_W, D_A)), _const((1, D_A)), _const((D_LORA_A, D_A)),
                  _const((D_LORA_G, D_A)), _const((1, D_A)), _const((1, D_A)), _const((D_A, D_A))],
        out_specs=[_rows(tb, D_A)] * 7,
        out_shape=[shp] * 7,
        compiler_params=_cparams("parallel", "parallel"),
        name="rwkv_prep",
    )(ps, prev, _row2(p["mu_shift"]), _row2(p["rwkv_w0"]), p["rwkv_w2"], _row2(p["rwkv_a0"]), p["rwkv_a2"],
      p["rwkv_g2"], _row2(p["rwkv_k_k"]), _row2(p["rwkv_k_a"]), _head_sum_matrix())


def _rwkv_chunk_kernel(r_ref, lw_ref, k_ref, v_ref, kk_ref, kka_ref, s0_ref, y_ref, s_ref):
    c = RWKV_CHUNK

    @pl.when(pl.program_id(1) == 0)
    def _():
        s_ref[...] = s0_ref[...]

    row = lax.broadcasted_iota(I32, (c, c), 0)
    col = lax.broadcasted_iota(I32, (c, c), 1)
    tril = row >= col
    stril = row > col
    lw = lw_ref[...]
    cum = _dot_exact_lhs(tril.astype(BF16), lw)
    cum_end = cum[c - 1:c, :]
    g_inv = jnp.exp(-cum)
    g_end = jnp.exp(cum_end - cum)
    a_hat = -kk_ref[...] * jnp.exp(cum - lw)
    b_hat = kka_ref[...] * g_inv
    k_hat = k_ref[...] * g_inv
    r_til = r_ref[...] * jnp.exp(cum)
    b_end = kka_ref[...] * g_end
    k_end = k_ref[...] * g_end
    gam_end = jnp.exp(cum_end)
    v_all = v_ref[...]
    n_steps = int(math.log2(c))
    for h in range(H_A):
        sl = slice(h * HD_A, (h + 1) * HD_A)
        vh = v_all[:, sl]
        s0 = s_ref[h]
        ar = jnp.concatenate([a_hat[:, sl], r_til[:, sl]], axis=0)
        bk = jnp.concatenate([b_hat[:, sl], k_hat[:, sl]], axis=0)
        x = _dot3(ar, bk, _NT)
        a_ab = jnp.where(stril, x[:c, :c], 0.0)
        a_ak = jnp.where(stril, x[:c, c:], 0.0)
        a_rb = jnp.where(tril, x[c:, :c], 0.0)
        a_rk = jnp.where(tril, x[c:, c:], 0.0)
        ars = _dot3(ar, s0, _NT)
        u = ars[:c] + _dot3(a_ak, vh)
        n = a_ab
        for it in range(n_steps):
            u = u + _dot3(n, u)
            if it + 1 < n_steps:
                n = _dot3(n, n)
        uv = jnp.concatenate([u, vh], axis=0)
        y_ref[:, sl] = ars[c:] + _dot3(jnp.concatenate([a_rb, a_rk], axis=1), uv)
        bke = jnp.concatenate([b_end[:, sl], k_end[:, sl]], axis=0)
        s_ref[h] = s0 * gam_end[:, sl] + _dot3(uv, bke, _TN)


def _rwkv_chunk_scan(r, lw, k, v, kk, kka, s0):
    n, t, _ = r.shape
    c = RWKV_CHUNK
    seq = pl.BlockSpec((None, c, D_A), lambda b, i: (b, i, 0))
    st = pl.BlockSpec((None, H_A, HD_A, HD_A), lambda b, i: (b, 0, 0, 0))
    return pl.pallas_call(
        _rwkv_chunk_kernel,
        grid=(n, t // c),
        in_specs=[seq] * 6 + [st],
        out_specs=[seq, st],
        out_shape=[jax.ShapeDtypeStruct((n, t, D_A), F32), jax.ShapeDtypeStruct((n, H_A, HD_A, HD_A), F32)],
        compiler_params=_cparams("parallel", "arbitrary"),
        name="rwkv_chunk_scan",
    )(r, lw, k, v, kk, kka, s0)


def _rwkv_step_kernel(s_ref, r_ref, lw_ref, k_ref, kk_ref, kka_ref, v_ref, y_ref, so_ref):
    s = s_ref[...]
    sa = jnp.sum(s * (-kk_ref[...]), axis=-1, keepdims=True)
    s = s * jnp.exp(lw_ref[...]) + sa * kka_ref[...] + v_ref[...] * k_ref[...]
    so_ref[...] = s
    y_ref[...] = jnp.sum(s * r_ref[...], axis=-1, keepdims=True)


def _rwkv_step(s0, r, lw, k, kk, kka, v, nb):
    n = s0.shape[0]
    key = lambda z: z.reshape(n, H_A, 1, HD_A)
    st = pl.BlockSpec((nb, H_A, HD_A, HD_A), lambda i: (i, 0, 0, 0))
    ks = pl.BlockSpec((nb, H_A, 1, HD_A), lambda i: (i, 0, 0, 0))
    vs = pl.BlockSpec((nb, H_A, HD_A, 1), lambda i: (i, 0, 0, 0))
    y, s = pl.pallas_call(
        _rwkv_step_kernel,
        grid=(n // nb,),
        in_specs=[st, ks, ks, ks, ks, ks, vs],
        out_specs=[vs, st],
        out_shape=[jax.ShapeDtypeStruct((n, H_A, HD_A, 1), F32), jax.ShapeDtypeStruct(s0.shape, F32)],
        compiler_params=_cparams("parallel"),
        name="rwkv_step",
    )(s0, key(r), key(lw), key(k), key(kk), key(kka), v.reshape(n, H_A, HD_A, 1))
    return y.reshape(n, D_A), s


def _rwkv_post_kernel(y_ref, r_ref, k_ref, v_ref, g_ref, gn_g_ref, gn_b_ref, rk_ref, hsum_ref, o_ref):
    y = y_ref[...]
    hs = hsum_ref[...]
    mu = _dot3(y, hs) * (1.0 / HD_A)
    yc = y - mu
    var = _dot3(yc * yc, hs) * (1.0 / HD_A)
    yn = yc * lax.rsqrt(var + GN_EPS) * gn_g_ref[...] + gn_b_ref[...]
    bonus = _dot3(r_ref[...] * k_ref[...] * rk_ref[...], hs) * v_ref[...]
    o_ref[...] = (yn + bonus) * g_ref[...]


def _rwkv_post(y, r, k, v, g, p, tb):
    gg, rr, _ = y.shape
    return pl.pallas_call(
        _rwkv_post_kernel,
        grid=(gg, rr // tb),
        in_specs=[_rows(tb, D_A)] * 5 + [_const((1, D_A))] * 3 + [_const((D_A, D_A))],
        out_specs=_rows(tb, D_A),
        out_shape=jax.ShapeDtypeStruct((gg, rr, D_A), F32),
        compiler_params=_cparams("parallel", "parallel"),
        name="rwkv_post",
    )(y, r, k, v, g, _row2(p["rwkv_gn_g"]), _row2(p["rwkv_gn_b"]), _row2(p["rwkv_r_k"]), _head_sum_matrix())


def _sink_softmax_pv(s, sink, v):
    m = jnp.maximum(jnp.max(s, axis=-1, keepdims=True), sink)
    p = jnp.exp(s - m)
    den = jnp.sum(p, axis=-1, keepdims=True) + jnp.exp(sink - m)
    return p / den


def _attn_band_kernel(cur_ref, prev_ref, sink_ref, o_ref):
    blk = WINDOW
    i = pl.program_id(1)
    cur = cur_ref[...]
    prev = prev_ref[...]
    qi = lax.broadcasted_iota(I32, (G_Q * blk, 2 * blk), 0) % blk
    kj = lax.broadcasted_iota(I32, (G_Q * blk, 2 * blk), 1)
    rel = blk + qi - kj
    valid = (rel >= 0) & (rel <= WINDOW) & ((kj >= blk) | (i > 0))
    relf = rel.astype(F32)
    gidx = lax.broadcasted_iota(I32, (G_Q * blk, 1), 0) // blk
    for kvh in range(H_KV):
        q4 = jnp.concatenate([cur[:, (kvh * G_Q + g) * HD_B:(kvh * G_Q + g + 1) * HD_B] for g in range(G_Q)], axis=0)
        ko = D_B + kvh * HD_B
        vo = D_B + H_KV * HD_B + kvh * HD_B
        kmat = jnp.concatenate([prev[:, ko:ko + HD_B], cur[:, ko:ko + HD_B]], axis=0)
        vmat = jnp.concatenate([prev[:, vo:vo + HD_B], cur[:, vo:vo + HD_B]], axis=0)
        slope = jnp.zeros((G_Q * blk, 1), F32)
        sink = jnp.zeros((G_Q * blk, 1), F32)
        for g in range(G_Q):
            hq = kvh * G_Q + g
            slope = jnp.where(gidx == g, 2.0 ** (-8.0 * (hq + 1) / H_Q), slope)
            sink = jnp.where(gidx == g, sink_ref[hq], sink)
        s = _dotb(q4, kmat, _NT) * (HD_B ** -0.5)
        s = jnp.where(valid, s - slope * relf, NEG_INF)
        p = _sink_softmax_pv(s, sink, vmat)
        o = _dotb(p, vmat)
        for g in range(G_Q):
            hq = kvh * G_Q + g
            o_ref[:, hq * HD_B:(hq + 1) * HD_B] = o[g * blk:(g + 1) * blk]


def _attn_band(pattn, sinks):
    n, t, _ = pattn.shape
    blk = WINDOW
    return pl.pallas_call(
        _attn_band_kernel,
        grid=(n, t // blk),
        in_specs=[pl.BlockSpec((None, blk, ATTN_W), lambda b, i: (b, i, 0)),
                  pl.BlockSpec((None, blk, ATTN_W), lambda b, i: (b, jnp.maximum(i - 1, 0), 0)),
                  pl.BlockSpec(memory_space=pltpu.SMEM)],
        out_specs=pl.BlockSpec((None, blk, D_B), lambda b, i: (b, i, 0)),
        out_shape=jax.ShapeDtypeStruct((n, t, D_B), F32),
        compiler_params=_cparams("parallel", "parallel"),
        name="attn_band",
    )(pattn, pattn, sinks.astype(F32))


def _attn_cache_kernel(cur_ref, kc_ref, vc_ref, sink_ref, o_ref, *, nb):
    relc = (WINDOW - lax.broadcasted_iota(I32, (G_Q, WINDOW), 1)).astype(F32)
    gidx = lax.broadcasted_iota(I32, (G_Q, 1), 0)
    for b in range(nb):
        cur = cur_ref[b]
        for kvh in range(H_KV):
            q4 = jnp.concatenate([cur[:, (kvh * G_Q + g) * HD_B:(kvh * G_Q + g + 1) * HD_B] for g in range(G_Q)], axis=0)
            ko = D_B + kvh * HD_B
            vo = D_B + H_KV * HD_B + kvh * HD_B
            k_new = cur[:, ko:ko + HD_B]
            v_new = cur[:, vo:vo + HD_B]
            kc = kc_ref[b, :, kvh * HD_B:(kvh + 1) * HD_B]
            vc = vc_ref[b, :, kvh * HD_B:(kvh + 1) * HD_B]
            slope = jnp.zeros((G_Q, 1), F32)
            sink = jnp.zeros((G_Q, 1), F32)
            for g in range(G_Q):
                hq = kvh * G_Q + g
                slope = jnp.where(gidx == g, 2.0 ** (-8.0 * (hq + 1) / H_Q), slope)
                sink = jnp.where(gidx == g, sink_ref[hq], sink)
            scale = HD_B ** -0.5
            sc = _dotb(q4, kc, _NT) * scale - slope * relc
            sn = jnp.sum(q4.astype(BF16).astype(F32) * k_new.astype(BF16).astype(F32), axis=-1, keepdims=True) * scale
            m = jnp.maximum(jnp.maximum(jnp.max(sc, axis=-1, keepdims=True), sn), sink)
            pc = jnp.exp(sc - m)
            pn = jnp.exp(sn - m)
            den = jnp.sum(pc, axis=-1, keepdims=True) + pn + jnp.exp(sink - m)
            o = (_dotb(pc / den, vc) + (pn / den).astype(BF16).astype(F32) * v_new.astype(BF16).astype(F32))
            for g in range(G_Q):
                hq = kvh * G_Q + g
                o_ref[b, :, hq * HD_B:(hq + 1) * HD_B] = o[g:g + 1]


def _attn_cache(pattn, k_buf, v_buf, sinks, nb):
    n = pattn.shape[0]
    kc = k_buf.reshape(n, WINDOW, H_KV * HD_B)
    vc = v_buf.reshape(n, WINDOW, H_KV * HD_B)
    return pl.pallas_call(
        functools.partial(_attn_cache_kernel, nb=nb),
        grid=(n // nb,),
        in_specs=[pl.BlockSpec((nb, 1, ATTN_W), lambda i: (i, 0, 0)),
                  pl.BlockSpec((nb, WINDOW, H_KV * HD_B), lambda i: (i, 0, 0)),
                  pl.BlockSpec((nb, WINDOW, H_KV * HD_B), lambda i: (i, 0, 0)),
                  pl.BlockSpec(memory_space=pltpu.SMEM)],
        out_specs=pl.BlockSpec((nb, 1, D_B), lambda i: (i, 0, 0)),
        out_shape=jax.ShapeDtypeStruct((n, 1, D_B), F32),
        compiler_params=_cparams("parallel"),
        name="attn_cache",
    )(pattn, kc, vc, sinks.astype(F32))


def _merge_kernel(x_ref, ya_ref, ob_ref, pg_ref, gtm_ref, shf_ref, scf_ref, lng_ref, lnb_ref, l1g_ref, l1b_ref,
                  wpa_ref, wpb_ref, wo_ref, wq_ref, sk_ref, x1_ref, st_ref, *, alpha):
    ya = jnp.dot(ya_ref[...].astype(BF16), wpa_ref[...], preferred_element_type=F32)
    yb = jnp.dot(ob_ref[...].astype(BF16), wpb_ref[...], preferred_element_type=F32)
    pg = pg_ref[...]
    merged = jax.nn.sigmoid(pg[:, :D_MODEL]) * ya + jax.nn.sigmoid(pg[:, D_MODEL:]) * yb
    mix = jnp.dot(merged.astype(BF16), wo_ref[...], preferred_element_type=F32)
    xn = _layernorm(x_ref[...], lng_ref[...], lnb_ref[...])
    x1 = _layernorm(alpha * xn + gtm_ref[...] * mix, l1g_ref[...], l1b_ref[...])
    x1_ref[...] = x1
    h2 = x1 * (1.0 + scf_ref[...]) + shf_ref[...]
    q = jnp.dot(h2.astype(BF16), wq_ref[...], preferred_element_type=F32)
    for hc in range(2 * PEER_HEADS):
        st_ref[hc] = _dot3(sk_ref[hc % 2], q[:, hc * PEER_HALF:(hc + 1) * PEER_HALF], _NT)


def _merge(x, ya, ob, pg, mod, p, alpha, tb):
    g, r, _ = x.shape
    return pl.pallas_call(
        functools.partial(_merge_kernel, alpha=alpha),
        grid=(g, r // tb),
        in_specs=[_rows(tb, D_MODEL), _rows(tb, D_A), _rows(tb, D_B), _rows(tb, GATE_W),
                  _mod(mod, tb, 2), _mod(mod, tb, 3), _mod(mod, tb, 4)]
                 + [_const((1, D_MODEL))] * 4
                 + [_const((D_A, D_MODEL)), _const((D_B, D_MODEL)), _const((D_MODEL, D_MODEL)),
                    _const((D_MODEL, 2 * PEER_HEADS * PEER_HALF)), _const((2, N_KEYS, PEER_HALF))],
        out_specs=[_rows(tb, D_MODEL),
                   pl.BlockSpec((None, 2 * PEER_HEADS, N_KEYS, tb), lambda gi, i: (gi, 0, 0, i))],
        out_shape=[jax.ShapeDtypeStruct((g, r, D_MODEL), F32),
                   jax.ShapeDtypeStruct((g, 2 * PEER_HEADS, N_KEYS, r), F32)],
        compiler_params=_cparams("parallel", "parallel"),
        name="merge_ln1_peer_scores",
    )(x, ya, ob, pg, mod, mod, mod, _row2(p["ln_in_g"]), _row2(p["ln_in_b"]), _row2(p["ln1_g"]), _row2(p["ln1_b"]),
      p["w_pa"].astype(BF16), p["w_pb"].astype(BF16), p["w_o"].astype(BF16), p["peer_wq"].astype(BF16),
      p["peer_sub_keys"])


def _extract_top(vals, payload, n_rows, tb):
    rio = lax.broadcasted_iota(I32, (n_rows, tb), 0)
    top_v, top_i, top_p = [], [], []
    for _ in range(PEER_TOPK):
        m = jnp.max(vals, axis=0, keepdims=True)
        i = jnp.min(jnp.where(vals == m, rio, n_rows), axis=0, keepdims=True)
        sel = rio == i
        top_v.append(m)
        top_i.append(i)
        if payload is not None:
            top_p.append(jnp.max(jnp.where(sel, payload, -1), axis=0, keepdims=True))
        vals = jnp.where(sel, -jnp.inf, vals)
    cat = lambda z: jnp.concatenate(z, axis=0)
    return cat(top_v), cat(top_i), (cat(top_p) if payload is not None else None)


def _topk_kernel(s_ref, idx_ref, gate_ref):
    tb = s_ref.shape[-1]

    def head(h, carry):
        v1, i1, _ = _extract_top(s_ref[2 * h], None, N_KEYS, tb)
        v2, i2, _ = _extract_top(s_ref[2 * h + 1], None, N_KEYS, tb)
        cand = jnp.concatenate([v1[a:a + 1] + v2 for a in range(PEER_TOPK)], axis=0)
        eid = jnp.concatenate([i1[a:a + 1] * N_KEYS + i2 for a in range(PEER_TOPK)], axis=0)
        sc, _, ex = _extract_top(cand, eid, PEER_TOPK * PEER_TOPK, tb)
        pexp = jnp.exp(sc - sc[0:1])
        idx_ref[h] = ex
        gate_ref[h] = pexp / jnp.sum(pexp, axis=0, keepdims=True)
        return carry

    lax.fori_loop(0, PEER_HEADS, head, 0)


def _topk(scores_t, tb):
    g, _, _, r = scores_t.shape
    out = pl.BlockSpec((None, PEER_HEADS, PEER_TOPK, tb), lambda gi, i: (gi, 0, 0, i))
    return pl.pallas_call(
        _topk_kernel,
        grid=(g, r // tb),
        in_specs=[pl.BlockSpec((None, 2 * PEER_HEADS, N_KEYS, tb), lambda gi, i: (gi, 0, 0, i))],
        out_specs=[out, out],
        out_shape=[jax.ShapeDtypeStruct((g, PEER_HEADS, PEER_TOPK, r), I32),
                   jax.ShapeDtypeStruct((g, PEER_HEADS, PEER_TOPK, r), F32)],
        compiler_params=_cparams("parallel", "parallel"),
        name="peer_topk",
    )(scores_t)


def _sc_gather(table, idx):
    info = plsc.get_sparse_core_info()
    nc, ns = info.num_cores, info.num_subcores
    nw = nc * ns
    ni = idx.shape[0]
    w = table.shape[1]
    per_w = ni // nw
    n_chunks = per_w // SC_GATHER_ROWS
    n_outer = n_chunks // SC_IDX_ROWS
    assert per_w * nw == ni and n_chunks * SC_GATHER_ROWS == per_w and n_outer * SC_IDX_ROWS == n_chunks
    mesh = plsc.VectorSubcoreMesh(core_axis_name="c", subcore_axis_name="s")

    @functools.partial(
        pl.kernel, mesh=mesh,
        out_type=jax.ShapeDtypeStruct((ni, w), table.dtype),
        scratch_types=[pltpu.VMEM((SC_IDX_ROWS, SC_GATHER_ROWS), I32),
                       pltpu.VMEM((SC_GATHER_ROWS, w), table.dtype),
                       pltpu.VMEM((SC_GATHER_ROWS, w), table.dtype),
                       pltpu.SemaphoreType.DMA, pltpu.SemaphoreType.DMA],
        name="peer_row_gather",
    )
    def gather(table_hbm, idx_hbm, out_hbm, idx_v, rows0, rows1, sem0, sem1):
        wid = lax.axis_index("s") * nc + lax.axis_index("c")
        base = wid * per_w
        rows = (rows0, rows1)
        sems = (sem0, sem1)

        @pl.loop(0, n_outer)
        def _(o):
            pltpu.sync_copy(idx_hbm.at[wid, pl.ds(o * SC_IDX_ROWS, SC_IDX_ROWS)], idx_v)
            for j in range(SC_IDX_ROWS):
                b = j % 2
                pltpu.async_copy(table_hbm.at[idx_v.at[j]], rows[b], sems[b]).wait()
                pltpu.sync_copy(rows[b], out_hbm.at[pl.ds(base + (o * SC_IDX_ROWS + j) * SC_GATHER_ROWS, SC_GATHER_ROWS)])

    return gather(table, idx.reshape(nw, n_chunks, SC_GATHER_ROWS))


def _pack_bf16_pairs(t):
    half = t.shape[1] // 2
    b = lax.bitcast_convert_type(t.astype(BF16), jnp.uint16).astype(U32)
    return b[:, :half] | (b[:, half:] << 16)


def _gelu_erf(x):
    return 0.5 * x * (1.0 + lax.erf(x * (2.0 ** -0.5)))


def _unpack_pairs(words):
    lo = pltpu.bitcast(words << 16, F32)
    hi = pltpu.bitcast(words & jnp.uint32(0xFFFF0000), F32)
    return lo, hi


def _peer_mix_kernel(gu_ref, gv_ref, gate_ref, x1_ref, gtf_ref, shf_ref, scf_ref, l2g_ref, l2b_ref, o_ref, *, tb, alpha):
    half = D_MODEL // 2
    x1 = x1_ref[...]
    h2 = x1 * (1.0 + scf_ref[...]) + shf_ref[...]
    eye = (lax.broadcasted_iota(I32, (PEER_PICKS, PEER_PICKS), 0)
           == lax.broadcasted_iota(I32, (PEER_PICKS, PEER_PICKS), 1)).astype(BF16)
    gate_t = _dot_exact_lhs(eye, gate_ref[...], _NT)
    ff = []
    for t in range(tb):
        rows = slice(t * PEER_PICKS, (t + 1) * PEER_PICKS)
        ulo, uhi = _unpack_pairs(gu_ref[rows, :])
        hd = jnp.sum(ulo * h2[t:t + 1, :half] + uhi * h2[t:t + 1, half:], axis=-1, keepdims=True)
        w = _gelu_erf(hd) * gate_t[:, t:t + 1]
        vlo, vhi = _unpack_pairs(gv_ref[rows, :])
        ff.append(jnp.concatenate([jnp.sum(w * vlo, axis=0, keepdims=True),
                                   jnp.sum(w * vhi, axis=0, keepdims=True)], axis=1))
    ff = jnp.concatenate(ff, axis=0)
    o_ref[...] = _layernorm(alpha * x1 + gtf_ref[...] * ff, l2g_ref[...], l2b_ref[...])


def _peer_mix(gu, gv, gate, x1, mod, p, alpha, tb, row_block_offset):
    g, r, _ = x1.shape
    nblk = r // tb
    gspec = pl.BlockSpec((tb * PEER_PICKS, D_MODEL // 2), lambda gi, i: (row_block_offset + gi * nblk + i, 0))
    return pl.pallas_call(
        functools.partial(_peer_mix_kernel, tb=tb, alpha=alpha),
        grid=(g, nblk),
        in_specs=[gspec, gspec, _rows(tb, PEER_PICKS), _rows(tb, D_MODEL),
                  _mod(mod, tb, 5), _mod(mod, tb, 3), _mod(mod, tb, 4), _const((1, D_MODEL)), _const((1, D_MODEL))],
        out_specs=_rows(tb, D_MODEL),
        out_shape=jax.ShapeDtypeStruct((g, r, D_MODEL), F32),
        compiler_params=_cparams("parallel", "parallel"),
        name="peer_mix_ln2",
    )(gu, gv, gate, x1, mod, mod, mod, _row2(p["ln2_g"]), _row2(p["ln2_b"]))


PEER_TB = 16


def _token_stage(x, mod, prev_fn, wkv_fn, attn_fn, p, alpha, tb):
    ps, pattn, pgate = _inproj(x, mod, p["ln_in_g"], p["ln_in_b"], p["w_in_bf16"], tb)
    r, lw, k, v, kk, kka, gl = _rwkv_prep(ps, prev_fn(ps), p, tb)
    y, wkv_new = wkv_fn(r, lw, k, v, kk, kka)
    ya = _rwkv_post(y, r, k, v, gl, p, tb)
    ob = attn_fn(pattn)
    x1, scores_t = _merge(x, ya, ob, pgate, mod, p, alpha, tb)
    idx_t, gate_t = _topk(scores_t, 128)
    return ps, pattn, wkv_new, x1, idx_t, gate_t


def kernel(x_prompt, x_sample, state_wkv, state_shift, cache_k_win, cache_v_win, c_prompt, c_sample, ln_in_g, ln_in_b, w_ada, b_ada, w_in, mu_shift, rwkv_w0, rwkv_w2, rwkv_a0, rwkv_a2, rwkv_g2, rwkv_k_k, rwkv_k_a, rwkv_r_k, rwkv_gn_g, rwkv_gn_b, attn_sinks, w_pa, w_pb, w_o, ln1_g, ln1_b, peer_wq, peer_sub_keys, peer_u, peer_v, ln2_g, ln2_b):
    depth = w_in.shape[0]
    assert depth == 1, "single-layer trunk"
    alpha = (2.0 * depth) ** 0.25
    n_p, t_p, _ = x_prompt.shape
    n_s = x_sample.shape[0]
    p = dict(ln_in_g=ln_in_g, ln_in_b=ln_in_b, w_in_bf16=w_in[0].astype(BF16), mu_shift=mu_shift[0],
             rwkv_w0=rwkv_w0[0], rwkv_w2=rwkv_w2[0], rwkv_a0=rwkv_a0[0], rwkv_a2=rwkv_a2[0], rwkv_g2=rwkv_g2[0],
             rwkv_k_k=rwkv_k_k[0], rwkv_k_a=rwkv_k_a[0], rwkv_r_k=rwkv_r_k[0], rwkv_gn_g=rwkv_gn_g[0],
             rwkv_gn_b=rwkv_gn_b[0], w_pa=w_pa[0], w_pb=w_pb[0], w_o=w_o[0], ln1_g=ln1_g[0], ln1_b=ln1_b[0],
             peer_wq=peer_wq[0], peer_sub_keys=peer_sub_keys[0], ln2_g=ln2_g[0], ln2_b=ln2_b[0])
    sinks = attn_sinks[0]

    n_c = n_p + n_s
    pad = (-n_c) % 8
    c_all = jnp.concatenate([c_prompt, c_sample, jnp.zeros((pad, D_MODEL), F32)], axis=0)
    mod_all = _modulation(c_all, w_ada[0], b_ada[0])
    mod_p = mod_all[:n_p].reshape(n_p, 1, N_MOD * D_MODEL)
    mod_s = mod_all[n_p:n_c].reshape(1, n_s, N_MOD * D_MODEL)

    def prev_p(ps):
        return jnp.concatenate([jnp.zeros((n_p, 1, SHIFT_W), F32), ps[:, :-1]], axis=1)

    def wkv_p(r, lw, k, v, kk, kka):
        return _rwkv_chunk_scan(r, lw, k, v, kk, kka, jnp.zeros((n_p, H_A, HD_A, HD_A), F32))

    ps_p, pattn_p, wkv_p_new, x1_p, idx_p, gate_p = _token_stage(
        x_prompt, mod_p, prev_p, wkv_p, lambda pa: _attn_band(pa, sinks), p, alpha, 256)

    xs = x_sample.reshape(1, n_s, D_MODEL)

    def prev_s(ps):
        return state_shift[0].reshape(1, n_s, SHIFT_W)

    def wkv_s(r, lw, k, v, kk, kka):
        sq = lambda z: z.reshape(n_s, D_A)
        y, s = _rwkv_step(state_wkv[0], sq(r), sq(lw), sq(k), sq(kk), sq(kka), sq(v), 8)
        return y.reshape(1, n_s, D_A), s

    def attn_s(pa):
        o = _attn_cache(pa.reshape(n_s, 1, ATTN_W), cache_k_win[0], cache_v_win[0], sinks, 8)
        return o.reshape(1, n_s, D_B)

    ps_s, pattn_s, wkv_s_new, x1_s, idx_s, gate_s = _token_stage(xs, mod_s, prev_s, wkv_s, attn_s, p, alpha, 128)

    flat = lambda z: jnp.moveaxis(z, 3, 1).reshape(-1, PEER_PICKS)
    idx_all = jnp.concatenate([flat(idx_p), flat(idx_s)], axis=0).reshape(-1)
    gu = _sc_gather(_pack_bf16_pairs(peer_u[0]), idx_all)
    gv = _sc_gather(_pack_bf16_pairs(peer_v[0]), idx_all)
    gate_rows_p = flat(gate_p).reshape(n_p, t_p, PEER_PICKS)
    gate_rows_s = flat(gate_s).reshape(1, n_s, PEER_PICKS)
    y_p = _peer_mix(gu, gv, gate_rows_p, x1_p, mod_p, p, alpha, PEER_TB, 0)
    y_s = _peer_mix(gu, gv, gate_rows_s, x1_s, mod_s, p, alpha, PEER_TB, n_p * t_p // PEER_TB)

    kv = lambda pa, o: pa[..., o:o + H_KV * HD_B]
    ko, vo = D_B, D_B + H_KV * HD_B
    k_win_p = kv(pattn_p, ko)[:, -WINDOW:].reshape(n_p, WINDOW, H_KV, HD_B)
    v_win_p = kv(pattn_p, vo)[:, -WINDOW:].reshape(n_p, WINDOW, H_KV, HD_B)
    k_new_s = kv(pattn_s, ko).reshape(n_s, 1, H_KV, HD_B)
    v_new_s = kv(pattn_s, vo).reshape(n_s, 1, H_KV, HD_B)
    k_win_s = jnp.concatenate([cache_k_win[0], k_new_s], axis=1)[:, -WINDOW:]
    v_win_s = jnp.concatenate([cache_v_win[0], v_new_s], axis=1)[:, -WINDOW:]
    return (y_p, y_s.reshape(n_s, 1, D_MODEL), wkv_p_new[None], wkv_s_new[None],
            ps_p[:, -1][None], ps_s.reshape(n_s, SHIFT_W)[None],
            k_win_p[None], k_win_s[None], v_win_p[None], v_win_s[None])
```

```python
import functools
import math

import jax
import jax.numpy as jnp
from jax import lax
from jax.experimental import pallas as pl
from jax.experimental.pallas import tpu as pltpu
from jax.experimental.pallas import tpu_sc as plsc

F32 = jnp.float32
BF16 = jnp.bfloat16
I32 = jnp.int32
U32 = jnp.uint32

D_MODEL = 1024
H_A, HD_A = 8, 64
D_A = H_A * HD_A
D_LORA_W, D_LORA_A, D_LORA_G = 64, 64, 128
GN_EPS = 64e-5
H_Q, H_KV, HD_B = 8, 2, 64
G_Q = H_Q // H_KV
D_B = H_Q * HD_B
WINDOW = 128
N_KEYS = 128
PEER_HEADS, PEER_TOPK, PEER_HALF = 8, 16, 128
PEER_PICKS = PEER_HEADS * PEER_TOPK
N_MOD = 6
LN_EPS = 1e-5
NEG_INF = -1e30
OFF_WD = 3 * D_A
OFF_AD = OFF_WD + D_LORA_W
OFF_GD = OFF_AD + D_LORA_A
SHIFT_W = OFF_GD + D_LORA_G
ATTN_W = D_B + 2 * H_KV * HD_B
GATE_W = 2 * D_MODEL
D_IN = SHIFT_W + ATTN_W + GATE_W

VMEM_LIMIT = 48 * 1024 * 1024
RWKV_CHUNK = 64
SC_GATHER_ROWS = 32
SC_IDX_ROWS = 16
TOKEN_TB = 256
TOPK_TB = 128
PEER_TB = 16
STEP_NB = 8


def _cparams(*sem):
    return pltpu.CompilerParams(dimension_semantics=sem, vmem_limit_bytes=VMEM_LIMIT)


def _layernorm(x, g, b):
    mu = jnp.mean(x, -1, keepdims=True)
    xc = x - mu
    var = jnp.mean(xc * xc, -1, keepdims=True)
    return xc * lax.rsqrt(var + LN_EPS) * g + b


def _split(x):
    hi = x.astype(BF16)
    lo = (x - hi.astype(F32)).astype(BF16)
    return hi, lo


_NN = (((1,), (0,)), ((), ()))
_NT = (((1,), (1,)), ((), ()))
_TN = (((0,), (0,)), ((), ()))


def _dot3(a, b, dims=_NN):
    ah, al = _split(a)
    bh, bl = _split(b)
    d = functools.partial(lax.dot_general, dimension_numbers=dims, preferred_element_type=F32)
    return d(ah, bh) + d(ah, bl) + d(al, bh)


def _dot_exact_lhs(a_bf16, b, dims=_NN):
    b1 = b.astype(BF16)
    r1 = b - b1.astype(F32)
    b2 = r1.astype(BF16)
    b3 = (r1 - b2.astype(F32)).astype(BF16)
    d = functools.partial(lax.dot_general, dimension_numbers=dims, preferred_element_type=F32)
    return d(a_bf16, b1) + d(a_bf16, b2) + d(a_bf16, b3)


def _dotb(a, b, dims=_NN):
    return lax.dot_general(a.astype(BF16), b.astype(BF16), dims, preferred_element_type=F32)


def _rows(tb, width, col=0):
    return pl.BlockSpec((None, tb, width), lambda g, i: (g, i, col))


def _mod(mod, tb, col):
    if mod.shape[1] == 1:
        return pl.BlockSpec((None, 1, D_MODEL), lambda g, i: (g, 0, col))
    return pl.BlockSpec((None, tb, D_MODEL), lambda g, i: (g, i, col))


def _const(shape):
    n = len(shape)
    return pl.BlockSpec(shape, lambda g, i: (0,) * n)


def _row2(p):
    return p.reshape(1, -1).astype(F32)


def _mod_kernel(c_ref, w_ref, b_ref, o_ref):
    c = c_ref[...]
    a = c * jax.nn.sigmoid(c)
    o_ref[...] = _dot3(a, w_ref[...]) + b_ref[...]


def _modulation(c, w_ada, b_ada):
    n = c.shape[0]
    tn = D_MODEL
    return pl.pallas_call(
        _mod_kernel,
        grid=(w_ada.shape[1] // tn,),
        in_specs=[pl.BlockSpec((n, D_MODEL), lambda j: (0, 0)),
                  pl.BlockSpec((D_MODEL, tn), lambda j: (0, j)),
                  pl.BlockSpec((1, tn), lambda j: (0, j))],
        out_specs=pl.BlockSpec((n, tn), lambda j: (0, j)),
        out_shape=jax.ShapeDtypeStruct((n, w_ada.shape[1]), F32),
        compiler_params=_cparams("arbitrary"),
        name="modulation",
    )(c, w_ada, b_ada.reshape(1, -1))


def _inproj_kernel(x_ref, sh_ref, sc_ref, g_ref, b_ref, w_ref, ps_ref, pa_ref, pg_ref):
    xn = _layernorm(x_ref[...], g_ref[...], b_ref[...])
    h = (xn * (1.0 + sc_ref[...]) + sh_ref[...]).astype(BF16)
    ps_ref[...] = jnp.dot(h, w_ref[:, :SHIFT_W], preferred_element_type=F32)
    pa_ref[...] = jnp.dot(h, w_ref[:, SHIFT_W:SHIFT_W + ATTN_W], preferred_element_type=F32)
    pg_ref[...] = jnp.dot(h, w_ref[:, SHIFT_W + ATTN_W:], preferred_element_type=F32)


def _inproj(x, mod, ln_g, ln_b, w_in_bf16, tb):
    g, r, _ = x.shape
    shp = lambda w: jax.ShapeDtypeStruct((g, r, w), F32)
    return pl.pallas_call(
        _inproj_kernel,
        grid=(g, r // tb),
        in_specs=[_rows(tb, D_MODEL), _mod(mod, tb, 0), _mod(mod, tb, 1),
                  _const((1, D_MODEL)), _const((1, D_MODEL)), _const((D_MODEL, D_IN))],
        out_specs=[_rows(tb, SHIFT_W), _rows(tb, ATTN_W), _rows(tb, GATE_W)],
        out_shape=[shp(SHIFT_W), shp(ATTN_W), shp(GATE_W)],
        compiler_params=_cparams("parallel", "parallel"),
        name="inproj",
    )(x, mod, mod, _row2(ln_g), _row2(ln_b), w_in_bf16)


def _softplus(x):
    return jnp.maximum(x, 0.0) + jnp.log1p(jnp.exp(-jnp.abs(x)))


def _rwkv_prep_kernel(ps_ref, prev_ref, mu_ref, w0_ref, w2_ref, a0_ref, a2_ref, g2_ref, kk_w_ref, ka_w_ref,
                      hsum_ref, r_ref, lw_ref, k_ref, v_ref, kk_ref, kka_ref, g_ref):
    ps = ps_ref[...]
    xs = ps + (prev_ref[...] - ps) * mu_ref[...]
    r = xs[:, 0:D_A]
    k = xs[:, D_A:2 * D_A]
    v = xs[:, 2 * D_A:3 * D_A]
    wd = xs[:, OFF_WD:OFF_AD]
    ad = xs[:, OFF_AD:OFF_GD]
    gd = xs[:, OFF_GD:SHIFT_W]
    z = w0_ref[...] + _dot3(jnp.tanh(wd), w2_ref[...])
    w_log = -_softplus(-z) - 0.5
    a = jax.nn.sigmoid(a0_ref[...] + _dot3(ad, a2_ref[...]))
    kk = k * kk_w_ref[...]
    ss = _dot3(kk * kk, hsum_ref[...])
    kk = kk / jnp.maximum(jnp.sqrt(ss), 1e-12)
    r_ref[...] = r
    lw_ref[...] = -jnp.exp(w_log)
    k_ref[...] = k * (1.0 + (a - 1.0) * ka_w_ref[...])
    v_ref[...] = v
    kk_ref[...] = kk
    kka_ref[...] = kk * a
    g_ref[...] = _dot3(jax.nn.sigmoid(gd), g2_ref[...])


def _head_sum_matrix():
    h = jnp.arange(D_A) // HD_A
    return (h[:, None] == h[None, :]).astype(F32)


def _rwkv_prep(ps, prev, p, tb):
    g, r, _ = ps.shape
    shp = jax.ShapeDtypeStruct((g, r, D_A), F32)
    return pl.pallas_call(
        _rwkv_prep_kernel,
        grid=(g, r // tb),
        in_specs=[_rows(tb, SHIFT_W), _rows(tb, SHIFT_W), _const((1, SHIFT_W)),
                  _const((1, D_A)), _const((D_LORA_W, D_A)), _const((1, D_A)), _const((D_LORA_A, D_A)),
                  _const((D_LORA_G, D_A)), _const((1, D_A)), _const((1, D_A)), _const((D_A, D_A))],
        out_specs=[_rows(tb, D_A)] * 7,
        out_shape=[shp] * 7,
        compiler_params=_cparams("parallel", "parallel"),
        name="rwkv_prep",
    )(ps, prev, _row2(p["mu_shift"]), _row2(p["rwkv_w0"]), p["rwkv_w2"], _row2(p["rwkv_a0"]), p["rwkv_a2"],
      p["rwkv_g2"], _row2(p["rwkv_k_k"]), _row2(p["rwkv_k_a"]), _head_sum_matrix())


def _rwkv_chunk_kernel(r_ref, lw_ref, k_ref, v_ref, kk_ref, kka_ref, s0_ref, y_ref, s_ref):
    c = RWKV_CHUNK

    @pl.when(pl.program_id(1) == 0)
    def _():
        s_ref[...] = s0_ref[...]

    row = lax.broadcasted_iota(I32, (c, c), 0)
    col = lax.broadcasted_iota(I32, (c, c), 1)
    tril = row >= col
    stril = row > col
    lw = lw_ref[...]
    cum = _dot_exact_lhs(tril.astype(BF16), lw)
    cum_end = cum[c - 1:c, :]
    g_inv = jnp.exp(-cum)
    g_end = jnp.exp(cum_end - cum)
    a_hat = -kk_ref[...] * jnp.exp(cum - lw)
    b_hat = kka_ref[...] * g_inv
    k_hat = k_ref[...] * g_inv
    r_til = r_ref[...] * jnp.exp(cum)
    b_end = kka_ref[...] * g_end
    k_end = k_ref[...] * g_end
    gam_end = jnp.exp(cum_end)
    v_all = v_ref[...]
    s_all = s_ref[...]
    n_steps = int(math.log2(c))
    heads = range(H_A)
    sl = [slice(h * HD_A, (h + 1) * HD_A) for h in heads]
    vh = [v_all[:, sl[h]] for h in heads]
    ar = [jnp.concatenate([a_hat[:, sl[h]], r_til[:, sl[h]]], axis=0) for h in heads]
    bk = [jnp.concatenate([b_hat[:, sl[h]], k_hat[:, sl[h]]], axis=0) for h in heads]
    x = [_dot3(ar[h], bk[h], _NT) for h in heads]
    ars = [_dot3(ar[h], s_all[h], _NT) for h in heads]
    a_ak = [jnp.where(stril, x[h][:c, c:], 0.0) for h in heads]
    n = [jnp.where(stril, x[h][:c, :c], 0.0) for h in heads]
    u = [ars[h][:c] + _dot3(a_ak[h], vh[h]) for h in heads]
    for it in range(n_steps):
        u = [u[h] + _dot3(n[h], u[h]) for h in heads]
        if it + 1 < n_steps:
            n = [_dot3(n[h], n[h]) for h in heads]
    uv = [jnp.concatenate([u[h], vh[h]], axis=0) for h in heads]
    a_r = [jnp.concatenate([jnp.where(tril, x[h][c:, :c], 0.0), jnp.where(tril, x[h][c:, c:], 0.0)], axis=1)
           for h in heads]
    y = [ars[h][c:] + _dot3(a_r[h], uv[h]) for h in heads]
    bke = [jnp.concatenate([b_end[:, sl[h]], k_end[:, sl[h]]], axis=0) for h in heads]
    s_new = [s_all[h] * gam_end[:, sl[h]] + _dot3(uv[h], bke[h], _TN) for h in heads]
    for h in heads:
        y_ref[:, sl[h]] = y[h]
        s_ref[h] = s_new[h]


def _rwkv_chunk_scan(r, lw, k, v, kk, kka, s0):
    n, t, _ = r.shape
    c = RWKV_CHUNK
    seq = pl.BlockSpec((None, c, D_A), lambda b, i: (b, i, 0))
    st = pl.BlockSpec((None, H_A, HD_A, HD_A), lambda b, i: (b, 0, 0, 0))
    return pl.pallas_call(
        _rwkv_chunk_kernel,
        grid=(n, t // c),
        in_specs=[seq] * 6 + [st],
        out_specs=[seq, st],
        out_shape=[jax.ShapeDtypeStruct((n, t, D_A), F32), jax.ShapeDtypeStruct((n, H_A, HD_A, HD_A), F32)],
        compiler_params=_cparams("parallel", "arbitrary"),
        name="rwkv_chunk_scan",
    )(r, lw, k, v, kk, kka, s0)


def _rwkv_step_kernel(s_ref, r_ref, lw_ref, k_ref, kk_ref, kka_ref, v_ref, y_ref, so_ref):
    s = s_ref[...]
    sa = jnp.sum(s * (-kk_ref[...]), axis=-1, keepdims=True)
    s = s * jnp.exp(lw_ref[...]) + sa * kka_ref[...] + v_ref[...] * k_ref[...]
    so_ref[...] = s
    y_ref[...] = jnp.sum(s * r_ref[...], axis=-1, keepdims=True)


def _rwkv_step(s0, r, lw, k, kk, kka, v, nb):
    n = s0.shape[0]
    key = lambda z: z.reshape(n, H_A, 1, HD_A)
    st = pl.BlockSpec((nb, H_A, HD_A, HD_A), lambda i: (i, 0, 0, 0))
    ks = pl.BlockSpec((nb, H_A, 1, HD_A), lambda i: (i, 0, 0, 0))
    vs = pl.BlockSpec((nb, H_A, HD_A, 1), lambda i: (i, 0, 0, 0))
    y, s = pl.pallas_call(
        _rwkv_step_kernel,
        grid=(n // nb,),
        in_specs=[st, ks, ks, ks, ks, ks, vs],
        out_specs=[vs, st],
        out_shape=[jax.ShapeDtypeStruct((n, H_A, HD_A, 1), F32), jax.ShapeDtypeStruct(s0.shape, F32)],
        compiler_params=_cparams("parallel"),
        name="rwkv_step",
    )(s0, key(r), key(lw), key(k), key(kk), key(kka), v.reshape(n, H_A, HD_A, 1))
    return y.reshape(n, D_A), s


def _rwkv_post_kernel(y_ref, r_ref, k_ref, v_ref, g_ref, gn_g_ref, gn_b_ref, rk_ref, hsum_ref, o_ref):
    y = y_ref[...]
    hs = hsum_ref[...]
    mu = _dot3(y, hs) * (1.0 / HD_A)
    yc = y - mu
    var = _dot3(yc * yc, hs) * (1.0 / HD_A)
    yn = yc * lax.rsqrt(var + GN_EPS) * gn_g_ref[...] + gn_b_ref[...]
    bonus = _dot3(r_ref[...] * k_ref[...] * rk_ref[...], hs) * v_ref[...]
    o_ref[...] = (yn + bonus) * g_ref[...]


def _rwkv_post(y, r, k, v, g, p, tb):
    gg, rr, _ = y.shape
    return pl.pallas_call(
        _rwkv_post_kernel,
        grid=(gg, rr // tb),
        in_specs=[_rows(tb, D_A)] * 5 + [_const((1, D_A))] * 3 + [_const((D_A, D_A))],
        out_specs=_rows(tb, D_A),
        out_shape=jax.ShapeDtypeStruct((gg, rr, D_A), F32),
        compiler_params=_cparams("parallel", "parallel"),
        name="rwkv_post",
    )(y, r, k, v, g, _row2(p["rwkv_gn_g"]), _row2(p["rwkv_gn_b"]), _row2(p["rwkv_r_k"]), _head_sum_matrix())


def _sink_softmax(s, sink):
    m = jnp.maximum(jnp.max(s, axis=-1, keepdims=True), sink)
    p = jnp.exp(s - m)
    den = jnp.sum(p, axis=-1, keepdims=True) + jnp.exp(sink - m)
    return p / den


def _attn_band_kernel(cur_ref, prev_ref, sink_ref, o_ref):
    blk = WINDOW
    i = pl.program_id(1)
    cur = cur_ref[...]
    prev = prev_ref[...]
    qi = lax.broadcasted_iota(I32, (G_Q * blk, 2 * blk), 0) % blk
    kj = lax.broadcasted_iota(I32, (G_Q * blk, 2 * blk), 1)
    rel = blk + qi - kj
    valid = (rel >= 0) & (rel <= WINDOW) & ((kj >= blk) | (i > 0))
    relf = rel.astype(F32)
    gidx = lax.broadcasted_iota(I32, (G_Q * blk, 1), 0) // blk
    for kvh in range(H_KV):
        q4 = jnp.concatenate([cur[:, (kvh * G_Q + g) * HD_B:(kvh * G_Q + g + 1) * HD_B] for g in range(G_Q)], axis=0)
        ko = D_B + kvh * HD_B
        vo = D_B + H_KV * HD_B + kvh * HD_B
        kmat = jnp.concatenate([prev[:, ko:ko + HD_B], cur[:, ko:ko + HD_B]], axis=0)
        vmat = jnp.concatenate([prev[:, vo:vo + HD_B], cur[:, vo:vo + HD_B]], axis=0)
        slope = jnp.zeros((G_Q * blk, 1), F32)
        sink = jnp.zeros((G_Q * blk, 1), F32)
        for g in range(G_Q):
            hq = kvh * G_Q + g
            slope = jnp.where(gidx == g, 2.0 ** (-8.0 * (hq + 1) / H_Q), slope)
            sink = jnp.where(gidx == g, sink_ref[hq], sink)
        s = _dotb(q4, kmat, _NT) * (HD_B ** -0.5)
        s = jnp.where(valid, s - slope * relf, NEG_INF)
        p = _sink_softmax(s, sink)
        o = _dotb(p, vmat)
        for g in range(G_Q):
            hq = kvh * G_Q + g
            o_ref[:, hq * HD_B:(hq + 1) * HD_B] = o[g * blk:(g + 1) * blk]


def _attn_band(pattn, sinks):
    n, t, _ = pattn.shape
    blk = WINDOW
    return pl.pallas_call(
        _attn_band_kernel,
        grid=(n, t // blk),
        in_specs=[pl.BlockSpec((None, blk, ATTN_W), lambda b, i: (b, i, 0)),
                  pl.BlockSpec((None, blk, ATTN_W), lambda b, i: (b, jnp.maximum(i - 1, 0), 0)),
                  pl.BlockSpec(memory_space=pltpu.SMEM)],
        out_specs=pl.BlockSpec((None, blk, D_B), lambda b, i: (b, i, 0)),
        out_shape=jax.ShapeDtypeStruct((n, t, D_B), F32),
        compiler_params=_cparams("parallel", "parallel"),
        name="attn_band",
    )(pattn, pattn, sinks.astype(F32))


def _attn_cache_kernel(cur_ref, kc_ref, vc_ref, sink_ref, o_ref, *, nb):
    relc = (WINDOW - lax.broadcasted_iota(I32, (G_Q, WINDOW), 1)).astype(F32)
    gidx = lax.broadcasted_iota(I32, (G_Q, 1), 0)
    for b in range(nb):
        cur = cur_ref[b]
        for kvh in range(H_KV):
            q4 = jnp.concatenate([cur[:, (kvh * G_Q + g) * HD_B:(kvh * G_Q + g + 1) * HD_B] for g in range(G_Q)], axis=0)
            ko = D_B + kvh * HD_B
            vo = D_B + H_KV * HD_B + kvh * HD_B
            k_new = cur[:, ko:ko + HD_B]
            v_new = cur[:, vo:vo + HD_B]
            kc = kc_ref[b, :, kvh * HD_B:(kvh + 1) * HD_B]
            vc = vc_ref[b, :, kvh * HD_B:(kvh + 1) * HD_B]
            slope = jnp.zeros((G_Q, 1), F32)
            sink = jnp.zeros((G_Q, 1), F32)
            for g in range(G_Q):
                hq = kvh * G_Q + g
                slope = jnp.where(gidx == g, 2.0 ** (-8.0 * (hq + 1) / H_Q), slope)
                sink = jnp.where(gidx == g, sink_ref[hq], sink)
            scale = HD_B ** -0.5
            sc = _dotb(q4, kc, _NT) * scale - slope * relc
            sn = jnp.sum(q4.astype(BF16).astype(F32) * k_new.astype(BF16).astype(F32), axis=-1, keepdims=True) * scale
            m = jnp.maximum(jnp.maximum(jnp.max(sc, axis=-1, keepdims=True), sn), sink)
            pc = jnp.exp(sc - m)
            pn = jnp.exp(sn - m)
            den = jnp.sum(pc, axis=-1, keepdims=True) + pn + jnp.exp(sink - m)
            o = (_dotb(pc / den, vc) + (pn / den).astype(BF16).astype(F32) * v_new.astype(BF16).astype(F32))
            for g in range(G_Q):
                hq = kvh * G_Q + g
                o_ref[b, :, hq * HD_B:(hq + 1) * HD_B] = o[g:g + 1]


def _attn_cache(pattn, k_buf, v_buf, sinks, nb):
    n = pattn.shape[0]
    kc = k_buf.reshape(n, WINDOW, H_KV * HD_B)
    vc = v_buf.reshape(n, WINDOW, H_KV * HD_B)
    return pl.pallas_call(
        functools.partial(_attn_cache_kernel, nb=nb),
        grid=(n // nb,),
        in_specs=[pl.BlockSpec((nb, 1, ATTN_W), lambda i: (i, 0, 0)),
                  pl.BlockSpec((nb, WINDOW, H_KV * HD_B), lambda i: (i, 0, 0)),
                  pl.BlockSpec((nb, WINDOW, H_KV * HD_B), lambda i: (i, 0, 0)),
                  pl.BlockSpec(memory_space=pltpu.SMEM)],
        out_specs=pl.BlockSpec((nb, 1, D_B), lambda i: (i, 0, 0)),
        out_shape=jax.ShapeDtypeStruct((n, 1, D_B), F32),
        compiler_params=_cparams("parallel"),
        name="attn_cache",
    )(pattn, kc, vc, sinks.astype(F32))


def _merge_kernel(x_ref, ya_ref, ob_ref, pg_ref, gtm_ref, shf_ref, scf_ref, lng_ref, lnb_ref, l1g_ref, l1b_ref,
                  wpa_ref, wpb_ref, wo_ref, wq_ref, sk_ref, x1_ref, st_ref, *, alpha):
    ya = jnp.dot(ya_ref[...].astype(BF16), wpa_ref[...], preferred_element_type=F32)
    yb = jnp.dot(ob_ref[...].astype(BF16), wpb_ref[...], preferred_element_type=F32)
    pg = pg_ref[...]
    merged = jax.nn.sigmoid(pg[:, :D_MODEL]) * ya + jax.nn.sigmoid(pg[:, D_MODEL:]) * yb
    mix = jnp.dot(merged.astype(BF16), wo_ref[...], preferred_element_type=F32)
    xn = _layernorm(x_ref[...], lng_ref[...], lnb_ref[...])
    x1 = _layernorm(alpha * xn + gtm_ref[...] * mix, l1g_ref[...], l1b_ref[...])
    x1_ref[...] = x1
    h2 = x1 * (1.0 + scf_ref[...]) + shf_ref[...]
    q = jnp.dot(h2.astype(BF16), wq_ref[...], preferred_element_type=F32)
    for hc in range(2 * PEER_HEADS):
        st_ref[hc] = _dot3(sk_ref[hc % 2], q[:, hc * PEER_HALF:(hc + 1) * PEER_HALF], _NT)


def _merge(x, ya, ob, pg, mod, p, alpha, tb):
    g, r, _ = x.shape
    return pl.pallas_call(
        functools.partial(_merge_kernel, alpha=alpha),
        grid=(g, r // tb),
        in_specs=[_rows(tb, D_MODEL), _rows(tb, D_A), _rows(tb, D_B), _rows(tb, GATE_W),
                  _mod(mod, tb, 2), _mod(mod, tb, 3), _mod(mod, tb, 4)]
                 + [_const((1, D_MODEL))] * 4
                 + [_const((D_A, D_MODEL)), _const((D_B, D_MODEL)), _const((D_MODEL, D_MODEL)),
                    _const((D_MODEL, 2 * PEER_HEADS * PEER_HALF)), _const((2, N_KEYS, PEER_HALF))],
        out_specs=[_rows(tb, D_MODEL),
                   pl.BlockSpec((None, 2 * PEER_HEADS, N_KEYS, tb), lambda gi, i: (gi, 0, 0, i))],
        out_shape=[jax.ShapeDtypeStruct((g, r, D_MODEL), F32),
                   jax.ShapeDtypeStruct((g, 2 * PEER_HEADS, N_KEYS, r), F32)],
        compiler_params=_cparams("parallel", "parallel"),
        name="merge_ln1_peer_scores",
    )(x, ya, ob, pg, mod, mod, mod, _row2(p["ln_in_g"]), _row2(p["ln_in_b"]), _row2(p["ln1_g"]), _row2(p["ln1_b"]),
      p["w_pa"].astype(BF16), p["w_pb"].astype(BF16), p["w_o"].astype(BF16), p["peer_wq"].astype(BF16),
      p["peer_sub_keys"])


def _extract_top(vals, payload, n_rows, tb):
    rio = lax.broadcasted_iota(I32, (n_rows, tb), 0).astype(F32)
    top_v, top_i, top_p = [], [], []
    for _ in range(PEER_TOPK):
        m = jnp.max(vals, axis=0, keepdims=True)
        i = jnp.min(jnp.where(vals == m, rio, float(n_rows)), axis=0, keepdims=True)
        sel = rio == i
        top_v.append(m)
        top_i.append(i)
        if payload is not None:
            top_p.append(jnp.max(jnp.where(sel, payload, -1.0), axis=0, keepdims=True))
        vals = jnp.where(sel, -jnp.inf, vals)
    cat = lambda z: jnp.concatenate(z, axis=0)
    return cat(top_v), cat(top_i), (cat(top_p) if payload is not None else None)


def _pair_candidates(v1, i1, v2, i2, tb):
    k = PEER_TOPK
    sub = 8
    eid = lambda a0, a1, b0, b1: i1[a0:a1] * float(N_KEYS) + i2[b0:b1]
    vals = [v1[0:1] + v2, v1[1:2] + v2[0:sub]]
    ids = [eid(0, 1, 0, k), eid(1, 2, 0, sub)]
    brow = lax.broadcasted_iota(I32, (sub, tb), 0)
    for a in range(2, sub):
        vals.append(jnp.where(brow < k // (a + 1), v1[a:a + 1] + v2[0:sub], -jnp.inf))
        ids.append(eid(a, a + 1, 0, sub))
    vals.append(v1[sub:k] + v2[0:1])
    ids.append(eid(sub, k, 0, 1))
    return jnp.concatenate(vals, axis=0), jnp.concatenate(ids, axis=0)


def _topk_kernel(s_ref, idx_ref, gate_ref):
    tb = s_ref.shape[-1]

    def head(h, carry):
        v1, i1, _ = _extract_top(s_ref[2 * h], None, N_KEYS, tb)
        v2, i2, _ = _extract_top(s_ref[2 * h + 1], None, N_KEYS, tb)
        cand, eid = _pair_candidates(v1, i1, v2, i2, tb)
        sc, _, ex = _extract_top(cand, eid, cand.shape[0], tb)
        pexp = jnp.exp(sc - sc[0:1])
        idx_ref[h] = ex.astype(I32)
        gate_ref[h] = pexp / jnp.sum(pexp, axis=0, keepdims=True)
        return carry

    lax.fori_loop(0, PEER_HEADS, head, 0)


def _topk(scores_t, tb):
    g, _, _, r = scores_t.shape
    out = pl.BlockSpec((None, PEER_HEADS, PEER_TOPK, tb), lambda gi, i: (gi, 0, 0, i))
    return pl.pallas_call(
        _topk_kernel,
        grid=(g, r // tb),
        in_specs=[pl.BlockSpec((None, 2 * PEER_HEADS, N_KEYS, tb), lambda gi, i: (gi, 0, 0, i))],
        out_specs=[out, out],
        out_shape=[jax.ShapeDtypeStruct((g, PEER_HEADS, PEER_TOPK, r), I32),
                   jax.ShapeDtypeStruct((g, PEER_HEADS, PEER_TOPK, r), F32)],
        compiler_params=_cparams("parallel", "parallel"),
        name="peer_topk",
    )(scores_t)


def _sc_gather_pair(tu, tv, idx):
    info = plsc.get_sparse_core_info()
    nc, ns = info.num_cores, info.num_subcores
    nw = nc * ns
    ni = idx.shape[0]
    w = tu.shape[1]
    rr, ki = SC_GATHER_ROWS, SC_IDX_ROWS
    per_w = ni // nw
    n_chunks = per_w // rr
    n_outer = n_chunks // ki
    assert per_w * nw == ni and n_chunks * rr == per_w and n_outer * ki == n_chunks
    mesh = plsc.VectorSubcoreMesh(core_axis_name="c", subcore_axis_name="s")
    buf = pltpu.VMEM((rr, w), tu.dtype)
    out = jax.ShapeDtypeStruct((ni, w), tu.dtype)

    @functools.partial(
        pl.kernel, mesh=mesh, out_type=(out, out),
        scratch_types=[pltpu.VMEM((ki, rr), I32), buf, buf, buf, buf] + [pltpu.SemaphoreType.DMA] * 8,
        name="peer_row_gather",
    )
    def gather(tu_hbm, tv_hbm, idx_hbm, ou_hbm, ov_hbm, idx_v, bu0, bu1, bv0, bv1, *sems):
        wid = lax.axis_index("s") * nc + lax.axis_index("c")
        base = wid * per_w
        bufs = ((bu0, bv0), (bu1, bv1))
        gsem = ((sems[0], sems[1]), (sems[2], sems[3]))
        wsem = ((sems[4], sems[5]), (sems[6], sems[7]))

        @pl.loop(0, n_outer)
        def _(o):
            pltpu.sync_copy(idx_hbm.at[wid, pl.ds(o * ki, ki)], idx_v)

            def start_gather(j):
                b = j % 2
                return (pltpu.async_copy(tu_hbm.at[idx_v.at[j]], bufs[b][0], gsem[b][0]),
                        pltpu.async_copy(tv_hbm.at[idx_v.at[j]], bufs[b][1], gsem[b][1]))

            def start_write(j):
                b = j % 2
                dst = pl.ds(base + (o * ki + j) * rr, rr)
                return (pltpu.async_copy(bufs[b][0], ou_hbm.at[dst], wsem[b][0]),
                        pltpu.async_copy(bufs[b][1], ov_hbm.at[dst], wsem[b][1]))

            gathers = {0: start_gather(0)}
            writes = {}
            for j in range(ki):
                if j + 1 < ki:
                    if j >= 1:
                        for cp in writes.pop(j - 1):
                            cp.wait()
                    gathers[j + 1] = start_gather(j + 1)
                for cp in gathers.pop(j):
                    cp.wait()
                writes[j] = start_write(j)
            for j in sorted(writes):
                for cp in writes[j]:
                    cp.wait()

    return gather(tu, tv, idx.reshape(nw, n_chunks, rr))


def _pack_bf16_pairs(t):
    half = t.shape[1] // 2
    b = lax.bitcast_convert_type(t.astype(BF16), jnp.uint16).astype(U32)
    return b[:, :half] | (b[:, half:] << 16)


def _gelu_erf(x):
    return 0.5 * x * (1.0 + lax.erf(x * (2.0 ** -0.5)))


def _unpack_pairs(words):
    lo = pltpu.bitcast(words << 16, F32)
    hi = pltpu.bitcast(words & jnp.uint32(0xFFFF0000), F32)
    return lo, hi


def _peer_mix_kernel(gu_ref, gv_ref, gate_ref, x1_ref, gtf_ref, shf_ref, scf_ref, l2g_ref, l2b_ref, o_ref, *, tb, alpha):
    half = D_MODEL // 2
    x1 = x1_ref[...]
    h2 = x1 * (1.0 + scf_ref[...]) + shf_ref[...]
    eye = (lax.broadcasted_iota(I32, (PEER_PICKS, PEER_PICKS), 0)
           == lax.broadcasted_iota(I32, (PEER_PICKS, PEER_PICKS), 1)).astype(BF16)
    gate_t = _dot_exact_lhs(eye, gate_ref[...], _NT)
    ff = []
    for t in range(tb):
        rows = slice(t * PEER_PICKS, (t + 1) * PEER_PICKS)
        ulo, uhi = _unpack_pairs(gu_ref[rows, :])
        hd = jnp.sum(ulo * h2[t:t + 1, :half] + uhi * h2[t:t + 1, half:], axis=-1, keepdims=True)
        w = _gelu_erf(hd) * gate_t[:, t:t + 1]
        vlo, vhi = _unpack_pairs(gv_ref[rows, :])
        ff.append(jnp.concatenate([jnp.sum(w * vlo, axis=0, keepdims=True),
                                   jnp.sum(w * vhi, axis=0, keepdims=True)], axis=1))
    ff = jnp.concatenate(ff, axis=0)
    o_ref[...] = _layernorm(alpha * x1 + gtf_ref[...] * ff, l2g_ref[...], l2b_ref[...])


def _peer_mix(gu, gv, gate, x1, mod, p, alpha, tb, g0):
    r = x1.shape[1]
    gspec = pl.BlockSpec((tb * PEER_PICKS, D_MODEL // 2), lambda gi, i: (i, 0))
    rows = lambda w: pl.BlockSpec((None, tb, w), lambda gi, i: (g0, i, 0))
    if mod.shape[1] == 1:
        mods = lambda col: pl.BlockSpec((None, 1, D_MODEL), lambda gi, i: (g0, 0, col))
    else:
        mods = lambda col: pl.BlockSpec((None, tb, D_MODEL), lambda gi, i: (g0, i, col))
    return pl.pallas_call(
        functools.partial(_peer_mix_kernel, tb=tb, alpha=alpha),
        grid=(1, r // tb),
        in_specs=[gspec, gspec, rows(PEER_PICKS), rows(D_MODEL),
                  mods(5), mods(3), mods(4), _const((1, D_MODEL)), _const((1, D_MODEL))],
        out_specs=_rows(tb, D_MODEL),
        out_shape=jax.ShapeDtypeStruct((1, r, D_MODEL), F32),
        compiler_params=_cparams("parallel", "parallel"),
        name="peer_mix_ln2",
    )(gu, gv, gate, x1, mod, mod, mod, _row2(p["ln2_g"]), _row2(p["ln2_b"]))


def _token_stage(x, mod, prev_fn, wkv_fn, attn_fn, p, alpha, tb):
    ps, pattn, pgate = _inproj(x, mod, p["ln_in_g"], p["ln_in_b"], p["w_in_bf16"], tb)
    r, lw, k, v, kk, kka, gl = _rwkv_prep(ps, prev_fn(ps), p, tb)
    y, wkv_new = wkv_fn(r, lw, k, v, kk, kka)
    ya = _rwkv_post(y, r, k, v, gl, p, tb)
    ob = attn_fn(pattn)
    x1, scores_t = _merge(x, ya, ob, pgate, mod, p, alpha, tb)
    idx_t, gate_t = _topk(scores_t, TOPK_TB)
    return ps, pattn, wkv_new, x1, idx_t, gate_t


def kernel(x_prompt, x_sample, state_wkv, state_shift, cache_k_win, cache_v_win, c_prompt, c_sample, ln_in_g, ln_in_b, w_ada, b_ada, w_in, mu_shift, rwkv_w0, rwkv_w2, rwkv_a0, rwkv_a2, rwkv_g2, rwkv_k_k, rwkv_k_a, rwkv_r_k, rwkv_gn_g, rwkv_gn_b, attn_sinks, w_pa, w_pb, w_o, ln1_g, ln1_b, peer_wq, peer_sub_keys, peer_u, peer_v, ln2_g, ln2_b):
    depth = w_in.shape[0]
    assert depth == 1, "single-layer trunk"
    alpha = (2.0 * depth) ** 0.25
    n_p, t_p, _ = x_prompt.shape
    n_s = x_sample.shape[0]
    p = dict(ln_in_g=ln_in_g, ln_in_b=ln_in_b, w_in_bf16=w_in[0].astype(BF16), mu_shift=mu_shift[0],
             rwkv_w0=rwkv_w0[0], rwkv_w2=rwkv_w2[0], rwkv_a0=rwkv_a0[0], rwkv_a2=rwkv_a2[0], rwkv_g2=rwkv_g2[0],
             rwkv_k_k=rwkv_k_k[0], rwkv_k_a=rwkv_k_a[0], rwkv_r_k=rwkv_r_k[0], rwkv_gn_g=rwkv_gn_g[0],
             rwkv_gn_b=rwkv_gn_b[0], w_pa=w_pa[0], w_pb=w_pb[0], w_o=w_o[0], ln1_g=ln1_g[0], ln1_b=ln1_b[0],
             peer_wq=peer_wq[0], peer_sub_keys=peer_sub_keys[0], ln2_g=ln2_g[0], ln2_b=ln2_b[0])
    sinks = attn_sinks[0]

    n_c = n_p + n_s
    pad = (-n_c) % 8
    c_all = jnp.concatenate([c_prompt, c_sample, jnp.zeros((pad, D_MODEL), F32)], axis=0)
    mod_all = _modulation(c_all, w_ada[0], b_ada[0])
    mod_p = mod_all[:n_p].reshape(n_p, 1, N_MOD * D_MODEL)
    mod_s = mod_all[n_p:n_c].reshape(1, n_s, N_MOD * D_MODEL)

    def prev_p(ps):
        return jnp.concatenate([jnp.zeros((n_p, 1, SHIFT_W), F32), ps[:, :-1]], axis=1)

    def wkv_p(r, lw, k, v, kk, kka):
        return _rwkv_chunk_scan(r, lw, k, v, kk, kka, jnp.zeros((n_p, H_A, HD_A, HD_A), F32))

    ps_p, pattn_p, wkv_p_new, x1_p, idx_p, gate_p = _token_stage(
        x_prompt, mod_p, prev_p, wkv_p, lambda pa: _attn_band(pa, sinks), p, alpha, TOKEN_TB)

    xs = x_sample.reshape(1, n_s, D_MODEL)

    def prev_s(ps):
        return state_shift[0].reshape(1, n_s, SHIFT_W)

    def wkv_s(r, lw, k, v, kk, kka):
        sq = lambda z: z.reshape(n_s, D_A)
        y, s = _rwkv_step(state_wkv[0], sq(r), sq(lw), sq(k), sq(kk), sq(kka), sq(v), STEP_NB)
        return y.reshape(1, n_s, D_A), s

    def attn_s(pa):
        o = _attn_cache(pa.reshape(n_s, 1, ATTN_W), cache_k_win[0], cache_v_win[0], sinks, STEP_NB)
        return o.reshape(1, n_s, D_B)

    ps_s, pattn_s, wkv_s_new, x1_s, idx_s, gate_s = _token_stage(xs, mod_s, prev_s, wkv_s, attn_s, p, alpha, min(TOKEN_TB, n_s))

    flat = lambda z: jnp.moveaxis(z, 3, 1).reshape(z.shape[0], -1, PEER_PICKS)
    tu, tv = _pack_bf16_pairs(peer_u[0]), _pack_bf16_pairs(peer_v[0])
    idx_rows_p, gate_rows_p = flat(idx_p), flat(gate_p)
    idx_rows_s, gate_rows_s = flat(idx_s), flat(gate_s)
    y_parts = []
    for b in range(n_p):
        gu, gv = _sc_gather_pair(tu, tv, idx_rows_p[b].reshape(-1))
        y_parts.append(_peer_mix(gu, gv, gate_rows_p, x1_p, mod_p, p, alpha, PEER_TB, b))
    y_p = jnp.concatenate(y_parts, axis=0)
    gu, gv = _sc_gather_pair(tu, tv, idx_rows_s[0].reshape(-1))
    y_s = _peer_mix(gu, gv, gate_rows_s, x1_s, mod_s, p, alpha, PEER_TB, 0)

    kv = lambda pa, o: pa[..., o:o + H_KV * HD_B]
    ko, vo = D_B, D_B + H_KV * HD_B
    k_win_p = kv(pattn_p, ko)[:, -WINDOW:].reshape(n_p, WINDOW, H_KV, HD_B)
    v_win_p = kv(pattn_p, vo)[:, -WINDOW:].reshape(n_p, WINDOW, H_KV, HD_B)
    k_new_s = kv(pattn_s, ko).reshape(n_s, 1, H_KV, HD_B)
    v_new_s = kv(pattn_s, vo).reshape(n_s, 1, H_KV, HD_B)
    k_win_s = jnp.concatenate([cache_k_win[0], k_new_s], axis=1)[:, -WINDOW:]
    v_win_s = jnp.concatenate([cache_v_win[0], v_new_s], axis=1)[:, -WINDOW:]
    return (y_p, y_s.reshape(n_s, 1, D_MODEL), wkv_p_new[None], wkv_s_new[None],
            ps_p[:, -1][None], ps_s.reshape(n_s, SHIFT_W)[None],
            k_win_p[None], k_win_s[None], v_win_p[None], v_win_s[None])
```

```python
import functools
import math

import jax
import jax.numpy as jnp
from jax import lax
from jax.experimental import pallas as pl
from jax.experimental.pallas import tpu as pltpu
from jax.experimental.pallas import tpu_sc as plsc

F32 = jnp.float32
BF16 = jnp.bfloat16
I32 = jnp.int32
U32 = jnp.uint32

D_MODEL = 1024
H_A, HD_A = 8, 64
D_A = H_A * HD_A
D_LORA_W, D_LORA_A, D_LORA_G = 64, 64, 128
GN_EPS = 64e-5
H_Q, H_KV, HD_B = 8, 2, 64
G_Q = H_Q // H_KV
D_B = H_Q * HD_B
WINDOW = 128
N_KEYS = 128
PEER_HEADS, PEER_TOPK, PEER_HALF = 8, 16, 128
PEER_PICKS = PEER_HEADS * PEER_TOPK
N_MOD = 6
LN_EPS = 1e-5
NEG_INF = -1e30
OFF_WD = 3 * D_A
OFF_AD = OFF_WD + D_LORA_W
OFF_GD = OFF_AD + D_LORA_A
SHIFT_W = OFF_GD + D_LORA_G
ATTN_W = D_B + 2 * H_KV * HD_B
GATE_W = 2 * D_MODEL
D_IN = SHIFT_W + ATTN_W + GATE_W

VMEM_LIMIT = 48 * 1024 * 1024
RWKV_CHUNK = 64
SC_GATHER_ROWS = 32
SC_IDX_ROWS = 16
TOKEN_TB = 256
TOPK_TB = 128
PEER_TB = 16
STEP_NB = 8


def _cparams(*sem):
    return pltpu.CompilerParams(dimension_semantics=sem, vmem_limit_bytes=VMEM_LIMIT)


def _layernorm(x, g, b):
    mu = jnp.mean(x, -1, keepdims=True)
    xc = x - mu
    var = jnp.mean(xc * xc, -1, keepdims=True)
    return xc * lax.rsqrt(var + LN_EPS) * g + b


def _split(x):
    hi = x.astype(BF16)
    lo = (x - hi.astype(F32)).astype(BF16)
    return hi, lo


_NN = (((1,), (0,)), ((), ()))
_NT = (((1,), (1,)), ((), ()))
_TN = (((0,), (0,)), ((), ()))


def _dot3(a, b, dims=_NN):
    ah, al = _split(a)
    bh, bl = _split(b)
    d = functools.partial(lax.dot_general, dimension_numbers=dims, preferred_element_type=F32)
    return d(ah, bh) + d(ah, bl) + d(al, bh)


def _dot_exact_lhs(a_bf16, b, dims=_NN):
    b1 = b.astype(BF16)
    r1 = b - b1.astype(F32)
    b2 = r1.astype(BF16)
    b3 = (r1 - b2.astype(F32)).astype(BF16)
    d = functools.partial(lax.dot_general, dimension_numbers=dims, preferred_element_type=F32)
    return d(a_bf16, b1) + d(a_bf16, b2) + d(a_bf16, b3)


def _dotb(a, b, dims=_NN):
    return lax.dot_general(a.astype(BF16), b.astype(BF16), dims, preferred_element_type=F32)


def _rows(tb, width, col=0):
    return pl.BlockSpec((None, tb, width), lambda g, i: (g, i, col))


def _mod(mod, tb, col):
    if mod.shape[1] == 1:
        return pl.BlockSpec((None, 1, D_MODEL), lambda g, i: (g, 0, col))
    return pl.BlockSpec((None, tb, D_MODEL), lambda g, i: (g, i, col))


def _const(shape):
    n = len(shape)
    return pl.BlockSpec(shape, lambda g, i: (0,) * n)


def _row2(p):
    return p.reshape(1, -1).astype(F32)


def _mod_kernel(c_ref, w_ref, b_ref, o_ref):
    c = c_ref[...]
    a = c * jax.nn.sigmoid(c)
    o_ref[...] = _dot3(a, w_ref[...]) + b_ref[...]


def _modulation(c, w_ada, b_ada):
    n = c.shape[0]
    tn = D_MODEL
    return pl.pallas_call(
        _mod_kernel,
        grid=(w_ada.shape[1] // tn,),
        in_specs=[pl.BlockSpec((n, D_MODEL), lambda j: (0, 0)),
                  pl.BlockSpec((D_MODEL, tn), lambda j: (0, j)),
                  pl.BlockSpec((1, tn), lambda j: (0, j))],
        out_specs=pl.BlockSpec((n, tn), lambda j: (0, j)),
        out_shape=jax.ShapeDtypeStruct((n, w_ada.shape[1]), F32),
        compiler_params=_cparams("arbitrary"),
        name="modulation",
    )(c, w_ada, b_ada.reshape(1, -1))


def _inproj_kernel(x_ref, sh_ref, sc_ref, g_ref, b_ref, w_ref, ps_ref, pa_ref, pg_ref):
    xn = _layernorm(x_ref[...], g_ref[...], b_ref[...])
    h = (xn * (1.0 + sc_ref[...]) + sh_ref[...]).astype(BF16)
    ps_ref[...] = jnp.dot(h, w_ref[:, :SHIFT_W], preferred_element_type=F32)
    pa_ref[...] = jnp.dot(h, w_ref[:, SHIFT_W:SHIFT_W + ATTN_W], preferred_element_type=F32)
    pg_ref[...] = jnp.dot(h, w_ref[:, SHIFT_W + ATTN_W:], preferred_element_type=F32)


def _inproj(x, mod, ln_g, ln_b, w_in_bf16, tb):
    g, r, _ = x.shape
    shp = lambda w: jax.ShapeDtypeStruct((g, r, w), F32)
    return pl.pallas_call(
        _inproj_kernel,
        grid=(g, r // tb),
        in_specs=[_rows(tb, D_MODEL), _mod(mod, tb, 0), _mod(mod, tb, 1),
                  _const((1, D_MODEL)), _const((1, D_MODEL)), _const((D_MODEL, D_IN))],
        out_specs=[_rows(tb, SHIFT_W), _rows(tb, ATTN_W), _rows(tb, GATE_W)],
        out_shape=[shp(SHIFT_W), shp(ATTN_W), shp(GATE_W)],
        compiler_params=_cparams("parallel", "parallel"),
        name="inproj",
    )(x, mod, mod, _row2(ln_g), _row2(ln_b), w_in_bf16)


def _softplus(x):
    return jnp.maximum(x, 0.0) + jnp.log1p(jnp.exp(-jnp.abs(x)))


def _rwkv_prep_kernel(ps_ref, prev_ref, mu_ref, w0_ref, w2_ref, a0_ref, a2_ref, g2_ref, kk_w_ref, ka_w_ref,
                      hsum_ref, r_ref, lw_ref, k_ref, v_ref, kk_ref, kka_ref, g_ref):
    ps = ps_ref[...]
    xs = ps + (prev_ref[...] - ps) * mu_ref[...]
    r = xs[:, 0:D_A]
    k = xs[:, D_A:2 * D_A]
    v = xs[:, 2 * D_A:3 * D_A]
    wd = xs[:, OFF_WD:OFF_AD]
    ad = xs[:, OFF_AD:OFF_GD]
    gd = xs[:, OFF_GD:SHIFT_W]
    z = w0_ref[...] + _dot3(jnp.tanh(wd), w2_ref[...])
    w_log = -_softplus(-z) - 0.5
    a = jax.nn.sigmoid(a0_ref[...] + _dot3(ad, a2_ref[...]))
    kk = k * kk_w_ref[...]
    ss = _dot3(kk * kk, hsum_ref[...])
    kk = kk / jnp.maximum(jnp.sqrt(ss), 1e-12)
    r_ref[...] = r
    lw_ref[...] = -jnp.exp(w_log)
    k_ref[...] = k * (1.0 + (a - 1.0) * ka_w_ref[...])
    v_ref[...] = v
    kk_ref[...] = kk
    kka_ref[...] = kk * a
    g_ref[...] = _dot3(jax.nn.sigmoid(gd), g2_ref[...])


def _head_sum_matrix():
    h = jnp.arange(D_A) // HD_A
    return (h[:, None] == h[None, :]).astype(F32)


def _rwkv_prep(ps, prev, p, tb):
    g, r, _ = ps.shape
    shp = jax.ShapeDtypeStruct((g, r, D_A), F32)
    return pl.pallas_call(
        _rwkv_prep_kernel,
        grid=(g, r // tb),
        in_specs=[_rows(tb, SHIFT_W), _rows(tb, SHIFT_W), _const((1, SHIFT_W)),
                  _const((1, D_A)), _const((D_LORA_W, D_A)), _const((1, D_A)), _const((D_LORA_A, D_A)),
                  _const((D_LORA_G, D_A)), _const((1, D_A)), _const((1, D_A)), _const((D_A, D_A))],
        out_specs=[_rows(tb, D_A)] * 7,
        out_shape=[shp] * 7,
        compiler_params=_cparams("parallel", "parallel"),
        name="rwkv_prep",
    )(ps, prev, _row2(p["mu_shift"]), _row2(p["rwkv_w0"]), p["rwkv_w2"], _row2(p["rwkv_a0"]), p["rwkv_a2"],
      p["rwkv_g2"], _row2(p["rwkv_k_k"]), _row2(p["rwkv_k_a"]), _head_sum_matrix())


def _rwkv_chunk_kernel(r_ref, lw_ref, k_ref, v_ref, kk_ref, kka_ref, s0_ref, y_ref, s_ref):
    c = RWKV_CHUNK

    @pl.when(pl.program_id(1) == 0)
    def _():
        s_ref[...] = s0_ref[...]

    row = lax.broadcasted_iota(I32, (c, c), 0)
    col = lax.broadcasted_iota(I32, (c, c), 1)
    tril = row >= col
    stril = row > col
    lw = lw_ref[...]
    cum = _dot_exact_lhs(tril.astype(BF16), lw)
    cum_end = cum[c - 1:c, :]
    g_inv = jnp.exp(-cum)
    g_end = jnp.exp(cum_end - cum)
    a_hat = -kk_ref[...] * jnp.exp(cum - lw)
    b_hat = kka_ref[...] * g_inv
    k_hat = k_ref[...] * g_inv
    r_til = r_ref[...] * jnp.exp(cum)
    b_end = kka_ref[...] * g_end
    k_end = k_ref[...] * g_end
    gam_end = jnp.exp(cum_end)
    v_all = v_ref[...]
    s_all = s_ref[...]
    n_steps = int(math.log2(c))
    heads = range(H_A)
    sl = [slice(h * HD_A, (h + 1) * HD_A) for h in heads]
    vh = [v_all[:, sl[h]] for h in heads]
    ar = [jnp.concatenate([a_hat[:, sl[h]], r_til[:, sl[h]]], axis=0) for h in heads]
    bk = [jnp.concatenate([b_hat[:, sl[h]], k_hat[:, sl[h]]], axis=0) for h in heads]
    x = [_dot3(ar[h], bk[h], _NT) for h in heads]
    ars = [_dot3(ar[h], s_all[h], _NT) for h in heads]
    a_ak = [jnp.where(stril, x[h][:c, c:], 0.0) for h in heads]
    n = [jnp.where(stril, x[h][:c, :c], 0.0) for h in heads]
    u = [ars[h][:c] + _dot3(a_ak[h], vh[h]) for h in heads]
    for it in range(n_steps):
        u = [u[h] + _dot3(n[h], u[h]) for h in heads]
        if it + 1 < n_steps:
            n = [_dot3(n[h], n[h]) for h in heads]
    uv = [jnp.concatenate([u[h], vh[h]], axis=0) for h in heads]
    a_r = [jnp.concatenate([jnp.where(tril, x[h][c:, :c], 0.0), jnp.where(tril, x[h][c:, c:], 0.0)], axis=1)
           for h in heads]
    y = [ars[h][c:] + _dot3(a_r[h], uv[h]) for h in heads]
    bke = [jnp.concatenate([b_end[:, sl[h]], k_end[:, sl[h]]], axis=0) for h in heads]
    s_new = [s_all[h] * gam_end[:, sl[h]] + _dot3(uv[h], bke[h], _TN) for h in heads]
    for h in heads:
        y_ref[:, sl[h]] = y[h]
        s_ref[h] = s_new[h]


def _rwkv_chunk_scan(r, lw, k, v, kk, kka, s0):
    n, t, _ = r.shape
    c = RWKV_CHUNK
    seq = pl.BlockSpec((None, c, D_A), lambda b, i: (b, i, 0))
    st = pl.BlockSpec((None, H_A, HD_A, HD_A), lambda b, i: (b, 0, 0, 0))
    return pl.pallas_call(
        _rwkv_chunk_kernel,
        grid=(n, t // c),
        in_specs=[seq] * 6 + [st],
        out_specs=[seq, st],
        out_shape=[jax.ShapeDtypeStruct((n, t, D_A), F32), jax.ShapeDtypeStruct((n, H_A, HD_A, HD_A), F32)],
        compiler_params=_cparams("parallel", "arbitrary"),
        name="rwkv_chunk_scan",
    )(r, lw, k, v, kk, kka, s0)


def _rwkv_step_kernel(s_ref, r_ref, lw_ref, k_ref, kk_ref, kka_ref, v_ref, y_ref, so_ref):
    s = s_ref[...]
    sa = jnp.sum(s * (-kk_ref[...]), axis=-1, keepdims=True)
    s = s * jnp.exp(lw_ref[...]) + sa * kka_ref[...] + v_ref[...] * k_ref[...]
    so_ref[...] = s
    y_ref[...] = jnp.sum(s * r_ref[...], axis=-1, keepdims=True)


def _rwkv_step(s0, r, lw, k, kk, kka, v, nb):
    n = s0.shape[0]
    key = lambda z: z.reshape(n, H_A, 1, HD_A)
    st = pl.BlockSpec((nb, H_A, HD_A, HD_A), lambda i: (i, 0, 0, 0))
    ks = pl.BlockSpec((nb, H_A, 1, HD_A), lambda i: (i, 0, 0, 0))
    vs = pl.BlockSpec((nb, H_A, HD_A, 1), lambda i: (i, 0, 0, 0))
    y, s = pl.pallas_call(
        _rwkv_step_kernel,
        grid=(n // nb,),
        in_specs=[st, ks, ks, ks, ks, ks, vs],
        out_specs=[vs, st],
        out_shape=[jax.ShapeDtypeStruct((n, H_A, HD_A, 1), F32), jax.ShapeDtypeStruct(s0.shape, F32)],
        compiler_params=_cparams("parallel"),
        name="rwkv_step",
    )(s0, key(r), key(lw), key(k), key(kk), key(kka), v.reshape(n, H_A, HD_A, 1))
    return y.reshape(n, D_A), s


def _rwkv_post_kernel(y_ref, r_ref, k_ref, v_ref, g_ref, gn_g_ref, gn_b_ref, rk_ref, hsum_ref, o_ref):
    y = y_ref[...]
    hs = hsum_ref[...]
    mu = _dot3(y, hs) * (1.0 / HD_A)
    yc = y - mu
    var = _dot3(yc * yc, hs) * (1.0 / HD_A)
    yn = yc * lax.rsqrt(var + GN_EPS) * gn_g_ref[...] + gn_b_ref[...]
    bonus = _dot3(r_ref[...] * k_ref[...] * rk_ref[...], hs) * v_ref[...]
    o_ref[...] = (yn + bonus) * g_ref[...]


def _rwkv_post(y, r, k, v, g, p, tb):
    gg, rr, _ = y.shape
    return pl.pallas_call(
        _rwkv_post_kernel,
        grid=(gg, rr // tb),
        in_specs=[_rows(tb, D_A)] * 5 + [_const((1, D_A))] * 3 + [_const((D_A, D_A))],
        out_specs=_rows(tb, D_A),
        out_shape=jax.ShapeDtypeStruct((gg, rr, D_A), F32),
        compiler_params=_cparams("parallel", "parallel"),
        name="rwkv_post",
    )(y, r, k, v, g, _row2(p["rwkv_gn_g"]), _row2(p["rwkv_gn_b"]), _row2(p["rwkv_r_k"]), _head_sum_matrix())


def _sink_softmax(s, sink):
    m = jnp.maximum(jnp.max(s, axis=-1, keepdims=True), sink)
    p = jnp.exp(s - m)
    den = jnp.sum(p, axis=-1, keepdims=True) + jnp.exp(sink - m)
    return p / den


def _attn_band_kernel(cur_ref, prev_ref, sink_ref, o_ref):
    blk = WINDOW
    i = pl.program_id(1)
    cur = cur_ref[...]
    prev = prev_ref[...]
    qi = lax.broadcasted_iota(I32, (G_Q * blk, 2 * blk), 0) % blk
    kj = lax.broadcasted_iota(I32, (G_Q * blk, 2 * blk), 1)
    rel = blk + qi - kj
    valid = (rel >= 0) & (rel <= WINDOW) & ((kj >= blk) | (i > 0))
    relf = rel.astype(F32)
    gidx = lax.broadcasted_iota(I32, (G_Q * blk, 1), 0) // blk
    for kvh in range(H_KV):
        q4 = jnp.concatenate([cur[:, (kvh * G_Q + g) * HD_B:(kvh * G_Q + g + 1) * HD_B] for g in range(G_Q)], axis=0)
        ko = D_B + kvh * HD_B
        vo = D_B + H_KV * HD_B + kvh * HD_B
        kmat = jnp.concatenate([prev[:, ko:ko + HD_B], cur[:, ko:ko + HD_B]], axis=0)
        vmat = jnp.concatenate([prev[:, vo:vo + HD_B], cur[:, vo:vo + HD_B]], axis=0)
        slope = jnp.zeros((G_Q * blk, 1), F32)
        sink = jnp.zeros((G_Q * blk, 1), F32)
        for g in range(G_Q):
            hq = kvh * G_Q + g
            slope = jnp.where(gidx == g, 2.0 ** (-8.0 * (hq + 1) / H_Q), slope)
            sink = jnp.where(gidx == g, sink_ref[hq], sink)
        s = _dotb(q4, kmat, _NT) * (HD_B ** -0.5)
        s = jnp.where(valid, s - slope * relf, NEG_INF)
        p = _sink_softmax(s, sink)
        o = _dotb(p, vmat)
        for g in range(G_Q):
            hq = kvh * G_Q + g
            o_ref[:, hq * HD_B:(hq + 1) * HD_B] = o[g * blk:(g + 1) * blk]


def _attn_band(pattn, sinks):
    n, t, _ = pattn.shape
    blk = WINDOW
    return pl.pallas_call(
        _attn_band_kernel,
        grid=(n, t // blk),
        in_specs=[pl.BlockSpec((None, blk, ATTN_W), lambda b, i: (b, i, 0)),
                  pl.BlockSpec((None, blk, ATTN_W), lambda b, i: (b, jnp.maximum(i - 1, 0), 0)),
                  pl.BlockSpec(memory_space=pltpu.SMEM)],
        out_specs=pl.BlockSpec((None, blk, D_B), lambda b, i: (b, i, 0)),
        out_shape=jax.ShapeDtypeStruct((n, t, D_B), F32),
        compiler_params=_cparams("parallel", "parallel"),
        name="attn_band",
    )(pattn, pattn, sinks.astype(F32))


def _attn_cache_kernel(cur_ref, kc_ref, vc_ref, sink_ref, o_ref, *, nb):
    relc = (WINDOW - lax.broadcasted_iota(I32, (G_Q, WINDOW), 1)).astype(F32)
    gidx = lax.broadcasted_iota(I32, (G_Q, 1), 0)
    for b in range(nb):
        cur = cur_ref[b]
        for kvh in range(H_KV):
            q4 = jnp.concatenate([cur[:, (kvh * G_Q + g) * HD_B:(kvh * G_Q + g + 1) * HD_B] for g in range(G_Q)], axis=0)
            ko = D_B + kvh * HD_B
            vo = D_B + H_KV * HD_B + kvh * HD_B
            k_new = cur[:, ko:ko + HD_B]
            v_new = cur[:, vo:vo + HD_B]
            kc = kc_ref[b, :, kvh * HD_B:(kvh + 1) * HD_B]
            vc = vc_ref[b, :, kvh * HD_B:(kvh + 1) * HD_B]
            slope = jnp.zeros((G_Q, 1), F32)
            sink = jnp.zeros((G_Q, 1), F32)
            for g in range(G_Q):
                hq = kvh * G_Q + g
                slope = jnp.where(gidx == g, 2.0 ** (-8.0 * (hq + 1) / H_Q), slope)
                sink = jnp.where(gidx == g, sink_ref[hq], sink)
            scale = HD_B ** -0.5
            sc = _dotb(q4, kc, _NT) * scale - slope * relc
            sn = jnp.sum(q4.astype(BF16).astype(F32) * k_new.astype(BF16).astype(F32), axis=-1, keepdims=True) * scale
            m = jnp.maximum(jnp.maximum(jnp.max(sc, axis=-1, keepdims=True), sn), sink)
            pc = jnp.exp(sc - m)
            pn = jnp.exp(sn - m)
            den = jnp.sum(pc, axis=-1, keepdims=True) + pn + jnp.exp(sink - m)
            o = (_dotb(pc / den, vc) + (pn / den).astype(BF16).astype(F32) * v_new.astype(BF16).astype(F32))
            for g in range(G_Q):
                hq = kvh * G_Q + g
                o_ref[b, :, hq * HD_B:(hq + 1) * HD_B] = o[g:g + 1]


def _attn_cache(pattn, k_buf, v_buf, sinks, nb):
    n = pattn.shape[0]
    kc = k_buf.reshape(n, WINDOW, H_KV * HD_B)
    vc = v_buf.reshape(n, WINDOW, H_KV * HD_B)
    return pl.pallas_call(
        functools.partial(_attn_cache_kernel, nb=nb),
        grid=(n // nb,),
        in_specs=[pl.BlockSpec((nb, 1, ATTN_W), lambda i: (i, 0, 0)),
                  pl.BlockSpec((nb, WINDOW, H_KV * HD_B), lambda i: (i, 0, 0)),
                  pl.BlockSpec((nb, WINDOW, H_KV * HD_B), lambda i: (i, 0, 0)),
                  pl.BlockSpec(memory_space=pltpu.SMEM)],
        out_specs=pl.BlockSpec((nb, 1, D_B), lambda i: (i, 0, 0)),
        out_shape=jax.ShapeDtypeStruct((n, 1, D_B), F32),
        compiler_params=_cparams("parallel"),
        name="attn_cache",
    )(pattn, kc, vc, sinks.astype(F32))


def _merge_kernel(x_ref, ya_ref, ob_ref, pg_ref, gtm_ref, shf_ref, scf_ref, lng_ref, lnb_ref, l1g_ref, l1b_ref,
                  wpa_ref, wpb_ref, wo_ref, wq_ref, sk_ref, x1_ref, st_ref, *, alpha):
    ya = jnp.dot(ya_ref[...].astype(BF16), wpa_ref[...], preferred_element_type=F32)
    yb = jnp.dot(ob_ref[...].astype(BF16), wpb_ref[...], preferred_element_type=F32)
    pg = pg_ref[...]
    merged = jax.nn.sigmoid(pg[:, :D_MODEL]) * ya + jax.nn.sigmoid(pg[:, D_MODEL:]) * yb
    mix = jnp.dot(merged.astype(BF16), wo_ref[...], preferred_element_type=F32)
    xn = _layernorm(x_ref[...], lng_ref[...], lnb_ref[...])
    x1 = _layernorm(alpha * xn + gtm_ref[...] * mix, l1g_ref[...], l1b_ref[...])
    x1_ref[...] = x1
    h2 = x1 * (1.0 + scf_ref[...]) + shf_ref[...]
    q = jnp.dot(h2.astype(BF16), wq_ref[...], preferred_element_type=F32)
    for hc in range(2 * PEER_HEADS):
        st_ref[hc] = _dot3(sk_ref[hc % 2], q[:, hc * PEER_HALF:(hc + 1) * PEER_HALF], _NT)


def _merge(x, ya, ob, pg, mod, p, alpha, tb):
    g, r, _ = x.shape
    return pl.pallas_call(
        functools.partial(_merge_kernel, alpha=alpha),
        grid=(g, r // tb),
        in_specs=[_rows(tb, D_MODEL), _rows(tb, D_A), _rows(tb, D_B), _rows(tb, GATE_W),
                  _mod(mod, tb, 2), _mod(mod, tb, 3), _mod(mod, tb, 4)]
                 + [_const((1, D_MODEL))] * 4
                 + [_const((D_A, D_MODEL)), _const((D_B, D_MODEL)), _const((D_MODEL, D_MODEL)),
                    _const((D_MODEL, 2 * PEER_HEADS * PEER_HALF)), _const((2, N_KEYS, PEER_HALF))],
        out_specs=[_rows(tb, D_MODEL),
                   pl.BlockSpec((None, 2 * PEER_HEADS, N_KEYS, tb), lambda gi, i: (gi, 0, 0, i))],
        out_shape=[jax.ShapeDtypeStruct((g, r, D_MODEL), F32),
                   jax.ShapeDtypeStruct((g, 2 * PEER_HEADS, N_KEYS, r), F32)],
        compiler_params=_cparams("parallel", "parallel"),
        name="merge_ln1_peer_scores",
    )(x, ya, ob, pg, mod, mod, mod, _row2(p["ln_in_g"]), _row2(p["ln_in_b"]), _row2(p["ln1_g"]), _row2(p["ln1_b"]),
      p["w_pa"].astype(BF16), p["w_pb"].astype(BF16), p["w_o"].astype(BF16), p["peer_wq"].astype(BF16),
      p["peer_sub_keys"])


def _extract_top(vals, payload, n_rows, tb):
    rio = lax.broadcasted_iota(I32, (n_rows, tb), 0).astype(F32)
    top_v, top_i, top_p = [], [], []
    for _ in range(PEER_TOPK):
        m = jnp.max(vals, axis=0, keepdims=True)
        i = jnp.min(jnp.where(vals == m, rio, float(n_rows)), axis=0, keepdims=True)
        sel = rio == i
        top_v.append(m)
        top_i.append(i)
        if payload is not None:
            top_p.append(jnp.max(jnp.where(sel, payload, -1.0), axis=0, keepdims=True))
        vals = jnp.where(sel, -jnp.inf, vals)
    cat = lambda z: jnp.concatenate(z, axis=0)
    return cat(top_v), cat(top_i), (cat(top_p) if payload is not None else None)


def _pair_candidates(v1, i1, v2, i2, tb):
    k = PEER_TOPK
    sub = 8
    eid = lambda a0, a1, b0, b1: i1[a0:a1] * float(N_KEYS) + i2[b0:b1]
    vals = [v1[0:1] + v2, v1[1:2] + v2[0:sub]]
    ids = [eid(0, 1, 0, k), eid(1, 2, 0, sub)]
    brow = lax.broadcasted_iota(I32, (sub, tb), 0)
    for a in range(2, sub):
        vals.append(jnp.where(brow < k // (a + 1), v1[a:a + 1] + v2[0:sub], -jnp.inf))
        ids.append(eid(a, a + 1, 0, sub))
    vals.append(v1[sub:k] + v2[0:1])
    ids.append(eid(sub, k, 0, 1))
    return jnp.concatenate(vals, axis=0), jnp.concatenate(ids, axis=0)


def _topk_kernel(s_ref, idx_ref, gate_ref):
    tb = s_ref.shape[-1]

    def head(h, carry):
        v1, i1, _ = _extract_top(s_ref[2 * h], None, N_KEYS, tb)
        v2, i2, _ = _extract_top(s_ref[2 * h + 1], None, N_KEYS, tb)
        cand, eid = _pair_candidates(v1, i1, v2, i2, tb)
        sc, _, ex = _extract_top(cand, eid, cand.shape[0], tb)
        pexp = jnp.exp(sc - sc[0:1])
        idx_ref[h] = ex.astype(I32)
        gate_ref[h] = pexp / jnp.sum(pexp, axis=0, keepdims=True)
        return carry

    lax.fori_loop(0, PEER_HEADS, head, 0)


def _topk(scores_t, tb):
    g, _, _, r = scores_t.shape
    out = pl.BlockSpec((None, PEER_HEADS, PEER_TOPK, tb), lambda gi, i: (gi, 0, 0, i))
    return pl.pallas_call(
        _topk_kernel,
        grid=(g, r // tb),
        in_specs=[pl.BlockSpec((None, 2 * PEER_HEADS, N_KEYS, tb), lambda gi, i: (gi, 0, 0, i))],
        out_specs=[out, out],
        out_shape=[jax.ShapeDtypeStruct((g, PEER_HEADS, PEER_TOPK, r), I32),
                   jax.ShapeDtypeStruct((g, PEER_HEADS, PEER_TOPK, r), F32)],
        compiler_params=_cparams("parallel", "parallel"),
        name="peer_topk",
    )(scores_t)


def _sc_gather_pair(tu, tv, idx):
    info = plsc.get_sparse_core_info()
    nc, ns = info.num_cores, info.num_subcores
    nw = nc * ns
    ni = idx.shape[0]
    w = tu.shape[1]
    rr, ki = SC_GATHER_ROWS, SC_IDX_ROWS
    per_w = ni // nw
    n_chunks = per_w // rr
    n_outer = n_chunks // ki
    assert per_w * nw == ni and n_chunks * rr == per_w and n_outer * ki == n_chunks
    mesh = plsc.VectorSubcoreMesh(core_axis_name="c", subcore_axis_name="s")
    buf = pltpu.VMEM((rr, w), tu.dtype)
    out = jax.ShapeDtypeStruct((ni, w), tu.dtype)

    @functools.partial(
        pl.kernel, mesh=mesh, out_type=(out, out),
        scratch_types=[pltpu.VMEM((ki, rr), I32), buf, buf, buf, buf] + [pltpu.SemaphoreType.DMA] * 8,
        name="peer_row_gather",
    )
    def gather(tu_hbm, tv_hbm, idx_hbm, ou_hbm, ov_hbm, idx_v, bu0, bu1, bv0, bv1, *sems):
        wid = lax.axis_index("s") * nc + lax.axis_index("c")
        base = wid * per_w
        bufs = ((bu0, bv0), (bu1, bv1))
        gsem = ((sems[0], sems[1]), (sems[2], sems[3]))
        wsem = ((sems[4], sems[5]), (sems[6], sems[7]))

        @pl.loop(0, n_outer)
        def _(o):
            pltpu.sync_copy(idx_hbm.at[wid, pl.ds(o * ki, ki)], idx_v)

            def start_gather(j):
                b = j % 2
                return (pltpu.async_copy(tu_hbm.at[idx_v.at[j]], bufs[b][0], gsem[b][0]),
                        pltpu.async_copy(tv_hbm.at[idx_v.at[j]], bufs[b][1], gsem[b][1]))

            def start_write(j):
                b = j % 2
                dst = pl.ds(base + (o * ki + j) * rr, rr)
                return (pltpu.async_copy(bufs[b][0], ou_hbm.at[dst], wsem[b][0]),
                        pltpu.async_copy(bufs[b][1], ov_hbm.at[dst], wsem[b][1]))

            gathers = {0: start_gather(0)}
            writes = {}
            for j in range(ki):
                if j + 1 < ki:
                    if j >= 1:
                        for cp in writes.pop(j - 1):
                            cp.wait()
                    gathers[j + 1] = start_gather(j + 1)
                for cp in gathers.pop(j):
                    cp.wait()
                writes[j] = start_write(j)
            for j in sorted(writes):
                for cp in writes[j]:
                    cp.wait()

    return gather(tu, tv, idx.reshape(nw, n_chunks, rr))


def _pack_bf16_pairs(t):
    half = t.shape[1] // 2
    b = lax.bitcast_convert_type(t.astype(BF16), jnp.uint16).astype(U32)
    return b[:, :half] | (b[:, half:] << 16)


def _gelu_erf(x):
    return 0.5 * x * (1.0 + lax.erf(x * (2.0 ** -0.5)))


def _unpack_pairs(words):
    lo = pltpu.bitcast(words << 16, F32)
    hi = pltpu.bitcast(words & jnp.uint32(0xFFFF0000), F32)
    return lo, hi


def _peer_mix_kernel(gu_ref, gv_ref, gate_ref, x1_ref, gtf_ref, shf_ref, scf_ref, l2g_ref, l2b_ref, o_ref, *, tb, alpha):
    half = D_MODEL // 2
    x1 = x1_ref[...]
    h2 = x1 * (1.0 + scf_ref[...]) + shf_ref[...]
    eye = (lax.broadcasted_iota(I32, (PEER_PICKS, PEER_PICKS), 0)
           == lax.broadcasted_iota(I32, (PEER_PICKS, PEER_PICKS), 1)).astype(BF16)
    gate_t = _dot_exact_lhs(eye, gate_ref[...], _NT)
    ff = []
    for t in range(tb):
        rows = slice(t * PEER_PICKS, (t + 1) * PEER_PICKS)
        ulo, uhi = _unpack_pairs(gu_ref[rows, :])
        hd = jnp.sum(ulo * h2[t:t + 1, :half] + uhi * h2[t:t + 1, half:], axis=-1, keepdims=True)
        w = _gelu_erf(hd) * gate_t[:, t:t + 1]
        vlo, vhi = _unpack_pairs(gv_ref[rows, :])
        ff.append(jnp.concatenate([jnp.sum(w * vlo, axis=0, keepdims=True),
                                   jnp.sum(w * vhi, axis=0, keepdims=True)], axis=1))
    ff = jnp.concatenate(ff, axis=0)
    o_ref[...] = _layernorm(alpha * x1 + gtf_ref[...] * ff, l2g_ref[...], l2b_ref[...])


def _peer_mix(gu, gv, gate, x1, mod, p, alpha, tb, g0):
    r = x1.shape[1]
    gspec = pl.BlockSpec((tb * PEER_PICKS, D_MODEL // 2), lambda gi, i: (i, 0))
    rows = lambda w: pl.BlockSpec((None, tb, w), lambda gi, i: (g0, i, 0))
    if mod.shape[1] == 1:
        mods = lambda col: pl.BlockSpec((None, 1, D_MODEL), lambda gi, i: (g0, 0, col))
    else:
        mods = lambda col: pl.BlockSpec((None, tb, D_MODEL), lambda gi, i: (g0, i, col))
    return pl.pallas_call(
        functools.partial(_peer_mix_kernel, tb=tb, alpha=alpha),
        grid=(1, r // tb),
        in_specs=[gspec, gspec, rows(PEER_PICKS), rows(D_MODEL),
                  mods(5), mods(3), mods(4), _const((1, D_MODEL)), _const((1, D_MODEL))],
        out_specs=_rows(tb, D_MODEL),
        out_shape=jax.ShapeDtypeStruct((1, r, D_MODEL), F32),
        compiler_params=_cparams("parallel", "parallel"),
        name="peer_mix_ln2",
    )(gu, gv, gate, x1, mod, mod, mod, _row2(p["ln2_g"]), _row2(p["ln2_b"]))


def _token_stage(x, mod, prev_fn, wkv_fn, attn_fn, p, alpha, tb):
    ps, pattn, pgate = _inproj(x, mod, p["ln_in_g"], p["ln_in_b"], p["w_in_bf16"], tb)
    r, lw, k, v, kk, kka, gl = _rwkv_prep(ps, prev_fn(ps), p, tb)
    y, wkv_new = wkv_fn(r, lw, k, v, kk, kka)
    ya = _rwkv_post(y, r, k, v, gl, p, tb)
    ob = attn_fn(pattn)
    x1, scores_t = _merge(x, ya, ob, pgate, mod, p, alpha, tb)
    idx_t, gate_t = _topk(scores_t, TOPK_TB)
    return ps, pattn, wkv_new, x1, idx_t, gate_t


def kernel(x_prompt, x_sample, state_wkv, state_shift, cache_k_win, cache_v_win, c_prompt, c_sample, ln_in_g, ln_in_b, w_ada, b_ada, w_in, mu_shift, rwkv_w0, rwkv_w2, rwkv_a0, rwkv_a2, rwkv_g2, rwkv_k_k, rwkv_k_a, rwkv_r_k, rwkv_gn_g, rwkv_gn_b, attn_sinks, w_pa, w_pb, w_o, ln1_g, ln1_b, peer_wq, peer_sub_keys, peer_u, peer_v, ln2_g, ln2_b):
    depth = w_in.shape[0]
    assert depth == 1, "single-layer trunk"
    alpha = (2.0 * depth) ** 0.25
    n_p, t_p, _ = x_prompt.shape
    n_s = x_sample.shape[0]
    p = dict(ln_in_g=ln_in_g, ln_in_b=ln_in_b, w_in_bf16=w_in[0].astype(BF16), mu_shift=mu_shift[0],
             rwkv_w0=rwkv_w0[0], rwkv_w2=rwkv_w2[0], rwkv_a0=rwkv_a0[0], rwkv_a2=rwkv_a2[0], rwkv_g2=rwkv_g2[0],
             rwkv_k_k=rwkv_k_k[0], rwkv_k_a=rwkv_k_a[0], rwkv_r_k=rwkv_r_k[0], rwkv_gn_g=rwkv_gn_g[0],
             rwkv_gn_b=rwkv_gn_b[0], w_pa=w_pa[0], w_pb=w_pb[0], w_o=w_o[0], ln1_g=ln1_g[0], ln1_b=ln1_b[0],
             peer_wq=peer_wq[0], peer_sub_keys=peer_sub_keys[0], ln2_g=ln2_g[0], ln2_b=ln2_b[0])
    sinks = attn_sinks[0]

    n_c = n_p + n_s
    pad = (-n_c) % 8
    c_all = jnp.concatenate([c_prompt, c_sample, jnp.zeros((pad, D_MODEL), F32)], axis=0)
    mod_all = _modulation(c_all, w_ada[0], b_ada[0])
    mod_p = mod_all[:n_p].reshape(n_p, 1, N_MOD * D_MODEL)
    mod_s = mod_all[n_p:n_c].reshape(1, n_s, N_MOD * D_MODEL)

    def prev_p(ps):
        return jnp.concatenate([jnp.zeros((1, 1, SHIFT_W), F32), ps[:, :-1]], axis=1)

    def wkv_p(r, lw, k, v, kk, kka):
        return _rwkv_chunk_scan(r, lw, k, v, kk, kka, jnp.zeros((1, H_A, HD_A, HD_A), F32))

    flat = lambda z: jnp.moveaxis(z, 3, 1).reshape(z.shape[0], -1, PEER_PICKS)
    tu, tv = _pack_bf16_pairs(peer_u[0]), _pack_bf16_pairs(peer_v[0])

    def select(x, mod, prev_fn, wkv_fn, attn_fn, tb):
        ps, pattn, wkv_new, x1, idx_t, gate_t = _token_stage(x, mod, prev_fn, wkv_fn, attn_fn, p, alpha, tb)
        rows = _sc_gather_pair(tu, tv, flat(idx_t).reshape(-1))
        return ps, pattn, wkv_new, (rows, flat(gate_t), x1, mod)

    def mix(pending):
        (gu, gv), gate, x1, mod = pending
        return _peer_mix(gu, gv, gate, x1, mod, p, alpha, PEER_TB, 0)

    ps_l, pattn_l, wkv_l, y_l = [], [], [], []
    pending = None
    for b in range(n_p):
        ps_b, pattn_b, wkv_b, nxt = select(x_prompt[b:b + 1], mod_p[b:b + 1], prev_p, wkv_p,
                                           lambda pa: _attn_band(pa, sinks), TOKEN_TB)
        ps_l.append(ps_b)
        pattn_l.append(pattn_b)
        wkv_l.append(wkv_b)
        if pending is not None:
            y_l.append(mix(pending))
        pending = nxt

    xs = x_sample.reshape(1, n_s, D_MODEL)

    def prev_s(ps):
        return state_shift[0].reshape(1, n_s, SHIFT_W)

    def wkv_s(r, lw, k, v, kk, kka):
        sq = lambda z: z.reshape(n_s, D_A)
        y, s = _rwkv_step(state_wkv[0], sq(r), sq(lw), sq(k), sq(kk), sq(kka), sq(v), STEP_NB)
        return y.reshape(1, n_s, D_A), s

    def attn_s(pa):
        o = _attn_cache(pa.reshape(n_s, 1, ATTN_W), cache_k_win[0], cache_v_win[0], sinks, STEP_NB)
        return o.reshape(1, n_s, D_B)

    ps_s, pattn_s, wkv_s_new, nxt = select(xs, mod_s, prev_s, wkv_s, attn_s, min(TOKEN_TB, n_s))
    y_l.append(mix(pending))
    y_s = mix(nxt)
    y_p = jnp.concatenate(y_l, axis=0)
    shift_p = jnp.concatenate([ps_b[:, -1] for ps_b in ps_l], axis=0)
    pattn_p = jnp.concatenate([pa[:, -WINDOW:] for pa in pattn_l], axis=0)
    wkv_p_new = jnp.concatenate(wkv_l, axis=0)

    kv = lambda pa, o: pa[..., o:o + H_KV * HD_B]
    ko, vo = D_B, D_B + H_KV * HD_B
    k_win_p = kv(pattn_p, ko)[:, -WINDOW:].reshape(n_p, WINDOW, H_KV, HD_B)
    v_win_p = kv(pattn_p, vo)[:, -WINDOW:].reshape(n_p, WINDOW, H_KV, HD_B)
    k_new_s = kv(pattn_s, ko).reshape(n_s, 1, H_KV, HD_B)
    v_new_s = kv(pattn_s, vo).reshape(n_s, 1, H_KV, HD_B)
    k_win_s = jnp.concatenate([cache_k_win[0], k_new_s], axis=1)[:, -WINDOW:]
    v_win_s = jnp.concatenate([cache_v_win[0], v_new_s], axis=1)[:, -WINDOW:]
    return (y_p, y_s.reshape(n_s, 1, D_MODEL), wkv_p_new[None], wkv_s_new[None],
            shift_p[None], ps_s.reshape(n_s, SHIFT_W)[None],
            k_win_p[None], k_win_s[None], v_win_p[None], v_win_s[None])
```

```python
import functools
import math

import jax
import jax.numpy as jnp
from jax import lax
from jax.experimental import pallas as pl
from jax.experimental.pallas import tpu as pltpu
from jax.experimental.pallas import tpu_sc as plsc

F32 = jnp.float32
BF16 = jnp.bfloat16
I32 = jnp.int32
U32 = jnp.uint32

D_MODEL = 1024
H_A, HD_A = 8, 64
D_A = H_A * HD_A
D_LORA_W, D_LORA_A, D_LORA_G = 64, 64, 128
GN_EPS = 64e-5
H_Q, H_KV, HD_B = 8, 2, 64
G_Q = H_Q // H_KV
D_B = H_Q * HD_B
WINDOW = 128
N_KEYS = 128
PEER_HEADS, PEER_TOPK, PEER_HALF = 8, 16, 128
PEER_PICKS = PEER_HEADS * PEER_TOPK
N_MOD = 6
LN_EPS = 1e-5
NEG_INF = -1e30
OFF_WD = 3 * D_A
OFF_AD = OFF_WD + D_LORA_W
OFF_GD = OFF_AD + D_LORA_A
SHIFT_W = OFF_GD + D_LORA_G
ATTN_W = D_B + 2 * H_KV * HD_B
GATE_W = 2 * D_MODEL
D_IN = SHIFT_W + ATTN_W + GATE_W

VMEM_LIMIT = 48 * 1024 * 1024
RWKV_CHUNK = 64
SC_GATHER_ROWS = 32
SC_IDX_ROWS = 16
SC_ACC_TOKENS = 16
SC_ACC_PANELS = 4
TOKEN_TB = 256
TOPK_TB = 128
PEER_TB = 16
STEP_NB = 8


def _cparams(*sem):
    return pltpu.CompilerParams(dimension_semantics=sem, vmem_limit_bytes=VMEM_LIMIT)


def _layernorm(x, g, b):
    mu = jnp.mean(x, -1, keepdims=True)
    xc = x - mu
    var = jnp.mean(xc * xc, -1, keepdims=True)
    return xc * lax.rsqrt(var + LN_EPS) * g + b


def _split(x):
    hi = x.astype(BF16)
    lo = (x - hi.astype(F32)).astype(BF16)
    return hi, lo


_NN = (((1,), (0,)), ((), ()))
_NT = (((1,), (1,)), ((), ()))
_TN = (((0,), (0,)), ((), ()))


def _dot3(a, b, dims=_NN):
    ah, al = _split(a)
    bh, bl = _split(b)
    d = functools.partial(lax.dot_general, dimension_numbers=dims, preferred_element_type=F32)
    return d(ah, bh) + d(ah, bl) + d(al, bh)


def _dot_exact_lhs(a_bf16, b, dims=_NN):
    b1 = b.astype(BF16)
    r1 = b - b1.astype(F32)
    b2 = r1.astype(BF16)
    b3 = (r1 - b2.astype(F32)).astype(BF16)
    d = functools.partial(lax.dot_general, dimension_numbers=dims, preferred_element_type=F32)
    return d(a_bf16, b1) + d(a_bf16, b2) + d(a_bf16, b3)


def _dotb(a, b, dims=_NN):
    return lax.dot_general(a.astype(BF16), b.astype(BF16), dims, preferred_element_type=F32)


def _rows(tb, width, col=0):
    return pl.BlockSpec((None, tb, width), lambda g, i: (g, i, col))


def _mod(mod, tb, col):
    if mod.shape[1] == 1:
        return pl.BlockSpec((None, 1, D_MODEL), lambda g, i: (g, 0, col))
    return pl.BlockSpec((None, tb, D_MODEL), lambda g, i: (g, i, col))


def _const(shape):
    n = len(shape)
    return pl.BlockSpec(shape, lambda g, i: (0,) * n)


def _row2(p):
    return p.reshape(1, -1).astype(F32)


def _mod_kernel(c_ref, w_ref, b_ref, o_ref):
    c = c_ref[...]
    a = c * jax.nn.sigmoid(c)
    o_ref[...] = _dot3(a, w_ref[...]) + b_ref[...]


def _modulation(c, w_ada, b_ada):
    n = c.shape[0]
    tn = D_MODEL
    return pl.pallas_call(
        _mod_kernel,
        grid=(w_ada.shape[1] // tn,),
        in_specs=[pl.BlockSpec((n, D_MODEL), lambda j: (0, 0)),
                  pl.BlockSpec((D_MODEL, tn), lambda j: (0, j)),
                  pl.BlockSpec((1, tn), lambda j: (0, j))],
        out_specs=pl.BlockSpec((n, tn), lambda j: (0, j)),
        out_shape=jax.ShapeDtypeStruct((n, w_ada.shape[1]), F32),
        compiler_params=_cparams("arbitrary"),
        name="modulation",
    )(c, w_ada, b_ada.reshape(1, -1))


def _inproj_kernel(x_ref, sh_ref, sc_ref, g_ref, b_ref, w_ref, ps_ref, pa_ref, pg_ref):
    xn = _layernorm(x_ref[...], g_ref[...], b_ref[...])
    h = (xn * (1.0 + sc_ref[...]) + sh_ref[...]).astype(BF16)
    ps_ref[...] = jnp.dot(h, w_ref[:, :SHIFT_W], preferred_element_type=F32)
    pa_ref[...] = jnp.dot(h, w_ref[:, SHIFT_W:SHIFT_W + ATTN_W], preferred_element_type=F32)
    pg_ref[...] = jnp.dot(h, w_ref[:, SHIFT_W + ATTN_W:], preferred_element_type=F32)


def _inproj(x, mod, ln_g, ln_b, w_in_bf16, tb):
    g, r, _ = x.shape
    shp = lambda w: jax.ShapeDtypeStruct((g, r, w), F32)
    return pl.pallas_call(
        _inproj_kernel,
        grid=(g, r // tb),
        in_specs=[_rows(tb, D_MODEL), _mod(mod, tb, 0), _mod(mod, tb, 1),
                  _const((1, D_MODEL)), _const((1, D_MODEL)), _const((D_MODEL, D_IN))],
        out_specs=[_rows(tb, SHIFT_W), _rows(tb, ATTN_W), _rows(tb, GATE_W)],
        out_shape=[shp(SHIFT_W), shp(ATTN_W), shp(GATE_W)],
        compiler_params=_cparams("parallel", "parallel"),
        name="inproj",
    )(x, mod, mod, _row2(ln_g), _row2(ln_b), w_in_bf16)


def _softplus(x):
    return jnp.maximum(x, 0.0) + jnp.log1p(jnp.exp(-jnp.abs(x)))


def _rwkv_prep_kernel(ps_ref, prev_ref, mu_ref, w0_ref, w2_ref, a0_ref, a2_ref, g2_ref, kk_w_ref, ka_w_ref,
                      hsum_ref, r_ref, lw_ref, k_ref, v_ref, kk_ref, kka_ref, g_ref):
    ps = ps_ref[...]
    xs = ps + (prev_ref[...] - ps) * mu_ref[...]
    r = xs[:, 0:D_A]
    k = xs[:, D_A:2 * D_A]
    v = xs[:, 2 * D_A:3 * D_A]
    wd = xs[:, OFF_WD:OFF_AD]
    ad = xs[:, OFF_AD:OFF_GD]
    gd = xs[:, OFF_GD:SHIFT_W]
    z = w0_ref[...] + _dot3(jnp.tanh(wd), w2_ref[...])
    w_log = -_softplus(-z) - 0.5
    a = jax.nn.sigmoid(a0_ref[...] + _dot3(ad, a2_ref[...]))
    kk = k * kk_w_ref[...]
    ss = _dot3(kk * kk, hsum_ref[...])
    kk = kk / jnp.maximum(jnp.sqrt(ss), 1e-12)
    r_ref[...] = r
    lw_ref[...] = -jnp.exp(w_log)
    k_ref[...] = k * (1.0 + (a - 1.0) * ka_w_ref[...])
    v_ref[...] = v
    kk_ref[...] = kk
    kka_ref[...] = kk * a
    g_ref[...] = _dot3(jax.nn.sigmoid(gd), g2_ref[...])


def _head_sum_matrix():
    h = jnp.arange(D_A) // HD_A
    return (h[:, None] == h[None, :]).astype(F32)


def _rwkv_prep(ps, prev, p, tb):
    g, r, _ = ps.shape
    shp = jax.ShapeDtypeStruct((g, r, D_A), F32)
    return pl.pallas_call(
        _rwkv_prep_kernel,
        grid=(g, r // tb),
        in_specs=[_rows(tb, SHIFT_W), _rows(tb, SHIFT_W), _const((1, SHIFT_W)),
                  _const((1, D_A)), _const((D_LORA_W, D_A)), _const((1, D_A)), _const((D_LORA_A, D_A)),
                  _const((D_LORA_G, D_A)), _const((1, D_A)), _const((1, D_A)), _const((D_A, D_A))],
        out_specs=[_rows(tb, D_A)] * 7,
        out_shape=[shp] * 7,
        compiler_params=_cparams("parallel", "parallel"),
        name="rwkv_prep",
    )(ps, prev, _row2(p["mu_shift"]), _row2(p["rwkv_w0"]), p["rwkv_w2"], _row2(p["rwkv_a0"]), p["rwkv_a2"],
      p["rwkv_g2"], _row2(p["rwkv_k_k"]), _row2(p["rwkv_k_a"]), _head_sum_matrix())


def _rwkv_chunk_kernel(r_ref, lw_ref, k_ref, v_ref, kk_ref, kka_ref, s0_ref, y_ref, s_ref):
    c = RWKV_CHUNK

    @pl.when(pl.program_id(1) == 0)
    def _():
        s_ref[...] = s0_ref[...]

    row = lax.broadcasted_iota(I32, (c, c), 0)
    col = lax.broadcasted_iota(I32, (c, c), 1)
    tril = row >= col
    stril = row > col
    lw = lw_ref[...]
    cum = _dot_exact_lhs(tril.astype(BF16), lw)
    cum_end = cum[c - 1:c, :]
    g_inv = jnp.exp(-cum)
    g_end = jnp.exp(cum_end - cum)
    a_hat = -kk_ref[...] * jnp.exp(cum - lw)
    b_hat = kka_ref[...] * g_inv
    k_hat = k_ref[...] * g_inv
    r_til = r_ref[...] * jnp.exp(cum)
    b_end = kka_ref[...] * g_end
    k_end = k_ref[...] * g_end
    gam_end = jnp.exp(cum_end)
    v_all = v_ref[...]
    s_all = s_ref[...]
    n_steps = int(math.log2(c))
    heads = range(H_A)
    sl = [slice(h * HD_A, (h + 1) * HD_A) for h in heads]
    vh = [v_all[:, sl[h]] for h in heads]
    ar = [jnp.concatenate([a_hat[:, sl[h]], r_til[:, sl[h]]], axis=0) for h in heads]
    bk = [jnp.concatenate([b_hat[:, sl[h]], k_hat[:, sl[h]]], axis=0) for h in heads]
    x = [_dot3(ar[h], bk[h], _NT) for h in heads]
    ars = [_dot3(ar[h], s_all[h], _NT) for h in heads]
    a_ak = [jnp.where(stril, x[h][:c, c:], 0.0) for h in heads]
    n = [jnp.where(stril, x[h][:c, :c], 0.0) for h in heads]
    u = [ars[h][:c] + _dot3(a_ak[h], vh[h]) for h in heads]
    for it in range(n_steps):
        u = [u[h] + _dot3(n[h], u[h]) for h in heads]
        if it + 1 < n_steps:
            n = [_dot3(n[h], n[h]) for h in heads]
    uv = [jnp.concatenate([u[h], vh[h]], axis=0) for h in heads]
    a_r = [jnp.concatenate([jnp.where(tril, x[h][c:, :c], 0.0), jnp.where(tril, x[h][c:, c:], 0.0)], axis=1)
           for h in heads]
    y = [ars[h][c:] + _dot3(a_r[h], uv[h]) for h in heads]
    bke = [jnp.concatenate([b_end[:, sl[h]], k_end[:, sl[h]]], axis=0) for h in heads]
    s_new = [s_all[h] * gam_end[:, sl[h]] + _dot3(uv[h], bke[h], _TN) for h in heads]
    for h in heads:
        y_ref[:, sl[h]] = y[h]
        s_ref[h] = s_new[h]


def _rwkv_chunk_scan(r, lw, k, v, kk, kka, s0):
    n, t, _ = r.shape
    c = RWKV_CHUNK
    seq = pl.BlockSpec((None, c, D_A), lambda b, i: (b, i, 0))
    st = pl.BlockSpec((None, H_A, HD_A, HD_A), lambda b, i: (b, 0, 0, 0))
    return pl.pallas_call(
        _rwkv_chunk_kernel,
        grid=(n, t // c),
        in_specs=[seq] * 6 + [st],
        out_specs=[seq, st],
        out_shape=[jax.ShapeDtypeStruct((n, t, D_A), F32), jax.ShapeDtypeStruct((n, H_A, HD_A, HD_A), F32)],
        compiler_params=_cparams("parallel", "arbitrary"),
        name="rwkv_chunk_scan",
    )(r, lw, k, v, kk, kka, s0)


def _rwkv_step_kernel(s_ref, r_ref, lw_ref, k_ref, kk_ref, kka_ref, v_ref, y_ref, so_ref):
    s = s_ref[...]
    sa = jnp.sum(s * (-kk_ref[...]), axis=-1, keepdims=True)
    s = s * jnp.exp(lw_ref[...]) + sa * kka_ref[...] + v_ref[...] * k_ref[...]
    so_ref[...] = s
    y_ref[...] = jnp.sum(s * r_ref[...], axis=-1, keepdims=True)


def _rwkv_step(s0, r, lw, k, kk, kka, v, nb):
    n = s0.shape[0]
    key = lambda z: z.reshape(n, H_A, 1, HD_A)
    st = pl.BlockSpec((nb, H_A, HD_A, HD_A), lambda i: (i, 0, 0, 0))
    ks = pl.BlockSpec((nb, H_A, 1, HD_A), lambda i: (i, 0, 0, 0))
    vs = pl.BlockSpec((nb, H_A, HD_A, 1), lambda i: (i, 0, 0, 0))
    y, s = pl.pallas_call(
        _rwkv_step_kernel,
        grid=(n // nb,),
        in_specs=[st, ks, ks, ks, ks, ks, vs],
        out_specs=[vs, st],
        out_shape=[jax.ShapeDtypeStruct((n, H_A, HD_A, 1), F32), jax.ShapeDtypeStruct(s0.shape, F32)],
        compiler_params=_cparams("parallel"),
        name="rwkv_step",
    )(s0, key(r), key(lw), key(k), key(kk), key(kka), v.reshape(n, H_A, HD_A, 1))
    return y.reshape(n, D_A), s


def _rwkv_post_kernel(y_ref, r_ref, k_ref, v_ref, g_ref, gn_g_ref, gn_b_ref, rk_ref, hsum_ref, o_ref):
    y = y_ref[...]
    hs = hsum_ref[...]
    mu = _dot3(y, hs) * (1.0 / HD_A)
    yc = y - mu
    var = _dot3(yc * yc, hs) * (1.0 / HD_A)
    yn = yc * lax.rsqrt(var + GN_EPS) * gn_g_ref[...] + gn_b_ref[...]
    bonus = _dot3(r_ref[...] * k_ref[...] * rk_ref[...], hs) * v_ref[...]
    o_ref[...] = (yn + bonus) * g_ref[...]


def _rwkv_post(y, r, k, v, g, p, tb):
    gg, rr, _ = y.shape
    return pl.pallas_call(
        _rwkv_post_kernel,
        grid=(gg, rr // tb),
        in_specs=[_rows(tb, D_A)] * 5 + [_const((1, D_A))] * 3 + [_const((D_A, D_A))],
        out_specs=_rows(tb, D_A),
        out_shape=jax.ShapeDtypeStruct((gg, rr, D_A), F32),
        compiler_params=_cparams("parallel", "parallel"),
        name="rwkv_post",
    )(y, r, k, v, g, _row2(p["rwkv_gn_g"]), _row2(p["rwkv_gn_b"]), _row2(p["rwkv_r_k"]), _head_sum_matrix())


def _sink_softmax(s, sink):
    m = jnp.maximum(jnp.max(s, axis=-1, keepdims=True), sink)
    p = jnp.exp(s - m)
    den = jnp.sum(p, axis=-1, keepdims=True) + jnp.exp(sink - m)
    return p / den


def _attn_band_kernel(cur_ref, prev_ref, sink_ref, o_ref):
    blk = WINDOW
    i = pl.program_id(1)
    cur = cur_ref[...]
    prev = prev_ref[...]
    qi = lax.broadcasted_iota(I32, (G_Q * blk, 2 * blk), 0) % blk
    kj = lax.broadcasted_iota(I32, (G_Q * blk, 2 * blk), 1)
    rel = blk + qi - kj
    valid = (rel >= 0) & (rel <= WINDOW) & ((kj >= blk) | (i > 0))
    relf = rel.astype(F32)
    gidx = lax.broadcasted_iota(I32, (G_Q * blk, 1), 0) // blk
    for kvh in range(H_KV):
        q4 = jnp.concatenate([cur[:, (kvh * G_Q + g) * HD_B:(kvh * G_Q + g + 1) * HD_B] for g in range(G_Q)], axis=0)
        ko = D_B + kvh * HD_B
        vo = D_B + H_KV * HD_B + kvh * HD_B
        kmat = jnp.concatenate([prev[:, ko:ko + HD_B], cur[:, ko:ko + HD_B]], axis=0)
        vmat = jnp.concatenate([prev[:, vo:vo + HD_B], cur[:, vo:vo + HD_B]], axis=0)
        slope = jnp.zeros((G_Q * blk, 1), F32)
        sink = jnp.zeros((G_Q * blk, 1), F32)
        for g in range(G_Q):
            hq = kvh * G_Q + g
            slope = jnp.where(gidx == g, 2.0 ** (-8.0 * (hq + 1) / H_Q), slope)
            sink = jnp.where(gidx == g, sink_ref[hq], sink)
        s = _dotb(q4, kmat, _NT) * (HD_B ** -0.5)
        s = jnp.where(valid, s - slope * relf, NEG_INF)
        p = _sink_softmax(s, sink)
        o = _dotb(p, vmat)
        for g in range(G_Q):
            hq = kvh * G_Q + g
            o_ref[:, hq * HD_B:(hq + 1) * HD_B] = o[g * blk:(g + 1) * blk]


def _attn_band(pattn, sinks):
    n, t, _ = pattn.shape
    blk = WINDOW
    return pl.pallas_call(
        _attn_band_kernel,
        grid=(n, t // blk),
        in_specs=[pl.BlockSpec((None, blk, ATTN_W), lambda b, i: (b, i, 0)),
                  pl.BlockSpec((None, blk, ATTN_W), lambda b, i: (b, jnp.maximum(i - 1, 0), 0)),
                  pl.BlockSpec(memory_space=pltpu.SMEM)],
        out_specs=pl.BlockSpec((None, blk, D_B), lambda b, i: (b, i, 0)),
        out_shape=jax.ShapeDtypeStruct((n, t, D_B), F32),
        compiler_params=_cparams("parallel", "parallel"),
        name="attn_band",
    )(pattn, pattn, sinks.astype(F32))


def _attn_cache_kernel(cur_ref, kc_ref, vc_ref, sink_ref, o_ref, *, nb):
    relc = (WINDOW - lax.broadcasted_iota(I32, (G_Q, WINDOW), 1)).astype(F32)
    gidx = lax.broadcasted_iota(I32, (G_Q, 1), 0)
    for b in range(nb):
        cur = cur_ref[b]
        for kvh in range(H_KV):
            q4 = jnp.concatenate([cur[:, (kvh * G_Q + g) * HD_B:(kvh * G_Q + g + 1) * HD_B] for g in range(G_Q)], axis=0)
            ko = D_B + kvh * HD_B
            vo = D_B + H_KV * HD_B + kvh * HD_B
            k_new = cur[:, ko:ko + HD_B]
            v_new = cur[:, vo:vo + HD_B]
            kc = kc_ref[b, :, kvh * HD_B:(kvh + 1) * HD_B]
            vc = vc_ref[b, :, kvh * HD_B:(kvh + 1) * HD_B]
            slope = jnp.zeros((G_Q, 1), F32)
            sink = jnp.zeros((G_Q, 1), F32)
            for g in range(G_Q):
                hq = kvh * G_Q + g
                slope = jnp.where(gidx == g, 2.0 ** (-8.0 * (hq + 1) / H_Q), slope)
                sink = jnp.where(gidx == g, sink_ref[hq], sink)
            scale = HD_B ** -0.5
            sc = _dotb(q4, kc, _NT) * scale - slope * relc
            sn = jnp.sum(q4.astype(BF16).astype(F32) * k_new.astype(BF16).astype(F32), axis=-1, keepdims=True) * scale
            m = jnp.maximum(jnp.maximum(jnp.max(sc, axis=-1, keepdims=True), sn), sink)
            pc = jnp.exp(sc - m)
            pn = jnp.exp(sn - m)
            den = jnp.sum(pc, axis=-1, keepdims=True) + pn + jnp.exp(sink - m)
            o = (_dotb(pc / den, vc) + (pn / den).astype(BF16).astype(F32) * v_new.astype(BF16).astype(F32))
            for g in range(G_Q):
                hq = kvh * G_Q + g
                o_ref[b, :, hq * HD_B:(hq + 1) * HD_B] = o[g:g + 1]


def _attn_cache(pattn, k_buf, v_buf, sinks, nb):
    n = pattn.shape[0]
    kc = k_buf.reshape(n, WINDOW, H_KV * HD_B)
    vc = v_buf.reshape(n, WINDOW, H_KV * HD_B)
    return pl.pallas_call(
        functools.partial(_attn_cache_kernel, nb=nb),
        grid=(n // nb,),
        in_specs=[pl.BlockSpec((nb, 1, ATTN_W), lambda i: (i, 0, 0)),
                  pl.BlockSpec((nb, WINDOW, H_KV * HD_B), lambda i: (i, 0, 0)),
                  pl.BlockSpec((nb, WINDOW, H_KV * HD_B), lambda i: (i, 0, 0)),
                  pl.BlockSpec(memory_space=pltpu.SMEM)],
        out_specs=pl.BlockSpec((nb, 1, D_B), lambda i: (i, 0, 0)),
        out_shape=jax.ShapeDtypeStruct((n, 1, D_B), F32),
        compiler_params=_cparams("parallel"),
        name="attn_cache",
    )(pattn, kc, vc, sinks.astype(F32))


def _merge_kernel(x_ref, ya_ref, ob_ref, pg_ref, gtm_ref, shf_ref, scf_ref, lng_ref, lnb_ref, l1g_ref, l1b_ref,
                  wpa_ref, wpb_ref, wo_ref, wq_ref, sk_ref, x1_ref, st_ref, *, alpha):
    ya = jnp.dot(ya_ref[...].astype(BF16), wpa_ref[...], preferred_element_type=F32)
    yb = jnp.dot(ob_ref[...].astype(BF16), wpb_ref[...], preferred_element_type=F32)
    pg = pg_ref[...]
    merged = jax.nn.sigmoid(pg[:, :D_MODEL]) * ya + jax.nn.sigmoid(pg[:, D_MODEL:]) * yb
    mix = jnp.dot(merged.astype(BF16), wo_ref[...], preferred_element_type=F32)
    xn = _layernorm(x_ref[...], lng_ref[...], lnb_ref[...])
    x1 = _layernorm(alpha * xn + gtm_ref[...] * mix, l1g_ref[...], l1b_ref[...])
    x1_ref[...] = x1
    h2 = x1 * (1.0 + scf_ref[...]) + shf_ref[...]
    q = jnp.dot(h2.astype(BF16), wq_ref[...], preferred_element_type=F32)
    for hc in range(2 * PEER_HEADS):
        st_ref[hc] = _dot3(sk_ref[hc % 2], q[:, hc * PEER_HALF:(hc + 1) * PEER_HALF], _NT)


def _merge(x, ya, ob, pg, mod, p, alpha, tb):
    g, r, _ = x.shape
    return pl.pallas_call(
        functools.partial(_merge_kernel, alpha=alpha),
        grid=(g, r // tb),
        in_specs=[_rows(tb, D_MODEL), _rows(tb, D_A), _rows(tb, D_B), _rows(tb, GATE_W),
                  _mod(mod, tb, 2), _mod(mod, tb, 3), _mod(mod, tb, 4)]
                 + [_const((1, D_MODEL))] * 4
                 + [_const((D_A, D_MODEL)), _const((D_B, D_MODEL)), _const((D_MODEL, D_MODEL)),
                    _const((D_MODEL, 2 * PEER_HEADS * PEER_HALF)), _const((2, N_KEYS, PEER_HALF))],
        out_specs=[_rows(tb, D_MODEL),
                   pl.BlockSpec((None, 2 * PEER_HEADS, N_KEYS, tb), lambda gi, i: (gi, 0, 0, i))],
        out_shape=[jax.ShapeDtypeStruct((g, r, D_MODEL), F32),
                   jax.ShapeDtypeStruct((g, 2 * PEER_HEADS, N_KEYS, r), F32)],
        compiler_params=_cparams("parallel", "parallel"),
        name="merge_ln1_peer_scores",
    )(x, ya, ob, pg, mod, mod, mod, _row2(p["ln_in_g"]), _row2(p["ln_in_b"]), _row2(p["ln1_g"]), _row2(p["ln1_b"]),
      p["w_pa"].astype(BF16), p["w_pb"].astype(BF16), p["w_o"].astype(BF16), p["peer_wq"].astype(BF16),
      p["peer_sub_keys"])


def _extract_top(vals, payload, n_rows, tb):
    rio = lax.broadcasted_iota(I32, (n_rows, tb), 0).astype(F32)
    top_v, top_i, top_p = [], [], []
    for _ in range(PEER_TOPK):
        m = jnp.max(vals, axis=0, keepdims=True)
        i = jnp.min(jnp.where(vals == m, rio, float(n_rows)), axis=0, keepdims=True)
        sel = rio == i
        top_v.append(m)
        top_i.append(i)
        if payload is not None:
            top_p.append(jnp.max(jnp.where(sel, payload, -1.0), axis=0, keepdims=True))
        vals = jnp.where(sel, -jnp.inf, vals)
    cat = lambda z: jnp.concatenate(z, axis=0)
    return cat(top_v), cat(top_i), (cat(top_p) if payload is not None else None)


def _pair_candidates(v1, i1, v2, i2, tb):
    k = PEER_TOPK
    sub = 8
    eid = lambda a0, a1, b0, b1: i1[a0:a1] * float(N_KEYS) + i2[b0:b1]
    vals = [v1[0:1] + v2, v1[1:2] + v2[0:sub]]
    ids = [eid(0, 1, 0, k), eid(1, 2, 0, sub)]
    brow = lax.broadcasted_iota(I32, (sub, tb), 0)
    for a in range(2, sub):
        vals.append(jnp.where(brow < k // (a + 1), v1[a:a + 1] + v2[0:sub], -jnp.inf))
        ids.append(eid(a, a + 1, 0, sub))
    vals.append(v1[sub:k] + v2[0:1])
    ids.append(eid(sub, k, 0, 1))
    return jnp.concatenate(vals, axis=0), jnp.concatenate(ids, axis=0)


def _topk_kernel(s_ref, idx_ref, gate_ref):
    tb = s_ref.shape[-1]

    def head(h, carry):
        v1, i1, _ = _extract_top(s_ref[2 * h], None, N_KEYS, tb)
        v2, i2, _ = _extract_top(s_ref[2 * h + 1], None, N_KEYS, tb)
        cand, eid = _pair_candidates(v1, i1, v2, i2, tb)
        sc, _, ex = _extract_top(cand, eid, cand.shape[0], tb)
        pexp = jnp.exp(sc - sc[0:1])
        idx_ref[h] = ex.astype(I32)
        gate_ref[h] = pexp / jnp.sum(pexp, axis=0, keepdims=True)
        return carry

    lax.fori_loop(0, PEER_HEADS, head, 0)


def _topk(scores_t, tb):
    g, _, _, r = scores_t.shape
    out = pl.BlockSpec((None, PEER_HEADS, PEER_TOPK, tb), lambda gi, i: (gi, 0, 0, i))
    return pl.pallas_call(
        _topk_kernel,
        grid=(g, r // tb),
        in_specs=[pl.BlockSpec((None, 2 * PEER_HEADS, N_KEYS, tb), lambda gi, i: (gi, 0, 0, i))],
        out_specs=[out, out],
        out_shape=[jax.ShapeDtypeStruct((g, PEER_HEADS, PEER_TOPK, r), I32),
                   jax.ShapeDtypeStruct((g, PEER_HEADS, PEER_TOPK, r), F32)],
        compiler_params=_cparams("parallel", "parallel"),
        name="peer_topk",
    )(scores_t)


def _sc_mesh():
    info = plsc.get_sparse_core_info()
    mesh = plsc.VectorSubcoreMesh(core_axis_name="c", subcore_axis_name="s")
    return info.num_cores, info.num_subcores, info.num_lanes, mesh


def _sc_gather_rows(table, idx):
    nc, ns, _, mesh = _sc_mesh()
    nw = nc * ns
    ni = idx.shape[0]
    w = table.shape[1]
    rr = SC_GATHER_ROWS
    per_w = ni // nw
    n_chunks = per_w // rr
    ki = min(SC_IDX_ROWS, n_chunks)
    n_outer = n_chunks // ki
    assert per_w * nw == ni and n_chunks * rr == per_w and n_outer * ki == n_chunks
    buf = pltpu.VMEM((rr, w), table.dtype)

    @functools.partial(
        pl.kernel, mesh=mesh, out_type=jax.ShapeDtypeStruct((ni, w), table.dtype),
        scratch_types=[pltpu.VMEM((ki, rr), I32), buf, buf] + [pltpu.SemaphoreType.DMA] * 4,
        name="peer_row_gather",
    )
    def gather(tab_hbm, idx_hbm, out_hbm, idx_v, buf0, buf1, g0, g1, w0, w1):
        wid = lax.axis_index("s") * nc + lax.axis_index("c")
        base = wid * per_w
        bufs, gsem, wsem = (buf0, buf1), (g0, g1), (w0, w1)

        @pl.loop(0, n_outer)
        def _(o):
            pltpu.sync_copy(idx_hbm.at[wid, pl.ds(o * ki, ki)], idx_v)

            def start_gather(j):
                return pltpu.async_copy(tab_hbm.at[idx_v.at[j]], bufs[j % 2], gsem[j % 2])

            def start_write(j):
                dst = pl.ds(base + (o * ki + j) * rr, rr)
                return pltpu.async_copy(bufs[j % 2], out_hbm.at[dst], wsem[j % 2])

            gathers = {0: start_gather(0)}
            writes = {}
            for j in range(ki):
                if j + 1 < ki:
                    if j >= 1:
                        writes.pop(j - 1).wait()
                    gathers[j + 1] = start_gather(j + 1)
                gathers.pop(j).wait()
                writes[j] = start_write(j)
            for j in sorted(writes):
                writes[j].wait()

    return gather(table, idx.reshape(nw, n_chunks, rr))


def _sc_weighted_row_sum(table, idx, wgt):
    nc, ns, lanes, mesh = _sc_mesh()
    nw = nc * ns
    n_tok, picks = idx.shape
    ww = table.shape[1]
    rr = SC_GATHER_ROWS
    nq = picks // rr
    tpw = n_tok // nw
    kt = min(SC_ACC_TOKENS, tpw)
    n_outer = tpw // kt
    pw = ww // SC_ACC_PANELS
    nv = pw // lanes
    assert tpw * nw == n_tok and n_outer * kt == tpw and nq * rr == picks and nq % 2 == 0 and nv * lanes == pw
    cp = pltpu.CompilerParams(needs_layout_passes=False)

    @functools.partial(
        pl.kernel, mesh=mesh, out_type=jax.ShapeDtypeStruct((n_tok, 2 * ww), F32),
        scratch_types=[pltpu.VMEM((kt * nq, rr), I32), pltpu.VMEM((kt, picks), F32),
                       pltpu.VMEM((rr, ww), table.dtype), pltpu.VMEM((rr, ww), table.dtype),
                       pltpu.VMEM((kt, 2 * ww), F32), pltpu.SemaphoreType.DMA, pltpu.SemaphoreType.DMA],
        compiler_params=cp, name="peer_weighted_row_sum",
    )
    def kern(tab_hbm, idx_hbm, w_hbm, out_hbm, idx_v, w_v, rows0, rows1, acc_v, sem0, sem1):
        wid = lax.axis_index("s") * nc + lax.axis_index("c")
        rows, sems = (rows0, rows1), (sem0, sem1)

        def chunk_copy(chunk, b):
            return pltpu.make_async_copy(tab_hbm.at[idx_v.at[chunk]], rows[b], sems[b])

        @pl.loop(0, n_outer)
        def _(o):
            tok0 = wid * tpw + o * kt
            pltpu.sync_copy(idx_hbm.at[pl.ds(tok0 * nq, kt * nq)], idx_v)
            pltpu.sync_copy(w_hbm.at[pl.ds(tok0, kt)], w_v)
            chunk_copy(0, 0).start()

            @pl.loop(0, kt)
            def _(t):
                tvec = jnp.full((lanes,), t, I32)
                for q in range(nq):
                    b = q % 2
                    if q + 1 < nq:
                        chunk_copy(t * nq + q + 1, 1 - b).start()
                    else:
                        @pl.when(t + 1 < kt)
                        def _():
                            chunk_copy((t + 1) * nq, 1 - b).start()
                    chunk_copy(t * nq + q, b).wait()
                    for pan in range(SC_ACC_PANELS):
                        lo_at = lambda c: pl.ds(pan * pw + c * lanes, lanes)
                        hi_at = lambda c: pl.ds(ww + pan * pw + c * lanes, lanes)

                        def row_body(j, acc):
                            wj = plsc.load_gather(w_v, [tvec, jnp.full((lanes,), q * rr, I32) + j])
                            new = []
                            for c in range(nv):
                                word = rows[b][j, lo_at(c)]
                                new.append(acc[2 * c] + wj * plsc.bitcast(word << 16, F32))
                                new.append(acc[2 * c + 1] + wj * plsc.bitcast(word & jnp.uint32(0xFFFF0000), F32))
                            return tuple(new)

                        if q == 0:
                            init = tuple(jnp.zeros((lanes,), F32) for _ in range(2 * nv))
                        else:
                            init = tuple(acc_v[t, at(c)] for c in range(nv) for at in (lo_at, hi_at))
                        acc = lax.fori_loop(0, rr, row_body, init)
                        for c in range(nv):
                            acc_v[t, lo_at(c)] = acc[2 * c]
                            acc_v[t, hi_at(c)] = acc[2 * c + 1]

            pltpu.sync_copy(acc_v, out_hbm.at[pl.ds(tok0, kt)])

    return kern(table, idx.reshape(n_tok * nq, rr), wgt)


def _pack_bf16_pairs(t):
    half = t.shape[1] // 2
    b = lax.bitcast_convert_type(t.astype(BF16), jnp.uint16).astype(U32)
    return b[:, :half] | (b[:, half:] << 16)


def _gelu_erf(x):
    return 0.5 * x * (1.0 + lax.erf(x * (2.0 ** -0.5)))


def _unpack_pairs(words):
    lo = pltpu.bitcast(words << 16, F32)
    hi = pltpu.bitcast(words & jnp.uint32(0xFFFF0000), F32)
    return lo, hi


def _peer_hidden_kernel(gu_ref, gate_ref, x1_ref, shf_ref, scf_ref, w_ref, hd_ref, *, tb):
    half = D_MODEL // 2
    h2 = x1_ref[...] * (1.0 + scf_ref[...]) + shf_ref[...]
    for t in range(tb):
        ulo, uhi = _unpack_pairs(gu_ref[t * PEER_PICKS:(t + 1) * PEER_PICKS, :])
        hd_ref[:, t:t + 1] = jnp.sum(ulo * h2[t:t + 1, :half] + uhi * h2[t:t + 1, half:], axis=-1, keepdims=True)
    w_ref[...] = _gelu_erf(hd_ref[...]) * gate_ref[...]


def _peer_hidden(gu, gate_blocks, x1, mod, tb):
    r = x1.shape[1]
    blk = pl.BlockSpec((None, PEER_PICKS, tb), lambda gi, i: (i, 0, 0))
    return pl.pallas_call(
        functools.partial(_peer_hidden_kernel, tb=tb),
        grid=(1, r // tb),
        in_specs=[pl.BlockSpec((tb * PEER_PICKS, D_MODEL // 2), lambda gi, i: (i, 0)), blk,
                  _rows(tb, D_MODEL), _mod(mod, tb, 3), _mod(mod, tb, 4)],
        out_specs=blk,
        out_shape=jax.ShapeDtypeStruct((r // tb, PEER_PICKS, tb), F32),
        scratch_shapes=[pltpu.VMEM((PEER_PICKS, tb), F32)],
        compiler_params=_cparams("parallel", "parallel"),
        name="peer_hidden",
    )(gu, gate_blocks, x1, mod, mod)


def _peer_out_kernel(x1_ref, ff_ref, gtf_ref, l2g_ref, l2b_ref, o_ref, *, alpha):
    o_ref[...] = _layernorm(alpha * x1_ref[...] + gtf_ref[...] * ff_ref[...], l2g_ref[...], l2b_ref[...])


def _peer_out(x1, ff, mod, p, alpha, tb):
    g, r, _ = x1.shape
    return pl.pallas_call(
        functools.partial(_peer_out_kernel, alpha=alpha),
        grid=(g, r // tb),
        in_specs=[_rows(tb, D_MODEL), _rows(tb, D_MODEL), _mod(mod, tb, 5), _const((1, D_MODEL)), _const((1, D_MODEL))],
        out_specs=_rows(tb, D_MODEL),
        out_shape=jax.ShapeDtypeStruct((g, r, D_MODEL), F32),
        compiler_params=_cparams("parallel", "parallel"),
        name="peer_out_ln2",
    )(x1, ff, mod, _row2(p["ln2_g"]), _row2(p["ln2_b"]))


def _token_stage(x, mod, prev_fn, wkv_fn, attn_fn, p, alpha, tb):
    ps, pattn, pgate = _inproj(x, mod, p["ln_in_g"], p["ln_in_b"], p["w_in_bf16"], tb)
    r, lw, k, v, kk, kka, gl = _rwkv_prep(ps, prev_fn(ps), p, tb)
    y, wkv_new = wkv_fn(r, lw, k, v, kk, kka)
    ya = _rwkv_post(y, r, k, v, gl, p, tb)
    ob = attn_fn(pattn)
    x1, scores_t = _merge(x, ya, ob, pgate, mod, p, alpha, tb)
    idx_t, gate_t = _topk(scores_t, TOPK_TB)
    return ps, pattn, wkv_new, x1, idx_t, gate_t


def kernel(x_prompt, x_sample, state_wkv, state_shift, cache_k_win, cache_v_win, c_prompt, c_sample, ln_in_g, ln_in_b, w_ada, b_ada, w_in, mu_shift, rwkv_w0, rwkv_w2, rwkv_a0, rwkv_a2, rwkv_g2, rwkv_k_k, rwkv_k_a, rwkv_r_k, rwkv_gn_g, rwkv_gn_b, attn_sinks, w_pa, w_pb, w_o, ln1_g, ln1_b, peer_wq, peer_sub_keys, peer_u, peer_v, ln2_g, ln2_b):
    depth = w_in.shape[0]
    assert depth == 1, "single-layer trunk"
    alpha = (2.0 * depth) ** 0.25
    n_p, t_p, _ = x_prompt.shape
    n_s = x_sample.shape[0]
    p = dict(ln_in_g=ln_in_g, ln_in_b=ln_in_b, w_in_bf16=w_in[0].astype(BF16), mu_shift=mu_shift[0],
             rwkv_w0=rwkv_w0[0], rwkv_w2=rwkv_w2[0], rwkv_a0=rwkv_a0[0], rwkv_a2=rwkv_a2[0], rwkv_g2=rwkv_g2[0],
             rwkv_k_k=rwkv_k_k[0], rwkv_k_a=rwkv_k_a[0], rwkv_r_k=rwkv_r_k[0], rwkv_gn_g=rwkv_gn_g[0],
             rwkv_gn_b=rwkv_gn_b[0], w_pa=w_pa[0], w_pb=w_pb[0], w_o=w_o[0], ln1_g=ln1_g[0], ln1_b=ln1_b[0],
             peer_wq=peer_wq[0], peer_sub_keys=peer_sub_keys[0], ln2_g=ln2_g[0], ln2_b=ln2_b[0])
    sinks = attn_sinks[0]

    n_c = n_p + n_s
    pad = (-n_c) % 8
    c_all = jnp.concatenate([c_prompt, c_sample, jnp.zeros((pad, D_MODEL), F32)], axis=0)
    mod_all = _modulation(c_all, w_ada[0], b_ada[0])
    mod_p = mod_all[:n_p].reshape(n_p, 1, N_MOD * D_MODEL)
    mod_s = mod_all[n_p:n_c].reshape(1, n_s, N_MOD * D_MODEL)

    def prev_p(ps):
        return jnp.concatenate([jnp.zeros((1, 1, SHIFT_W), F32), ps[:, :-1]], axis=1)

    def wkv_p(r, lw, k, v, kk, kka):
        return _rwkv_chunk_scan(r, lw, k, v, kk, kka, jnp.zeros((1, H_A, HD_A, HD_A), F32))

    tu, tv = _pack_bf16_pairs(peer_u[0]), _pack_bf16_pairs(peer_v[0])

    def select(x, mod, prev_fn, wkv_fn, attn_fn, tb):
        ps, pattn, wkv_new, x1, idx_t, gate_t = _token_stage(x, mod, prev_fn, wkv_fn, attn_fn, p, alpha, tb)
        r = x1.shape[1]
        idx = jnp.transpose(idx_t.reshape(PEER_PICKS, r))
        gate_blocks = jnp.transpose(gate_t.reshape(PEER_PICKS, r // PEER_TB, PEER_TB), (1, 0, 2))
        gu = _sc_gather_rows(tu, idx.reshape(-1))
        return ps, pattn, wkv_new, (gu, idx, gate_blocks, x1, mod)

    def weigh(sel):
        gu, idx, gate_blocks, x1, mod = sel
        w_blocks = _peer_hidden(gu, gate_blocks, x1, mod, PEER_TB)
        wgt = jnp.transpose(w_blocks, (0, 2, 1)).reshape(-1, PEER_PICKS)
        return _sc_weighted_row_sum(tv, idx, wgt), x1, mod

    def finish(wsum):
        ff, x1, mod = wsum
        return _peer_out(x1, ff[None], mod, p, alpha, min(TOKEN_TB, x1.shape[1]))

    xs = x_sample.reshape(1, n_s, D_MODEL)

    def prev_s(ps):
        return state_shift[0].reshape(1, n_s, SHIFT_W)

    def wkv_s(r, lw, k, v, kk, kka):
        sq = lambda z: z.reshape(n_s, D_A)
        y, s = _rwkv_step(state_wkv[0], sq(r), sq(lw), sq(k), sq(kk), sq(kka), sq(v), STEP_NB)
        return y.reshape(1, n_s, D_A), s

    def attn_s(pa):
        o = _attn_cache(pa.reshape(n_s, 1, ATTN_W), cache_k_win[0], cache_v_win[0], sinks, STEP_NB)
        return o.reshape(1, n_s, D_B)

    attn_p = lambda pa: _attn_band(pa, sinks)
    groups = [(x_prompt[b:b + 1], mod_p[b:b + 1], prev_p, wkv_p, attn_p, TOKEN_TB) for b in range(n_p)]
    groups.append((xs, mod_s, prev_s, wkv_s, attn_s, min(TOKEN_TB, n_s)))
    n_g = len(groups)
    dense, sel, wsum, y_l = [None] * n_g, [None] * n_g, [None] * n_g, [None] * n_g
    for step in range(n_g + 2):
        if step < n_g:
            *dense[step], sel[step] = select(*groups[step])
        if 0 <= step - 1 < n_g:
            wsum[step - 1] = weigh(sel[step - 1])
        if 0 <= step - 2 < n_g:
            y_l[step - 2] = finish(wsum[step - 2])
    ps_s, pattn_s, wkv_s_new = dense[n_p]
    y_s = y_l[n_p]
    y_p = jnp.concatenate(y_l[:n_p], axis=0)
    shift_p = jnp.concatenate([d[0][:, -1] for d in dense[:n_p]], axis=0)
    pattn_p = jnp.concatenate([d[1][:, -WINDOW:] for d in dense[:n_p]], axis=0)
    wkv_p_new = jnp.concatenate([d[2] for d in dense[:n_p]], axis=0)

    kv = lambda pa, o: pa[..., o:o + H_KV * HD_B]
    ko, vo = D_B, D_B + H_KV * HD_B
    k_win_p = kv(pattn_p, ko)[:, -WINDOW:].reshape(n_p, WINDOW, H_KV, HD_B)
    v_win_p = kv(pattn_p, vo)[:, -WINDOW:].reshape(n_p, WINDOW, H_KV, HD_B)
    k_new_s = kv(pattn_s, ko).reshape(n_s, 1, H_KV, HD_B)
    v_new_s = kv(pattn_s, vo).reshape(n_s, 1, H_KV, HD_B)
    k_win_s = jnp.concatenate([cache_k_win[0], k_new_s], axis=1)[:, -WINDOW:]
    v_win_s = jnp.concatenate([cache_v_win[0], v_new_s], axis=1)[:, -WINDOW:]
    return (y_p, y_s.reshape(n_s, 1, D_MODEL), wkv_p_new[None], wkv_s_new[None],
            shift_p[None], ps_s.reshape(n_s, SHIFT_W)[None],
            k_win_p[None], k_win_s[None], v_win_p[None], v_win_s[None])
```

```python
import functools
import math

import jax
import jax.numpy as jnp
from jax import lax
from jax.experimental import pallas as pl
from jax.experimental.pallas import tpu as pltpu
from jax.experimental.pallas import tpu_sc as plsc

F32 = jnp.float32
BF16 = jnp.bfloat16
I32 = jnp.int32
U32 = jnp.uint32

D_MODEL = 1024
H_A, HD_A = 8, 64
D_A = H_A * HD_A
D_LORA_W, D_LORA_A, D_LORA_G = 64, 64, 128
GN_EPS = 64e-5
H_Q, H_KV, HD_B = 8, 2, 64
G_Q = H_Q // H_KV
D_B = H_Q * HD_B
WINDOW = 128
N_KEYS = 128
PEER_HEADS, PEER_TOPK, PEER_HALF = 8, 16, 128
PEER_PICKS = PEER_HEADS * PEER_TOPK
N_MOD = 6
LN_EPS = 1e-5
NEG_INF = -1e30
OFF_WD = 3 * D_A
OFF_AD = OFF_WD + D_LORA_W
OFF_GD = OFF_AD + D_LORA_A
SHIFT_W = OFF_GD + D_LORA_G
ATTN_W = D_B + 2 * H_KV * HD_B
GATE_W = 2 * D_MODEL
D_IN = SHIFT_W + ATTN_W + GATE_W

VMEM_LIMIT = 48 * 1024 * 1024
RWKV_CHUNK = 64
SC_GATHER_ROWS = 32
SC_IDX_ROWS = 32
SC_GATHER_BUFS = 4
SC_GATHER_AHEAD = 2
SC_ACC_TOKENS = 16
SC_ACC_PANELS = 4
TOKEN_TB = 256
TOPK_TB = 128
PEER_TB = 16
PROMPT_SEGMENT = 2048
STEP_NB = 8


def _cparams(*sem):
    return pltpu.CompilerParams(dimension_semantics=sem, vmem_limit_bytes=VMEM_LIMIT)


def _layernorm(x, g, b):
    mu = jnp.mean(x, -1, keepdims=True)
    xc = x - mu
    var = jnp.mean(xc * xc, -1, keepdims=True)
    return xc * lax.rsqrt(var + LN_EPS) * g + b


def _split(x):
    hi = x.astype(BF16)
    lo = (x - hi.astype(F32)).astype(BF16)
    return hi, lo


_NN = (((1,), (0,)), ((), ()))
_NT = (((1,), (1,)), ((), ()))
_TN = (((0,), (0,)), ((), ()))


def _dot3(a, b, dims=_NN):
    ah, al = _split(a)
    bh, bl = _split(b)
    d = functools.partial(lax.dot_general, dimension_numbers=dims, preferred_element_type=F32)
    return d(ah, bh) + d(ah, bl) + d(al, bh)


def _dot_exact_lhs(a_bf16, b, dims=_NN):
    b1 = b.astype(BF16)
    r1 = b - b1.astype(F32)
    b2 = r1.astype(BF16)
    b3 = (r1 - b2.astype(F32)).astype(BF16)
    d = functools.partial(lax.dot_general, dimension_numbers=dims, preferred_element_type=F32)
    return d(a_bf16, b1) + d(a_bf16, b2) + d(a_bf16, b3)


def _dotb(a, b, dims=_NN):
    return lax.dot_general(a.astype(BF16), b.astype(BF16), dims, preferred_element_type=F32)


def _rows(tb, width, col=0):
    return pl.BlockSpec((None, tb, width), lambda g, i: (g, i, col))


def _mod(mod, tb, col):
    if mod.shape[1] == 1:
        return pl.BlockSpec((None, 1, D_MODEL), lambda g, i: (g, 0, col))
    return pl.BlockSpec((None, tb, D_MODEL), lambda g, i: (g, i, col))


def _const(shape):
    n = len(shape)
    return pl.BlockSpec(shape, lambda g, i: (0,) * n)


def _row2(p):
    return p.reshape(1, -1).astype(F32)


def _mod_kernel(c_ref, w_ref, b_ref, o_ref):
    c = c_ref[...]
    a = c * jax.nn.sigmoid(c)
    o_ref[...] = _dot3(a, w_ref[...]) + b_ref[...]


def _modulation(c, w_ada, b_ada):
    n = c.shape[0]
    tn = D_MODEL
    return pl.pallas_call(
        _mod_kernel,
        grid=(w_ada.shape[1] // tn,),
        in_specs=[pl.BlockSpec((n, D_MODEL), lambda j: (0, 0)),
                  pl.BlockSpec((D_MODEL, tn), lambda j: (0, j)),
                  pl.BlockSpec((1, tn), lambda j: (0, j))],
        out_specs=pl.BlockSpec((n, tn), lambda j: (0, j)),
        out_shape=jax.ShapeDtypeStruct((n, w_ada.shape[1]), F32),
        compiler_params=_cparams("arbitrary"),
        name="modulation",
    )(c, w_ada, b_ada.reshape(1, -1))


def _inproj_kernel(x_ref, sh_ref, sc_ref, g_ref, b_ref, w_ref, ps_ref, pa_ref, pg_ref):
    xn = _layernorm(x_ref[...], g_ref[...], b_ref[...])
    h = (xn * (1.0 + sc_ref[...]) + sh_ref[...]).astype(BF16)
    ps_ref[...] = jnp.dot(h, w_ref[:, :SHIFT_W], preferred_element_type=F32)
    pa_ref[...] = jnp.dot(h, w_ref[:, SHIFT_W:SHIFT_W + ATTN_W], preferred_element_type=F32)
    pg_ref[...] = jnp.dot(h, w_ref[:, SHIFT_W + ATTN_W:], preferred_element_type=F32)


def _inproj(x, mod, ln_g, ln_b, w_in_bf16, tb):
    g, r, _ = x.shape
    shp = lambda w: jax.ShapeDtypeStruct((g, r, w), F32)
    return pl.pallas_call(
        _inproj_kernel,
        grid=(g, r // tb),
        in_specs=[_rows(tb, D_MODEL), _mod(mod, tb, 0), _mod(mod, tb, 1),
                  _const((1, D_MODEL)), _const((1, D_MODEL)), _const((D_MODEL, D_IN))],
        out_specs=[_rows(tb, SHIFT_W), _rows(tb, ATTN_W), _rows(tb, GATE_W)],
        out_shape=[shp(SHIFT_W), shp(ATTN_W), shp(GATE_W)],
        compiler_params=_cparams("parallel", "parallel"),
        name="inproj",
    )(x, mod, mod, _row2(ln_g), _row2(ln_b), w_in_bf16)


def _softplus(x):
    return jnp.maximum(x, 0.0) + jnp.log1p(jnp.exp(-jnp.abs(x)))


def _rwkv_prep_kernel(ps_ref, prev_ref, mu_ref, w0_ref, w2_ref, a0_ref, a2_ref, g2_ref, kk_w_ref, ka_w_ref,
                      hsum_ref, r_ref, lw_ref, k_ref, v_ref, kk_ref, kka_ref, g_ref):
    ps = ps_ref[...]
    xs = ps + (prev_ref[...] - ps) * mu_ref[...]
    r = xs[:, 0:D_A]
    k = xs[:, D_A:2 * D_A]
    v = xs[:, 2 * D_A:3 * D_A]
    wd = xs[:, OFF_WD:OFF_AD]
    ad = xs[:, OFF_AD:OFF_GD]
    gd = xs[:, OFF_GD:SHIFT_W]
    z = w0_ref[...] + _dot3(jnp.tanh(wd), w2_ref[...])
    w_log = -_softplus(-z) - 0.5
    a = jax.nn.sigmoid(a0_ref[...] + _dot3(ad, a2_ref[...]))
    kk = k * kk_w_ref[...]
    ss = _dot3(kk * kk, hsum_ref[...])
    kk = kk / jnp.maximum(jnp.sqrt(ss), 1e-12)
    r_ref[...] = r
    lw_ref[...] = -jnp.exp(w_log)
    k_ref[...] = k * (1.0 + (a - 1.0) * ka_w_ref[...])
    v_ref[...] = v
    kk_ref[...] = kk
    kka_ref[...] = kk * a
    g_ref[...] = _dot3(jax.nn.sigmoid(gd), g2_ref[...])


def _head_sum_matrix():
    h = jnp.arange(D_A) // HD_A
    return (h[:, None] == h[None, :]).astype(F32)


def _rwkv_prep(ps, prev, p, tb):
    g, r, _ = ps.shape
    shp = jax.ShapeDtypeStruct((g, r, D_A), F32)
    return pl.pallas_call(
        _rwkv_prep_kernel,
        grid=(g, r // tb),
        in_specs=[_rows(tb, SHIFT_W), _rows(tb, SHIFT_W), _const((1, SHIFT_W)),
                  _const((1, D_A)), _const((D_LORA_W, D_A)), _const((1, D_A)), _const((D_LORA_A, D_A)),
                  _const((D_LORA_G, D_A)), _const((1, D_A)), _const((1, D_A)), _const((D_A, D_A))],
        out_specs=[_rows(tb, D_A)] * 7,
        out_shape=[shp] * 7,
        compiler_params=_cparams("parallel", "parallel"),
        name="rwkv_prep",
    )(ps, prev, _row2(p["mu_shift"]), _row2(p["rwkv_w0"]), p["rwkv_w2"], _row2(p["rwkv_a0"]), p["rwkv_a2"],
      p["rwkv_g2"], _row2(p["rwkv_k_k"]), _row2(p["rwkv_k_a"]), _head_sum_matrix())


def _rwkv_chunk_kernel(r_ref, lw_ref, k_ref, v_ref, kk_ref, kka_ref, s0_ref, y_ref, s_ref):
    c = RWKV_CHUNK

    @pl.when(pl.program_id(1) == 0)
    def _():
        s_ref[...] = s0_ref[...]

    row = lax.broadcasted_iota(I32, (c, c), 0)
    col = lax.broadcasted_iota(I32, (c, c), 1)
    tril = row >= col
    stril = row > col
    lw = lw_ref[...]
    cum = _dot_exact_lhs(tril.astype(BF16), lw)
    cum_end = cum[c - 1:c, :]
    g_inv = jnp.exp(-cum)
    g_end = jnp.exp(cum_end - cum)
    a_hat = -kk_ref[...] * jnp.exp(cum - lw)
    b_hat = kka_ref[...] * g_inv
    k_hat = k_ref[...] * g_inv
    r_til = r_ref[...] * jnp.exp(cum)
    b_end = kka_ref[...] * g_end
    k_end = k_ref[...] * g_end
    gam_end = jnp.exp(cum_end)
    v_all = v_ref[...]
    s_all = s_ref[...]
    n_steps = int(math.log2(c))
    heads = range(H_A)
    sl = [slice(h * HD_A, (h + 1) * HD_A) for h in heads]
    vh = [v_all[:, sl[h]] for h in heads]
    ar = [jnp.concatenate([a_hat[:, sl[h]], r_til[:, sl[h]]], axis=0) for h in heads]
    bk = [jnp.concatenate([b_hat[:, sl[h]], k_hat[:, sl[h]]], axis=0) for h in heads]
    x = [_dot3(ar[h], bk[h], _NT) for h in heads]
    ars = [_dot3(ar[h], s_all[h], _NT) for h in heads]
    a_ak = [jnp.where(stril, x[h][:c, c:], 0.0) for h in heads]
    n = [jnp.where(stril, x[h][:c, :c], 0.0) for h in heads]
    u = [ars[h][:c] + _dot3(a_ak[h], vh[h]) for h in heads]
    for it in range(n_steps):
        u = [u[h] + _dot3(n[h], u[h]) for h in heads]
        if it + 1 < n_steps:
            n = [_dot3(n[h], n[h]) for h in heads]
    uv = [jnp.concatenate([u[h], vh[h]], axis=0) for h in heads]
    a_r = [jnp.concatenate([jnp.where(tril, x[h][c:, :c], 0.0), jnp.where(tril, x[h][c:, c:], 0.0)], axis=1)
           for h in heads]
    y = [ars[h][c:] + _dot3(a_r[h], uv[h]) for h in heads]
    bke = [jnp.concatenate([b_end[:, sl[h]], k_end[:, sl[h]]], axis=0) for h in heads]
    s_new = [s_all[h] * gam_end[:, sl[h]] + _dot3(uv[h], bke[h], _TN) for h in heads]
    for h in heads:
        y_ref[:, sl[h]] = y[h]
        s_ref[h] = s_new[h]


def _rwkv_chunk_scan(r, lw, k, v, kk, kka, s0):
    n, t, _ = r.shape
    c = RWKV_CHUNK
    seq = pl.BlockSpec((None, c, D_A), lambda b, i: (b, i, 0))
    st = pl.BlockSpec((None, H_A, HD_A, HD_A), lambda b, i: (b, 0, 0, 0))
    return pl.pallas_call(
        _rwkv_chunk_kernel,
        grid=(n, t // c),
        in_specs=[seq] * 6 + [st],
        out_specs=[seq, st],
        out_shape=[jax.ShapeDtypeStruct((n, t, D_A), F32), jax.ShapeDtypeStruct((n, H_A, HD_A, HD_A), F32)],
        compiler_params=_cparams("parallel", "arbitrary"),
        name="rwkv_chunk_scan",
    )(r, lw, k, v, kk, kka, s0)


def _rwkv_step_kernel(s_ref, r_ref, lw_ref, k_ref, kk_ref, kka_ref, v_ref, y_ref, so_ref):
    s = s_ref[...]
    sa = jnp.sum(s * (-kk_ref[...]), axis=-1, keepdims=True)
    s = s * jnp.exp(lw_ref[...]) + sa * kka_ref[...] + v_ref[...] * k_ref[...]
    so_ref[...] = s
    y_ref[...] = jnp.sum(s * r_ref[...], axis=-1, keepdims=True)


def _rwkv_step(s0, r, lw, k, kk, kka, v, nb):
    n = s0.shape[0]
    key = lambda z: z.reshape(n, H_A, 1, HD_A)
    st = pl.BlockSpec((nb, H_A, HD_A, HD_A), lambda i: (i, 0, 0, 0))
    ks = pl.BlockSpec((nb, H_A, 1, HD_A), lambda i: (i, 0, 0, 0))
    vs = pl.BlockSpec((nb, H_A, HD_A, 1), lambda i: (i, 0, 0, 0))
    y, s = pl.pallas_call(
        _rwkv_step_kernel,
        grid=(n // nb,),
        in_specs=[st, ks, ks, ks, ks, ks, vs],
        out_specs=[vs, st],
        out_shape=[jax.ShapeDtypeStruct((n, H_A, HD_A, 1), F32), jax.ShapeDtypeStruct(s0.shape, F32)],
        compiler_params=_cparams("parallel"),
        name="rwkv_step",
    )(s0, key(r), key(lw), key(k), key(kk), key(kka), v.reshape(n, H_A, HD_A, 1))
    return y.reshape(n, D_A), s


def _rwkv_post_kernel(y_ref, r_ref, k_ref, v_ref, g_ref, gn_g_ref, gn_b_ref, rk_ref, hsum_ref, o_ref):
    y = y_ref[...]
    hs = hsum_ref[...]
    mu = _dot3(y, hs) * (1.0 / HD_A)
    yc = y - mu
    var = _dot3(yc * yc, hs) * (1.0 / HD_A)
    yn = yc * lax.rsqrt(var + GN_EPS) * gn_g_ref[...] + gn_b_ref[...]
    bonus = _dot3(r_ref[...] * k_ref[...] * rk_ref[...], hs) * v_ref[...]
    o_ref[...] = (yn + bonus) * g_ref[...]


def _rwkv_post(y, r, k, v, g, p, tb):
    gg, rr, _ = y.shape
    return pl.pallas_call(
        _rwkv_post_kernel,
        grid=(gg, rr // tb),
        in_specs=[_rows(tb, D_A)] * 5 + [_const((1, D_A))] * 3 + [_const((D_A, D_A))],
        out_specs=_rows(tb, D_A),
        out_shape=jax.ShapeDtypeStruct((gg, rr, D_A), F32),
        compiler_params=_cparams("parallel", "parallel"),
        name="rwkv_post",
    )(y, r, k, v, g, _row2(p["rwkv_gn_g"]), _row2(p["rwkv_gn_b"]), _row2(p["rwkv_r_k"]), _head_sum_matrix())


def _sink_softmax(s, sink):
    m = jnp.maximum(jnp.max(s, axis=-1, keepdims=True), sink)
    p = jnp.exp(s - m)
    den = jnp.sum(p, axis=-1, keepdims=True) + jnp.exp(sink - m)
    return p / den


def _attn_band_kernel(cur_ref, prev_ref, carry_ref, sink_ref, o_ref, *, has_carry):
    blk = WINDOW
    i = pl.program_id(1)
    cur = cur_ref[...]
    prev = jnp.where(i == 0, carry_ref[...], prev_ref[...])
    qi = lax.broadcasted_iota(I32, (G_Q * blk, 2 * blk), 0) % blk
    kj = lax.broadcasted_iota(I32, (G_Q * blk, 2 * blk), 1)
    rel = blk + qi - kj
    valid = (rel >= 0) & (rel <= WINDOW)
    if not has_carry:
        valid = valid & ((kj >= blk) | (i > 0))
    relf = rel.astype(F32)
    gidx = lax.broadcasted_iota(I32, (G_Q * blk, 1), 0) // blk
    for kvh in range(H_KV):
        q4 = jnp.concatenate([cur[:, (kvh * G_Q + g) * HD_B:(kvh * G_Q + g + 1) * HD_B] for g in range(G_Q)], axis=0)
        ko = D_B + kvh * HD_B
        vo = D_B + H_KV * HD_B + kvh * HD_B
        kmat = jnp.concatenate([prev[:, ko:ko + HD_B], cur[:, ko:ko + HD_B]], axis=0)
        vmat = jnp.concatenate([prev[:, vo:vo + HD_B], cur[:, vo:vo + HD_B]], axis=0)
        slope = jnp.zeros((G_Q * blk, 1), F32)
        sink = jnp.zeros((G_Q * blk, 1), F32)
        for g in range(G_Q):
            hq = kvh * G_Q + g
            slope = jnp.where(gidx == g, 2.0 ** (-8.0 * (hq + 1) / H_Q), slope)
            sink = jnp.where(gidx == g, sink_ref[hq], sink)
        s = _dotb(q4, kmat, _NT) * (HD_B ** -0.5)
        s = jnp.where(valid, s - slope * relf, NEG_INF)
        p = _sink_softmax(s, sink)
        o = _dotb(p, vmat)
        for g in range(G_Q):
            hq = kvh * G_Q + g
            o_ref[:, hq * HD_B:(hq + 1) * HD_B] = o[g * blk:(g + 1) * blk]


def _attn_band(pattn, carry, sinks):
    n, t, _ = pattn.shape
    blk = WINDOW
    has_carry = carry is not None
    if not has_carry:
        carry = jnp.zeros((n, blk, ATTN_W), F32)
    return pl.pallas_call(
        functools.partial(_attn_band_kernel, has_carry=has_carry),
        grid=(n, t // blk),
        in_specs=[pl.BlockSpec((None, blk, ATTN_W), lambda b, i: (b, i, 0)),
                  pl.BlockSpec((None, blk, ATTN_W), lambda b, i: (b, jnp.maximum(i - 1, 0), 0)),
                  pl.BlockSpec((None, blk, ATTN_W), lambda b, i: (b, 0, 0)),
                  pl.BlockSpec(memory_space=pltpu.SMEM)],
        out_specs=pl.BlockSpec((None, blk, D_B), lambda b, i: (b, i, 0)),
        out_shape=jax.ShapeDtypeStruct((n, t, D_B), F32),
        compiler_params=_cparams("parallel", "parallel"),
        name="attn_band",
    )(pattn, pattn, carry, sinks.astype(F32))


def _attn_cache_kernel(cur_ref, kc_ref, vc_ref, sink_ref, o_ref, *, nb):
    relc = (WINDOW - lax.broadcasted_iota(I32, (G_Q, WINDOW), 1)).astype(F32)
    gidx = lax.broadcasted_iota(I32, (G_Q, 1), 0)
    for b in range(nb):
        cur = cur_ref[b]
        for kvh in range(H_KV):
            q4 = jnp.concatenate([cur[:, (kvh * G_Q + g) * HD_B:(kvh * G_Q + g + 1) * HD_B] for g in range(G_Q)], axis=0)
            ko = D_B + kvh * HD_B
            vo = D_B + H_KV * HD_B + kvh * HD_B
            k_new = cur[:, ko:ko + HD_B]
            v_new = cur[:, vo:vo + HD_B]
            kc = kc_ref[b, :, kvh * HD_B:(kvh + 1) * HD_B]
            vc = vc_ref[b, :, kvh * HD_B:(kvh + 1) * HD_B]
            slope = jnp.zeros((G_Q, 1), F32)
            sink = jnp.zeros((G_Q, 1), F32)
            for g in range(G_Q):
                hq = kvh * G_Q + g
                slope = jnp.where(gidx == g, 2.0 ** (-8.0 * (hq + 1) / H_Q), slope)
                sink = jnp.where(gidx == g, sink_ref[hq], sink)
            scale = HD_B ** -0.5
            sc = _dotb(q4, kc, _NT) * scale - slope * relc
            sn = jnp.sum(q4.astype(BF16).astype(F32) * k_new.astype(BF16).astype(F32), axis=-1, keepdims=True) * scale
            m = jnp.maximum(jnp.maximum(jnp.max(sc, axis=-1, keepdims=True), sn), sink)
            pc = jnp.exp(sc - m)
            pn = jnp.exp(sn - m)
            den = jnp.sum(pc, axis=-1, keepdims=True) + pn + jnp.exp(sink - m)
            o = (_dotb(pc / den, vc) + (pn / den).astype(BF16).astype(F32) * v_new.astype(BF16).astype(F32))
            for g in range(G_Q):
                hq = kvh * G_Q + g
                o_ref[b, :, hq * HD_B:(hq + 1) * HD_B] = o[g:g + 1]


def _attn_cache(pattn, k_buf, v_buf, sinks, nb):
    n = pattn.shape[0]
    kc = k_buf.reshape(n, WINDOW, H_KV * HD_B)
    vc = v_buf.reshape(n, WINDOW, H_KV * HD_B)
    return pl.pallas_call(
        functools.partial(_attn_cache_kernel, nb=nb),
        grid=(n // nb,),
        in_specs=[pl.BlockSpec((nb, 1, ATTN_W), lambda i: (i, 0, 0)),
                  pl.BlockSpec((nb, WINDOW, H_KV * HD_B), lambda i: (i, 0, 0)),
                  pl.BlockSpec((nb, WINDOW, H_KV * HD_B), lambda i: (i, 0, 0)),
                  pl.BlockSpec(memory_space=pltpu.SMEM)],
        out_specs=pl.BlockSpec((nb, 1, D_B), lambda i: (i, 0, 0)),
        out_shape=jax.ShapeDtypeStruct((n, 1, D_B), F32),
        compiler_params=_cparams("parallel"),
        name="attn_cache",
    )(pattn, kc, vc, sinks.astype(F32))


def _merge_kernel(x_ref, ya_ref, ob_ref, pg_ref, gtm_ref, shf_ref, scf_ref, lng_ref, lnb_ref, l1g_ref, l1b_ref,
                  wpa_ref, wpb_ref, wo_ref, wq_ref, sk_ref, x1_ref, st_ref, *, alpha):
    ya = jnp.dot(ya_ref[...].astype(BF16), wpa_ref[...], preferred_element_type=F32)
    yb = jnp.dot(ob_ref[...].astype(BF16), wpb_ref[...], preferred_element_type=F32)
    pg = pg_ref[...]
    merged = jax.nn.sigmoid(pg[:, :D_MODEL]) * ya + jax.nn.sigmoid(pg[:, D_MODEL:]) * yb
    mix = jnp.dot(merged.astype(BF16), wo_ref[...], preferred_element_type=F32)
    xn = _layernorm(x_ref[...], lng_ref[...], lnb_ref[...])
    x1 = _layernorm(alpha * xn + gtm_ref[...] * mix, l1g_ref[...], l1b_ref[...])
    x1_ref[...] = x1
    h2 = x1 * (1.0 + scf_ref[...]) + shf_ref[...]
    q = jnp.dot(h2.astype(BF16), wq_ref[...], preferred_element_type=F32)
    for hc in range(2 * PEER_HEADS):
        st_ref[hc] = _dot3(sk_ref[hc % 2], q[:, hc * PEER_HALF:(hc + 1) * PEER_HALF], _NT)


def _merge(x, ya, ob, pg, mod, p, alpha, tb):
    g, r, _ = x.shape
    return pl.pallas_call(
        functools.partial(_merge_kernel, alpha=alpha),
        grid=(g, r // tb),
        in_specs=[_rows(tb, D_MODEL), _rows(tb, D_A), _rows(tb, D_B), _rows(tb, GATE_W),
                  _mod(mod, tb, 2), _mod(mod, tb, 3), _mod(mod, tb, 4)]
                 + [_const((1, D_MODEL))] * 4
                 + [_const((D_A, D_MODEL)), _const((D_B, D_MODEL)), _const((D_MODEL, D_MODEL)),
                    _const((D_MODEL, 2 * PEER_HEADS * PEER_HALF)), _const((2, N_KEYS, PEER_HALF))],
        out_specs=[_rows(tb, D_MODEL),
                   pl.BlockSpec((None, 2 * PEER_HEADS, N_KEYS, tb), lambda gi, i: (gi, 0, 0, i))],
        out_shape=[jax.ShapeDtypeStruct((g, r, D_MODEL), F32),
                   jax.ShapeDtypeStruct((g, 2 * PEER_HEADS, N_KEYS, r), F32)],
        compiler_params=_cparams("parallel", "parallel"),
        name="merge_ln1_peer_scores",
    )(x, ya, ob, pg, mod, mod, mod, _row2(p["ln_in_g"]), _row2(p["ln_in_b"]), _row2(p["ln1_g"]), _row2(p["ln1_b"]),
      p["w_pa"].astype(BF16), p["w_pb"].astype(BF16), p["w_o"].astype(BF16), p["peer_wq"].astype(BF16),
      p["peer_sub_keys"])


def _extract_top(vals, payload, n_rows, tb):
    rio = lax.broadcasted_iota(I32, (n_rows, tb), 0).astype(F32)
    top_v, top_i, top_p = [], [], []
    for _ in range(PEER_TOPK):
        m = jnp.max(vals, axis=0, keepdims=True)
        i = jnp.min(jnp.where(vals == m, rio, float(n_rows)), axis=0, keepdims=True)
        sel = rio == i
        top_v.append(m)
        top_i.append(i)
        if payload is not None:
            top_p.append(jnp.max(jnp.where(sel, payload, -1.0), axis=0, keepdims=True))
        vals = jnp.where(sel, -jnp.inf, vals)
    cat = lambda z: jnp.concatenate(z, axis=0)
    return cat(top_v), cat(top_i), (cat(top_p) if payload is not None else None)


def _pair_candidates(v1, i1, v2, i2, tb):
    k = PEER_TOPK
    sub = 8
    eid = lambda a0, a1, b0, b1: i1[a0:a1] * float(N_KEYS) + i2[b0:b1]
    vals = [v1[0:1] + v2, v1[1:2] + v2[0:sub]]
    ids = [eid(0, 1, 0, k), eid(1, 2, 0, sub)]
    brow = lax.broadcasted_iota(I32, (sub, tb), 0)
    for a in range(2, sub):
        vals.append(jnp.where(brow < k // (a + 1), v1[a:a + 1] + v2[0:sub], -jnp.inf))
        ids.append(eid(a, a + 1, 0, sub))
    vals.append(v1[sub:k] + v2[0:1])
    ids.append(eid(sub, k, 0, 1))
    return jnp.concatenate(vals, axis=0), jnp.concatenate(ids, axis=0)


def _topk_kernel(s_ref, idx_ref, gate_ref):
    tb = s_ref.shape[-1]

    def head(h, carry):
        v1, i1, _ = _extract_top(s_ref[2 * h], None, N_KEYS, tb)
        v2, i2, _ = _extract_top(s_ref[2 * h + 1], None, N_KEYS, tb)
        cand, eid = _pair_candidates(v1, i1, v2, i2, tb)
        sc, _, ex = _extract_top(cand, eid, cand.shape[0], tb)
        pexp = jnp.exp(sc - sc[0:1])
        idx_ref[h] = ex.astype(I32)
        gate_ref[h] = pexp / jnp.sum(pexp, axis=0, keepdims=True)
        return carry

    lax.fori_loop(0, PEER_HEADS, head, 0)


def _topk(scores_t, tb):
    g, _, _, r = scores_t.shape
    out = pl.BlockSpec((None, PEER_HEADS, PEER_TOPK, tb), lambda gi, i: (gi, 0, 0, i))
    return pl.pallas_call(
        _topk_kernel,
        grid=(g, r // tb),
        in_specs=[pl.BlockSpec((None, 2 * PEER_HEADS, N_KEYS, tb), lambda gi, i: (gi, 0, 0, i))],
        out_specs=[out, out],
        out_shape=[jax.ShapeDtypeStruct((g, PEER_HEADS, PEER_TOPK, r), I32),
                   jax.ShapeDtypeStruct((g, PEER_HEADS, PEER_TOPK, r), F32)],
        compiler_params=_cparams("parallel", "parallel"),
        name="peer_topk",
    )(scores_t)


def _sc_mesh():
    info = plsc.get_sparse_core_info()
    mesh = plsc.VectorSubcoreMesh(core_axis_name="c", subcore_axis_name="s")
    return info.num_cores, info.num_subcores, info.num_lanes, mesh


def _sc_gather_rows(table, idx):
    nc, ns, _, mesh = _sc_mesh()
    nw = nc * ns
    ni = idx.shape[0]
    w = table.shape[1]
    rr = SC_GATHER_ROWS
    per_w = ni // nw
    n_chunks = per_w // rr
    ki = min(SC_IDX_ROWS, n_chunks)
    n_outer = n_chunks // ki
    assert per_w * nw == ni and n_chunks * rr == per_w and n_outer * ki == n_chunks
    nb, ahead = SC_GATHER_BUFS, SC_GATHER_AHEAD
    buf = pltpu.VMEM((rr, w), table.dtype)

    @functools.partial(
        pl.kernel, mesh=mesh, out_type=jax.ShapeDtypeStruct((ni, w), table.dtype),
        scratch_types=[pltpu.VMEM((ki, rr), I32)] + [buf] * nb + [pltpu.SemaphoreType.DMA] * (2 * nb),
        name="peer_row_gather",
    )
    def gather(tab_hbm, idx_hbm, out_hbm, idx_v, *scratch):
        wid = lax.axis_index("s") * nc + lax.axis_index("c")
        base = wid * per_w
        bufs, gsem, wsem = scratch[:nb], scratch[nb:2 * nb], scratch[2 * nb:]

        @pl.loop(0, n_outer)
        def _(o):
            pltpu.sync_copy(idx_hbm.at[wid, pl.ds(o * ki, ki)], idx_v)

            def start_gather(j):
                return pltpu.async_copy(tab_hbm.at[idx_v.at[j]], bufs[j % nb], gsem[j % nb])

            def start_write(j):
                dst = pl.ds(base + (o * ki + j) * rr, rr)
                return pltpu.async_copy(bufs[j % nb], out_hbm.at[dst], wsem[j % nb])

            gathers = {j: start_gather(j) for j in range(min(ahead, ki))}
            writes = {}
            for j in range(ki):
                if j + ahead < ki:
                    if j + ahead - nb >= 0:
                        writes.pop(j + ahead - nb).wait()
                    gathers[j + ahead] = start_gather(j + ahead)
                gathers.pop(j).wait()
                writes[j] = start_write(j)
            for j in sorted(writes):
                writes[j].wait()

    return gather(table, idx.reshape(nw, n_chunks, rr))


def _sc_weighted_row_sum(table, idx, wgt):
    nc, ns, lanes, mesh = _sc_mesh()
    nw = nc * ns
    n_tok, picks = idx.shape
    ww = table.shape[1]
    rr = SC_GATHER_ROWS
    nq = picks // rr
    tpw = n_tok // nw
    kt = min(SC_ACC_TOKENS, tpw)
    n_outer = tpw // kt
    pw = ww // SC_ACC_PANELS
    nv = pw // lanes
    ahead = SC_GATHER_AHEAD
    assert tpw * nw == n_tok and n_outer * kt == tpw and nq * rr == picks and ahead < nq and nv * lanes == pw
    cp = pltpu.CompilerParams(needs_layout_passes=False)
    buf = pltpu.VMEM((rr, ww), table.dtype)

    @functools.partial(
        pl.kernel, mesh=mesh, out_type=jax.ShapeDtypeStruct((n_tok, 2 * ww), F32),
        scratch_types=[pltpu.VMEM((kt * nq, rr), I32), pltpu.VMEM((kt, picks), F32), pltpu.VMEM((kt, 2 * ww), F32)]
                      + [buf] * nq + [pltpu.SemaphoreType.DMA] * nq,
        compiler_params=cp, name="peer_weighted_row_sum",
    )
    def kern(tab_hbm, idx_hbm, w_hbm, out_hbm, idx_v, w_v, acc_v, *scratch):
        wid = lax.axis_index("s") * nc + lax.axis_index("c")
        rows, sems = scratch[:nq], scratch[nq:]

        def chunk_copy(t, q):
            return pltpu.make_async_copy(tab_hbm.at[idx_v.at[t * nq + q]], rows[q], sems[q])

        @pl.loop(0, n_outer)
        def _(o):
            tok0 = wid * tpw + o * kt
            pltpu.sync_copy(idx_hbm.at[pl.ds(tok0 * nq, kt * nq)], idx_v)
            pltpu.sync_copy(w_hbm.at[pl.ds(tok0, kt)], w_v)
            for q in range(ahead):
                chunk_copy(0, q).start()

            @pl.loop(0, kt)
            def _(t):
                tvec = jnp.full((lanes,), t, I32)
                for q in range(nq):
                    b = q
                    if q + ahead < nq:
                        chunk_copy(t, q + ahead).start()
                    else:
                        @pl.when(t + 1 < kt)
                        def _():
                            chunk_copy(t + 1, q + ahead - nq).start()
                    chunk_copy(t, q).wait()
                    for pan in range(SC_ACC_PANELS):
                        lo_at = lambda c: pl.ds(pan * pw + c * lanes, lanes)
                        hi_at = lambda c: pl.ds(ww + pan * pw + c * lanes, lanes)

                        def row_body(j, acc):
                            wj = plsc.load_gather(w_v, [tvec, jnp.full((lanes,), q * rr, I32) + j])
                            new = []
                            for c in range(nv):
                                word = rows[b][j, lo_at(c)]
                                new.append(acc[2 * c] + wj * plsc.bitcast(word << 16, F32))
                                new.append(acc[2 * c + 1] + wj * plsc.bitcast(word & jnp.uint32(0xFFFF0000), F32))
                            return tuple(new)

                        if q == 0:
                            init = tuple(jnp.zeros((lanes,), F32) for _ in range(2 * nv))
                        else:
                            init = tuple(acc_v[t, at(c)] for c in range(nv) for at in (lo_at, hi_at))
                        acc = lax.fori_loop(0, rr, row_body, init)
                        for c in range(nv):
                            acc_v[t, lo_at(c)] = acc[2 * c]
                            acc_v[t, hi_at(c)] = acc[2 * c + 1]

            pltpu.sync_copy(acc_v, out_hbm.at[pl.ds(tok0, kt)])

    return kern(table, idx.reshape(n_tok * nq, rr), wgt)


def _pack_bf16_pairs(t):
    half = t.shape[1] // 2
    b = lax.bitcast_convert_type(t.astype(BF16), jnp.uint16).astype(U32)
    return b[:, :half] | (b[:, half:] << 16)


def _gelu_erf(x):
    return 0.5 * x * (1.0 + lax.erf(x * (2.0 ** -0.5)))


def _unpack_pairs(words):
    lo = pltpu.bitcast(words << 16, F32)
    hi = pltpu.bitcast(words & jnp.uint32(0xFFFF0000), F32)
    return lo, hi


def _peer_hidden_kernel(gu_ref, gate_ref, x1_ref, shf_ref, scf_ref, w_ref, hd_ref, *, tb):
    half = D_MODEL // 2
    h2 = x1_ref[...] * (1.0 + scf_ref[...]) + shf_ref[...]
    for t in range(tb):
        ulo, uhi = _unpack_pairs(gu_ref[t * PEER_PICKS:(t + 1) * PEER_PICKS, :])
        hd_ref[:, t:t + 1] = jnp.sum(ulo * h2[t:t + 1, :half] + uhi * h2[t:t + 1, half:], axis=-1, keepdims=True)
    w_ref[...] = _gelu_erf(hd_ref[...]) * gate_ref[...]


def _peer_hidden(gu, gate_blocks, x1, mod, tb):
    r = x1.shape[1]
    blk = pl.BlockSpec((None, PEER_PICKS, tb), lambda gi, i: (i, 0, 0))
    return pl.pallas_call(
        functools.partial(_peer_hidden_kernel, tb=tb),
        grid=(1, r // tb),
        in_specs=[pl.BlockSpec((tb * PEER_PICKS, D_MODEL // 2), lambda gi, i: (i, 0)), blk,
                  _rows(tb, D_MODEL), _mod(mod, tb, 3), _mod(mod, tb, 4)],
        out_specs=blk,
        out_shape=jax.ShapeDtypeStruct((r // tb, PEER_PICKS, tb), F32),
        scratch_shapes=[pltpu.VMEM((PEER_PICKS, tb), F32)],
        compiler_params=_cparams("parallel", "parallel"),
        name="peer_hidden",
    )(gu, gate_blocks, x1, mod, mod)


def _peer_out_kernel(x1_ref, ff_ref, gtf_ref, l2g_ref, l2b_ref, o_ref, *, alpha):
    o_ref[...] = _layernorm(alpha * x1_ref[...] + gtf_ref[...] * ff_ref[...], l2g_ref[...], l2b_ref[...])


def _peer_out(x1, ff, mod, p, alpha, tb):
    g, r, _ = x1.shape
    return pl.pallas_call(
        functools.partial(_peer_out_kernel, alpha=alpha),
        grid=(g, r // tb),
        in_specs=[_rows(tb, D_MODEL), _rows(tb, D_MODEL), _mod(mod, tb, 5), _const((1, D_MODEL)), _const((1, D_MODEL))],
        out_specs=_rows(tb, D_MODEL),
        out_shape=jax.ShapeDtypeStruct((g, r, D_MODEL), F32),
        compiler_params=_cparams("parallel", "parallel"),
        name="peer_out_ln2",
    )(x1, ff, mod, _row2(p["ln2_g"]), _row2(p["ln2_b"]))


def _token_stage(x, mod, prev_fn, wkv_fn, attn_fn, p, alpha, tb):
    ps, pattn, pgate = _inproj(x, mod, p["ln_in_g"], p["ln_in_b"], p["w_in_bf16"], tb)
    r, lw, k, v, kk, kka, gl = _rwkv_prep(ps, prev_fn(ps), p, tb)
    y, wkv_new = wkv_fn(r, lw, k, v, kk, kka)
    ya = _rwkv_post(y, r, k, v, gl, p, tb)
    ob = attn_fn(pattn)
    x1, scores_t = _merge(x, ya, ob, pgate, mod, p, alpha, tb)
    idx_t, gate_t = _topk(scores_t, TOPK_TB)
    return ps, pattn, wkv_new, x1, idx_t, gate_t


def kernel(x_prompt, x_sample, state_wkv, state_shift, cache_k_win, cache_v_win, c_prompt, c_sample, ln_in_g, ln_in_b, w_ada, b_ada, w_in, mu_shift, rwkv_w0, rwkv_w2, rwkv_a0, rwkv_a2, rwkv_g2, rwkv_k_k, rwkv_k_a, rwkv_r_k, rwkv_gn_g, rwkv_gn_b, attn_sinks, w_pa, w_pb, w_o, ln1_g, ln1_b, peer_wq, peer_sub_keys, peer_u, peer_v, ln2_g, ln2_b):
    depth = w_in.shape[0]
    assert depth == 1, "single-layer trunk"
    alpha = (2.0 * depth) ** 0.25
    n_p, t_p, _ = x_prompt.shape
    n_s = x_sample.shape[0]
    p = dict(ln_in_g=ln_in_g, ln_in_b=ln_in_b, w_in_bf16=w_in[0].astype(BF16), mu_shift=mu_shift[0],
             rwkv_w0=rwkv_w0[0], rwkv_w2=rwkv_w2[0], rwkv_a0=rwkv_a0[0], rwkv_a2=rwkv_a2[0], rwkv_g2=rwkv_g2[0],
             rwkv_k_k=rwkv_k_k[0], rwkv_k_a=rwkv_k_a[0], rwkv_r_k=rwkv_r_k[0], rwkv_gn_g=rwkv_gn_g[0],
             rwkv_gn_b=rwkv_gn_b[0], w_pa=w_pa[0], w_pb=w_pb[0], w_o=w_o[0], ln1_g=ln1_g[0], ln1_b=ln1_b[0],
             peer_wq=peer_wq[0], peer_sub_keys=peer_sub_keys[0], ln2_g=ln2_g[0], ln2_b=ln2_b[0])
    sinks = attn_sinks[0]

    n_c = n_p + n_s
    pad = (-n_c) % 8
    c_all = jnp.concatenate([c_prompt, c_sample, jnp.zeros((pad, D_MODEL), F32)], axis=0)
    mod_all = _modulation(c_all, w_ada[0], b_ada[0])
    mod_p = mod_all[:n_p].reshape(n_p, 1, N_MOD * D_MODEL)
    mod_s = mod_all[n_p:n_c].reshape(1, n_s, N_MOD * D_MODEL)

    seg = min(PROMPT_SEGMENT, t_p)
    n_seg = t_p // seg
    assert n_seg * seg == t_p
    carry = {}

    def prompt_group(b, s):
        def prev_fn(ps):
            first = carry[b][0] if s > 0 else jnp.zeros((1, 1, SHIFT_W), F32)
            return jnp.concatenate([first, ps[:, :-1]], axis=1)

        def wkv_fn(r, lw, k, v, kk, kka):
            s0 = carry[b][1] if s > 0 else jnp.zeros((1, H_A, HD_A, HD_A), F32)
            return _rwkv_chunk_scan(r, lw, k, v, kk, kka, s0)

        def attn_fn(pa):
            return _attn_band(pa, carry[b][2] if s > 0 else None, sinks)

        return (x_prompt[b:b + 1, s * seg:(s + 1) * seg], mod_p[b:b + 1], prev_fn, wkv_fn, attn_fn, TOKEN_TB)

    tu, tv = _pack_bf16_pairs(peer_u[0]), _pack_bf16_pairs(peer_v[0])

    def select(x, mod, prev_fn, wkv_fn, attn_fn, tb):
        ps, pattn, wkv_new, x1, idx_t, gate_t = _token_stage(x, mod, prev_fn, wkv_fn, attn_fn, p, alpha, tb)
        r = x1.shape[1]
        idx = jnp.transpose(idx_t.reshape(PEER_PICKS, r))
        gate_blocks = jnp.transpose(gate_t.reshape(PEER_PICKS, r // PEER_TB, PEER_TB), (1, 0, 2))
        gu = _sc_gather_rows(tu, idx.reshape(-1))
        return ps, pattn, wkv_new, (gu, idx, gate_blocks, x1, mod)

    def weigh(sel):
        gu, idx, gate_blocks, x1, mod = sel
        w_blocks = _peer_hidden(gu, gate_blocks, x1, mod, PEER_TB)
        wgt = jnp.transpose(w_blocks, (0, 2, 1)).reshape(-1, PEER_PICKS)
        return _sc_weighted_row_sum(tv, idx, wgt), x1, mod

    def finish(wsum):
        ff, x1, mod = wsum
        return _peer_out(x1, ff[None], mod, p, alpha, min(TOKEN_TB, x1.shape[1]))

    xs = x_sample.reshape(1, n_s, D_MODEL)

    def prev_s(ps):
        return state_shift[0].reshape(1, n_s, SHIFT_W)

    def wkv_s(r, lw, k, v, kk, kka):
        sq = lambda z: z.reshape(n_s, D_A)
        y, s = _rwkv_step(state_wkv[0], sq(r), sq(lw), sq(k), sq(kk), sq(kka), sq(v), STEP_NB)
        return y.reshape(1, n_s, D_A), s

    def attn_s(pa):
        o = _attn_cache(pa.reshape(n_s, 1, ATTN_W), cache_k_win[0], cache_v_win[0], sinks, STEP_NB)
        return o.reshape(1, n_s, D_B)

    prompt_ids = [(b, s) for b in range(n_p) for s in range(n_seg)]
    n_g = len(prompt_ids) + 1
    sel, wsum, y_l = [None] * n_g, [None] * n_g, [None] * n_g
    for step in range(n_g + 2):
        if step < n_g - 1:
            b, s = prompt_ids[step]
            ps, pattn, wkv_new, sel[step] = select(*prompt_group(b, s))
            carry[b] = (ps[:, -1:], wkv_new, pattn[:, -WINDOW:])
        elif step == n_g - 1:
            ps_s, pattn_s, wkv_s_new, sel[step] = select(xs, mod_s, prev_s, wkv_s, attn_s, min(TOKEN_TB, n_s))
        if 0 <= step - 1 < n_g:
            wsum[step - 1] = weigh(sel[step - 1])
        if 0 <= step - 2 < n_g:
            y_l[step - 2] = finish(wsum[step - 2])
    y_s = y_l[n_g - 1]
    y_p = jnp.concatenate(y_l[:n_g - 1], axis=1).reshape(n_p, t_p, D_MODEL)
    shift_p = jnp.concatenate([carry[b][0][:, 0] for b in range(n_p)], axis=0)
    pattn_p = jnp.concatenate([carry[b][2] for b in range(n_p)], axis=0)
    wkv_p_new = jnp.concatenate([carry[b][1] for b in range(n_p)], axis=0)

    kv = lambda pa, o: pa[..., o:o + H_KV * HD_B]
    ko, vo = D_B, D_B + H_KV * HD_B
    k_win_p = kv(pattn_p, ko)[:, -WINDOW:].reshape(n_p, WINDOW, H_KV, HD_B)
    v_win_p = kv(pattn_p, vo)[:, -WINDOW:].reshape(n_p, WINDOW, H_KV, HD_B)
    k_new_s = kv(pattn_s, ko).reshape(n_s, 1, H_KV, HD_B)
    v_new_s = kv(pattn_s, vo).reshape(n_s, 1, H_KV, HD_B)
    k_win_s = jnp.concatenate([cache_k_win[0], k_new_s], axis=1)[:, -WINDOW:]
    v_win_s = jnp.concatenate([cache_v_win[0], v_new_s], axis=1)[:, -WINDOW:]
    return (y_p, y_s.reshape(n_s, 1, D_MODEL), wkv_p_new[None], wkv_s_new[None],
            shift_p[None], ps_s.reshape(n_s, SHIFT_W)[None],
            k_win_p[None], k_win_s[None], v_win_p[None], v_win_s[None])
```

```python
import functools
import math

import jax
import jax.numpy as jnp
from jax import lax
from jax.experimental import pallas as pl
from jax.experimental.pallas import tpu as pltpu
from jax.experimental.pallas import tpu_sc as plsc

F32 = jnp.float32
BF16 = jnp.bfloat16
I32 = jnp.int32
U32 = jnp.uint32

D_MODEL = 1024
H_A, HD_A = 8, 64
D_A = H_A * HD_A
D_LORA_W, D_LORA_A, D_LORA_G = 64, 64, 128
GN_EPS = 64e-5
H_Q, H_KV, HD_B = 8, 2, 64
G_Q = H_Q // H_KV
D_B = H_Q * HD_B
WINDOW = 128
N_KEYS = 128
PEER_HEADS, PEER_TOPK, PEER_HALF = 8, 16, 128
PEER_PICKS = PEER_HEADS * PEER_TOPK
N_MOD = 6
LN_EPS = 1e-5
NEG_INF = -1e30
OFF_WD = 3 * D_A
OFF_AD = OFF_WD + D_LORA_W
OFF_GD = OFF_AD + D_LORA_A
SHIFT_W = OFF_GD + D_LORA_G
ATTN_W = D_B + 2 * H_KV * HD_B
GATE_W = 2 * D_MODEL
D_IN = SHIFT_W + ATTN_W + GATE_W

VMEM_LIMIT = 48 * 1024 * 1024
RWKV_CHUNK = 64
SC_GATHER_ROWS = 32
SC_IDX_ROWS = 32
SC_GATHER_BUFS = 4
SC_GATHER_AHEAD = 2
SC_ACC_TOKENS = 16
SC_ACC_TOKENS_FUSED = 8
SC_ACC_PANELS = 4
TOKEN_TB = 256
TOPK_TB = 128
PEER_TB = 16
PROMPT_SEGMENT = 2048
STEP_NB = 8


def _cparams(*sem):
    return pltpu.CompilerParams(dimension_semantics=sem, vmem_limit_bytes=VMEM_LIMIT)


def _layernorm(x, g, b):
    mu = jnp.mean(x, -1, keepdims=True)
    xc = x - mu
    var = jnp.mean(xc * xc, -1, keepdims=True)
    return xc * lax.rsqrt(var + LN_EPS) * g + b


def _split(x):
    hi = x.astype(BF16)
    lo = (x - hi.astype(F32)).astype(BF16)
    return hi, lo


_NN = (((1,), (0,)), ((), ()))
_NT = (((1,), (1,)), ((), ()))
_TN = (((0,), (0,)), ((), ()))


def _dot3(a, b, dims=_NN):
    ah, al = _split(a)
    bh, bl = _split(b)
    d = functools.partial(lax.dot_general, dimension_numbers=dims, preferred_element_type=F32)
    return d(ah, bh) + d(ah, bl) + d(al, bh)


def _dot_exact_lhs(a_bf16, b, dims=_NN):
    b1 = b.astype(BF16)
    r1 = b - b1.astype(F32)
    b2 = r1.astype(BF16)
    b3 = (r1 - b2.astype(F32)).astype(BF16)
    d = functools.partial(lax.dot_general, dimension_numbers=dims, preferred_element_type=F32)
    return d(a_bf16, b1) + d(a_bf16, b2) + d(a_bf16, b3)


def _dotb(a, b, dims=_NN):
    return lax.dot_general(a.astype(BF16), b.astype(BF16), dims, preferred_element_type=F32)


def _rows(tb, width, col=0):
    return pl.BlockSpec((None, tb, width), lambda g, i: (g, i, col))


def _mod(mod, tb, col):
    if mod.shape[1] == 1:
        return pl.BlockSpec((None, 1, D_MODEL), lambda g, i: (g, 0, col))
    return pl.BlockSpec((None, tb, D_MODEL), lambda g, i: (g, i, col))


def _const(shape):
    n = len(shape)
    return pl.BlockSpec(shape, lambda g, i: (0,) * n)


def _row2(p):
    return p.reshape(1, -1).astype(F32)


def _mod_kernel(c_ref, w_ref, b_ref, o_ref):
    c = c_ref[...]
    a = c * jax.nn.sigmoid(c)
    o_ref[...] = _dot3(a, w_ref[...]) + b_ref[...]


def _modulation(c, w_ada, b_ada):
    n = c.shape[0]
    tn = D_MODEL
    return pl.pallas_call(
        _mod_kernel,
        grid=(w_ada.shape[1] // tn,),
        in_specs=[pl.BlockSpec((n, D_MODEL), lambda j: (0, 0)),
                  pl.BlockSpec((D_MODEL, tn), lambda j: (0, j)),
                  pl.BlockSpec((1, tn), lambda j: (0, j))],
        out_specs=pl.BlockSpec((n, tn), lambda j: (0, j)),
        out_shape=jax.ShapeDtypeStruct((n, w_ada.shape[1]), F32),
        compiler_params=_cparams("arbitrary"),
        name="modulation",
    )(c, w_ada, b_ada.reshape(1, -1))


def _inproj_kernel(x_ref, sh_ref, sc_ref, g_ref, b_ref, w_ref, ps_ref, pa_ref, pg_ref):
    xn = _layernorm(x_ref[...], g_ref[...], b_ref[...])
    h = (xn * (1.0 + sc_ref[...]) + sh_ref[...]).astype(BF16)
    ps_ref[...] = jnp.dot(h, w_ref[:, :SHIFT_W], preferred_element_type=F32)
    pa_ref[...] = jnp.dot(h, w_ref[:, SHIFT_W:SHIFT_W + ATTN_W], preferred_element_type=F32)
    pg_ref[...] = jnp.dot(h, w_ref[:, SHIFT_W + ATTN_W:], preferred_element_type=F32)


def _inproj(x, mod, ln_g, ln_b, w_in_bf16, tb):
    g, r, _ = x.shape
    shp = lambda w: jax.ShapeDtypeStruct((g, r, w), F32)
    return pl.pallas_call(
        _inproj_kernel,
        grid=(g, r // tb),
        in_specs=[_rows(tb, D_MODEL), _mod(mod, tb, 0), _mod(mod, tb, 1),
                  _const((1, D_MODEL)), _const((1, D_MODEL)), _const((D_MODEL, D_IN))],
        out_specs=[_rows(tb, SHIFT_W), _rows(tb, ATTN_W), _rows(tb, GATE_W)],
        out_shape=[shp(SHIFT_W), shp(ATTN_W), shp(GATE_W)],
        compiler_params=_cparams("parallel", "parallel"),
        name="inproj",
    )(x, mod, mod, _row2(ln_g), _row2(ln_b), w_in_bf16)


def _softplus(x):
    return jnp.maximum(x, 0.0) + jnp.log1p(jnp.exp(-jnp.abs(x)))


def _rwkv_prep_kernel(ps_ref, prev_ref, mu_ref, w0_ref, w2_ref, a0_ref, a2_ref, g2_ref, kk_w_ref, ka_w_ref,
                      hsum_ref, r_ref, lw_ref, k_ref, v_ref, kk_ref, kka_ref, g_ref):
    ps = ps_ref[...]
    xs = ps + (prev_ref[...] - ps) * mu_ref[...]
    r = xs[:, 0:D_A]
    k = xs[:, D_A:2 * D_A]
    v = xs[:, 2 * D_A:3 * D_A]
    wd = xs[:, OFF_WD:OFF_AD]
    ad = xs[:, OFF_AD:OFF_GD]
    gd = xs[:, OFF_GD:SHIFT_W]
    z = w0_ref[...] + _dot3(jnp.tanh(wd), w2_ref[...])
    w_log = -_softplus(-z) - 0.5
    a = jax.nn.sigmoid(a0_ref[...] + _dot3(ad, a2_ref[...]))
    kk = k * kk_w_ref[...]
    ss = _dot3(kk * kk, hsum_ref[...])
    kk = kk / jnp.maximum(jnp.sqrt(ss), 1e-12)
    r_ref[...] = r
    lw_ref[...] = -jnp.exp(w_log)
    k_ref[...] = k * (1.0 + (a - 1.0) * ka_w_ref[...])
    v_ref[...] = v
    kk_ref[...] = kk
    kka_ref[...] = kk * a
    g_ref[...] = _dot3(jax.nn.sigmoid(gd), g2_ref[...])


def _head_sum_matrix():
    h = jnp.arange(D_A) // HD_A
    return (h[:, None] == h[None, :]).astype(F32)


def _rwkv_prep(ps, prev, p, tb):
    g, r, _ = ps.shape
    shp = jax.ShapeDtypeStruct((g, r, D_A), F32)
    return pl.pallas_call(
        _rwkv_prep_kernel,
        grid=(g, r // tb),
        in_specs=[_rows(tb, SHIFT_W), _rows(tb, SHIFT_W), _const((1, SHIFT_W)),
                  _const((1, D_A)), _const((D_LORA_W, D_A)), _const((1, D_A)), _const((D_LORA_A, D_A)),
                  _const((D_LORA_G, D_A)), _const((1, D_A)), _const((1, D_A)), _const((D_A, D_A))],
        out_specs=[_rows(tb, D_A)] * 7,
        out_shape=[shp] * 7,
        compiler_params=_cparams("parallel", "parallel"),
        name="rwkv_prep",
    )(ps, prev, _row2(p["mu_shift"]), _row2(p["rwkv_w0"]), p["rwkv_w2"], _row2(p["rwkv_a0"]), p["rwkv_a2"],
      p["rwkv_g2"], _row2(p["rwkv_k_k"]), _row2(p["rwkv_k_a"]), _head_sum_matrix())


def _rwkv_chunk_kernel(r_ref, lw_ref, k_ref, v_ref, kk_ref, kka_ref, s0_ref, y_ref, s_ref):
    c = RWKV_CHUNK

    @pl.when(pl.program_id(1) == 0)
    def _():
        s_ref[...] = s0_ref[...]

    row = lax.broadcasted_iota(I32, (c, c), 0)
    col = lax.broadcasted_iota(I32, (c, c), 1)
    tril = row >= col
    stril = row > col
    lw = lw_ref[...]
    cum = _dot_exact_lhs(tril.astype(BF16), lw)
    cum_end = cum[c - 1:c, :]
    g_inv = jnp.exp(-cum)
    g_end = jnp.exp(cum_end - cum)
    a_hat = -kk_ref[...] * jnp.exp(cum - lw)
    b_hat = kka_ref[...] * g_inv
    k_hat = k_ref[...] * g_inv
    r_til = r_ref[...] * jnp.exp(cum)
    b_end = kka_ref[...] * g_end
    k_end = k_ref[...] * g_end
    gam_end = jnp.exp(cum_end)
    v_all = v_ref[...]
    s_all = s_ref[...]
    n_steps = int(math.log2(c))
    heads = range(H_A)
    sl = [slice(h * HD_A, (h + 1) * HD_A) for h in heads]
    vh = [v_all[:, sl[h]] for h in heads]
    ar = [jnp.concatenate([a_hat[:, sl[h]], r_til[:, sl[h]]], axis=0) for h in heads]
    bk = [jnp.concatenate([b_hat[:, sl[h]], k_hat[:, sl[h]]], axis=0) for h in heads]
    x = [_dot3(ar[h], bk[h], _NT) for h in heads]
    ars = [_dot3(ar[h], s_all[h], _NT) for h in heads]
    a_ak = [jnp.where(stril, x[h][:c, c:], 0.0) for h in heads]
    n = [jnp.where(stril, x[h][:c, :c], 0.0) for h in heads]
    u = [ars[h][:c] + _dot3(a_ak[h], vh[h]) for h in heads]
    for it in range(n_steps):
        u = [u[h] + _dot3(n[h], u[h]) for h in heads]
        if it + 1 < n_steps:
            n = [_dot3(n[h], n[h]) for h in heads]
    uv = [jnp.concatenate([u[h], vh[h]], axis=0) for h in heads]
    a_r = [jnp.concatenate([jnp.where(tril, x[h][c:, :c], 0.0), jnp.where(tril, x[h][c:, c:], 0.0)], axis=1)
           for h in heads]
    y = [ars[h][c:] + _dot3(a_r[h], uv[h]) for h in heads]
    bke = [jnp.concatenate([b_end[:, sl[h]], k_end[:, sl[h]]], axis=0) for h in heads]
    s_new = [s_all[h] * gam_end[:, sl[h]] + _dot3(uv[h], bke[h], _TN) for h in heads]
    for h in heads:
        y_ref[:, sl[h]] = y[h]
        s_ref[h] = s_new[h]


def _rwkv_chunk_scan(r, lw, k, v, kk, kka, s0):
    n, t, _ = r.shape
    c = RWKV_CHUNK
    seq = pl.BlockSpec((None, c, D_A), lambda b, i: (b, i, 0))
    st = pl.BlockSpec((None, H_A, HD_A, HD_A), lambda b, i: (b, 0, 0, 0))
    return pl.pallas_call(
        _rwkv_chunk_kernel,
        grid=(n, t // c),
        in_specs=[seq] * 6 + [st],
        out_specs=[seq, st],
        out_shape=[jax.ShapeDtypeStruct((n, t, D_A), F32), jax.ShapeDtypeStruct((n, H_A, HD_A, HD_A), F32)],
        compiler_params=_cparams("parallel", "arbitrary"),
        name="rwkv_chunk_scan",
    )(r, lw, k, v, kk, kka, s0)


def _rwkv_step_kernel(s_ref, r_ref, lw_ref, k_ref, kk_ref, kka_ref, v_ref, y_ref, so_ref):
    s = s_ref[...]
    sa = jnp.sum(s * (-kk_ref[...]), axis=-1, keepdims=True)
    s = s * jnp.exp(lw_ref[...]) + sa * kka_ref[...] + v_ref[...] * k_ref[...]
    so_ref[...] = s
    y_ref[...] = jnp.sum(s * r_ref[...], axis=-1, keepdims=True)


def _rwkv_step(s0, r, lw, k, kk, kka, v, nb):
    n = s0.shape[0]
    key = lambda z: z.reshape(n, H_A, 1, HD_A)
    st = pl.BlockSpec((nb, H_A, HD_A, HD_A), lambda i: (i, 0, 0, 0))
    ks = pl.BlockSpec((nb, H_A, 1, HD_A), lambda i: (i, 0, 0, 0))
    vs = pl.BlockSpec((nb, H_A, HD_A, 1), lambda i: (i, 0, 0, 0))
    y, s = pl.pallas_call(
        _rwkv_step_kernel,
        grid=(n // nb,),
        in_specs=[st, ks, ks, ks, ks, ks, vs],
        out_specs=[vs, st],
        out_shape=[jax.ShapeDtypeStruct((n, H_A, HD_A, 1), F32), jax.ShapeDtypeStruct(s0.shape, F32)],
        compiler_params=_cparams("parallel"),
        name="rwkv_step",
    )(s0, key(r), key(lw), key(k), key(kk), key(kka), v.reshape(n, H_A, HD_A, 1))
    return y.reshape(n, D_A), s


def _rwkv_post_kernel(y_ref, r_ref, k_ref, v_ref, g_ref, gn_g_ref, gn_b_ref, rk_ref, hsum_ref, o_ref):
    y = y_ref[...]
    hs = hsum_ref[...]
    mu = _dot3(y, hs) * (1.0 / HD_A)
    yc = y - mu
    var = _dot3(yc * yc, hs) * (1.0 / HD_A)
    yn = yc * lax.rsqrt(var + GN_EPS) * gn_g_ref[...] + gn_b_ref[...]
    bonus = _dot3(r_ref[...] * k_ref[...] * rk_ref[...], hs) * v_ref[...]
    o_ref[...] = (yn + bonus) * g_ref[...]


def _rwkv_post(y, r, k, v, g, p, tb):
    gg, rr, _ = y.shape
    return pl.pallas_call(
        _rwkv_post_kernel,
        grid=(gg, rr // tb),
        in_specs=[_rows(tb, D_A)] * 5 + [_const((1, D_A))] * 3 + [_const((D_A, D_A))],
        out_specs=_rows(tb, D_A),
        out_shape=jax.ShapeDtypeStruct((gg, rr, D_A), F32),
        compiler_params=_cparams("parallel", "parallel"),
        name="rwkv_post",
    )(y, r, k, v, g, _row2(p["rwkv_gn_g"]), _row2(p["rwkv_gn_b"]), _row2(p["rwkv_r_k"]), _head_sum_matrix())


def _sink_softmax(s, sink):
    m = jnp.maximum(jnp.max(s, axis=-1, keepdims=True), sink)
    p = jnp.exp(s - m)
    den = jnp.sum(p, axis=-1, keepdims=True) + jnp.exp(sink - m)
    return p / den


def _attn_band_kernel(cur_ref, prev_ref, carry_ref, sink_ref, o_ref, *, has_carry):
    blk = WINDOW
    i = pl.program_id(1)
    cur = cur_ref[...]
    prev = jnp.where(i == 0, carry_ref[...], prev_ref[...])
    qi = lax.broadcasted_iota(I32, (G_Q * blk, 2 * blk), 0) % blk
    kj = lax.broadcasted_iota(I32, (G_Q * blk, 2 * blk), 1)
    rel = blk + qi - kj
    valid = (rel >= 0) & (rel <= WINDOW)
    if not has_carry:
        valid = valid & ((kj >= blk) | (i > 0))
    relf = rel.astype(F32)
    gidx = lax.broadcasted_iota(I32, (G_Q * blk, 1), 0) // blk
    for kvh in range(H_KV):
        q4 = jnp.concatenate([cur[:, (kvh * G_Q + g) * HD_B:(kvh * G_Q + g + 1) * HD_B] for g in range(G_Q)], axis=0)
        ko = D_B + kvh * HD_B
        vo = D_B + H_KV * HD_B + kvh * HD_B
        kmat = jnp.concatenate([prev[:, ko:ko + HD_B], cur[:, ko:ko + HD_B]], axis=0)
        vmat = jnp.concatenate([prev[:, vo:vo + HD_B], cur[:, vo:vo + HD_B]], axis=0)
        slope = jnp.zeros((G_Q * blk, 1), F32)
        sink = jnp.zeros((G_Q * blk, 1), F32)
        for g in range(G_Q):
            hq = kvh * G_Q + g
            slope = jnp.where(gidx == g, 2.0 ** (-8.0 * (hq + 1) / H_Q), slope)
            sink = jnp.where(gidx == g, sink_ref[hq], sink)
        s = _dotb(q4, kmat, _NT) * (HD_B ** -0.5)
        s = jnp.where(valid, s - slope * relf, NEG_INF)
        p = _sink_softmax(s, sink)
        o = _dotb(p, vmat)
        for g in range(G_Q):
            hq = kvh * G_Q + g
            o_ref[:, hq * HD_B:(hq + 1) * HD_B] = o[g * blk:(g + 1) * blk]


def _attn_band(pattn, carry, sinks):
    n, t, _ = pattn.shape
    blk = WINDOW
    has_carry = carry is not None
    if not has_carry:
        carry = jnp.zeros((n, blk, ATTN_W), F32)
    return pl.pallas_call(
        functools.partial(_attn_band_kernel, has_carry=has_carry),
        grid=(n, t // blk),
        in_specs=[pl.BlockSpec((None, blk, ATTN_W), lambda b, i: (b, i, 0)),
                  pl.BlockSpec((None, blk, ATTN_W), lambda b, i: (b, jnp.maximum(i - 1, 0), 0)),
                  pl.BlockSpec((None, blk, ATTN_W), lambda b, i: (b, 0, 0)),
                  pl.BlockSpec(memory_space=pltpu.SMEM)],
        out_specs=pl.BlockSpec((None, blk, D_B), lambda b, i: (b, i, 0)),
        out_shape=jax.ShapeDtypeStruct((n, t, D_B), F32),
        compiler_params=_cparams("parallel", "parallel"),
        name="attn_band",
    )(pattn, pattn, carry, sinks.astype(F32))


def _attn_cache_kernel(cur_ref, kc_ref, vc_ref, sink_ref, o_ref, *, nb):
    relc = (WINDOW - lax.broadcasted_iota(I32, (G_Q, WINDOW), 1)).astype(F32)
    gidx = lax.broadcasted_iota(I32, (G_Q, 1), 0)
    for b in range(nb):
        cur = cur_ref[b]
        for kvh in range(H_KV):
            q4 = jnp.concatenate([cur[:, (kvh * G_Q + g) * HD_B:(kvh * G_Q + g + 1) * HD_B] for g in range(G_Q)], axis=0)
            ko = D_B + kvh * HD_B
            vo = D_B + H_KV * HD_B + kvh * HD_B
            k_new = cur[:, ko:ko + HD_B]
            v_new = cur[:, vo:vo + HD_B]
            kc = kc_ref[b, :, kvh * HD_B:(kvh + 1) * HD_B]
            vc = vc_ref[b, :, kvh * HD_B:(kvh + 1) * HD_B]
            slope = jnp.zeros((G_Q, 1), F32)
            sink = jnp.zeros((G_Q, 1), F32)
            for g in range(G_Q):
                hq = kvh * G_Q + g
                slope = jnp.where(gidx == g, 2.0 ** (-8.0 * (hq + 1) / H_Q), slope)
                sink = jnp.where(gidx == g, sink_ref[hq], sink)
            scale = HD_B ** -0.5
            sc = _dotb(q4, kc, _NT) * scale - slope * relc
            sn = jnp.sum(q4.astype(BF16).astype(F32) * k_new.astype(BF16).astype(F32), axis=-1, keepdims=True) * scale
            m = jnp.maximum(jnp.maximum(jnp.max(sc, axis=-1, keepdims=True), sn), sink)
            pc = jnp.exp(sc - m)
            pn = jnp.exp(sn - m)
            den = jnp.sum(pc, axis=-1, keepdims=True) + pn + jnp.exp(sink - m)
            o = (_dotb(pc / den, vc) + (pn / den).astype(BF16).astype(F32) * v_new.astype(BF16).astype(F32))
            for g in range(G_Q):
                hq = kvh * G_Q + g
                o_ref[b, :, hq * HD_B:(hq + 1) * HD_B] = o[g:g + 1]


def _attn_cache(pattn, k_buf, v_buf, sinks, nb):
    n = pattn.shape[0]
    kc = k_buf.reshape(n, WINDOW, H_KV * HD_B)
    vc = v_buf.reshape(n, WINDOW, H_KV * HD_B)
    return pl.pallas_call(
        functools.partial(_attn_cache_kernel, nb=nb),
        grid=(n // nb,),
        in_specs=[pl.BlockSpec((nb, 1, ATTN_W), lambda i: (i, 0, 0)),
                  pl.BlockSpec((nb, WINDOW, H_KV * HD_B), lambda i: (i, 0, 0)),
                  pl.BlockSpec((nb, WINDOW, H_KV * HD_B), lambda i: (i, 0, 0)),
                  pl.BlockSpec(memory_space=pltpu.SMEM)],
        out_specs=pl.BlockSpec((nb, 1, D_B), lambda i: (i, 0, 0)),
        out_shape=jax.ShapeDtypeStruct((n, 1, D_B), F32),
        compiler_params=_cparams("parallel"),
        name="attn_cache",
    )(pattn, kc, vc, sinks.astype(F32))


def _merge_kernel(x_ref, ya_ref, ob_ref, pg_ref, gtm_ref, shf_ref, scf_ref, lng_ref, lnb_ref, l1g_ref, l1b_ref,
                  wpa_ref, wpb_ref, wo_ref, wq_ref, sk_ref, x1_ref, st_ref, *, alpha):
    ya = jnp.dot(ya_ref[...].astype(BF16), wpa_ref[...], preferred_element_type=F32)
    yb = jnp.dot(ob_ref[...].astype(BF16), wpb_ref[...], preferred_element_type=F32)
    pg = pg_ref[...]
    merged = jax.nn.sigmoid(pg[:, :D_MODEL]) * ya + jax.nn.sigmoid(pg[:, D_MODEL:]) * yb
    mix = jnp.dot(merged.astype(BF16), wo_ref[...], preferred_element_type=F32)
    xn = _layernorm(x_ref[...], lng_ref[...], lnb_ref[...])
    x1 = _layernorm(alpha * xn + gtm_ref[...] * mix, l1g_ref[...], l1b_ref[...])
    x1_ref[...] = x1
    h2 = x1 * (1.0 + scf_ref[...]) + shf_ref[...]
    q = jnp.dot(h2.astype(BF16), wq_ref[...], preferred_element_type=F32)
    for hc in range(2 * PEER_HEADS):
        st_ref[hc] = _dot3(sk_ref[hc % 2], q[:, hc * PEER_HALF:(hc + 1) * PEER_HALF], _NT)


def _merge(x, ya, ob, pg, mod, p, alpha, tb):
    g, r, _ = x.shape
    return pl.pallas_call(
        functools.partial(_merge_kernel, alpha=alpha),
        grid=(g, r // tb),
        in_specs=[_rows(tb, D_MODEL), _rows(tb, D_A), _rows(tb, D_B), _rows(tb, GATE_W),
                  _mod(mod, tb, 2), _mod(mod, tb, 3), _mod(mod, tb, 4)]
                 + [_const((1, D_MODEL))] * 4
                 + [_const((D_A, D_MODEL)), _const((D_B, D_MODEL)), _const((D_MODEL, D_MODEL)),
                    _const((D_MODEL, 2 * PEER_HEADS * PEER_HALF)), _const((2, N_KEYS, PEER_HALF))],
        out_specs=[_rows(tb, D_MODEL),
                   pl.BlockSpec((None, 2 * PEER_HEADS, N_KEYS, tb), lambda gi, i: (gi, 0, 0, i))],
        out_shape=[jax.ShapeDtypeStruct((g, r, D_MODEL), F32),
                   jax.ShapeDtypeStruct((g, 2 * PEER_HEADS, N_KEYS, r), F32)],
        compiler_params=_cparams("parallel", "parallel"),
        name="merge_ln1_peer_scores",
    )(x, ya, ob, pg, mod, mod, mod, _row2(p["ln_in_g"]), _row2(p["ln_in_b"]), _row2(p["ln1_g"]), _row2(p["ln1_b"]),
      p["w_pa"].astype(BF16), p["w_pb"].astype(BF16), p["w_o"].astype(BF16), p["peer_wq"].astype(BF16),
      p["peer_sub_keys"])


def _extract_top(vals, payload, n_rows, tb):
    rio = lax.broadcasted_iota(I32, (n_rows, tb), 0).astype(F32)
    top_v, top_i, top_p = [], [], []
    for _ in range(PEER_TOPK):
        m = jnp.max(vals, axis=0, keepdims=True)
        i = jnp.min(jnp.where(vals == m, rio, float(n_rows)), axis=0, keepdims=True)
        sel = rio == i
        top_v.append(m)
        top_i.append(i)
        if payload is not None:
            top_p.append(jnp.max(jnp.where(sel, payload, -1.0), axis=0, keepdims=True))
        vals = jnp.where(sel, -jnp.inf, vals)
    cat = lambda z: jnp.concatenate(z, axis=0)
    return cat(top_v), cat(top_i), (cat(top_p) if payload is not None else None)


def _pair_candidates(v1, i1, v2, i2, tb):
    k = PEER_TOPK
    sub = 8
    eid = lambda a0, a1, b0, b1: i1[a0:a1] * float(N_KEYS) + i2[b0:b1]
    vals = [v1[0:1] + v2, v1[1:2] + v2[0:sub]]
    ids = [eid(0, 1, 0, k), eid(1, 2, 0, sub)]
    brow = lax.broadcasted_iota(I32, (sub, tb), 0)
    for a in range(2, sub):
        vals.append(jnp.where(brow < k // (a + 1), v1[a:a + 1] + v2[0:sub], -jnp.inf))
        ids.append(eid(a, a + 1, 0, sub))
    vals.append(v1[sub:k] + v2[0:1])
    ids.append(eid(sub, k, 0, 1))
    return jnp.concatenate(vals, axis=0), jnp.concatenate(ids, axis=0)


def _topk_kernel(s_ref, idx_ref, gate_ref):
    tb = s_ref.shape[-1]

    def head(h, carry):
        v1, i1, _ = _extract_top(s_ref[2 * h], None, N_KEYS, tb)
        v2, i2, _ = _extract_top(s_ref[2 * h + 1], None, N_KEYS, tb)
        cand, eid = _pair_candidates(v1, i1, v2, i2, tb)
        sc, _, ex = _extract_top(cand, eid, cand.shape[0], tb)
        pexp = jnp.exp(sc - sc[0:1])
        idx_ref[h] = ex.astype(I32)
        gate_ref[h] = pexp / jnp.sum(pexp, axis=0, keepdims=True)
        return carry

    lax.fori_loop(0, PEER_HEADS, head, 0)


def _topk(scores_t, tb):
    g, _, _, r = scores_t.shape
    out = pl.BlockSpec((None, PEER_HEADS, PEER_TOPK, tb), lambda gi, i: (gi, 0, 0, i))
    return pl.pallas_call(
        _topk_kernel,
        grid=(g, r // tb),
        in_specs=[pl.BlockSpec((None, 2 * PEER_HEADS, N_KEYS, tb), lambda gi, i: (gi, 0, 0, i))],
        out_specs=[out, out],
        out_shape=[jax.ShapeDtypeStruct((g, PEER_HEADS, PEER_TOPK, r), I32),
                   jax.ShapeDtypeStruct((g, PEER_HEADS, PEER_TOPK, r), F32)],
        compiler_params=_cparams("parallel", "parallel"),
        name="peer_topk",
    )(scores_t)


def _sc_mesh():
    info = plsc.get_sparse_core_info()
    mesh = plsc.VectorSubcoreMesh(core_axis_name="c", subcore_axis_name="s")
    return info.num_cores, info.num_subcores, info.num_lanes, mesh


def _sc_gather_rows(table, idx):
    nc, ns, _, mesh = _sc_mesh()
    nw = nc * ns
    ni = idx.shape[0]
    w = table.shape[1]
    rr = SC_GATHER_ROWS
    per_w = ni // nw
    n_chunks = per_w // rr
    ki = min(SC_IDX_ROWS, n_chunks)
    n_outer = n_chunks // ki
    assert per_w * nw == ni and n_chunks * rr == per_w and n_outer * ki == n_chunks
    nb, ahead = SC_GATHER_BUFS, SC_GATHER_AHEAD
    buf = pltpu.VMEM((rr, w), table.dtype)

    @functools.partial(
        pl.kernel, mesh=mesh, out_type=jax.ShapeDtypeStruct((ni, w), table.dtype),
        scratch_types=[pltpu.VMEM((ki, rr), I32)] + [buf] * nb + [pltpu.SemaphoreType.DMA] * (2 * nb),
        name="peer_row_gather",
    )
    def gather(tab_hbm, idx_hbm, out_hbm, idx_v, *scratch):
        wid = lax.axis_index("s") * nc + lax.axis_index("c")
        base = wid * per_w
        bufs, gsem, wsem = scratch[:nb], scratch[nb:2 * nb], scratch[2 * nb:]

        @pl.loop(0, n_outer)
        def _(o):
            pltpu.sync_copy(idx_hbm.at[wid, pl.ds(o * ki, ki)], idx_v)

            def start_gather(j):
                return pltpu.async_copy(tab_hbm.at[idx_v.at[j]], bufs[j % nb], gsem[j % nb])

            def start_write(j):
                dst = pl.ds(base + (o * ki + j) * rr, rr)
                return pltpu.async_copy(bufs[j % nb], out_hbm.at[dst], wsem[j % nb])

            gathers = {j: start_gather(j) for j in range(min(ahead, ki))}
            writes = {}
            for j in range(ki):
                if j + ahead < ki:
                    if j + ahead - nb >= 0:
                        writes.pop(j + ahead - nb).wait()
                    gathers[j + ahead] = start_gather(j + ahead)
                gathers.pop(j).wait()
                writes[j] = start_write(j)
            for j in sorted(writes):
                writes[j].wait()

    return gather(table, idx.reshape(nw, n_chunks, rr))


def _sc_weighted_row_sum(table, idx, wgt, side_gather=None):
    nc, ns, lanes, mesh = _sc_mesh()
    nw = nc * ns
    n_tok, picks = idx.shape
    ww = table.shape[1]
    rr = SC_GATHER_ROWS
    nq = picks // rr
    tpw = n_tok // nw
    fused = side_gather is not None
    kt = min(SC_ACC_TOKENS_FUSED if fused else SC_ACC_TOKENS, tpw)
    n_outer = tpw // kt
    pw = ww // SC_ACC_PANELS
    nv = pw // lanes
    nb, ahead = SC_GATHER_BUFS, SC_GATHER_AHEAD
    assert tpw * nw == n_tok and n_outer * kt == tpw and nq * rr == picks and nq % 2 == 0 and nv * lanes == pw
    assert nb == nq and ahead < nq
    cp = pltpu.CompilerParams(needs_layout_passes=False)
    buf = pltpu.VMEM((rr, ww), table.dtype)
    out_type = [jax.ShapeDtypeStruct((n_tok, 2 * ww), F32)]
    scratch = [pltpu.VMEM((kt * nq, rr), I32), pltpu.VMEM((kt, picks), F32), pltpu.VMEM((kt, 2 * ww), F32),
               buf, buf, pltpu.SemaphoreType.DMA, pltpu.SemaphoreType.DMA]
    if fused:
        out_type.append(jax.ShapeDtypeStruct((n_tok * picks, ww), table.dtype))
        scratch += [pltpu.VMEM((kt * nq, rr), I32)] + [buf] * nb + [pltpu.SemaphoreType.DMA] * (2 * nb)

    def body(*refs):
        if fused:
            (tab_hbm, idx_hbm, w_hbm, tg_hbm, idxg_hbm, out_hbm, gout_hbm,
             idx_v, w_v, acc_v, rows0, rows1, sem0, sem1, idxg_v, *gs) = refs
            gbufs, ggsem, gwsem = gs[:nb], gs[nb:2 * nb], gs[2 * nb:]
        else:
            tab_hbm, idx_hbm, w_hbm, out_hbm, idx_v, w_v, acc_v, rows0, rows1, sem0, sem1 = refs
        wid = lax.axis_index("s") * nc + lax.axis_index("c")
        rows, sems = (rows0, rows1), (sem0, sem1)

        def chunk_copy(t, q):
            return pltpu.make_async_copy(tab_hbm.at[idx_v.at[t * nq + q]], rows[q % 2], sems[q % 2])

        @pl.loop(0, n_outer)
        def _(o):
            tok0 = wid * tpw + o * kt
            pltpu.sync_copy(idx_hbm.at[pl.ds(tok0 * nq, kt * nq)], idx_v)
            pltpu.sync_copy(w_hbm.at[pl.ds(tok0, kt)], w_v)
            chunk_copy(0, 0).start()
            if fused:
                pltpu.sync_copy(idxg_hbm.at[pl.ds(tok0 * nq, kt * nq)], idxg_v)

                def g_in(t, q):
                    return pltpu.make_async_copy(tg_hbm.at[idxg_v.at[t * nq + q]], gbufs[q], ggsem[q])

                def g_out(t, q):
                    dst = pl.ds((tok0 + t) * picks + q * rr, rr)
                    return pltpu.make_async_copy(gbufs[q], gout_hbm.at[dst], gwsem[q])

                for q in range(ahead):
                    g_in(0, q).start()

            @pl.loop(0, kt)
            def _(t):
                tvec = jnp.full((lanes,), t, I32)
                for q in range(nq):
                    b = q % 2
                    if q + 1 < nq:
                        chunk_copy(t, q + 1).start()
                    else:
                        @pl.when(t + 1 < kt)
                        def _():
                            chunk_copy(t + 1, 0).start()
                    if fused:
                        if q + ahead < nq:
                            @pl.when(t > 0)
                            def _():
                                g_out(t - 1, q + ahead).wait()
                            g_in(t, q + ahead).start()
                        else:
                            @pl.when(t + 1 < kt)
                            def _():
                                g_out(t, q + ahead - nq).wait()
                                g_in(t + 1, q + ahead - nq).start()
                        g_in(t, q).wait()
                        g_out(t, q).start()
                    chunk_copy(t, q).wait()
                    for pan in range(SC_ACC_PANELS):
                        lo_at = lambda c: pl.ds(pan * pw + c * lanes, lanes)
                        hi_at = lambda c: pl.ds(ww + pan * pw + c * lanes, lanes)

                        def row_body(j, acc):
                            wj = plsc.load_gather(w_v, [tvec, jnp.full((lanes,), q * rr, I32) + j])
                            new = []
                            for c in range(nv):
                                word = rows[b][j, lo_at(c)]
                                new.append(acc[2 * c] + wj * plsc.bitcast(word << 16, F32))
                                new.append(acc[2 * c + 1] + wj * plsc.bitcast(word & jnp.uint32(0xFFFF0000), F32))
                            return tuple(new)

                        if q == 0:
                            init = tuple(jnp.zeros((lanes,), F32) for _ in range(2 * nv))
                        else:
                            init = tuple(acc_v[t, at(c)] for c in range(nv) for at in (lo_at, hi_at))
                        acc = lax.fori_loop(0, rr, row_body, init)
                        for c in range(nv):
                            acc_v[t, lo_at(c)] = acc[2 * c]
                            acc_v[t, hi_at(c)] = acc[2 * c + 1]

            if fused:
                for q in range(nq):
                    g_out(kt - 1, q).wait()
            pltpu.sync_copy(acc_v, out_hbm.at[pl.ds(tok0, kt)])

    kern = pl.kernel(body, mesh=mesh, out_type=tuple(out_type) if fused else out_type[0], scratch_types=scratch,
                     compiler_params=cp, name="peer_weighted_row_sum_gather" if fused else "peer_weighted_row_sum")
    if fused:
        table_g, idx_g = side_gather
        return kern(table, idx.reshape(n_tok * nq, rr), wgt, table_g, idx_g.reshape(n_tok * nq, rr))
    return kern(table, idx.reshape(n_tok * nq, rr), wgt)


def _pack_bf16_pairs(t):
    half = t.shape[1] // 2
    b = lax.bitcast_convert_type(t.astype(BF16), jnp.uint16).astype(U32)
    return b[:, :half] | (b[:, half:] << 16)


def _gelu_erf(x):
    return 0.5 * x * (1.0 + lax.erf(x * (2.0 ** -0.5)))


def _unpack_pairs(words):
    lo = pltpu.bitcast(words << 16, F32)
    hi = pltpu.bitcast(words & jnp.uint32(0xFFFF0000), F32)
    return lo, hi


def _peer_hidden_kernel(gu_ref, gate_ref, x1_ref, shf_ref, scf_ref, w_ref, hd_ref, *, tb):
    half = D_MODEL // 2
    h2 = x1_ref[...] * (1.0 + scf_ref[...]) + shf_ref[...]
    for t in range(tb):
        ulo, uhi = _unpack_pairs(gu_ref[t * PEER_PICKS:(t + 1) * PEER_PICKS, :])
        hd_ref[:, t:t + 1] = jnp.sum(ulo * h2[t:t + 1, :half] + uhi * h2[t:t + 1, half:], axis=-1, keepdims=True)
    w_ref[...] = _gelu_erf(hd_ref[...]) * gate_ref[...]


def _peer_hidden(gu, gate_blocks, x1, mod, tb):
    r = x1.shape[1]
    blk = pl.BlockSpec((None, PEER_PICKS, tb), lambda gi, i: (i, 0, 0))
    return pl.pallas_call(
        functools.partial(_peer_hidden_kernel, tb=tb),
        grid=(1, r // tb),
        in_specs=[pl.BlockSpec((tb * PEER_PICKS, D_MODEL // 2), lambda gi, i: (i, 0)), blk,
                  _rows(tb, D_MODEL), _mod(mod, tb, 3), _mod(mod, tb, 4)],
        out_specs=blk,
        out_shape=jax.ShapeDtypeStruct((r // tb, PEER_PICKS, tb), F32),
        scratch_shapes=[pltpu.VMEM((PEER_PICKS, tb), F32)],
        compiler_params=_cparams("parallel", "parallel"),
        name="peer_hidden",
    )(gu, gate_blocks, x1, mod, mod)


def _peer_out_kernel(x1_ref, ff_ref, gtf_ref, l2g_ref, l2b_ref, o_ref, *, alpha):
    o_ref[...] = _layernorm(alpha * x1_ref[...] + gtf_ref[...] * ff_ref[...], l2g_ref[...], l2b_ref[...])


def _peer_out(x1, ff, mod, p, alpha, tb):
    g, r, _ = x1.shape
    return pl.pallas_call(
        functools.partial(_peer_out_kernel, alpha=alpha),
        grid=(g, r // tb),
        in_specs=[_rows(tb, D_MODEL), _rows(tb, D_MODEL), _mod(mod, tb, 5), _const((1, D_MODEL)), _const((1, D_MODEL))],
        out_specs=_rows(tb, D_MODEL),
        out_shape=jax.ShapeDtypeStruct((g, r, D_MODEL), F32),
        compiler_params=_cparams("parallel", "parallel"),
        name="peer_out_ln2",
    )(x1, ff, mod, _row2(p["ln2_g"]), _row2(p["ln2_b"]))


def _token_stage(x, mod, prev_fn, wkv_fn, attn_fn, p, alpha, tb):
    ps, pattn, pgate = _inproj(x, mod, p["ln_in_g"], p["ln_in_b"], p["w_in_bf16"], tb)
    r, lw, k, v, kk, kka, gl = _rwkv_prep(ps, prev_fn(ps), p, tb)
    y, wkv_new = wkv_fn(r, lw, k, v, kk, kka)
    ya = _rwkv_post(y, r, k, v, gl, p, tb)
    ob = attn_fn(pattn)
    x1, scores_t = _merge(x, ya, ob, pgate, mod, p, alpha, tb)
    idx_t, gate_t = _topk(scores_t, TOPK_TB)
    return ps, pattn, wkv_new, x1, idx_t, gate_t


def kernel(x_prompt, x_sample, state_wkv, state_shift, cache_k_win, cache_v_win, c_prompt, c_sample, ln_in_g, ln_in_b, w_ada, b_ada, w_in, mu_shift, rwkv_w0, rwkv_w2, rwkv_a0, rwkv_a2, rwkv_g2, rwkv_k_k, rwkv_k_a, rwkv_r_k, rwkv_gn_g, rwkv_gn_b, attn_sinks, w_pa, w_pb, w_o, ln1_g, ln1_b, peer_wq, peer_sub_keys, peer_u, peer_v, ln2_g, ln2_b):
    depth = w_in.shape[0]
    assert depth == 1, "single-layer trunk"
    alpha = (2.0 * depth) ** 0.25
    n_p, t_p, _ = x_prompt.shape
    n_s = x_sample.shape[0]
    p = dict(ln_in_g=ln_in_g, ln_in_b=ln_in_b, w_in_bf16=w_in[0].astype(BF16), mu_shift=mu_shift[0],
             rwkv_w0=rwkv_w0[0], rwkv_w2=rwkv_w2[0], rwkv_a0=rwkv_a0[0], rwkv_a2=rwkv_a2[0], rwkv_g2=rwkv_g2[0],
             rwkv_k_k=rwkv_k_k[0], rwkv_k_a=rwkv_k_a[0], rwkv_r_k=rwkv_r_k[0], rwkv_gn_g=rwkv_gn_g[0],
             rwkv_gn_b=rwkv_gn_b[0], w_pa=w_pa[0], w_pb=w_pb[0], w_o=w_o[0], ln1_g=ln1_g[0], ln1_b=ln1_b[0],
             peer_wq=peer_wq[0], peer_sub_keys=peer_sub_keys[0], ln2_g=ln2_g[0], ln2_b=ln2_b[0])
    sinks = attn_sinks[0]

    n_c = n_p + n_s
    pad = (-n_c) % 8
    c_all = jnp.concatenate([c_prompt, c_sample, jnp.zeros((pad, D_MODEL), F32)], axis=0)
    mod_all = _modulation(c_all, w_ada[0], b_ada[0])
    mod_p = mod_all[:n_p].reshape(n_p, 1, N_MOD * D_MODEL)
    mod_s = mod_all[n_p:n_c].reshape(1, n_s, N_MOD * D_MODEL)

    seg = min(PROMPT_SEGMENT, t_p)
    n_seg = t_p // seg
    assert n_seg * seg == t_p
    carry = {}

    def prompt_group(b, s):
        def prev_fn(ps):
            first = carry[b][0] if s > 0 else jnp.zeros((1, 1, SHIFT_W), F32)
            return jnp.concatenate([first, ps[:, :-1]], axis=1)

        def wkv_fn(r, lw, k, v, kk, kka):
            s0 = carry[b][1] if s > 0 else jnp.zeros((1, H_A, HD_A, HD_A), F32)
            return _rwkv_chunk_scan(r, lw, k, v, kk, kka, s0)

        def attn_fn(pa):
            return _attn_band(pa, carry[b][2] if s > 0 else None, sinks)

        return (x_prompt[b:b + 1, s * seg:(s + 1) * seg], mod_p[b:b + 1], prev_fn, wkv_fn, attn_fn, TOKEN_TB)

    tu, tv = _pack_bf16_pairs(peer_u[0]), _pack_bf16_pairs(peer_v[0])

    def select(x, mod, prev_fn, wkv_fn, attn_fn, tb):
        ps, pattn, wkv_new, x1, idx_t, gate_t = _token_stage(x, mod, prev_fn, wkv_fn, attn_fn, p, alpha, tb)
        r = x1.shape[1]
        idx = jnp.transpose(idx_t.reshape(PEER_PICKS, r))
        gate_blocks = jnp.transpose(gate_t.reshape(PEER_PICKS, r // PEER_TB, PEER_TB), (1, 0, 2))
        return ps, pattn, wkv_new, dict(idx=idx, gate_blocks=gate_blocks, x1=x1, mod=mod)

    def pick_weights(g):
        g["w_blocks"] = _peer_hidden(g["gu"], g["gate_blocks"], g["x1"], g["mod"], PEER_TB)
        return jnp.transpose(g["w_blocks"], (0, 2, 1)).reshape(-1, PEER_PICKS)

    def finish(g):
        return _peer_out(g["x1"], g["ff"][None], g["mod"], p, alpha, min(TOKEN_TB, g["x1"].shape[1]))

    xs = x_sample.reshape(1, n_s, D_MODEL)

    def prev_s(ps):
        return state_shift[0].reshape(1, n_s, SHIFT_W)

    def wkv_s(r, lw, k, v, kk, kka):
        sq = lambda z: z.reshape(n_s, D_A)
        y, s = _rwkv_step(state_wkv[0], sq(r), sq(lw), sq(k), sq(kk), sq(kka), sq(v), STEP_NB)
        return y.reshape(1, n_s, D_A), s

    def attn_s(pa):
        o = _attn_cache(pa.reshape(n_s, 1, ATTN_W), cache_k_win[0], cache_v_win[0], sinks, STEP_NB)
        return o.reshape(1, n_s, D_B)

    prompt_ids = [(b, s) for b in range(n_p) for s in range(n_seg)]
    n_pg = len(prompt_ids)
    groups = []
    for k, (b, s) in enumerate(prompt_ids):
        x, mod, prev_fn, wkv_fn, attn_fn, tb = prompt_group(b, s)
        if k >= 3:
            x, _ = lax.optimization_barrier((x, groups[k - 3]["w_blocks"]))
        ps, pattn, wkv_new, g = select(x, mod, prev_fn, wkv_fn, attn_fn, tb)
        carry[b] = (ps[:, -1:], wkv_new, pattn[:, -WINDOW:])
        groups.append(g)
        if k < 2:
            g["gu"] = _sc_gather_rows(tu, g["idx"].reshape(-1))
        else:
            j = groups[k - 2]
            j["ff"], g["gu"] = _sc_weighted_row_sum(tv, j["idx"], pick_weights(j), (tu, g["idx"]))
    for j in groups[max(n_pg - 2, 0):]:
        j["ff"] = _sc_weighted_row_sum(tv, j["idx"], pick_weights(j))
    ps_s, pattn_s, wkv_s_new, gs = select(xs, mod_s, prev_s, wkv_s, attn_s, min(TOKEN_TB, n_s))
    gs["gu"] = _sc_gather_rows(tu, gs["idx"].reshape(-1))
    gs["ff"] = _sc_weighted_row_sum(tv, gs["idx"], pick_weights(gs))
    for k, g in enumerate(groups):
        if k + 2 < n_pg:
            g["ff"], _ = lax.optimization_barrier((g["ff"], groups[k + 2]["w_blocks"]))
    y_s = finish(gs)
    y_p = jnp.concatenate([finish(g) for g in groups], axis=1).reshape(n_p, t_p, D_MODEL)
    shift_p = jnp.concatenate([carry[b][0][:, 0] for b in range(n_p)], axis=0)
    pattn_p = jnp.concatenate([carry[b][2] for b in range(n_p)], axis=0)
    wkv_p_new = jnp.concatenate([carry[b][1] for b in range(n_p)], axis=0)

    kv = lambda pa, o: pa[..., o:o + H_KV * HD_B]
    ko, vo = D_B, D_B + H_KV * HD_B
    k_win_p = kv(pattn_p, ko)[:, -WINDOW:].reshape(n_p, WINDOW, H_KV, HD_B)
    v_win_p = kv(pattn_p, vo)[:, -WINDOW:].reshape(n_p, WINDOW, H_KV, HD_B)
    k_new_s = kv(pattn_s, ko).reshape(n_s, 1, H_KV, HD_B)
    v_new_s = kv(pattn_s, vo).reshape(n_s, 1, H_KV, HD_B)
    k_win_s = jnp.concatenate([cache_k_win[0], k_new_s], axis=1)[:, -WINDOW:]
    v_win_s = jnp.concatenate([cache_v_win[0], v_new_s], axis=1)[:, -WINDOW:]
    return (y_p, y_s.reshape(n_s, 1, D_MODEL), wkv_p_new[None], wkv_s_new[None],
            shift_p[None], ps_s.reshape(n_s, SHIFT_W)[None],
            k_win_p[None], k_win_s[None], v_win_p[None], v_win_s[None])
```

```python
import functools
import math

import jax
import jax.numpy as jnp
from jax import lax
from jax.experimental import pallas as pl
from jax.experimental.pallas import tpu as pltpu
from jax.experimental.pallas import tpu_sc as plsc

F32 = jnp.float32
BF16 = jnp.bfloat16
I32 = jnp.int32
U32 = jnp.uint32

D_MODEL = 1024
H_A, HD_A = 8, 64
D_A = H_A * HD_A
D_LORA_W, D_LORA_A, D_LORA_G = 64, 64, 128
GN_EPS = 64e-5
H_Q, H_KV, HD_B = 8, 2, 64
G_Q = H_Q // H_KV
D_B = H_Q * HD_B
WINDOW = 128
N_KEYS = 128
PEER_HEADS, PEER_TOPK, PEER_HALF = 8, 16, 128
PEER_PICKS = PEER_HEADS * PEER_TOPK
N_MOD = 6
LN_EPS = 1e-5
NEG_INF = -1e30
OFF_WD = 3 * D_A
OFF_AD = OFF_WD + D_LORA_W
OFF_GD = OFF_AD + D_LORA_A
SHIFT_W = OFF_GD + D_LORA_G
ATTN_W = D_B + 2 * H_KV * HD_B
GATE_W = 2 * D_MODEL
D_IN = SHIFT_W + ATTN_W + GATE_W

VMEM_LIMIT = 48 * 1024 * 1024
SUBLANES = 8
RWKV_CHUNK = 64
SC_GATHER_ROWS = 32
SC_IDX_ROWS = 32
SC_GATHER_BUFS = 4
SC_GATHER_AHEAD = 2
SC_ACC_TOKENS = 16
SC_ACC_PANELS = 4
TOKEN_TB = 256
TOPK_TB = 128
PEER_TB = 16
PROMPT_SEGMENT = 2048
STEP_NB = 8


def _cparams(*sem):
    return pltpu.CompilerParams(dimension_semantics=sem, vmem_limit_bytes=VMEM_LIMIT)


def _layernorm(x, g, b):
    mu = jnp.mean(x, -1, keepdims=True)
    xc = x - mu
    var = jnp.mean(xc * xc, -1, keepdims=True)
    return xc * lax.rsqrt(var + LN_EPS) * g + b


def _split(x):
    hi = x.astype(BF16)
    lo = (x - hi.astype(F32)).astype(BF16)
    return hi, lo


_NN = (((1,), (0,)), ((), ()))
_NT = (((1,), (1,)), ((), ()))
_TN = (((0,), (0,)), ((), ()))


def _dot3(a, b, dims=_NN):
    ah, al = _split(a)
    bh, bl = _split(b)
    d = functools.partial(lax.dot_general, dimension_numbers=dims, preferred_element_type=F32)
    return d(ah, bh) + d(ah, bl) + d(al, bh)


def _dot_exact_lhs(a_bf16, b, dims=_NN):
    b1 = b.astype(BF16)
    r1 = b - b1.astype(F32)
    b2 = r1.astype(BF16)
    b3 = (r1 - b2.astype(F32)).astype(BF16)
    d = functools.partial(lax.dot_general, dimension_numbers=dims, preferred_element_type=F32)
    return d(a_bf16, b1) + d(a_bf16, b2) + d(a_bf16, b3)


def _dotb(a, b, dims=_NN):
    return lax.dot_general(a.astype(BF16), b.astype(BF16), dims, preferred_element_type=F32)


def _rows(tb, width, col=0):
    return pl.BlockSpec((None, tb, width), lambda g, i: (g, i, col))


def _mod(mod, tb, col):
    if mod.shape[1] == 1:
        return pl.BlockSpec((None, 1, D_MODEL), lambda g, i: (g, 0, col))
    return pl.BlockSpec((None, tb, D_MODEL), lambda g, i: (g, i, col))


def _const(shape):
    n = len(shape)
    return pl.BlockSpec(shape, lambda g, i: (0,) * n)


def _row2(p):
    return p.reshape(1, -1).astype(F32)


def _mod_kernel(c_ref, w_ref, b_ref, o_ref):
    c = c_ref[...]
    a = c * jax.nn.sigmoid(c)
    o_ref[...] = _dot3(a, w_ref[...]) + b_ref[...]


def _modulation(c, w_ada, b_ada):
    n = c.shape[0]
    tn = D_MODEL
    return pl.pallas_call(
        _mod_kernel,
        grid=(w_ada.shape[1] // tn,),
        in_specs=[pl.BlockSpec((n, D_MODEL), lambda j: (0, 0)),
                  pl.BlockSpec((D_MODEL, tn), lambda j: (0, j)),
                  pl.BlockSpec((1, tn), lambda j: (0, j))],
        out_specs=pl.BlockSpec((n, tn), lambda j: (0, j)),
        out_shape=jax.ShapeDtypeStruct((n, w_ada.shape[1]), F32),
        compiler_params=_cparams("arbitrary"),
        name="modulation",
    )(c, w_ada, b_ada.reshape(1, -1))


def _inproj_kernel(x_ref, sh_ref, sc_ref, g_ref, b_ref, w_ref, ps_ref, pa_ref, pg_ref):
    xn = _layernorm(x_ref[...], g_ref[...], b_ref[...])
    h = (xn * (1.0 + sc_ref[...]) + sh_ref[...]).astype(BF16)
    ps_ref[...] = jnp.dot(h, w_ref[:, :SHIFT_W], preferred_element_type=F32)
    pa_ref[...] = jnp.dot(h, w_ref[:, SHIFT_W:SHIFT_W + ATTN_W], preferred_element_type=F32)
    pg_ref[...] = jnp.dot(h, w_ref[:, SHIFT_W + ATTN_W:], preferred_element_type=F32)


def _inproj(x, mod, ln_g, ln_b, w_in_bf16, tb):
    g, r, _ = x.shape
    shp = lambda w: jax.ShapeDtypeStruct((g, r, w), F32)
    return pl.pallas_call(
        _inproj_kernel,
        grid=(g, r // tb),
        in_specs=[_rows(tb, D_MODEL), _mod(mod, tb, 0), _mod(mod, tb, 1),
                  _const((1, D_MODEL)), _const((1, D_MODEL)), _const((D_MODEL, D_IN))],
        out_specs=[_rows(tb, SHIFT_W), _rows(tb, ATTN_W), _rows(tb, GATE_W)],
        out_shape=[shp(SHIFT_W), shp(ATTN_W), shp(GATE_W)],
        compiler_params=_cparams("parallel", "parallel"),
        name="inproj",
    )(x, mod, mod, _row2(ln_g), _row2(ln_b), w_in_bf16)


def _softplus(x):
    return jnp.maximum(x, 0.0) + jnp.log1p(jnp.exp(-jnp.abs(x)))


def _rwkv_prep_kernel(ps_ref, prev_ref, first_ref, mu_ref, w0_ref, w2_ref, a0_ref, a2_ref, g2_ref, kk_w_ref, ka_w_ref,
                      hsum_ref, r_ref, lw_ref, k_ref, v_ref, kk_ref, kka_ref, g_ref, *, sequential):
    ps = ps_ref[...]
    if sequential:
        before = jnp.where(pl.program_id(1) == 0, first_ref[...], prev_ref[SUBLANES - 1:SUBLANES, :])
        row = lax.broadcasted_iota(I32, ps.shape, 0)
        prev = jnp.where(row == 0, before, pltpu.roll(ps, 1, 0))
    else:
        prev = prev_ref[...]
    xs = ps + (prev - ps) * mu_ref[...]
    r = xs[:, 0:D_A]
    k = xs[:, D_A:2 * D_A]
    v = xs[:, 2 * D_A:3 * D_A]
    wd = xs[:, OFF_WD:OFF_AD]
    ad = xs[:, OFF_AD:OFF_GD]
    gd = xs[:, OFF_GD:SHIFT_W]
    z = w0_ref[...] + _dot3(jnp.tanh(wd), w2_ref[...])
    w_log = -_softplus(-z) - 0.5
    a = jax.nn.sigmoid(a0_ref[...] + _dot3(ad, a2_ref[...]))
    kk = k * kk_w_ref[...]
    ss = _dot3(kk * kk, hsum_ref[...])
    kk = kk / jnp.maximum(jnp.sqrt(ss), 1e-12)
    r_ref[...] = r
    lw_ref[...] = -jnp.exp(w_log)
    k_ref[...] = k * (1.0 + (a - 1.0) * ka_w_ref[...])
    v_ref[...] = v
    kk_ref[...] = kk
    kka_ref[...] = kk * a
    g_ref[...] = _dot3(jax.nn.sigmoid(gd), g2_ref[...])


def _head_sum_matrix():
    h = jnp.arange(D_A) // HD_A
    return (h[:, None] == h[None, :]).astype(F32)


def _rwkv_prep(ps, prev, p, tb, sequential):
    g, r, _ = ps.shape
    shp = jax.ShapeDtypeStruct((g, r, D_A), F32)
    if sequential:
        per = tb // SUBLANES
        prev_args = (ps, prev)
        prev_specs = [pl.BlockSpec((None, SUBLANES, SHIFT_W), lambda gi, i: (gi, jnp.maximum(i * per - 1, 0), 0)),
                      pl.BlockSpec((None, 1, SHIFT_W), lambda gi, i: (gi, 0, 0))]
    else:
        prev_args = (prev, prev[:, :1])
        prev_specs = [_rows(tb, SHIFT_W), pl.BlockSpec((None, 1, SHIFT_W), lambda gi, i: (gi, 0, 0))]
    return pl.pallas_call(
        functools.partial(_rwkv_prep_kernel, sequential=sequential),
        grid=(g, r // tb),
        in_specs=[_rows(tb, SHIFT_W)] + prev_specs + [_const((1, SHIFT_W)),
                  _const((1, D_A)), _const((D_LORA_W, D_A)), _const((1, D_A)), _const((D_LORA_A, D_A)),
                  _const((D_LORA_G, D_A)), _const((1, D_A)), _const((1, D_A)), _const((D_A, D_A))],
        out_specs=[_rows(tb, D_A)] * 7,
        out_shape=[shp] * 7,
        compiler_params=_cparams("parallel", "parallel"),
        name="rwkv_prep",
    )(ps, *prev_args, _row2(p["mu_shift"]), _row2(p["rwkv_w0"]), p["rwkv_w2"], _row2(p["rwkv_a0"]), p["rwkv_a2"],
      p["rwkv_g2"], _row2(p["rwkv_k_k"]), _row2(p["rwkv_k_a"]), _head_sum_matrix())


def _rwkv_chunk_kernel(r_ref, lw_ref, k_ref, v_ref, kk_ref, kka_ref, s0_ref, y_ref, s_ref):
    c = RWKV_CHUNK

    @pl.when(pl.program_id(1) == 0)
    def _():
        s_ref[...] = s0_ref[...]

    row = lax.broadcasted_iota(I32, (c, c), 0)
    col = lax.broadcasted_iota(I32, (c, c), 1)
    tril = row >= col
    stril = row > col
    lw = lw_ref[...]
    cum = _dot_exact_lhs(tril.astype(BF16), lw)
    cum_end = cum[c - 1:c, :]
    g_inv = jnp.exp(-cum)
    g_end = jnp.exp(cum_end - cum)
    a_hat = -kk_ref[...] * jnp.exp(cum - lw)
    b_hat = kka_ref[...] * g_inv
    k_hat = k_ref[...] * g_inv
    r_til = r_ref[...] * jnp.exp(cum)
    b_end = kka_ref[...] * g_end
    k_end = k_ref[...] * g_end
    gam_end = jnp.exp(cum_end)
    v_all = v_ref[...]
    s_all = s_ref[...]
    n_steps = int(math.log2(c))
    heads = range(H_A)
    sl = [slice(h * HD_A, (h + 1) * HD_A) for h in heads]
    vh = [v_all[:, sl[h]] for h in heads]
    ar = [jnp.concatenate([a_hat[:, sl[h]], r_til[:, sl[h]]], axis=0) for h in heads]
    bk = [jnp.concatenate([b_hat[:, sl[h]], k_hat[:, sl[h]]], axis=0) for h in heads]
    x = [_dot3(ar[h], bk[h], _NT) for h in heads]
    ars = [_dot3(ar[h], s_all[h], _NT) for h in heads]
    a_ak = [jnp.where(stril, x[h][:c, c:], 0.0) for h in heads]
    n = [jnp.where(stril, x[h][:c, :c], 0.0) for h in heads]
    u = [ars[h][:c] + _dot3(a_ak[h], vh[h]) for h in heads]
    for it in range(n_steps):
        u = [u[h] + _dot3(n[h], u[h]) for h in heads]
        if it + 1 < n_steps:
            n = [_dot3(n[h], n[h]) for h in heads]
    uv = [jnp.concatenate([u[h], vh[h]], axis=0) for h in heads]
    a_r = [jnp.concatenate([jnp.where(tril, x[h][c:, :c], 0.0), jnp.where(tril, x[h][c:, c:], 0.0)], axis=1)
           for h in heads]
    y = [ars[h][c:] + _dot3(a_r[h], uv[h]) for h in heads]
    bke = [jnp.concatenate([b_end[:, sl[h]], k_end[:, sl[h]]], axis=0) for h in heads]
    s_new = [s_all[h] * gam_end[:, sl[h]] + _dot3(uv[h], bke[h], _TN) for h in heads]
    for h in heads:
        y_ref[:, sl[h]] = y[h]
        s_ref[h] = s_new[h]


def _rwkv_chunk_scan(r, lw, k, v, kk, kka, s0):
    n, t, _ = r.shape
    c = RWKV_CHUNK
    seq = pl.BlockSpec((None, c, D_A), lambda b, i: (b, i, 0))
    st = pl.BlockSpec((None, H_A, HD_A, HD_A), lambda b, i: (b, 0, 0, 0))
    return pl.pallas_call(
        _rwkv_chunk_kernel,
        grid=(n, t // c),
        in_specs=[seq] * 6 + [st],
        out_specs=[seq, st],
        out_shape=[jax.ShapeDtypeStruct((n, t, D_A), F32), jax.ShapeDtypeStruct((n, H_A, HD_A, HD_A), F32)],
        compiler_params=_cparams("parallel", "arbitrary"),
        name="rwkv_chunk_scan",
    )(r, lw, k, v, kk, kka, s0)


def _rwkv_step_kernel(s_ref, r_ref, lw_ref, k_ref, kk_ref, kka_ref, v_ref, y_ref, so_ref):
    s = s_ref[...]
    sa = jnp.sum(s * (-kk_ref[...]), axis=-1, keepdims=True)
    s = s * jnp.exp(lw_ref[...]) + sa * kka_ref[...] + v_ref[...] * k_ref[...]
    so_ref[...] = s
    y_ref[...] = jnp.sum(s * r_ref[...], axis=-1, keepdims=True)


def _rwkv_step(s0, r, lw, k, kk, kka, v, nb):
    n = s0.shape[0]
    key = lambda z: z.reshape(n, H_A, 1, HD_A)
    st = pl.BlockSpec((nb, H_A, HD_A, HD_A), lambda i: (i, 0, 0, 0))
    ks = pl.BlockSpec((nb, H_A, 1, HD_A), lambda i: (i, 0, 0, 0))
    vs = pl.BlockSpec((nb, H_A, HD_A, 1), lambda i: (i, 0, 0, 0))
    y, s = pl.pallas_call(
        _rwkv_step_kernel,
        grid=(n // nb,),
        in_specs=[st, ks, ks, ks, ks, ks, vs],
        out_specs=[vs, st],
        out_shape=[jax.ShapeDtypeStruct((n, H_A, HD_A, 1), F32), jax.ShapeDtypeStruct(s0.shape, F32)],
        compiler_params=_cparams("parallel"),
        name="rwkv_step",
    )(s0, key(r), key(lw), key(k), key(kk), key(kka), v.reshape(n, H_A, HD_A, 1))
    return y.reshape(n, D_A), s


def _rwkv_post_kernel(y_ref, r_ref, k_ref, v_ref, g_ref, gn_g_ref, gn_b_ref, rk_ref, hsum_ref, o_ref):
    y = y_ref[...]
    hs = hsum_ref[...]
    mu = _dot3(y, hs) * (1.0 / HD_A)
    yc = y - mu
    var = _dot3(yc * yc, hs) * (1.0 / HD_A)
    yn = yc * lax.rsqrt(var + GN_EPS) * gn_g_ref[...] + gn_b_ref[...]
    bonus = _dot3(r_ref[...] * k_ref[...] * rk_ref[...], hs) * v_ref[...]
    o_ref[...] = (yn + bonus) * g_ref[...]


def _rwkv_post(y, r, k, v, g, p, tb):
    gg, rr, _ = y.shape
    return pl.pallas_call(
        _rwkv_post_kernel,
        grid=(gg, rr // tb),
        in_specs=[_rows(tb, D_A)] * 5 + [_const((1, D_A))] * 3 + [_const((D_A, D_A))],
        out_specs=_rows(tb, D_A),
        out_shape=jax.ShapeDtypeStruct((gg, rr, D_A), F32),
        compiler_params=_cparams("parallel", "parallel"),
        name="rwkv_post",
    )(y, r, k, v, g, _row2(p["rwkv_gn_g"]), _row2(p["rwkv_gn_b"]), _row2(p["rwkv_r_k"]), _head_sum_matrix())


def _sink_softmax(s, sink):
    m = jnp.maximum(jnp.max(s, axis=-1, keepdims=True), sink)
    p = jnp.exp(s - m)
    den = jnp.sum(p, axis=-1, keepdims=True) + jnp.exp(sink - m)
    return p / den


def _attn_band_kernel(cur_ref, prev_ref, carry_ref, sink_ref, o_ref, *, has_carry):
    blk = WINDOW
    i = pl.program_id(1)
    cur = cur_ref[...]
    prev = jnp.where(i == 0, carry_ref[...], prev_ref[...])
    qi = lax.broadcasted_iota(I32, (G_Q * blk, 2 * blk), 0) % blk
    kj = lax.broadcasted_iota(I32, (G_Q * blk, 2 * blk), 1)
    rel = blk + qi - kj
    valid = (rel >= 0) & (rel <= WINDOW)
    if not has_carry:
        valid = valid & ((kj >= blk) | (i > 0))
    relf = rel.astype(F32)
    gidx = lax.broadcasted_iota(I32, (G_Q * blk, 1), 0) // blk
    for kvh in range(H_KV):
        q4 = jnp.concatenate([cur[:, (kvh * G_Q + g) * HD_B:(kvh * G_Q + g + 1) * HD_B] for g in range(G_Q)], axis=0)
        ko = D_B + kvh * HD_B
        vo = D_B + H_KV * HD_B + kvh * HD_B
        kmat = jnp.concatenate([prev[:, ko:ko + HD_B], cur[:, ko:ko + HD_B]], axis=0)
        vmat = jnp.concatenate([prev[:, vo:vo + HD_B], cur[:, vo:vo + HD_B]], axis=0)
        slope = jnp.zeros((G_Q * blk, 1), F32)
        sink = jnp.zeros((G_Q * blk, 1), F32)
        for g in range(G_Q):
            hq = kvh * G_Q + g
            slope = jnp.where(gidx == g, 2.0 ** (-8.0 * (hq + 1) / H_Q), slope)
            sink = jnp.where(gidx == g, sink_ref[hq], sink)
        s = _dotb(q4, kmat, _NT) * (HD_B ** -0.5)
        s = jnp.where(valid, s - slope * relf, NEG_INF)
        p = _sink_softmax(s, sink)
        o = _dotb(p, vmat)
        for g in range(G_Q):
            hq = kvh * G_Q + g
            o_ref[:, hq * HD_B:(hq + 1) * HD_B] = o[g * blk:(g + 1) * blk]


def _attn_band(pattn, carry, sinks):
    n, t, _ = pattn.shape
    blk = WINDOW
    has_carry = carry is not None
    if not has_carry:
        carry = jnp.zeros((n, blk, ATTN_W), F32)
    return pl.pallas_call(
        functools.partial(_attn_band_kernel, has_carry=has_carry),
        grid=(n, t // blk),
        in_specs=[pl.BlockSpec((None, blk, ATTN_W), lambda b, i: (b, i, 0)),
                  pl.BlockSpec((None, blk, ATTN_W), lambda b, i: (b, jnp.maximum(i - 1, 0), 0)),
                  pl.BlockSpec((None, blk, ATTN_W), lambda b, i: (b, 0, 0)),
                  pl.BlockSpec(memory_space=pltpu.SMEM)],
        out_specs=pl.BlockSpec((None, blk, D_B), lambda b, i: (b, i, 0)),
        out_shape=jax.ShapeDtypeStruct((n, t, D_B), F32),
        compiler_params=_cparams("parallel", "parallel"),
        name="attn_band",
    )(pattn, pattn, carry, sinks.astype(F32))


def _attn_cache_kernel(cur_ref, kc_ref, vc_ref, sink_ref, o_ref, *, nb):
    relc = (WINDOW - lax.broadcasted_iota(I32, (G_Q, WINDOW), 1)).astype(F32)
    gidx = lax.broadcasted_iota(I32, (G_Q, 1), 0)
    for b in range(nb):
        cur = cur_ref[b]
        for kvh in range(H_KV):
            q4 = jnp.concatenate([cur[:, (kvh * G_Q + g) * HD_B:(kvh * G_Q + g + 1) * HD_B] for g in range(G_Q)], axis=0)
            ko = D_B + kvh * HD_B
            vo = D_B + H_KV * HD_B + kvh * HD_B
            k_new = cur[:, ko:ko + HD_B]
            v_new = cur[:, vo:vo + HD_B]
            kc = kc_ref[b, :, kvh * HD_B:(kvh + 1) * HD_B]
            vc = vc_ref[b, :, kvh * HD_B:(kvh + 1) * HD_B]
            slope = jnp.zeros((G_Q, 1), F32)
            sink = jnp.zeros((G_Q, 1), F32)
            for g in range(G_Q):
                hq = kvh * G_Q + g
                slope = jnp.where(gidx == g, 2.0 ** (-8.0 * (hq + 1) / H_Q), slope)
                sink = jnp.where(gidx == g, sink_ref[hq], sink)
            scale = HD_B ** -0.5
            sc = _dotb(q4, kc, _NT) * scale - slope * relc
            sn = jnp.sum(q4.astype(BF16).astype(F32) * k_new.astype(BF16).astype(F32), axis=-1, keepdims=True) * scale
            m = jnp.maximum(jnp.maximum(jnp.max(sc, axis=-1, keepdims=True), sn), sink)
            pc = jnp.exp(sc - m)
            pn = jnp.exp(sn - m)
            den = jnp.sum(pc, axis=-1, keepdims=True) + pn + jnp.exp(sink - m)
            o = (_dotb(pc / den, vc) + (pn / den).astype(BF16).astype(F32) * v_new.astype(BF16).astype(F32))
            for g in range(G_Q):
                hq = kvh * G_Q + g
                o_ref[b, :, hq * HD_B:(hq + 1) * HD_B] = o[g:g + 1]


def _attn_cache(pattn, k_buf, v_buf, sinks, nb):
    n = pattn.shape[0]
    kc = k_buf.reshape(n, WINDOW, H_KV * HD_B)
    vc = v_buf.reshape(n, WINDOW, H_KV * HD_B)
    return pl.pallas_call(
        functools.partial(_attn_cache_kernel, nb=nb),
        grid=(n // nb,),
        in_specs=[pl.BlockSpec((nb, 1, ATTN_W), lambda i: (i, 0, 0)),
                  pl.BlockSpec((nb, WINDOW, H_KV * HD_B), lambda i: (i, 0, 0)),
                  pl.BlockSpec((nb, WINDOW, H_KV * HD_B), lambda i: (i, 0, 0)),
                  pl.BlockSpec(memory_space=pltpu.SMEM)],
        out_specs=pl.BlockSpec((nb, 1, D_B), lambda i: (i, 0, 0)),
        out_shape=jax.ShapeDtypeStruct((n, 1, D_B), F32),
        compiler_params=_cparams("parallel"),
        name="attn_cache",
    )(pattn, kc, vc, sinks.astype(F32))


def _merge_kernel(x_ref, ya_ref, ob_ref, pg_ref, gtm_ref, shf_ref, scf_ref, lng_ref, lnb_ref, l1g_ref, l1b_ref,
                  wpa_ref, wpb_ref, wo_ref, wq_ref, sk_ref, x1_ref, st_ref, *, alpha):
    ya = jnp.dot(ya_ref[...].astype(BF16), wpa_ref[...], preferred_element_type=F32)
    yb = jnp.dot(ob_ref[...].astype(BF16), wpb_ref[...], preferred_element_type=F32)
    pg = pg_ref[...]
    merged = jax.nn.sigmoid(pg[:, :D_MODEL]) * ya + jax.nn.sigmoid(pg[:, D_MODEL:]) * yb
    mix = jnp.dot(merged.astype(BF16), wo_ref[...], preferred_element_type=F32)
    xn = _layernorm(x_ref[...], lng_ref[...], lnb_ref[...])
    x1 = _layernorm(alpha * xn + gtm_ref[...] * mix, l1g_ref[...], l1b_ref[...])
    x1_ref[...] = x1
    h2 = x1 * (1.0 + scf_ref[...]) + shf_ref[...]
    q = jnp.dot(h2.astype(BF16), wq_ref[...], preferred_element_type=F32)
    for hc in range(2 * PEER_HEADS):
        st_ref[hc] = _dot3(sk_ref[hc % 2], q[:, hc * PEER_HALF:(hc + 1) * PEER_HALF], _NT)


def _merge(x, ya, ob, pg, mod, p, alpha, tb):
    g, r, _ = x.shape
    return pl.pallas_call(
        functools.partial(_merge_kernel, alpha=alpha),
        grid=(g, r // tb),
        in_specs=[_rows(tb, D_MODEL), _rows(tb, D_A), _rows(tb, D_B), _rows(tb, GATE_W),
                  _mod(mod, tb, 2), _mod(mod, tb, 3), _mod(mod, tb, 4)]
                 + [_const((1, D_MODEL))] * 4
                 + [_const((D_A, D_MODEL)), _const((D_B, D_MODEL)), _const((D_MODEL, D_MODEL)),
                    _const((D_MODEL, 2 * PEER_HEADS * PEER_HALF)), _const((2, N_KEYS, PEER_HALF))],
        out_specs=[_rows(tb, D_MODEL),
                   pl.BlockSpec((None, 2 * PEER_HEADS, N_KEYS, tb), lambda gi, i: (gi, 0, 0, i))],
        out_shape=[jax.ShapeDtypeStruct((g, r, D_MODEL), F32),
                   jax.ShapeDtypeStruct((g, 2 * PEER_HEADS, N_KEYS, r), F32)],
        compiler_params=_cparams("parallel", "parallel"),
        name="merge_ln1_peer_scores",
    )(x, ya, ob, pg, mod, mod, mod, _row2(p["ln_in_g"]), _row2(p["ln_in_b"]), _row2(p["ln1_g"]), _row2(p["ln1_b"]),
      p["w_pa"].astype(BF16), p["w_pb"].astype(BF16), p["w_o"].astype(BF16), p["peer_wq"].astype(BF16),
      p["peer_sub_keys"])


def _extract_top(problems, n_rows, tb):
    rio = lax.broadcasted_iota(I32, (n_rows, tb), 0).astype(F32)
    vals = [v for v, _ in problems]
    tops = [([], [], []) for _ in problems]
    for _ in range(PEER_TOPK):
        for n, (_, payload) in enumerate(problems):
            m = jnp.max(vals[n], axis=0, keepdims=True)
            i = jnp.min(jnp.where(vals[n] == m, rio, float(n_rows)), axis=0, keepdims=True)
            sel = rio == i
            tops[n][0].append(m)
            tops[n][1].append(i)
            if payload is not None:
                tops[n][2].append(jnp.max(jnp.where(sel, payload, -1.0), axis=0, keepdims=True))
            vals[n] = jnp.where(sel, -jnp.inf, vals[n])
    cat = lambda z: jnp.concatenate(z, axis=0) if z else None
    return [(cat(v), cat(i), cat(pl_)) for v, i, pl_ in tops]


def _pair_candidates(v1, i1, v2, i2, tb):
    k = PEER_TOPK
    sub = 8
    eid = lambda a0, a1, b0, b1: i1[a0:a1] * float(N_KEYS) + i2[b0:b1]
    vals = [v1[0:1] + v2, v1[1:2] + v2[0:sub]]
    ids = [eid(0, 1, 0, k), eid(1, 2, 0, sub)]
    brow = lax.broadcasted_iota(I32, (sub, tb), 0)
    for a in range(2, sub):
        vals.append(jnp.where(brow < k // (a + 1), v1[a:a + 1] + v2[0:sub], -jnp.inf))
        ids.append(eid(a, a + 1, 0, sub))
    vals.append(v1[sub:k] + v2[0:1])
    ids.append(eid(sub, k, 0, 1))
    return jnp.concatenate(vals, axis=0), jnp.concatenate(ids, axis=0)


def _topk_kernel(s_ref, idx_ref, gate_ref):
    tb = s_ref.shape[-1]

    def head_pair(hp, carry):
        tops = _extract_top([(s_ref[4 * hp + n], None) for n in range(4)], N_KEYS, tb)
        cands = [_pair_candidates(tops[2 * n][0], tops[2 * n][1], tops[2 * n + 1][0], tops[2 * n + 1][1], tb)
                 for n in range(2)]
        picked = _extract_top(cands, cands[0][0].shape[0], tb)
        for n, (sc, _, ex) in enumerate(picked):
            pexp = jnp.exp(sc - sc[0:1])
            idx_ref[2 * hp + n] = ex.astype(I32)
            gate_ref[2 * hp + n] = pexp / jnp.sum(pexp, axis=0, keepdims=True)
        return carry

    lax.fori_loop(0, PEER_HEADS // 2, head_pair, 0)


def _topk(scores_t, tb):
    g, _, _, r = scores_t.shape
    out = pl.BlockSpec((None, PEER_HEADS, PEER_TOPK, tb), lambda gi, i: (gi, 0, 0, i))
    return pl.pallas_call(
        _topk_kernel,
        grid=(g, r // tb),
        in_specs=[pl.BlockSpec((None, 2 * PEER_HEADS, N_KEYS, tb), lambda gi, i: (gi, 0, 0, i))],
        out_specs=[out, out],
        out_shape=[jax.ShapeDtypeStruct((g, PEER_HEADS, PEER_TOPK, r), I32),
                   jax.ShapeDtypeStruct((g, PEER_HEADS, PEER_TOPK, r), F32)],
        compiler_params=_cparams("parallel", "parallel"),
        name="peer_topk",
    )(scores_t)


def _sc_mesh():
    info = plsc.get_sparse_core_info()
    mesh = plsc.VectorSubcoreMesh(core_axis_name="c", subcore_axis_name="s")
    return info.num_cores, info.num_subcores, info.num_lanes, mesh


def _sc_gather_rows(table, idx):
    nc, ns, _, mesh = _sc_mesh()
    nw = nc * ns
    ni = idx.shape[0]
    w = table.shape[1]
    rr = SC_GATHER_ROWS
    per_w = ni // nw
    n_chunks = per_w // rr
    ki = min(SC_IDX_ROWS, n_chunks)
    n_outer = n_chunks // ki
    assert per_w * nw == ni and n_chunks * rr == per_w and n_outer * ki == n_chunks
    nb, ahead = SC_GATHER_BUFS, SC_GATHER_AHEAD
    buf = pltpu.VMEM((rr, w), table.dtype)

    @functools.partial(
        pl.kernel, mesh=mesh, out_type=jax.ShapeDtypeStruct((ni, w), table.dtype),
        scratch_types=[pltpu.VMEM((ki, rr), I32)] + [buf] * nb + [pltpu.SemaphoreType.DMA] * (2 * nb),
        name="peer_row_gather",
    )
    def gather(tab_hbm, idx_hbm, out_hbm, idx_v, *scratch):
        wid = lax.axis_index("s") * nc + lax.axis_index("c")
        base = wid * per_w
        bufs, gsem, wsem = scratch[:nb], scratch[nb:2 * nb], scratch[2 * nb:]

        @pl.loop(0, n_outer)
        def _(o):
            pltpu.sync_copy(idx_hbm.at[wid, pl.ds(o * ki, ki)], idx_v)

            def start_gather(j):
                return pltpu.async_copy(tab_hbm.at[idx_v.at[j]], bufs[j % nb], gsem[j % nb])

            def start_write(j):
                dst = pl.ds(base + (o * ki + j) * rr, rr)
                return pltpu.async_copy(bufs[j % nb], out_hbm.at[dst], wsem[j % nb])

            gathers = {j: start_gather(j) for j in range(min(ahead, ki))}
            writes = {}
            for j in range(ki):
                if j + ahead < ki:
                    if j + ahead - nb >= 0:
                        writes.pop(j + ahead - nb).wait()
                    gathers[j + ahead] = start_gather(j + ahead)
                gathers.pop(j).wait()
                writes[j] = start_write(j)
            for j in sorted(writes):
                writes[j].wait()

    return gather(table, idx.reshape(nw, n_chunks, rr))


def _sc_weighted_row_sum(table, idx, wgt):
    nc, ns, lanes, mesh = _sc_mesh()
    nw = nc * ns
    n_tok, picks = idx.shape
    ww = table.shape[1]
    rr = SC_GATHER_ROWS
    nq = picks // rr
    tpw = n_tok // nw
    kt = min(SC_ACC_TOKENS, tpw)
    n_outer = tpw // kt
    pw = ww // SC_ACC_PANELS
    nv = pw // lanes
    assert tpw * nw == n_tok and n_outer * kt == tpw and nq * rr == picks and nq % 2 == 0 and nv * lanes == pw
    cp = pltpu.CompilerParams(needs_layout_passes=False)
    buf = pltpu.VMEM((rr, ww), table.dtype)

    @functools.partial(
        pl.kernel, mesh=mesh, out_type=jax.ShapeDtypeStruct((n_tok, 2 * ww), F32),
        scratch_types=[pltpu.VMEM((kt * nq, rr), I32), pltpu.VMEM((kt, picks), F32), pltpu.VMEM((kt, 2 * ww), F32),
                       buf, buf, pltpu.SemaphoreType.DMA, pltpu.SemaphoreType.DMA],
        compiler_params=cp, name="peer_weighted_row_sum",
    )
    def kern(tab_hbm, idx_hbm, w_hbm, out_hbm, idx_v, w_v, acc_v, rows0, rows1, sem0, sem1):
        wid = lax.axis_index("s") * nc + lax.axis_index("c")
        rows, sems = (rows0, rows1), (sem0, sem1)

        def chunk_copy(t, q):
            return pltpu.make_async_copy(tab_hbm.at[idx_v.at[t * nq + q]], rows[q % 2], sems[q % 2])

        @pl.loop(0, n_outer)
        def _(o):
            tok0 = wid * tpw + o * kt
            pltpu.sync_copy(idx_hbm.at[pl.ds(tok0 * nq, kt * nq)], idx_v)
            pltpu.sync_copy(w_hbm.at[pl.ds(tok0, kt)], w_v)
            chunk_copy(0, 0).start()

            @pl.loop(0, kt)
            def _(t):
                tvec = jnp.full((lanes,), t, I32)
                for q in range(nq):
                    b = q % 2
                    if q + 1 < nq:
                        chunk_copy(t, q + 1).start()
                    else:
                        @pl.when(t + 1 < kt)
                        def _():
                            chunk_copy(t + 1, 0).start()
                    chunk_copy(t, q).wait()
                    for pan in range(SC_ACC_PANELS):
                        lo_at = lambda c: pl.ds(pan * pw + c * lanes, lanes)
                        hi_at = lambda c: pl.ds(ww + pan * pw + c * lanes, lanes)

                        def row_body(j, acc):
                            wj = plsc.load_gather(w_v, [tvec, jnp.full((lanes,), q * rr, I32) + j])
                            new = []
                            for c in range(nv):
                                word = rows[b][j, lo_at(c)]
                                new.append(acc[2 * c] + wj * plsc.bitcast(word << 16, F32))
                                new.append(acc[2 * c + 1] + wj * plsc.bitcast(word & jnp.uint32(0xFFFF0000), F32))
                            return tuple(new)

                        if q == 0:
                            init = tuple(jnp.zeros((lanes,), F32) for _ in range(2 * nv))
                        else:
                            init = tuple(acc_v[t, at(c)] for c in range(nv) for at in (lo_at, hi_at))
                        acc = lax.fori_loop(0, rr, row_body, init)
                        for c in range(nv):
                            acc_v[t, lo_at(c)] = acc[2 * c]
                            acc_v[t, hi_at(c)] = acc[2 * c + 1]

            pltpu.sync_copy(acc_v, out_hbm.at[pl.ds(tok0, kt)])

    return kern(table, idx.reshape(n_tok * nq, rr), wgt)


def _pack_bf16_pairs(t):
    half = t.shape[1] // 2
    b = lax.bitcast_convert_type(t.astype(BF16), jnp.uint16).astype(U32)
    return b[:, :half] | (b[:, half:] << 16)


def _gelu_erf(x):
    return 0.5 * x * (1.0 + lax.erf(x * (2.0 ** -0.5)))


def _unpack_pairs(words):
    lo = pltpu.bitcast(words << 16, F32)
    hi = pltpu.bitcast(words & jnp.uint32(0xFFFF0000), F32)
    return lo, hi


def _peer_hidden_kernel(gu_ref, gate_ref, x1_ref, shf_ref, scf_ref, w_ref, hd_ref, *, tb):
    half = D_MODEL // 2
    h2 = x1_ref[...] * (1.0 + scf_ref[...]) + shf_ref[...]
    for t in range(tb):
        ulo, uhi = _unpack_pairs(gu_ref[t * PEER_PICKS:(t + 1) * PEER_PICKS, :])
        hd_ref[:, t:t + 1] = jnp.sum(ulo * h2[t:t + 1, :half] + uhi * h2[t:t + 1, half:], axis=-1, keepdims=True)
    w_ref[...] = _gelu_erf(hd_ref[...]) * gate_ref[...]


def _peer_hidden(gu, gate_blocks, x1, mod, tb):
    r = x1.shape[1]
    blk = pl.BlockSpec((None, PEER_PICKS, tb), lambda gi, i: (i, 0, 0))
    return pl.pallas_call(
        functools.partial(_peer_hidden_kernel, tb=tb),
        grid=(1, r // tb),
        in_specs=[pl.BlockSpec((tb * PEER_PICKS, D_MODEL // 2), lambda gi, i: (i, 0)), blk,
                  _rows(tb, D_MODEL), _mod(mod, tb, 3), _mod(mod, tb, 4)],
        out_specs=blk,
        out_shape=jax.ShapeDtypeStruct((r // tb, PEER_PICKS, tb), F32),
        scratch_shapes=[pltpu.VMEM((PEER_PICKS, tb), F32)],
        compiler_params=_cparams("parallel", "parallel"),
        name="peer_hidden",
    )(gu, gate_blocks, x1, mod, mod)


def _peer_out_kernel(x1_ref, ff_ref, gtf_ref, l2g_ref, l2b_ref, o_ref, *, alpha):
    o_ref[...] = _layernorm(alpha * x1_ref[...] + gtf_ref[...] * ff_ref[...], l2g_ref[...], l2b_ref[...])


def _peer_out(x1, ff, mod, p, alpha, tb):
    g, r, _ = x1.shape
    return pl.pallas_call(
        functools.partial(_peer_out_kernel, alpha=alpha),
        grid=(g, r // tb),
        in_specs=[_rows(tb, D_MODEL), _rows(tb, D_MODEL), _mod(mod, tb, 5), _const((1, D_MODEL)), _const((1, D_MODEL))],
        out_specs=_rows(tb, D_MODEL),
        out_shape=jax.ShapeDtypeStruct((g, r, D_MODEL), F32),
        compiler_params=_cparams("parallel", "parallel"),
        name="peer_out_ln2",
    )(x1, ff, mod, _row2(p["ln2_g"]), _row2(p["ln2_b"]))


def _token_stage(x, mod, prev_fn, wkv_fn, attn_fn, p, alpha, tb, sequential):
    ps, pattn, pgate = _inproj(x, mod, p["ln_in_g"], p["ln_in_b"], p["w_in_bf16"], tb)
    r, lw, k, v, kk, kka, gl = _rwkv_prep(ps, prev_fn(), p, tb, sequential)
    y, wkv_new = wkv_fn(r, lw, k, v, kk, kka)
    ya = _rwkv_post(y, r, k, v, gl, p, tb)
    ob = attn_fn(pattn)
    x1, scores_t = _merge(x, ya, ob, pgate, mod, p, alpha, tb)
    idx_t, gate_t = _topk(scores_t, TOPK_TB)
    return ps, pattn, wkv_new, x1, idx_t, gate_t


def kernel(x_prompt, x_sample, state_wkv, state_shift, cache_k_win, cache_v_win, c_prompt, c_sample, ln_in_g, ln_in_b, w_ada, b_ada, w_in, mu_shift, rwkv_w0, rwkv_w2, rwkv_a0, rwkv_a2, rwkv_g2, rwkv_k_k, rwkv_k_a, rwkv_r_k, rwkv_gn_g, rwkv_gn_b, attn_sinks, w_pa, w_pb, w_o, ln1_g, ln1_b, peer_wq, peer_sub_keys, peer_u, peer_v, ln2_g, ln2_b):
    depth = w_in.shape[0]
    assert depth == 1, "single-layer trunk"
    alpha = (2.0 * depth) ** 0.25
    n_p, t_p, _ = x_prompt.shape
    n_s = x_sample.shape[0]
    p = dict(ln_in_g=ln_in_g, ln_in_b=ln_in_b, w_in_bf16=w_in[0].astype(BF16), mu_shift=mu_shift[0],
             rwkv_w0=rwkv_w0[0], rwkv_w2=rwkv_w2[0], rwkv_a0=rwkv_a0[0], rwkv_a2=rwkv_a2[0], rwkv_g2=rwkv_g2[0],
             rwkv_k_k=rwkv_k_k[0], rwkv_k_a=rwkv_k_a[0], rwkv_r_k=rwkv_r_k[0], rwkv_gn_g=rwkv_gn_g[0],
             rwkv_gn_b=rwkv_gn_b[0], w_pa=w_pa[0], w_pb=w_pb[0], w_o=w_o[0], ln1_g=ln1_g[0], ln1_b=ln1_b[0],
             peer_wq=peer_wq[0], peer_sub_keys=peer_sub_keys[0], ln2_g=ln2_g[0], ln2_b=ln2_b[0])
    sinks = attn_sinks[0]

    n_c = n_p + n_s
    pad = (-n_c) % 8
    c_all = jnp.concatenate([c_prompt, c_sample, jnp.zeros((pad, D_MODEL), F32)], axis=0)
    mod_all = _modulation(c_all, w_ada[0], b_ada[0])
    mod_p = mod_all[:n_p].reshape(n_p, 1, N_MOD * D_MODEL)
    mod_s = mod_all[n_p:n_c].reshape(1, n_s, N_MOD * D_MODEL)

    seg = min(PROMPT_SEGMENT, t_p)
    n_seg = t_p // seg
    assert n_seg * seg == t_p
    carry = {}

    def prompt_group(b, s):
        def prev_fn():
            return carry[b][0] if s > 0 else jnp.zeros((1, 1, SHIFT_W), F32)

        def wkv_fn(r, lw, k, v, kk, kka):
            s0 = carry[b][1] if s > 0 else jnp.zeros((1, H_A, HD_A, HD_A), F32)
            return _rwkv_chunk_scan(r, lw, k, v, kk, kka, s0)

        def attn_fn(pa):
            return _attn_band(pa, carry[b][2] if s > 0 else None, sinks)

        return (x_prompt[b:b + 1, s * seg:(s + 1) * seg], mod_p[b:b + 1], prev_fn, wkv_fn, attn_fn, TOKEN_TB, True)

    tu, tv = _pack_bf16_pairs(peer_u[0]), _pack_bf16_pairs(peer_v[0])

    def select(x, mod, prev_fn, wkv_fn, attn_fn, tb, sequential):
        ps, pattn, wkv_new, x1, idx_t, gate_t = _token_stage(x, mod, prev_fn, wkv_fn, attn_fn, p, alpha, tb, sequential)
        r = x1.shape[1]
        idx = jnp.transpose(idx_t.reshape(PEER_PICKS, r))
        gate_blocks = jnp.transpose(gate_t.reshape(PEER_PICKS, r // PEER_TB, PEER_TB), (1, 0, 2))
        gu = _sc_gather_rows(tu, idx.reshape(-1))
        return ps, pattn, wkv_new, (gu, idx, gate_blocks, x1, mod)

    def weigh(sel):
        gu, idx, gate_blocks, x1, mod = sel
        w_blocks = _peer_hidden(gu, gate_blocks, x1, mod, PEER_TB)
        wgt = jnp.transpose(w_blocks, (0, 2, 1)).reshape(-1, PEER_PICKS)
        return _sc_weighted_row_sum(tv, idx, wgt), x1, mod

    def finish(wsum):
        ff, x1, mod = wsum
        return _peer_out(x1, ff[None], mod, p, alpha, min(TOKEN_TB, x1.shape[1]))

    xs = x_sample.reshape(1, n_s, D_MODEL)

    def prev_s():
        return state_shift[0].reshape(1, n_s, SHIFT_W)

    def wkv_s(r, lw, k, v, kk, kka):
        sq = lambda z: z.reshape(n_s, D_A)
        y, s = _rwkv_step(state_wkv[0], sq(r), sq(lw), sq(k), sq(kk), sq(kka), sq(v), STEP_NB)
        return y.reshape(1, n_s, D_A), s

    def attn_s(pa):
        o = _attn_cache(pa.reshape(n_s, 1, ATTN_W), cache_k_win[0], cache_v_win[0], sinks, STEP_NB)
        return o.reshape(1, n_s, D_B)

    prompt_ids = [(b, s) for b in range(n_p) for s in range(n_seg)]
    n_g = len(prompt_ids) + 1
    sel, wsum, y_l = [None] * n_g, [None] * n_g, [None] * n_g
    for step in range(n_g + 2):
        if step < n_g - 1:
            b, s = prompt_ids[step]
            ps, pattn, wkv_new, sel[step] = select(*prompt_group(b, s))
            carry[b] = (ps[:, -1:], wkv_new, pattn[:, -WINDOW:])
        elif step == n_g - 1:
            ps_s, pattn_s, wkv_s_new, sel[step] = select(xs, mod_s, prev_s, wkv_s, attn_s, min(TOKEN_TB, n_s), False)
        if 0 <= step - 1 < n_g:
            wsum[step - 1] = weigh(sel[step - 1])
        if 0 <= step - 2 < n_g:
            y_l[step - 2] = finish(wsum[step - 2])
    y_s = y_l[n_g - 1]
    y_p = jnp.concatenate(y_l[:n_g - 1], axis=1).reshape(n_p, t_p, D_MODEL)
    shift_p = jnp.concatenate([carry[b][0][:, 0] for b in range(n_p)], axis=0)
    pattn_p = jnp.concatenate([carry[b][2] for b in range(n_p)], axis=0)
    wkv_p_new = jnp.concatenate([carry[b][1] for b in range(n_p)], axis=0)

    kv = lambda pa, o: pa[..., o:o + H_KV * HD_B]
    ko, vo = D_B, D_B + H_KV * HD_B
    k_win_p = kv(pattn_p, ko)[:, -WINDOW:].reshape(n_p, WINDOW, H_KV, HD_B)
    v_win_p = kv(pattn_p, vo)[:, -WINDOW:].reshape(n_p, WINDOW, H_KV, HD_B)
    k_new_s = kv(pattn_s, ko).reshape(n_s, 1, H_KV, HD_B)
    v_new_s = kv(pattn_s, vo).reshape(n_s, 1, H_KV, HD_B)
    k_win_s = jnp.concatenate([cache_k_win[0], k_new_s], axis=1)[:, -WINDOW:]
    v_win_s = jnp.concatenate([cache_v_win[0], v_new_s], axis=1)[:, -WINDOW:]
    return (y_p, y_s.reshape(n_s, 1, D_MODEL), wkv_p_new[None], wkv_s_new[None],
            shift_p[None], ps_s.reshape(n_s, SHIFT_W)[None],
            k_win_p[None], k_win_s[None], v_win_p[None], v_win_s[None])
```

```python
import functools
import math

import jax
import jax.numpy as jnp
from jax import lax
from jax.experimental import pallas as pl
from jax.experimental.pallas import tpu as pltpu
from jax.experimental.pallas import tpu_sc as plsc

F32 = jnp.float32
BF16 = jnp.bfloat16
I32 = jnp.int32
U32 = jnp.uint32

D_MODEL = 1024
H_A, HD_A = 8, 64
D_A = H_A * HD_A
D_LORA_W, D_LORA_A, D_LORA_G = 64, 64, 128
GN_EPS = 64e-5
H_Q, H_KV, HD_B = 8, 2, 64
G_Q = H_Q // H_KV
D_B = H_Q * HD_B
WINDOW = 128
N_KEYS = 128
PEER_HEADS, PEER_TOPK, PEER_HALF = 8, 16, 128
PEER_PICKS = PEER_HEADS * PEER_TOPK
N_MOD = 6
LN_EPS = 1e-5
NEG_INF = -1e30
OFF_WD = 3 * D_A
OFF_AD = OFF_WD + D_LORA_W
OFF_GD = OFF_AD + D_LORA_A
SHIFT_W = OFF_GD + D_LORA_G
ATTN_W = D_B + 2 * H_KV * HD_B
GATE_W = 2 * D_MODEL
D_IN = SHIFT_W + ATTN_W + GATE_W

VMEM_LIMIT = 48 * 1024 * 1024
SUBLANES = 8
RWKV_CHUNK = 64
SC_GATHER_ROWS = 32
SC_IDX_ROWS = 32
SC_GATHER_BUFS = 6
SC_GATHER_AHEAD = 3
SC_ACC_AHEAD = 3
SC_ACC_TOKENS = 16
SC_ACC_PANELS = 4
TOKEN_TB = 256
TOPK_TB = 128
PEER_TB = 16
PROMPT_SEGMENT = 2048
PROMPT_FIRST_SEGMENT = 512
STEP_NB = 8


def _cparams(*sem):
    return pltpu.CompilerParams(dimension_semantics=sem, vmem_limit_bytes=VMEM_LIMIT)


def _layernorm(x, g, b):
    mu = jnp.mean(x, -1, keepdims=True)
    xc = x - mu
    var = jnp.mean(xc * xc, -1, keepdims=True)
    return xc * lax.rsqrt(var + LN_EPS) * g + b


def _split(x):
    hi = x.astype(BF16)
    lo = (x - hi.astype(F32)).astype(BF16)
    return hi, lo


_NN = (((1,), (0,)), ((), ()))
_NT = (((1,), (1,)), ((), ()))
_TN = (((0,), (0,)), ((), ()))


def _dot3(a, b, dims=_NN):
    ah, al = _split(a)
    bh, bl = _split(b)
    d = functools.partial(lax.dot_general, dimension_numbers=dims, preferred_element_type=F32)
    return d(ah, bh) + d(ah, bl) + d(al, bh)


def _dot_exact_lhs(a_bf16, b, dims=_NN):
    b1 = b.astype(BF16)
    r1 = b - b1.astype(F32)
    b2 = r1.astype(BF16)
    b3 = (r1 - b2.astype(F32)).astype(BF16)
    d = functools.partial(lax.dot_general, dimension_numbers=dims, preferred_element_type=F32)
    return d(a_bf16, b1) + d(a_bf16, b2) + d(a_bf16, b3)


def _dotb(a, b, dims=_NN):
    return lax.dot_general(a.astype(BF16), b.astype(BF16), dims, preferred_element_type=F32)


def _rows(tb, width, col=0):
    return pl.BlockSpec((None, tb, width), lambda g, i: (g, i, col))


def _mod(mod, tb, col):
    if mod.shape[1] == 1:
        return pl.BlockSpec((None, 1, D_MODEL), lambda g, i: (g, 0, col))
    return pl.BlockSpec((None, tb, D_MODEL), lambda g, i: (g, i, col))


def _const(shape):
    n = len(shape)
    return pl.BlockSpec(shape, lambda g, i: (0,) * n)


def _row2(p):
    return p.reshape(1, -1).astype(F32)


def _mod_kernel(c_ref, w_ref, b_ref, o_ref):
    c = c_ref[...]
    a = c * jax.nn.sigmoid(c)
    o_ref[...] = _dot3(a, w_ref[...]) + b_ref[...]


def _modulation(c, w_ada, b_ada):
    n = c.shape[0]
    tn = D_MODEL
    return pl.pallas_call(
        _mod_kernel,
        grid=(w_ada.shape[1] // tn,),
        in_specs=[pl.BlockSpec((n, D_MODEL), lambda j: (0, 0)),
                  pl.BlockSpec((D_MODEL, tn), lambda j: (0, j)),
                  pl.BlockSpec((1, tn), lambda j: (0, j))],
        out_specs=pl.BlockSpec((n, tn), lambda j: (0, j)),
        out_shape=jax.ShapeDtypeStruct((n, w_ada.shape[1]), F32),
        compiler_params=_cparams("arbitrary"),
        name="modulation",
    )(c, w_ada, b_ada.reshape(1, -1))


def _inproj_kernel(x_ref, sh_ref, sc_ref, g_ref, b_ref, w_ref, ps_ref, pa_ref, pg_ref):
    xn = _layernorm(x_ref[...], g_ref[...], b_ref[...])
    h = (xn * (1.0 + sc_ref[...]) + sh_ref[...]).astype(BF16)
    ps_ref[...] = jnp.dot(h, w_ref[:, :SHIFT_W], preferred_element_type=F32)
    pa_ref[...] = jnp.dot(h, w_ref[:, SHIFT_W:SHIFT_W + ATTN_W], preferred_element_type=F32)
    pg_ref[...] = jnp.dot(h, w_ref[:, SHIFT_W + ATTN_W:], preferred_element_type=F32)


def _inproj(x, mod, ln_g, ln_b, w_in_bf16, tb):
    g, r, _ = x.shape
    shp = lambda w: jax.ShapeDtypeStruct((g, r, w), F32)
    return pl.pallas_call(
        _inproj_kernel,
        grid=(g, r // tb),
        in_specs=[_rows(tb, D_MODEL), _mod(mod, tb, 0), _mod(mod, tb, 1),
                  _const((1, D_MODEL)), _const((1, D_MODEL)), _const((D_MODEL, D_IN))],
        out_specs=[_rows(tb, SHIFT_W), _rows(tb, ATTN_W), _rows(tb, GATE_W)],
        out_shape=[shp(SHIFT_W), shp(ATTN_W), shp(GATE_W)],
        compiler_params=_cparams("parallel", "parallel"),
        name="inproj",
    )(x, mod, mod, _row2(ln_g), _row2(ln_b), w_in_bf16)


def _softplus(x):
    return jnp.maximum(x, 0.0) + jnp.log1p(jnp.exp(-jnp.abs(x)))


def _rwkv_prep_kernel(ps_ref, prev_ref, first_ref, mu_ref, w0_ref, w2_ref, a0_ref, a2_ref, g2_ref, kk_w_ref, ka_w_ref,
                      hsum_ref, r_ref, lw_ref, k_ref, v_ref, kk_ref, kka_ref, g_ref, *, sequential):
    ps = ps_ref[...]
    if sequential:
        before = jnp.where(pl.program_id(1) == 0, first_ref[...], prev_ref[SUBLANES - 1:SUBLANES, :])
        row = lax.broadcasted_iota(I32, ps.shape, 0)
        prev = jnp.where(row == 0, before, pltpu.roll(ps, 1, 0))
    else:
        prev = prev_ref[...]
    xs = ps + (prev - ps) * mu_ref[...]
    r = xs[:, 0:D_A]
    k = xs[:, D_A:2 * D_A]
    v = xs[:, 2 * D_A:3 * D_A]
    wd = xs[:, OFF_WD:OFF_AD]
    ad = xs[:, OFF_AD:OFF_GD]
    gd = xs[:, OFF_GD:SHIFT_W]
    z = w0_ref[...] + _dot3(jnp.tanh(wd), w2_ref[...])
    w_log = -_softplus(-z) - 0.5
    a = jax.nn.sigmoid(a0_ref[...] + _dot3(ad, a2_ref[...]))
    kk = k * kk_w_ref[...]
    ss = _dot3(kk * kk, hsum_ref[...])
    kk = kk / jnp.maximum(jnp.sqrt(ss), 1e-12)
    r_ref[...] = r
    lw_ref[...] = -jnp.exp(w_log)
    k_ref[...] = k * (1.0 + (a - 1.0) * ka_w_ref[...])
    v_ref[...] = v
    kk_ref[...] = kk
    kka_ref[...] = kk * a
    g_ref[...] = _dot3(jax.nn.sigmoid(gd), g2_ref[...])


def _head_sum_matrix():
    h = jnp.arange(D_A) // HD_A
    return (h[:, None] == h[None, :]).astype(F32)


def _rwkv_prep(ps, prev, p, tb, sequential):
    g, r, _ = ps.shape
    shp = jax.ShapeDtypeStruct((g, r, D_A), F32)
    if sequential:
        per = tb // SUBLANES
        prev_args = (ps, prev)
        prev_specs = [pl.BlockSpec((None, SUBLANES, SHIFT_W), lambda gi, i: (gi, jnp.maximum(i * per - 1, 0), 0)),
                      pl.BlockSpec((None, 1, SHIFT_W), lambda gi, i: (gi, 0, 0))]
    else:
        prev_args = (prev, prev[:, :1])
        prev_specs = [_rows(tb, SHIFT_W), pl.BlockSpec((None, 1, SHIFT_W), lambda gi, i: (gi, 0, 0))]
    return pl.pallas_call(
        functools.partial(_rwkv_prep_kernel, sequential=sequential),
        grid=(g, r // tb),
        in_specs=[_rows(tb, SHIFT_W)] + prev_specs + [_const((1, SHIFT_W)),
                  _const((1, D_A)), _const((D_LORA_W, D_A)), _const((1, D_A)), _const((D_LORA_A, D_A)),
                  _const((D_LORA_G, D_A)), _const((1, D_A)), _const((1, D_A)), _const((D_A, D_A))],
        out_specs=[_rows(tb, D_A)] * 7,
        out_shape=[shp] * 7,
        compiler_params=_cparams("parallel", "parallel"),
        name="rwkv_prep",
    )(ps, *prev_args, _row2(p["mu_shift"]), _row2(p["rwkv_w0"]), p["rwkv_w2"], _row2(p["rwkv_a0"]), p["rwkv_a2"],
      p["rwkv_g2"], _row2(p["rwkv_k_k"]), _row2(p["rwkv_k_a"]), _head_sum_matrix())


def _rwkv_chunk_kernel(r_ref, lw_ref, k_ref, v_ref, kk_ref, kka_ref, s0_ref, y_ref, s_ref):
    c = RWKV_CHUNK

    @pl.when(pl.program_id(1) == 0)
    def _():
        s_ref[...] = s0_ref[...]

    row = lax.broadcasted_iota(I32, (c, c), 0)
    col = lax.broadcasted_iota(I32, (c, c), 1)
    tril = row >= col
    stril = row > col
    lw = lw_ref[...]
    cum = _dot_exact_lhs(tril.astype(BF16), lw)
    cum_end = cum[c - 1:c, :]
    g_inv = jnp.exp(-cum)
    g_end = jnp.exp(cum_end - cum)
    a_hat = -kk_ref[...] * jnp.exp(cum - lw)
    b_hat = kka_ref[...] * g_inv
    k_hat = k_ref[...] * g_inv
    r_til = r_ref[...] * jnp.exp(cum)
    b_end = kka_ref[...] * g_end
    k_end = k_ref[...] * g_end
    gam_end = jnp.exp(cum_end)
    v_all = v_ref[...]
    s_all = s_ref[...]
    n_steps = int(math.log2(c))
    heads = range(H_A)
    sl = [slice(h * HD_A, (h + 1) * HD_A) for h in heads]
    vh = [v_all[:, sl[h]] for h in heads]
    ar = [jnp.concatenate([a_hat[:, sl[h]], r_til[:, sl[h]]], axis=0) for h in heads]
    bk = [jnp.concatenate([b_hat[:, sl[h]], k_hat[:, sl[h]]], axis=0) for h in heads]
    x = [_dot3(ar[h], bk[h], _NT) for h in heads]
    ars = [_dot3(ar[h], s_all[h], _NT) for h in heads]
    a_ak = [jnp.where(stril, x[h][:c, c:], 0.0) for h in heads]
    n = [jnp.where(stril, x[h][:c, :c], 0.0) for h in heads]
    u = [ars[h][:c] + _dot3(a_ak[h], vh[h]) for h in heads]
    for it in range(n_steps):
        u = [u[h] + _dot3(n[h], u[h]) for h in heads]
        if it + 1 < n_steps:
            n = [_dot3(n[h], n[h]) for h in heads]
    uv = [jnp.concatenate([u[h], vh[h]], axis=0) for h in heads]
    a_r = [jnp.concatenate([jnp.where(tril, x[h][c:, :c], 0.0), jnp.where(tril, x[h][c:, c:], 0.0)], axis=1)
           for h in heads]
    y = [ars[h][c:] + _dot3(a_r[h], uv[h]) for h in heads]
    bke = [jnp.concatenate([b_end[:, sl[h]], k_end[:, sl[h]]], axis=0) for h in heads]
    s_new = [s_all[h] * gam_end[:, sl[h]] + _dot3(uv[h], bke[h], _TN) for h in heads]
    for h in heads:
        y_ref[:, sl[h]] = y[h]
        s_ref[h] = s_new[h]


def _rwkv_chunk_scan(r, lw, k, v, kk, kka, s0):
    n, t, _ = r.shape
    c = RWKV_CHUNK
    seq = pl.BlockSpec((None, c, D_A), lambda b, i: (b, i, 0))
    st = pl.BlockSpec((None, H_A, HD_A, HD_A), lambda b, i: (b, 0, 0, 0))
    return pl.pallas_call(
        _rwkv_chunk_kernel,
        grid=(n, t // c),
        in_specs=[seq] * 6 + [st],
        out_specs=[seq, st],
        out_shape=[jax.ShapeDtypeStruct((n, t, D_A), F32), jax.ShapeDtypeStruct((n, H_A, HD_A, HD_A), F32)],
        compiler_params=_cparams("parallel", "arbitrary"),
        name="rwkv_chunk_scan",
    )(r, lw, k, v, kk, kka, s0)


def _rwkv_step_kernel(s_ref, r_ref, lw_ref, k_ref, kk_ref, kka_ref, v_ref, y_ref, so_ref):
    s = s_ref[...]
    sa = jnp.sum(s * (-kk_ref[...]), axis=-1, keepdims=True)
    s = s * jnp.exp(lw_ref[...]) + sa * kka_ref[...] + v_ref[...] * k_ref[...]
    so_ref[...] = s
    y_ref[...] = jnp.sum(s * r_ref[...], axis=-1, keepdims=True)


def _rwkv_step(s0, r, lw, k, kk, kka, v, nb):
    n = s0.shape[0]
    key = lambda z: z.reshape(n, H_A, 1, HD_A)
    st = pl.BlockSpec((nb, H_A, HD_A, HD_A), lambda i: (i, 0, 0, 0))
    ks = pl.BlockSpec((nb, H_A, 1, HD_A), lambda i: (i, 0, 0, 0))
    vs = pl.BlockSpec((nb, H_A, HD_A, 1), lambda i: (i, 0, 0, 0))
    y, s = pl.pallas_call(
        _rwkv_step_kernel,
        grid=(n // nb,),
        in_specs=[st, ks, ks, ks, ks, ks, vs],
        out_specs=[vs, st],
        out_shape=[jax.ShapeDtypeStruct((n, H_A, HD_A, 1), F32), jax.ShapeDtypeStruct(s0.shape, F32)],
        compiler_params=_cparams("parallel"),
        name="rwkv_step",
    )(s0, key(r), key(lw), key(k), key(kk), key(kka), v.reshape(n, H_A, HD_A, 1))
    return y.reshape(n, D_A), s


def _rwkv_post_kernel(y_ref, r_ref, k_ref, v_ref, g_ref, gn_g_ref, gn_b_ref, rk_ref, hsum_ref, o_ref):
    y = y_ref[...]
    hs = hsum_ref[...]
    mu = _dot3(y, hs) * (1.0 / HD_A)
    yc = y - mu
    var = _dot3(yc * yc, hs) * (1.0 / HD_A)
    yn = yc * lax.rsqrt(var + GN_EPS) * gn_g_ref[...] + gn_b_ref[...]
    bonus = _dot3(r_ref[...] * k_ref[...] * rk_ref[...], hs) * v_ref[...]
    o_ref[...] = (yn + bonus) * g_ref[...]


def _rwkv_post(y, r, k, v, g, p, tb):
    gg, rr, _ = y.shape
    return pl.pallas_call(
        _rwkv_post_kernel,
        grid=(gg, rr // tb),
        in_specs=[_rows(tb, D_A)] * 5 + [_const((1, D_A))] * 3 + [_const((D_A, D_A))],
        out_specs=_rows(tb, D_A),
        out_shape=jax.ShapeDtypeStruct((gg, rr, D_A), F32),
        compiler_params=_cparams("parallel", "parallel"),
        name="rwkv_post",
    )(y, r, k, v, g, _row2(p["rwkv_gn_g"]), _row2(p["rwkv_gn_b"]), _row2(p["rwkv_r_k"]), _head_sum_matrix())


def _sink_softmax(s, sink):
    m = jnp.maximum(jnp.max(s, axis=-1, keepdims=True), sink)
    p = jnp.exp(s - m)
    den = jnp.sum(p, axis=-1, keepdims=True) + jnp.exp(sink - m)
    return p / den


def _attn_band_kernel(cur_ref, prev_ref, carry_ref, sink_ref, o_ref, *, has_carry):
    blk = WINDOW
    i = pl.program_id(1)
    cur = cur_ref[...]
    prev = jnp.where(i == 0, carry_ref[...], prev_ref[...])
    qi = lax.broadcasted_iota(I32, (G_Q * blk, 2 * blk), 0) % blk
    kj = lax.broadcasted_iota(I32, (G_Q * blk, 2 * blk), 1)
    rel = blk + qi - kj
    valid = (rel >= 0) & (rel <= WINDOW)
    if not has_carry:
        valid = valid & ((kj >= blk) | (i > 0))
    relf = rel.astype(F32)
    gidx = lax.broadcasted_iota(I32, (G_Q * blk, 1), 0) // blk
    for kvh in range(H_KV):
        q4 = jnp.concatenate([cur[:, (kvh * G_Q + g) * HD_B:(kvh * G_Q + g + 1) * HD_B] for g in range(G_Q)], axis=0)
        ko = D_B + kvh * HD_B
        vo = D_B + H_KV * HD_B + kvh * HD_B
        kmat = jnp.concatenate([prev[:, ko:ko + HD_B], cur[:, ko:ko + HD_B]], axis=0)
        vmat = jnp.concatenate([prev[:, vo:vo + HD_B], cur[:, vo:vo + HD_B]], axis=0)
        slope = jnp.zeros((G_Q * blk, 1), F32)
        sink = jnp.zeros((G_Q * blk, 1), F32)
        for g in range(G_Q):
            hq = kvh * G_Q + g
            slope = jnp.where(gidx == g, 2.0 ** (-8.0 * (hq + 1) / H_Q), slope)
            sink = jnp.where(gidx == g, sink_ref[hq], sink)
        s = _dotb(q4, kmat, _NT) * (HD_B ** -0.5)
        s = jnp.where(valid, s - slope * relf, NEG_INF)
        p = _sink_softmax(s, sink)
        o = _dotb(p, vmat)
        for g in range(G_Q):
            hq = kvh * G_Q + g
            o_ref[:, hq * HD_B:(hq + 1) * HD_B] = o[g * blk:(g + 1) * blk]


def _attn_band(pattn, carry, sinks):
    n, t, _ = pattn.shape
    blk = WINDOW
    has_carry = carry is not None
    if not has_carry:
        carry = jnp.zeros((n, blk, ATTN_W), F32)
    return pl.pallas_call(
        functools.partial(_attn_band_kernel, has_carry=has_carry),
        grid=(n, t // blk),
        in_specs=[pl.BlockSpec((None, blk, ATTN_W), lambda b, i: (b, i, 0)),
                  pl.BlockSpec((None, blk, ATTN_W), lambda b, i: (b, jnp.maximum(i - 1, 0), 0)),
                  pl.BlockSpec((None, blk, ATTN_W), lambda b, i: (b, 0, 0)),
                  pl.BlockSpec(memory_space=pltpu.SMEM)],
        out_specs=pl.BlockSpec((None, blk, D_B), lambda b, i: (b, i, 0)),
        out_shape=jax.ShapeDtypeStruct((n, t, D_B), F32),
        compiler_params=_cparams("parallel", "parallel"),
        name="attn_band",
    )(pattn, pattn, carry, sinks.astype(F32))


def _attn_cache_kernel(cur_ref, kc_ref, vc_ref, sink_ref, o_ref, *, nb):
    relc = (WINDOW - lax.broadcasted_iota(I32, (G_Q, WINDOW), 1)).astype(F32)
    gidx = lax.broadcasted_iota(I32, (G_Q, 1), 0)
    for b in range(nb):
        cur = cur_ref[b]
        for kvh in range(H_KV):
            q4 = jnp.concatenate([cur[:, (kvh * G_Q + g) * HD_B:(kvh * G_Q + g + 1) * HD_B] for g in range(G_Q)], axis=0)
            ko = D_B + kvh * HD_B
            vo = D_B + H_KV * HD_B + kvh * HD_B
            k_new = cur[:, ko:ko + HD_B]
            v_new = cur[:, vo:vo + HD_B]
            kc = kc_ref[b, :, kvh * HD_B:(kvh + 1) * HD_B]
            vc = vc_ref[b, :, kvh * HD_B:(kvh + 1) * HD_B]
            slope = jnp.zeros((G_Q, 1), F32)
            sink = jnp.zeros((G_Q, 1), F32)
            for g in range(G_Q):
                hq = kvh * G_Q + g
                slope = jnp.where(gidx == g, 2.0 ** (-8.0 * (hq + 1) / H_Q), slope)
                sink = jnp.where(gidx == g, sink_ref[hq], sink)
            scale = HD_B ** -0.5
            sc = _dotb(q4, kc, _NT) * scale - slope * relc
            sn = jnp.sum(q4.astype(BF16).astype(F32) * k_new.astype(BF16).astype(F32), axis=-1, keepdims=True) * scale
            m = jnp.maximum(jnp.maximum(jnp.max(sc, axis=-1, keepdims=True), sn), sink)
            pc = jnp.exp(sc - m)
            pn = jnp.exp(sn - m)
            den = jnp.sum(pc, axis=-1, keepdims=True) + pn + jnp.exp(sink - m)
            o = (_dotb(pc / den, vc) + (pn / den).astype(BF16).astype(F32) * v_new.astype(BF16).astype(F32))
            for g in range(G_Q):
                hq = kvh * G_Q + g
                o_ref[b, :, hq * HD_B:(hq + 1) * HD_B] = o[g:g + 1]


def _attn_cache(pattn, k_buf, v_buf, sinks, nb):
    n = pattn.shape[0]
    kc = k_buf.reshape(n, WINDOW, H_KV * HD_B)
    vc = v_buf.reshape(n, WINDOW, H_KV * HD_B)
    return pl.pallas_call(
        functools.partial(_attn_cache_kernel, nb=nb),
        grid=(n // nb,),
        in_specs=[pl.BlockSpec((nb, 1, ATTN_W), lambda i: (i, 0, 0)),
                  pl.BlockSpec((nb, WINDOW, H_KV * HD_B), lambda i: (i, 0, 0)),
                  pl.BlockSpec((nb, WINDOW, H_KV * HD_B), lambda i: (i, 0, 0)),
                  pl.BlockSpec(memory_space=pltpu.SMEM)],
        out_specs=pl.BlockSpec((nb, 1, D_B), lambda i: (i, 0, 0)),
        out_shape=jax.ShapeDtypeStruct((n, 1, D_B), F32),
        compiler_params=_cparams("parallel"),
        name="attn_cache",
    )(pattn, kc, vc, sinks.astype(F32))


def _merge_kernel(x_ref, ya_ref, ob_ref, pg_ref, gtm_ref, shf_ref, scf_ref, lng_ref, lnb_ref, l1g_ref, l1b_ref,
                  wpa_ref, wpb_ref, wo_ref, wq_ref, sk_ref, x1_ref, st_ref, *, alpha):
    ya = jnp.dot(ya_ref[...].astype(BF16), wpa_ref[...], preferred_element_type=F32)
    yb = jnp.dot(ob_ref[...].astype(BF16), wpb_ref[...], preferred_element_type=F32)
    pg = pg_ref[...]
    merged = jax.nn.sigmoid(pg[:, :D_MODEL]) * ya + jax.nn.sigmoid(pg[:, D_MODEL:]) * yb
    mix = jnp.dot(merged.astype(BF16), wo_ref[...], preferred_element_type=F32)
    xn = _layernorm(x_ref[...], lng_ref[...], lnb_ref[...])
    x1 = _layernorm(alpha * xn + gtm_ref[...] * mix, l1g_ref[...], l1b_ref[...])
    x1_ref[...] = x1
    h2 = x1 * (1.0 + scf_ref[...]) + shf_ref[...]
    q = jnp.dot(h2.astype(BF16), wq_ref[...], preferred_element_type=F32)
    for hc in range(2 * PEER_HEADS):
        st_ref[hc] = _dot3(sk_ref[hc % 2], q[:, hc * PEER_HALF:(hc + 1) * PEER_HALF], _NT)


def _merge(x, ya, ob, pg, mod, p, alpha, tb):
    g, r, _ = x.shape
    return pl.pallas_call(
        functools.partial(_merge_kernel, alpha=alpha),
        grid=(g, r // tb),
        in_specs=[_rows(tb, D_MODEL), _rows(tb, D_A), _rows(tb, D_B), _rows(tb, GATE_W),
                  _mod(mod, tb, 2), _mod(mod, tb, 3), _mod(mod, tb, 4)]
                 + [_const((1, D_MODEL))] * 4
                 + [_const((D_A, D_MODEL)), _const((D_B, D_MODEL)), _const((D_MODEL, D_MODEL)),
                    _const((D_MODEL, 2 * PEER_HEADS * PEER_HALF)), _const((2, N_KEYS, PEER_HALF))],
        out_specs=[_rows(tb, D_MODEL),
                   pl.BlockSpec((None, 2 * PEER_HEADS, N_KEYS, tb), lambda gi, i: (gi, 0, 0, i))],
        out_shape=[jax.ShapeDtypeStruct((g, r, D_MODEL), F32),
                   jax.ShapeDtypeStruct((g, 2 * PEER_HEADS, N_KEYS, r), F32)],
        compiler_params=_cparams("parallel", "parallel"),
        name="merge_ln1_peer_scores",
    )(x, ya, ob, pg, mod, mod, mod, _row2(p["ln_in_g"]), _row2(p["ln_in_b"]), _row2(p["ln1_g"]), _row2(p["ln1_b"]),
      p["w_pa"].astype(BF16), p["w_pb"].astype(BF16), p["w_o"].astype(BF16), p["peer_wq"].astype(BF16),
      p["peer_sub_keys"])


def _extract_top(problems, n_rows, tb):
    rio = lax.broadcasted_iota(I32, (n_rows, tb), 0).astype(F32)
    vals = [v for v, _ in problems]
    tops = [([], [], []) for _ in problems]
    for _ in range(PEER_TOPK):
        for n, (_, payload) in enumerate(problems):
            m = jnp.max(vals[n], axis=0, keepdims=True)
            i = jnp.min(jnp.where(vals[n] == m, rio, float(n_rows)), axis=0, keepdims=True)
            sel = rio == i
            tops[n][0].append(m)
            tops[n][1].append(i)
            if payload is not None:
                tops[n][2].append(jnp.max(jnp.where(sel, payload, -1.0), axis=0, keepdims=True))
            vals[n] = jnp.where(sel, -jnp.inf, vals[n])
    cat = lambda z: jnp.concatenate(z, axis=0) if z else None
    return [(cat(v), cat(i), cat(pl_)) for v, i, pl_ in tops]


def _pair_candidates(v1, i1, v2, i2, tb):
    k = PEER_TOPK
    sub = 8
    eid = lambda a0, a1, b0, b1: i1[a0:a1] * float(N_KEYS) + i2[b0:b1]
    vals = [v1[0:1] + v2, v1[1:2] + v2[0:sub]]
    ids = [eid(0, 1, 0, k), eid(1, 2, 0, sub)]
    brow = lax.broadcasted_iota(I32, (sub, tb), 0)
    for a in range(2, sub):
        vals.append(jnp.where(brow < k // (a + 1), v1[a:a + 1] + v2[0:sub], -jnp.inf))
        ids.append(eid(a, a + 1, 0, sub))
    vals.append(v1[sub:k] + v2[0:1])
    ids.append(eid(sub, k, 0, 1))
    return jnp.concatenate(vals, axis=0), jnp.concatenate(ids, axis=0)


def _topk_kernel(s_ref, idx_ref, gate_ref):
    tb = s_ref.shape[-1]

    def head_pair(hp, carry):
        tops = _extract_top([(s_ref[4 * hp + n], None) for n in range(4)], N_KEYS, tb)
        cands = [_pair_candidates(tops[2 * n][0], tops[2 * n][1], tops[2 * n + 1][0], tops[2 * n + 1][1], tb)
                 for n in range(2)]
        picked = _extract_top(cands, cands[0][0].shape[0], tb)
        for n, (sc, _, ex) in enumerate(picked):
            pexp = jnp.exp(sc - sc[0:1])
            idx_ref[2 * hp + n] = ex.astype(I32)
            gate_ref[2 * hp + n] = pexp / jnp.sum(pexp, axis=0, keepdims=True)
        return carry

    lax.fori_loop(0, PEER_HEADS // 2, head_pair, 0)


def _topk(scores_t, tb):
    g, _, _, r = scores_t.shape
    out = pl.BlockSpec((None, PEER_HEADS, PEER_TOPK, tb), lambda gi, i: (gi, 0, 0, i))
    return pl.pallas_call(
        _topk_kernel,
        grid=(g, r // tb),
        in_specs=[pl.BlockSpec((None, 2 * PEER_HEADS, N_KEYS, tb), lambda gi, i: (gi, 0, 0, i))],
        out_specs=[out, out],
        out_shape=[jax.ShapeDtypeStruct((g, PEER_HEADS, PEER_TOPK, r), I32),
                   jax.ShapeDtypeStruct((g, PEER_HEADS, PEER_TOPK, r), F32)],
        compiler_params=_cparams("parallel", "parallel"),
        name="peer_topk",
    )(scores_t)


def _sc_mesh():
    info = plsc.get_sparse_core_info()
    mesh = plsc.VectorSubcoreMesh(core_axis_name="c", subcore_axis_name="s")
    return info.num_cores, info.num_subcores, info.num_lanes, mesh


def _sc_gather_rows(table, idx):
    nc, ns, _, mesh = _sc_mesh()
    nw = nc * ns
    ni = idx.shape[0]
    w = table.shape[1]
    rr = SC_GATHER_ROWS
    per_w = ni // nw
    n_chunks = per_w // rr
    ki = min(SC_IDX_ROWS, n_chunks)
    n_outer = n_chunks // ki
    assert per_w * nw == ni and n_chunks * rr == per_w and n_outer * ki == n_chunks
    nb, ahead = SC_GATHER_BUFS, SC_GATHER_AHEAD
    buf = pltpu.VMEM((rr, w), table.dtype)

    @functools.partial(
        pl.kernel, mesh=mesh, out_type=jax.ShapeDtypeStruct((ni, w), table.dtype),
        scratch_types=[pltpu.VMEM((ki, rr), I32)] + [buf] * nb + [pltpu.SemaphoreType.DMA] * (2 * nb),
        name="peer_row_gather",
    )
    def gather(tab_hbm, idx_hbm, out_hbm, idx_v, *scratch):
        wid = lax.axis_index("s") * nc + lax.axis_index("c")
        base = wid * per_w
        bufs, gsem, wsem = scratch[:nb], scratch[nb:2 * nb], scratch[2 * nb:]

        @pl.loop(0, n_outer)
        def _(o):
            pltpu.sync_copy(idx_hbm.at[wid, pl.ds(o * ki, ki)], idx_v)

            def start_gather(j):
                return pltpu.async_copy(tab_hbm.at[idx_v.at[j]], bufs[j % nb], gsem[j % nb])

            def start_write(j):
                dst = pl.ds(base + (o * ki + j) * rr, rr)
                return pltpu.async_copy(bufs[j % nb], out_hbm.at[dst], wsem[j % nb])

            gathers = {j: start_gather(j) for j in range(min(ahead, ki))}
            writes = {}
            for j in range(ki):
                if j + ahead < ki:
                    if j + ahead - nb >= 0:
                        writes.pop(j + ahead - nb).wait()
                    gathers[j + ahead] = start_gather(j + ahead)
                gathers.pop(j).wait()
                writes[j] = start_write(j)
            for j in sorted(writes):
                writes[j].wait()

    return gather(table, idx.reshape(nw, n_chunks, rr))


def _sc_weighted_row_sum(table, idx, wgt):
    nc, ns, lanes, mesh = _sc_mesh()
    nw = nc * ns
    n_tok, picks = idx.shape
    ww = table.shape[1]
    rr = SC_GATHER_ROWS
    nq = picks // rr
    tpw = n_tok // nw
    kt = min(SC_ACC_TOKENS, tpw)
    n_outer = tpw // kt
    pw = ww // SC_ACC_PANELS
    nv = pw // lanes
    ahead = SC_ACC_AHEAD
    assert tpw * nw == n_tok and n_outer * kt == tpw and nq * rr == picks and ahead < nq and nv * lanes == pw
    cp = pltpu.CompilerParams(needs_layout_passes=False)
    buf = pltpu.VMEM((rr, ww), table.dtype)

    @functools.partial(
        pl.kernel, mesh=mesh, out_type=jax.ShapeDtypeStruct((n_tok, 2 * ww), F32),
        scratch_types=[pltpu.VMEM((kt * nq, rr), I32), pltpu.VMEM((kt, picks), F32), pltpu.VMEM((kt, 2 * ww), F32)]
                      + [buf] * nq + [pltpu.SemaphoreType.DMA] * nq,
        compiler_params=cp, name="peer_weighted_row_sum",
    )
    def kern(tab_hbm, idx_hbm, w_hbm, out_hbm, idx_v, w_v, acc_v, *scratch):
        wid = lax.axis_index("s") * nc + lax.axis_index("c")
        rows, sems = scratch[:nq], scratch[nq:]

        def chunk_copy(t, q):
            return pltpu.make_async_copy(tab_hbm.at[idx_v.at[t * nq + q]], rows[q], sems[q])

        @pl.loop(0, n_outer)
        def _(o):
            tok0 = wid * tpw + o * kt
            pltpu.sync_copy(idx_hbm.at[pl.ds(tok0 * nq, kt * nq)], idx_v)
            pltpu.sync_copy(w_hbm.at[pl.ds(tok0, kt)], w_v)
            for q in range(ahead):
                chunk_copy(0, q).start()

            @pl.loop(0, kt)
            def _(t):
                tvec = jnp.full((lanes,), t, I32)
                for q in range(nq):
                    b = q
                    if q + ahead < nq:
                        chunk_copy(t, q + ahead).start()
                    else:
                        @pl.when(t + 1 < kt)
                        def _():
                            chunk_copy(t + 1, q + ahead - nq).start()
                    chunk_copy(t, q).wait()
                    for pan in range(SC_ACC_PANELS):
                        lo_at = lambda c: pl.ds(pan * pw + c * lanes, lanes)
                        hi_at = lambda c: pl.ds(ww + pan * pw + c * lanes, lanes)

                        def row_body(j, acc):
                            wj = plsc.load_gather(w_v, [tvec, jnp.full((lanes,), q * rr, I32) + j])
                            new = []
                            for c in range(nv):
                                word = rows[b][j, lo_at(c)]
                                new.append(acc[2 * c] + wj * plsc.bitcast(word << 16, F32))
                                new.append(acc[2 * c + 1] + wj * plsc.bitcast(word & jnp.uint32(0xFFFF0000), F32))
                            return tuple(new)

                        if q == 0:
                            init = tuple(jnp.zeros((lanes,), F32) for _ in range(2 * nv))
                        else:
                            init = tuple(acc_v[t, at(c)] for c in range(nv) for at in (lo_at, hi_at))
                        acc = lax.fori_loop(0, rr, row_body, init)
                        for c in range(nv):
                            acc_v[t, lo_at(c)] = acc[2 * c]
                            acc_v[t, hi_at(c)] = acc[2 * c + 1]

            pltpu.sync_copy(acc_v, out_hbm.at[pl.ds(tok0, kt)])

    return kern(table, idx.reshape(n_tok * nq, rr), wgt)


def _pack_bf16_pairs(t):
    half = t.shape[1] // 2
    b = lax.bitcast_convert_type(t.astype(BF16), jnp.uint16).astype(U32)
    return b[:, :half] | (b[:, half:] << 16)


def _gelu_erf(x):
    return 0.5 * x * (1.0 + lax.erf(x * (2.0 ** -0.5)))


def _unpack_pairs(words):
    lo = pltpu.bitcast(words << 16, F32)
    hi = pltpu.bitcast(words & jnp.uint32(0xFFFF0000), F32)
    return lo, hi


def _peer_hidden_kernel(gu_ref, gate_ref, x1_ref, shf_ref, scf_ref, w_ref, hd_ref, *, tb):
    half = D_MODEL // 2
    h2 = x1_ref[...] * (1.0 + scf_ref[...]) + shf_ref[...]
    for t in range(tb):
        ulo, uhi = _unpack_pairs(gu_ref[t * PEER_PICKS:(t + 1) * PEER_PICKS, :])
        hd_ref[:, t:t + 1] = jnp.sum(ulo * h2[t:t + 1, :half] + uhi * h2[t:t + 1, half:], axis=-1, keepdims=True)
    w_ref[...] = _gelu_erf(hd_ref[...]) * gate_ref[...]


def _peer_hidden(gu, gate_blocks, x1, mod, tb):
    r = x1.shape[1]
    blk = pl.BlockSpec((None, PEER_PICKS, tb), lambda gi, i: (i, 0, 0))
    return pl.pallas_call(
        functools.partial(_peer_hidden_kernel, tb=tb),
        grid=(1, r // tb),
        in_specs=[pl.BlockSpec((tb * PEER_PICKS, D_MODEL // 2), lambda gi, i: (i, 0)), blk,
                  _rows(tb, D_MODEL), _mod(mod, tb, 3), _mod(mod, tb, 4)],
        out_specs=blk,
        out_shape=jax.ShapeDtypeStruct((r // tb, PEER_PICKS, tb), F32),
        scratch_shapes=[pltpu.VMEM((PEER_PICKS, tb), F32)],
        compiler_params=_cparams("parallel", "parallel"),
        name="peer_hidden",
    )(gu, gate_blocks, x1, mod, mod)


def _peer_out_kernel(x1_ref, ff_ref, gtf_ref, l2g_ref, l2b_ref, o_ref, *, alpha):
    o_ref[...] = _layernorm(alpha * x1_ref[...] + gtf_ref[...] * ff_ref[...], l2g_ref[...], l2b_ref[...])


def _peer_out(x1, ff, mod, p, alpha, tb):
    g, r, _ = x1.shape
    return pl.pallas_call(
        functools.partial(_peer_out_kernel, alpha=alpha),
        grid=(g, r // tb),
        in_specs=[_rows(tb, D_MODEL), _rows(tb, D_MODEL), _mod(mod, tb, 5), _const((1, D_MODEL)), _const((1, D_MODEL))],
        out_specs=_rows(tb, D_MODEL),
        out_shape=jax.ShapeDtypeStruct((g, r, D_MODEL), F32),
        compiler_params=_cparams("parallel", "parallel"),
        name="peer_out_ln2",
    )(x1, ff, mod, _row2(p["ln2_g"]), _row2(p["ln2_b"]))


def _token_stage(x, mod, prev_fn, wkv_fn, attn_fn, p, alpha, tb, sequential):
    ps, pattn, pgate = _inproj(x, mod, p["ln_in_g"], p["ln_in_b"], p["w_in_bf16"], tb)
    r, lw, k, v, kk, kka, gl = _rwkv_prep(ps, prev_fn(), p, tb, sequential)
    y, wkv_new = wkv_fn(r, lw, k, v, kk, kka)
    ya = _rwkv_post(y, r, k, v, gl, p, tb)
    ob = attn_fn(pattn)
    x1, scores_t = _merge(x, ya, ob, pgate, mod, p, alpha, tb)
    idx_t, gate_t = _topk(scores_t, TOPK_TB)
    return ps, pattn, wkv_new, x1, idx_t, gate_t


def kernel(x_prompt, x_sample, state_wkv, state_shift, cache_k_win, cache_v_win, c_prompt, c_sample, ln_in_g, ln_in_b, w_ada, b_ada, w_in, mu_shift, rwkv_w0, rwkv_w2, rwkv_a0, rwkv_a2, rwkv_g2, rwkv_k_k, rwkv_k_a, rwkv_r_k, rwkv_gn_g, rwkv_gn_b, attn_sinks, w_pa, w_pb, w_o, ln1_g, ln1_b, peer_wq, peer_sub_keys, peer_u, peer_v, ln2_g, ln2_b):
    depth = w_in.shape[0]
    assert depth == 1, "single-layer trunk"
    alpha = (2.0 * depth) ** 0.25
    n_p, t_p, _ = x_prompt.shape
    n_s = x_sample.shape[0]
    p = dict(ln_in_g=ln_in_g, ln_in_b=ln_in_b, w_in_bf16=w_in[0].astype(BF16), mu_shift=mu_shift[0],
             rwkv_w0=rwkv_w0[0], rwkv_w2=rwkv_w2[0], rwkv_a0=rwkv_a0[0], rwkv_a2=rwkv_a2[0], rwkv_g2=rwkv_g2[0],
             rwkv_k_k=rwkv_k_k[0], rwkv_k_a=rwkv_k_a[0], rwkv_r_k=rwkv_r_k[0], rwkv_gn_g=rwkv_gn_g[0],
             rwkv_gn_b=rwkv_gn_b[0], w_pa=w_pa[0], w_pb=w_pb[0], w_o=w_o[0], ln1_g=ln1_g[0], ln1_b=ln1_b[0],
             peer_wq=peer_wq[0], peer_sub_keys=peer_sub_keys[0], ln2_g=ln2_g[0], ln2_b=ln2_b[0])
    sinks = attn_sinks[0]

    n_c = n_p + n_s
    pad = (-n_c) % 8
    c_all = jnp.concatenate([c_prompt, c_sample, jnp.zeros((pad, D_MODEL), F32)], axis=0)
    mod_all = _modulation(c_all, w_ada[0], b_ada[0])
    mod_p = mod_all[:n_p].reshape(n_p, 1, N_MOD * D_MODEL)
    mod_s = mod_all[n_p:n_c].reshape(1, n_s, N_MOD * D_MODEL)

    seg = min(PROMPT_SEGMENT, t_p)
    assert t_p % seg == 0
    first = min(PROMPT_FIRST_SEGMENT, seg)

    def segments(b):
        cuts = list(range(0, t_p + 1, seg))
        if b == 0 and first < seg:
            cuts.insert(1, first)
        return zip(cuts[:-1], cuts[1:])

    prompt_ids = [(b, lo, hi) for b in range(n_p) for lo, hi in segments(b)]
    assert all((hi - lo) % TOKEN_TB == 0 for _, lo, hi in prompt_ids)
    carry = {}

    def prompt_group(b, lo, hi):
        def prev_fn():
            return carry[b][0] if lo > 0 else jnp.zeros((1, 1, SHIFT_W), F32)

        def wkv_fn(r, lw, k, v, kk, kka):
            s0 = carry[b][1] if lo > 0 else jnp.zeros((1, H_A, HD_A, HD_A), F32)
            return _rwkv_chunk_scan(r, lw, k, v, kk, kka, s0)

        def attn_fn(pa):
            return _attn_band(pa, carry[b][2] if lo > 0 else None, sinks)

        return (x_prompt[b:b + 1, lo:hi], mod_p[b:b + 1], prev_fn, wkv_fn, attn_fn, TOKEN_TB, True)

    tu, tv = _pack_bf16_pairs(peer_u[0]), _pack_bf16_pairs(peer_v[0])

    def select(x, mod, prev_fn, wkv_fn, attn_fn, tb, sequential):
        ps, pattn, wkv_new, x1, idx_t, gate_t = _token_stage(x, mod, prev_fn, wkv_fn, attn_fn, p, alpha, tb, sequential)
        r = x1.shape[1]
        idx = jnp.transpose(idx_t.reshape(PEER_PICKS, r))
        gate_blocks = jnp.transpose(gate_t.reshape(PEER_PICKS, r // PEER_TB, PEER_TB), (1, 0, 2))
        gu = _sc_gather_rows(tu, idx.reshape(-1))
        return ps, pattn, wkv_new, (gu, idx, gate_blocks, x1, mod)

    def weigh(sel):
        gu, idx, gate_blocks, x1, mod = sel
        w_blocks = _peer_hidden(gu, gate_blocks, x1, mod, PEER_TB)
        wgt = jnp.transpose(w_blocks, (0, 2, 1)).reshape(-1, PEER_PICKS)
        return _sc_weighted_row_sum(tv, idx, wgt), x1, mod

    def finish(wsum):
        ff, x1, mod = wsum
        return _peer_out(x1, ff[None], mod, p, alpha, min(TOKEN_TB, x1.shape[1]))

    xs = x_sample.reshape(1, n_s, D_MODEL)

    def prev_s():
        return state_shift[0].reshape(1, n_s, SHIFT_W)

    def wkv_s(r, lw, k, v, kk, kka):
        sq = lambda z: z.reshape(n_s, D_A)
        y, s = _rwkv_step(state_wkv[0], sq(r), sq(lw), sq(k), sq(kk), sq(kka), sq(v), STEP_NB)
        return y.reshape(1, n_s, D_A), s

    def attn_s(pa):
        o = _attn_cache(pa.reshape(n_s, 1, ATTN_W), cache_k_win[0], cache_v_win[0], sinks, STEP_NB)
        return o.reshape(1, n_s, D_B)

    n_g = len(prompt_ids) + 1
    sel, wsum, y_l = [None] * n_g, [None] * n_g, [None] * n_g
    for step in range(n_g + 2):
        if step < n_g - 1:
            b, lo, hi = prompt_ids[step]
            ps, pattn, wkv_new, sel[step] = select(*prompt_group(b, lo, hi))
            carry[b] = (ps[:, -1:], wkv_new, pattn[:, -WINDOW:])
        elif step == n_g - 1:
            ps_s, pattn_s, wkv_s_new, sel[step] = select(xs, mod_s, prev_s, wkv_s, attn_s, min(TOKEN_TB, n_s), False)
        if 0 <= step - 1 < n_g:
            wsum[step - 1] = weigh(sel[step - 1])
        if 0 <= step - 2 < n_g:
            y_l[step - 2] = finish(wsum[step - 2])
    y_s = y_l[n_g - 1]
    y_p = jnp.concatenate(y_l[:n_g - 1], axis=1).reshape(n_p, t_p, D_MODEL)
    shift_p = jnp.concatenate([carry[b][0][:, 0] for b in range(n_p)], axis=0)
    pattn_p = jnp.concatenate([carry[b][2] for b in range(n_p)], axis=0)
    wkv_p_new = jnp.concatenate([carry[b][1] for b in range(n_p)], axis=0)

    kv = lambda pa, o: pa[..., o:o + H_KV * HD_B]
    ko, vo = D_B, D_B + H_KV * HD_B
    k_win_p = kv(pattn_p, ko)[:, -WINDOW:].reshape(n_p, WINDOW, H_KV, HD_B)
    v_win_p = kv(pattn_p, vo)[:, -WINDOW:].reshape(n_p, WINDOW, H_KV, HD_B)
    k_new_s = kv(pattn_s, ko).reshape(n_s, 1, H_KV, HD_B)
    v_new_s = kv(pattn_s, vo).reshape(n_s, 1, H_KV, HD_B)
    k_win_s = jnp.concatenate([cache_k_win[0], k_new_s], axis=1)[:, -WINDOW:]
    v_win_s = jnp.concatenate([cache_v_win[0], v_new_s], axis=1)[:, -WINDOW:]
    return (y_p, y_s.reshape(n_s, 1, D_MODEL), wkv_p_new[None], wkv_s_new[None],
            shift_p[None], ps_s.reshape(n_s, SHIFT_W)[None],
            k_win_p[None], k_win_s[None], v_win_p[None], v_win_s[None])
```

```python
import functools
import math

import jax
import jax.numpy as jnp
from jax import lax
from jax.experimental import pallas as pl
from jax.experimental.pallas import tpu as pltpu
from jax.experimental.pallas import tpu_sc as plsc

F32 = jnp.float32
BF16 = jnp.bfloat16
I32 = jnp.int32
U32 = jnp.uint32

D_MODEL = 1024
H_A, HD_A = 8, 64
D_A = H_A * HD_A
D_LORA_W, D_LORA_A, D_LORA_G = 64, 64, 128
GN_EPS = 64e-5
H_Q, H_KV, HD_B = 8, 2, 64
G_Q = H_Q // H_KV
D_B = H_Q * HD_B
WINDOW = 128
N_KEYS = 128
PEER_HEADS, PEER_TOPK, PEER_HALF = 8, 16, 128
PEER_PICKS = PEER_HEADS * PEER_TOPK
N_MOD = 6
LN_EPS = 1e-5
NEG_INF = -1e30
OFF_WD = 3 * D_A
OFF_AD = OFF_WD + D_LORA_W
OFF_GD = OFF_AD + D_LORA_A
SHIFT_W = OFF_GD + D_LORA_G
ATTN_W = D_B + 2 * H_KV * HD_B
GATE_W = 2 * D_MODEL
D_IN = SHIFT_W + ATTN_W + GATE_W

VMEM_LIMIT = 48 * 1024 * 1024
SUBLANES = 8
RWKV_CHUNK = 64
SC_ACC_ROWS = 32
SC_ACC_AHEAD = 3
SC_KEY_TOKENS = 8
SC_ACC_TOKENS = 16
SC_ACC_PANELS = 4
TOKEN_TB = 256
TOPK_TB = 128
PEER_TB = 16
PROMPT_SEGMENT = 2048
PROMPT_FIRST_SEGMENT = 512
STEP_NB = 8


def _cparams(*sem):
    return pltpu.CompilerParams(dimension_semantics=sem, vmem_limit_bytes=VMEM_LIMIT)


def _layernorm(x, g, b):
    mu = jnp.mean(x, -1, keepdims=True)
    xc = x - mu
    var = jnp.mean(xc * xc, -1, keepdims=True)
    return xc * lax.rsqrt(var + LN_EPS) * g + b


def _split(x):
    hi = x.astype(BF16)
    lo = (x - hi.astype(F32)).astype(BF16)
    return hi, lo


_NN = (((1,), (0,)), ((), ()))
_NT = (((1,), (1,)), ((), ()))
_TN = (((0,), (0,)), ((), ()))


def _dot3(a, b, dims=_NN):
    ah, al = _split(a)
    bh, bl = _split(b)
    d = functools.partial(lax.dot_general, dimension_numbers=dims, preferred_element_type=F32)
    return d(ah, bh) + d(ah, bl) + d(al, bh)


def _dot_exact_lhs(a_bf16, b, dims=_NN):
    b1 = b.astype(BF16)
    r1 = b - b1.astype(F32)
    b2 = r1.astype(BF16)
    b3 = (r1 - b2.astype(F32)).astype(BF16)
    d = functools.partial(lax.dot_general, dimension_numbers=dims, preferred_element_type=F32)
    return d(a_bf16, b1) + d(a_bf16, b2) + d(a_bf16, b3)


def _dotb(a, b, dims=_NN):
    return lax.dot_general(a.astype(BF16), b.astype(BF16), dims, preferred_element_type=F32)


def _rows(tb, width, col=0):
    return pl.BlockSpec((None, tb, width), lambda g, i: (g, i, col))


def _mod(mod, tb, col):
    if mod.shape[1] == 1:
        return pl.BlockSpec((None, 1, D_MODEL), lambda g, i: (g, 0, col))
    return pl.BlockSpec((None, tb, D_MODEL), lambda g, i: (g, i, col))


def _const(shape):
    n = len(shape)
    return pl.BlockSpec(shape, lambda g, i: (0,) * n)


def _row2(p):
    return p.reshape(1, -1).astype(F32)


def _mod_kernel(c_ref, w_ref, b_ref, o_ref):
    c = c_ref[...]
    a = c * jax.nn.sigmoid(c)
    o_ref[...] = _dot3(a, w_ref[...]) + b_ref[...]


def _modulation(c, w_ada, b_ada):
    n = c.shape[0]
    tn = D_MODEL
    return pl.pallas_call(
        _mod_kernel,
        grid=(w_ada.shape[1] // tn,),
        in_specs=[pl.BlockSpec((n, D_MODEL), lambda j: (0, 0)),
                  pl.BlockSpec((D_MODEL, tn), lambda j: (0, j)),
                  pl.BlockSpec((1, tn), lambda j: (0, j))],
        out_specs=pl.BlockSpec((n, tn), lambda j: (0, j)),
        out_shape=jax.ShapeDtypeStruct((n, w_ada.shape[1]), F32),
        compiler_params=_cparams("arbitrary"),
        name="modulation",
    )(c, w_ada, b_ada.reshape(1, -1))


def _inproj_kernel(x_ref, sh_ref, sc_ref, g_ref, b_ref, w_ref, ps_ref, pa_ref, pg_ref):
    xn = _layernorm(x_ref[...], g_ref[...], b_ref[...])
    h = (xn * (1.0 + sc_ref[...]) + sh_ref[...]).astype(BF16)
    ps_ref[...] = jnp.dot(h, w_ref[:, :SHIFT_W], preferred_element_type=F32)
    pa_ref[...] = jnp.dot(h, w_ref[:, SHIFT_W:SHIFT_W + ATTN_W], preferred_element_type=F32)
    pg_ref[...] = jnp.dot(h, w_ref[:, SHIFT_W + ATTN_W:], preferred_element_type=F32)


def _inproj(x, mod, ln_g, ln_b, w_in_bf16, tb):
    g, r, _ = x.shape
    shp = lambda w: jax.ShapeDtypeStruct((g, r, w), F32)
    return pl.pallas_call(
        _inproj_kernel,
        grid=(g, r // tb),
        in_specs=[_rows(tb, D_MODEL), _mod(mod, tb, 0), _mod(mod, tb, 1),
                  _const((1, D_MODEL)), _const((1, D_MODEL)), _const((D_MODEL, D_IN))],
        out_specs=[_rows(tb, SHIFT_W), _rows(tb, ATTN_W), _rows(tb, GATE_W)],
        out_shape=[shp(SHIFT_W), shp(ATTN_W), shp(GATE_W)],
        compiler_params=_cparams("parallel", "parallel"),
        name="inproj",
    )(x, mod, mod, _row2(ln_g), _row2(ln_b), w_in_bf16)


def _softplus(x):
    return jnp.maximum(x, 0.0) + jnp.log1p(jnp.exp(-jnp.abs(x)))


def _rwkv_prep_kernel(ps_ref, prev_ref, first_ref, mu_ref, w0_ref, w2_ref, a0_ref, a2_ref, g2_ref, kk_w_ref, ka_w_ref,
                      hsum_ref, r_ref, lw_ref, k_ref, v_ref, kk_ref, kka_ref, g_ref, *, sequential):
    ps = ps_ref[...]
    if sequential:
        before = jnp.where(pl.program_id(1) == 0, first_ref[...], prev_ref[SUBLANES - 1:SUBLANES, :])
        row = lax.broadcasted_iota(I32, ps.shape, 0)
        prev = jnp.where(row == 0, before, pltpu.roll(ps, 1, 0))
    else:
        prev = prev_ref[...]
    xs = ps + (prev - ps) * mu_ref[...]
    r = xs[:, 0:D_A]
    k = xs[:, D_A:2 * D_A]
    v = xs[:, 2 * D_A:3 * D_A]
    wd = xs[:, OFF_WD:OFF_AD]
    ad = xs[:, OFF_AD:OFF_GD]
    gd = xs[:, OFF_GD:SHIFT_W]
    z = w0_ref[...] + _dot3(jnp.tanh(wd), w2_ref[...])
    w_log = -_softplus(-z) - 0.5
    a = jax.nn.sigmoid(a0_ref[...] + _dot3(ad, a2_ref[...]))
    kk = k * kk_w_ref[...]
    ss = _dot3(kk * kk, hsum_ref[...])
    kk = kk / jnp.maximum(jnp.sqrt(ss), 1e-12)
    r_ref[...] = r
    lw_ref[...] = -jnp.exp(w_log)
    k_ref[...] = k * (1.0 + (a - 1.0) * ka_w_ref[...])
    v_ref[...] = v
    kk_ref[...] = kk
    kka_ref[...] = kk * a
    g_ref[...] = _dot3(jax.nn.sigmoid(gd), g2_ref[...])


def _head_sum_matrix():
    h = jnp.arange(D_A) // HD_A
    return (h[:, None] == h[None, :]).astype(F32)


def _rwkv_prep(ps, prev, p, tb, sequential):
    g, r, _ = ps.shape
    shp = jax.ShapeDtypeStruct((g, r, D_A), F32)
    if sequential:
        per = tb // SUBLANES
        prev_args = (ps, prev)
        prev_specs = [pl.BlockSpec((None, SUBLANES, SHIFT_W), lambda gi, i: (gi, jnp.maximum(i * per - 1, 0), 0)),
                      pl.BlockSpec((None, 1, SHIFT_W), lambda gi, i: (gi, 0, 0))]
    else:
        prev_args = (prev, prev[:, :1])
        prev_specs = [_rows(tb, SHIFT_W), pl.BlockSpec((None, 1, SHIFT_W), lambda gi, i: (gi, 0, 0))]
    return pl.pallas_call(
        functools.partial(_rwkv_prep_kernel, sequential=sequential),
        grid=(g, r // tb),
        in_specs=[_rows(tb, SHIFT_W)] + prev_specs + [_const((1, SHIFT_W)),
                  _const((1, D_A)), _const((D_LORA_W, D_A)), _const((1, D_A)), _const((D_LORA_A, D_A)),
                  _const((D_LORA_G, D_A)), _const((1, D_A)), _const((1, D_A)), _const((D_A, D_A))],
        out_specs=[_rows(tb, D_A)] * 7,
        out_shape=[shp] * 7,
        compiler_params=_cparams("parallel", "parallel"),
        name="rwkv_prep",
    )(ps, *prev_args, _row2(p["mu_shift"]), _row2(p["rwkv_w0"]), p["rwkv_w2"], _row2(p["rwkv_a0"]), p["rwkv_a2"],
      p["rwkv_g2"], _row2(p["rwkv_k_k"]), _row2(p["rwkv_k_a"]), _head_sum_matrix())


def _rwkv_chunk_kernel(r_ref, lw_ref, k_ref, v_ref, kk_ref, kka_ref, s0_ref, y_ref, s_ref):
    c = RWKV_CHUNK

    @pl.when(pl.program_id(1) == 0)
    def _():
        s_ref[...] = s0_ref[...]

    row = lax.broadcasted_iota(I32, (c, c), 0)
    col = lax.broadcasted_iota(I32, (c, c), 1)
    tril = row >= col
    stril = row > col
    lw = lw_ref[...]
    cum = _dot_exact_lhs(tril.astype(BF16), lw)
    cum_end = cum[c - 1:c, :]
    g_inv = jnp.exp(-cum)
    g_end = jnp.exp(cum_end - cum)
    a_hat = -kk_ref[...] * jnp.exp(cum - lw)
    b_hat = kka_ref[...] * g_inv
    k_hat = k_ref[...] * g_inv
    r_til = r_ref[...] * jnp.exp(cum)
    b_end = kka_ref[...] * g_end
    k_end = k_ref[...] * g_end
    gam_end = jnp.exp(cum_end)
    v_all = v_ref[...]
    s_all = s_ref[...]
    n_steps = int(math.log2(c))
    heads = range(H_A)
    sl = [slice(h * HD_A, (h + 1) * HD_A) for h in heads]
    vh = [v_all[:, sl[h]] for h in heads]
    ar = [jnp.concatenate([a_hat[:, sl[h]], r_til[:, sl[h]]], axis=0) for h in heads]
    bk = [jnp.concatenate([b_hat[:, sl[h]], k_hat[:, sl[h]]], axis=0) for h in heads]
    x = [_dot3(ar[h], bk[h], _NT) for h in heads]
    ars = [_dot3(ar[h], s_all[h], _NT) for h in heads]
    a_ak = [jnp.where(stril, x[h][:c, c:], 0.0) for h in heads]
    n = [jnp.where(stril, x[h][:c, :c], 0.0) for h in heads]
    u = [ars[h][:c] + _dot3(a_ak[h], vh[h]) for h in heads]
    for it in range(n_steps):
        u = [u[h] + _dot3(n[h], u[h]) for h in heads]
        if it + 1 < n_steps:
            n = [_dot3(n[h], n[h]) for h in heads]
    uv = [jnp.concatenate([u[h], vh[h]], axis=0) for h in heads]
    a_r = [jnp.concatenate([jnp.where(tril, x[h][c:, :c], 0.0), jnp.where(tril, x[h][c:, c:], 0.0)], axis=1)
           for h in heads]
    y = [ars[h][c:] + _dot3(a_r[h], uv[h]) for h in heads]
    bke = [jnp.concatenate([b_end[:, sl[h]], k_end[:, sl[h]]], axis=0) for h in heads]
    s_new = [s_all[h] * gam_end[:, sl[h]] + _dot3(uv[h], bke[h], _TN) for h in heads]
    for h in heads:
        y_ref[:, sl[h]] = y[h]
        s_ref[h] = s_new[h]


def _rwkv_chunk_scan(r, lw, k, v, kk, kka, s0):
    n, t, _ = r.shape
    c = RWKV_CHUNK
    seq = pl.BlockSpec((None, c, D_A), lambda b, i: (b, i, 0))
    st = pl.BlockSpec((None, H_A, HD_A, HD_A), lambda b, i: (b, 0, 0, 0))
    return pl.pallas_call(
        _rwkv_chunk_kernel,
        grid=(n, t // c),
        in_specs=[seq] * 6 + [st],
        out_specs=[seq, st],
        out_shape=[jax.ShapeDtypeStruct((n, t, D_A), F32), jax.ShapeDtypeStruct((n, H_A, HD_A, HD_A), F32)],
        compiler_params=_cparams("parallel", "arbitrary"),
        name="rwkv_chunk_scan",
    )(r, lw, k, v, kk, kka, s0)


def _rwkv_step_kernel(s_ref, r_ref, lw_ref, k_ref, kk_ref, kka_ref, v_ref, y_ref, so_ref):
    s = s_ref[...]
    sa = jnp.sum(s * (-kk_ref[...]), axis=-1, keepdims=True)
    s = s * jnp.exp(lw_ref[...]) + sa * kka_ref[...] + v_ref[...] * k_ref[...]
    so_ref[...] = s
    y_ref[...] = jnp.sum(s * r_ref[...], axis=-1, keepdims=True)


def _rwkv_step(s0, r, lw, k, kk, kka, v, nb):
    n = s0.shape[0]
    key = lambda z: z.reshape(n, H_A, 1, HD_A)
    st = pl.BlockSpec((nb, H_A, HD_A, HD_A), lambda i: (i, 0, 0, 0))
    ks = pl.BlockSpec((nb, H_A, 1, HD_A), lambda i: (i, 0, 0, 0))
    vs = pl.BlockSpec((nb, H_A, HD_A, 1), lambda i: (i, 0, 0, 0))
    y, s = pl.pallas_call(
        _rwkv_step_kernel,
        grid=(n // nb,),
        in_specs=[st, ks, ks, ks, ks, ks, vs],
        out_specs=[vs, st],
        out_shape=[jax.ShapeDtypeStruct((n, H_A, HD_A, 1), F32), jax.ShapeDtypeStruct(s0.shape, F32)],
        compiler_params=_cparams("parallel"),
        name="rwkv_step",
    )(s0, key(r), key(lw), key(k), key(kk), key(kka), v.reshape(n, H_A, HD_A, 1))
    return y.reshape(n, D_A), s


def _rwkv_post_kernel(y_ref, r_ref, k_ref, v_ref, g_ref, gn_g_ref, gn_b_ref, rk_ref, hsum_ref, o_ref):
    y = y_ref[...]
    hs = hsum_ref[...]
    mu = _dot3(y, hs) * (1.0 / HD_A)
    yc = y - mu
    var = _dot3(yc * yc, hs) * (1.0 / HD_A)
    yn = yc * lax.rsqrt(var + GN_EPS) * gn_g_ref[...] + gn_b_ref[...]
    bonus = _dot3(r_ref[...] * k_ref[...] * rk_ref[...], hs) * v_ref[...]
    o_ref[...] = (yn + bonus) * g_ref[...]


def _rwkv_post(y, r, k, v, g, p, tb):
    gg, rr, _ = y.shape
    return pl.pallas_call(
        _rwkv_post_kernel,
        grid=(gg, rr // tb),
        in_specs=[_rows(tb, D_A)] * 5 + [_const((1, D_A))] * 3 + [_const((D_A, D_A))],
        out_specs=_rows(tb, D_A),
        out_shape=jax.ShapeDtypeStruct((gg, rr, D_A), F32),
        compiler_params=_cparams("parallel", "parallel"),
        name="rwkv_post",
    )(y, r, k, v, g, _row2(p["rwkv_gn_g"]), _row2(p["rwkv_gn_b"]), _row2(p["rwkv_r_k"]), _head_sum_matrix())


def _sink_softmax(s, sink):
    m = jnp.maximum(jnp.max(s, axis=-1, keepdims=True), sink)
    p = jnp.exp(s - m)
    den = jnp.sum(p, axis=-1, keepdims=True) + jnp.exp(sink - m)
    return p / den


def _attn_band_kernel(cur_ref, prev_ref, carry_ref, sink_ref, o_ref, *, has_carry):
    blk = WINDOW
    i = pl.program_id(1)
    cur = cur_ref[...]
    prev = jnp.where(i == 0, carry_ref[...], prev_ref[...])
    qi = lax.broadcasted_iota(I32, (G_Q * blk, 2 * blk), 0) % blk
    kj = lax.broadcasted_iota(I32, (G_Q * blk, 2 * blk), 1)
    rel = blk + qi - kj
    valid = (rel >= 0) & (rel <= WINDOW)
    if not has_carry:
        valid = valid & ((kj >= blk) | (i > 0))
    relf = rel.astype(F32)
    gidx = lax.broadcasted_iota(I32, (G_Q * blk, 1), 0) // blk
    for kvh in range(H_KV):
        q4 = jnp.concatenate([cur[:, (kvh * G_Q + g) * HD_B:(kvh * G_Q + g + 1) * HD_B] for g in range(G_Q)], axis=0)
        ko = D_B + kvh * HD_B
        vo = D_B + H_KV * HD_B + kvh * HD_B
        kmat = jnp.concatenate([prev[:, ko:ko + HD_B], cur[:, ko:ko + HD_B]], axis=0)
        vmat = jnp.concatenate([prev[:, vo:vo + HD_B], cur[:, vo:vo + HD_B]], axis=0)
        slope = jnp.zeros((G_Q * blk, 1), F32)
        sink = jnp.zeros((G_Q * blk, 1), F32)
        for g in range(G_Q):
            hq = kvh * G_Q + g
            slope = jnp.where(gidx == g, 2.0 ** (-8.0 * (hq + 1) / H_Q), slope)
            sink = jnp.where(gidx == g, sink_ref[hq], sink)
        s = _dotb(q4, kmat, _NT) * (HD_B ** -0.5)
        s = jnp.where(valid, s - slope * relf, NEG_INF)
        p = _sink_softmax(s, sink)
        o = _dotb(p, vmat)
        for g in range(G_Q):
            hq = kvh * G_Q + g
            o_ref[:, hq * HD_B:(hq + 1) * HD_B] = o[g * blk:(g + 1) * blk]


def _attn_band(pattn, carry, sinks):
    n, t, _ = pattn.shape
    blk = WINDOW
    has_carry = carry is not None
    if not has_carry:
        carry = jnp.zeros((n, blk, ATTN_W), F32)
    return pl.pallas_call(
        functools.partial(_attn_band_kernel, has_carry=has_carry),
        grid=(n, t // blk),
        in_specs=[pl.BlockSpec((None, blk, ATTN_W), lambda b, i: (b, i, 0)),
                  pl.BlockSpec((None, blk, ATTN_W), lambda b, i: (b, jnp.maximum(i - 1, 0), 0)),
                  pl.BlockSpec((None, blk, ATTN_W), lambda b, i: (b, 0, 0)),
                  pl.BlockSpec(memory_space=pltpu.SMEM)],
        out_specs=pl.BlockSpec((None, blk, D_B), lambda b, i: (b, i, 0)),
        out_shape=jax.ShapeDtypeStruct((n, t, D_B), F32),
        compiler_params=_cparams("parallel", "parallel"),
        name="attn_band",
    )(pattn, pattn, carry, sinks.astype(F32))


def _attn_cache_kernel(cur_ref, kc_ref, vc_ref, sink_ref, o_ref, *, nb):
    relc = (WINDOW - lax.broadcasted_iota(I32, (G_Q, WINDOW), 1)).astype(F32)
    gidx = lax.broadcasted_iota(I32, (G_Q, 1), 0)
    for b in range(nb):
        cur = cur_ref[b]
        for kvh in range(H_KV):
            q4 = jnp.concatenate([cur[:, (kvh * G_Q + g) * HD_B:(kvh * G_Q + g + 1) * HD_B] for g in range(G_Q)], axis=0)
            ko = D_B + kvh * HD_B
            vo = D_B + H_KV * HD_B + kvh * HD_B
            k_new = cur[:, ko:ko + HD_B]
            v_new = cur[:, vo:vo + HD_B]
            kc = kc_ref[b, :, kvh * HD_B:(kvh + 1) * HD_B]
            vc = vc_ref[b, :, kvh * HD_B:(kvh + 1) * HD_B]
            slope = jnp.zeros((G_Q, 1), F32)
            sink = jnp.zeros((G_Q, 1), F32)
            for g in range(G_Q):
                hq = kvh * G_Q + g
                slope = jnp.where(gidx == g, 2.0 ** (-8.0 * (hq + 1) / H_Q), slope)
                sink = jnp.where(gidx == g, sink_ref[hq], sink)
            scale = HD_B ** -0.5
            sc = _dotb(q4, kc, _NT) * scale - slope * relc
            sn = jnp.sum(q4.astype(BF16).astype(F32) * k_new.astype(BF16).astype(F32), axis=-1, keepdims=True) * scale
            m = jnp.maximum(jnp.maximum(jnp.max(sc, axis=-1, keepdims=True), sn), sink)
            pc = jnp.exp(sc - m)
            pn = jnp.exp(sn - m)
            den = jnp.sum(pc, axis=-1, keepdims=True) + pn + jnp.exp(sink - m)
            o = (_dotb(pc / den, vc) + (pn / den).astype(BF16).astype(F32) * v_new.astype(BF16).astype(F32))
            for g in range(G_Q):
                hq = kvh * G_Q + g
                o_ref[b, :, hq * HD_B:(hq + 1) * HD_B] = o[g:g + 1]


def _attn_cache(pattn, k_buf, v_buf, sinks, nb):
    n = pattn.shape[0]
    kc = k_buf.reshape(n, WINDOW, H_KV * HD_B)
    vc = v_buf.reshape(n, WINDOW, H_KV * HD_B)
    return pl.pallas_call(
        functools.partial(_attn_cache_kernel, nb=nb),
        grid=(n // nb,),
        in_specs=[pl.BlockSpec((nb, 1, ATTN_W), lambda i: (i, 0, 0)),
                  pl.BlockSpec((nb, WINDOW, H_KV * HD_B), lambda i: (i, 0, 0)),
                  pl.BlockSpec((nb, WINDOW, H_KV * HD_B), lambda i: (i, 0, 0)),
                  pl.BlockSpec(memory_space=pltpu.SMEM)],
        out_specs=pl.BlockSpec((nb, 1, D_B), lambda i: (i, 0, 0)),
        out_shape=jax.ShapeDtypeStruct((n, 1, D_B), F32),
        compiler_params=_cparams("parallel"),
        name="attn_cache",
    )(pattn, kc, vc, sinks.astype(F32))


def _merge_kernel(x_ref, ya_ref, ob_ref, pg_ref, gtm_ref, shf_ref, scf_ref, lng_ref, lnb_ref, l1g_ref, l1b_ref,
                  wpa_ref, wpb_ref, wo_ref, wq_ref, sk_ref, x1_ref, h2_ref, st_ref, *, alpha):
    ya = jnp.dot(ya_ref[...].astype(BF16), wpa_ref[...], preferred_element_type=F32)
    yb = jnp.dot(ob_ref[...].astype(BF16), wpb_ref[...], preferred_element_type=F32)
    pg = pg_ref[...]
    merged = jax.nn.sigmoid(pg[:, :D_MODEL]) * ya + jax.nn.sigmoid(pg[:, D_MODEL:]) * yb
    mix = jnp.dot(merged.astype(BF16), wo_ref[...], preferred_element_type=F32)
    xn = _layernorm(x_ref[...], lng_ref[...], lnb_ref[...])
    x1 = _layernorm(alpha * xn + gtm_ref[...] * mix, l1g_ref[...], l1b_ref[...])
    x1_ref[...] = x1
    h2 = x1 * (1.0 + scf_ref[...]) + shf_ref[...]
    h2_ref[...] = h2
    q = jnp.dot(h2.astype(BF16), wq_ref[...], preferred_element_type=F32)
    for hc in range(2 * PEER_HEADS):
        st_ref[hc] = _dot3(sk_ref[hc % 2], q[:, hc * PEER_HALF:(hc + 1) * PEER_HALF], _NT)


def _merge(x, ya, ob, pg, mod, p, alpha, tb):
    g, r, _ = x.shape
    return pl.pallas_call(
        functools.partial(_merge_kernel, alpha=alpha),
        grid=(g, r // tb),
        in_specs=[_rows(tb, D_MODEL), _rows(tb, D_A), _rows(tb, D_B), _rows(tb, GATE_W),
                  _mod(mod, tb, 2), _mod(mod, tb, 3), _mod(mod, tb, 4)]
                 + [_const((1, D_MODEL))] * 4
                 + [_const((D_A, D_MODEL)), _const((D_B, D_MODEL)), _const((D_MODEL, D_MODEL)),
                    _const((D_MODEL, 2 * PEER_HEADS * PEER_HALF)), _const((2, N_KEYS, PEER_HALF))],
        out_specs=[_rows(tb, D_MODEL), _rows(tb, D_MODEL),
                   pl.BlockSpec((None, 2 * PEER_HEADS, N_KEYS, tb), lambda gi, i: (gi, 0, 0, i))],
        out_shape=[jax.ShapeDtypeStruct((g, r, D_MODEL), F32), jax.ShapeDtypeStruct((g, r, D_MODEL), F32),
                   jax.ShapeDtypeStruct((g, 2 * PEER_HEADS, N_KEYS, r), F32)],
        compiler_params=_cparams("parallel", "parallel"),
        name="merge_ln1_peer_scores",
    )(x, ya, ob, pg, mod, mod, mod, _row2(p["ln_in_g"]), _row2(p["ln_in_b"]), _row2(p["ln1_g"]), _row2(p["ln1_b"]),
      p["w_pa"].astype(BF16), p["w_pb"].astype(BF16), p["w_o"].astype(BF16), p["peer_wq"].astype(BF16),
      p["peer_sub_keys"])


def _extract_top(problems, n_rows, tb):
    rio = lax.broadcasted_iota(I32, (n_rows, tb), 0).astype(F32)
    vals = [v for v, _ in problems]
    tops = [([], [], []) for _ in problems]
    for _ in range(PEER_TOPK):
        for n, (_, payload) in enumerate(problems):
            m = jnp.max(vals[n], axis=0, keepdims=True)
            i = jnp.min(jnp.where(vals[n] == m, rio, float(n_rows)), axis=0, keepdims=True)
            sel = rio == i
            tops[n][0].append(m)
            tops[n][1].append(i)
            if payload is not None:
                tops[n][2].append(jnp.max(jnp.where(sel, payload, -1.0), axis=0, keepdims=True))
            vals[n] = jnp.where(sel, -jnp.inf, vals[n])
    cat = lambda z: jnp.concatenate(z, axis=0) if z else None
    return [(cat(v), cat(i), cat(pl_)) for v, i, pl_ in tops]


def _pair_candidates(v1, i1, v2, i2, tb):
    k = PEER_TOPK
    sub = 8
    eid = lambda a0, a1, b0, b1: i1[a0:a1] * float(N_KEYS) + i2[b0:b1]
    vals = [v1[0:1] + v2, v1[1:2] + v2[0:sub]]
    ids = [eid(0, 1, 0, k), eid(1, 2, 0, sub)]
    brow = lax.broadcasted_iota(I32, (sub, tb), 0)
    for a in range(2, sub):
        vals.append(jnp.where(brow < k // (a + 1), v1[a:a + 1] + v2[0:sub], -jnp.inf))
        ids.append(eid(a, a + 1, 0, sub))
    vals.append(v1[sub:k] + v2[0:1])
    ids.append(eid(sub, k, 0, 1))
    return jnp.concatenate(vals, axis=0), jnp.concatenate(ids, axis=0)


def _topk_kernel(s_ref, idx_ref, gate_ref):
    tb = s_ref.shape[-1]

    def head_pair(hp, carry):
        tops = _extract_top([(s_ref[4 * hp + n], None) for n in range(4)], N_KEYS, tb)
        cands = [_pair_candidates(tops[2 * n][0], tops[2 * n][1], tops[2 * n + 1][0], tops[2 * n + 1][1], tb)
                 for n in range(2)]
        picked = _extract_top(cands, cands[0][0].shape[0], tb)
        for n, (sc, _, ex) in enumerate(picked):
            pexp = jnp.exp(sc - sc[0:1])
            idx_ref[2 * hp + n] = ex.astype(I32)
            gate_ref[2 * hp + n] = pexp / jnp.sum(pexp, axis=0, keepdims=True)
        return carry

    lax.fori_loop(0, PEER_HEADS // 2, head_pair, 0)


def _topk(scores_t, tb):
    g, _, _, r = scores_t.shape
    out = pl.BlockSpec((None, PEER_HEADS, PEER_TOPK, tb), lambda gi, i: (gi, 0, 0, i))
    return pl.pallas_call(
        _topk_kernel,
        grid=(g, r // tb),
        in_specs=[pl.BlockSpec((None, 2 * PEER_HEADS, N_KEYS, tb), lambda gi, i: (gi, 0, 0, i))],
        out_specs=[out, out],
        out_shape=[jax.ShapeDtypeStruct((g, PEER_HEADS, PEER_TOPK, r), I32),
                   jax.ShapeDtypeStruct((g, PEER_HEADS, PEER_TOPK, r), F32)],
        compiler_params=_cparams("parallel", "parallel"),
        name="peer_topk",
    )(scores_t)


def _sc_mesh():
    info = plsc.get_sparse_core_info()
    mesh = plsc.VectorSubcoreMesh(core_axis_name="c", subcore_axis_name="s")
    return info.num_cores, info.num_subcores, info.num_lanes, mesh


def _sc_key_stage(table, idx, x):
    nc, ns, lanes, mesh = _sc_mesh()
    nw = nc * ns
    n_tok, picks = idx.shape
    ww = table.shape[1]
    rr = SC_ACC_ROWS
    nq = picks // rr
    ng = nq // 2
    half = ng * rr
    tpw = n_tok // nw
    kt = min(SC_KEY_TOKENS, tpw)
    n_outer = tpw // kt
    pw = ww // SC_ACC_PANELS
    nv = pw // lanes
    assert tpw * nw == n_tok and n_outer * kt == tpw and kt % 2 == 0 and nq == 4 and nv * lanes == pw
    cp = pltpu.CompilerParams(needs_layout_passes=False)
    buf = pltpu.VMEM((rr, ww), table.dtype)
    dma = pltpu.SemaphoreType.DMA

    @functools.partial(
        pl.kernel, mesh=mesh,
        out_type=(jax.ShapeDtypeStruct((n_tok * half, ww), table.dtype), jax.ShapeDtypeStruct((n_tok * half,), F32)),
        scratch_types=[pltpu.VMEM((kt * nq, rr), I32), pltpu.VMEM((kt, 2 * ww), F32), pltpu.VMEM((kt * half,), F32)]
                      + [buf] * 6 + [dma] * 10,
        compiler_params=cp, name="peer_key_stage",
    )
    def kern(tab_hbm, idx_hbm, x_hbm, gu_hbm, hd_hbm, idx_v, x_v, hd_v, *scratch):
        gbuf, dbuf = scratch[0:4], scratch[4:6]
        gin, gout, din = scratch[6:10], scratch[10:14], scratch[14:16]
        wid = lax.axis_index("s") * nc + lax.axis_index("c")
        last_lane = lax.iota(I32, lanes) == lanes - 1

        @pl.loop(0, n_outer)
        def _(o):
            tok0 = wid * tpw + o * kt
            pltpu.sync_copy(idx_hbm.at[pl.ds(tok0 * nq, kt * nq)], idx_v)
            pltpu.sync_copy(x_hbm.at[pl.ds(tok0, kt)], x_v)
            for z in range(kt * half // lanes):
                hd_v[pl.ds(z * lanes, lanes)] = jnp.zeros((lanes,), F32)

            def g_in(t, c, b):
                return pltpu.make_async_copy(tab_hbm.at[idx_v.at[t * nq + c]], gbuf[b], gin[b])

            def g_out(t, c, b):
                dst = pl.ds((tok0 + t) * half + c * rr, rr)
                return pltpu.make_async_copy(gbuf[b], gu_hbm.at[dst], gout[b])

            def d_in(t, c):
                return pltpu.make_async_copy(tab_hbm.at[idx_v.at[t * nq + ng + c]], dbuf[c], din[c])

            g_in(0, 0, 0).start()
            g_in(0, 1, 1).start()
            d_in(0, 0).start()

            @pl.loop(0, kt // 2)
            def _(tt):
                not_last = tt + 1 < kt // 2
                for par in range(2):
                    t = 2 * tt + par
                    for c in range(ng):
                        b = 2 * par + c
                        nxt = (b + 2) % 4
                        if par == 0:
                            @pl.when(tt > 0)
                            def _():
                                g_out(t - 1, c, nxt).wait()
                            g_in(t + 1, c, nxt).start()
                        else:
                            @pl.when(not_last)
                            def _():
                                g_out(t - 1, c, nxt).wait()
                                g_in(t + 1, c, nxt).start()
                        g_in(t, c, b).wait()
                        g_out(t, c, b).start()

                        if c == 0:
                            d_in(t, 1).start()
                        elif par == 0:
                            d_in(t + 1, 0).start()
                        else:
                            @pl.when(not_last)
                            def _():
                                d_in(t + 1, 0).start()
                        d_in(t, c).wait()
                        for pan in range(SC_ACC_PANELS):
                            lo_at = lambda v: pl.ds(pan * pw + v * lanes, lanes)
                            hi_at = lambda v: pl.ds(ww + pan * pw + v * lanes, lanes)
                            xlo = [x_v[t, lo_at(v)] for v in range(nv)]
                            xhi = [x_v[t, hi_at(v)] for v in range(nv)]

                            @pl.loop(0, rr)
                            def _(j):
                                terms = []
                                for v in range(nv):
                                    word = dbuf[c][j, lo_at(v)]
                                    terms.append(plsc.bitcast(word << 16, F32) * xlo[v])
                                    terms.append(plsc.bitcast(word & jnp.uint32(0xFFFF0000), F32) * xhi[v])
                                while len(terms) > 1:
                                    terms = [terms[i] + terms[i + 1] for i in range(0, len(terms), 2)]
                                slot = jnp.full((lanes,), t * half + c * rr, I32) + j
                                plsc.addupdate_scatter(hd_v, [slot], plsc.cumsum(terms[0]), mask=last_lane)

            for par in range(2):
                for c in range(ng):
                    g_out(kt - 2 + par, c, 2 * par + c).wait()
            pltpu.sync_copy(hd_v, hd_hbm.at[pl.ds(tok0 * half, kt * half)])

    gu, hd = kern(table, idx.reshape(n_tok * nq, rr), x)
    return gu, hd.reshape(n_tok, half)


def _sc_weighted_row_sum(table, idx, wgt):
    nc, ns, lanes, mesh = _sc_mesh()
    nw = nc * ns
    n_tok, picks = idx.shape
    ww = table.shape[1]
    rr = SC_ACC_ROWS
    nq = picks // rr
    tpw = n_tok // nw
    kt = min(SC_ACC_TOKENS, tpw)
    n_outer = tpw // kt
    pw = ww // SC_ACC_PANELS
    nv = pw // lanes
    ahead = SC_ACC_AHEAD
    assert tpw * nw == n_tok and n_outer * kt == tpw and nq * rr == picks and ahead < nq and nv * lanes == pw
    cp = pltpu.CompilerParams(needs_layout_passes=False)
    buf = pltpu.VMEM((rr, ww), table.dtype)

    @functools.partial(
        pl.kernel, mesh=mesh, out_type=jax.ShapeDtypeStruct((n_tok, 2 * ww), F32),
        scratch_types=[pltpu.VMEM((kt * nq, rr), I32), pltpu.VMEM((kt, picks), F32), pltpu.VMEM((kt, 2 * ww), F32)]
                      + [buf] * nq + [pltpu.SemaphoreType.DMA] * nq,
        compiler_params=cp, name="peer_weighted_row_sum",
    )
    def kern(tab_hbm, idx_hbm, w_hbm, out_hbm, idx_v, w_v, acc_v, *scratch):
        wid = lax.axis_index("s") * nc + lax.axis_index("c")
        rows, sems = scratch[:nq], scratch[nq:]

        def chunk_copy(t, q):
            return pltpu.make_async_copy(tab_hbm.at[idx_v.at[t * nq + q]], rows[q], sems[q])

        @pl.loop(0, n_outer)
        def _(o):
            tok0 = wid * tpw + o * kt
            pltpu.sync_copy(idx_hbm.at[pl.ds(tok0 * nq, kt * nq)], idx_v)
            pltpu.sync_copy(w_hbm.at[pl.ds(tok0, kt)], w_v)
            for q in range(ahead):
                chunk_copy(0, q).start()

            @pl.loop(0, kt)
            def _(t):
                tvec = jnp.full((lanes,), t, I32)
                for q in range(nq):
                    b = q
                    if q + ahead < nq:
                        chunk_copy(t, q + ahead).start()
                    else:
                        @pl.when(t + 1 < kt)
                        def _():
                            chunk_copy(t + 1, q + ahead - nq).start()
                    chunk_copy(t, q).wait()
                    for pan in range(SC_ACC_PANELS):
                        lo_at = lambda c: pl.ds(pan * pw + c * lanes, lanes)
                        hi_at = lambda c: pl.ds(ww + pan * pw + c * lanes, lanes)

                        def row_body(j, acc):
                            wj = plsc.load_gather(w_v, [tvec, jnp.full((lanes,), q * rr, I32) + j])
                            new = []
                            for c in range(nv):
                                word = rows[b][j, lo_at(c)]
                                new.append(acc[2 * c] + wj * plsc.bitcast(word << 16, F32))
                                new.append(acc[2 * c + 1] + wj * plsc.bitcast(word & jnp.uint32(0xFFFF0000), F32))
                            return tuple(new)

                        if q == 0:
                            init = tuple(jnp.zeros((lanes,), F32) for _ in range(2 * nv))
                        else:
                            init = tuple(acc_v[t, at(c)] for c in range(nv) for at in (lo_at, hi_at))
                        acc = lax.fori_loop(0, rr, row_body, init)
                        for c in range(nv):
                            acc_v[t, lo_at(c)] = acc[2 * c]
                            acc_v[t, hi_at(c)] = acc[2 * c + 1]

            pltpu.sync_copy(acc_v, out_hbm.at[pl.ds(tok0, kt)])

    return kern(table, idx.reshape(n_tok * nq, rr), wgt)


def _pack_bf16_pairs(t):
    half = t.shape[1] // 2
    b = lax.bitcast_convert_type(t.astype(BF16), jnp.uint16).astype(U32)
    return b[:, :half] | (b[:, half:] << 16)


def _gelu_erf(x):
    return 0.5 * x * (1.0 + lax.erf(x * (2.0 ** -0.5)))


def _unpack_pairs(words):
    lo = pltpu.bitcast(words << 16, F32)
    hi = pltpu.bitcast(words & jnp.uint32(0xFFFF0000), F32)
    return lo, hi


def _peer_hidden_kernel(gu_ref, hd_sc_ref, gate_ref, h2_ref, w_ref, hd_ref, *, tb):
    half = D_MODEL // 2
    n_g = PEER_PICKS // 2
    h2 = h2_ref[...]
    for t in range(tb):
        ulo, uhi = _unpack_pairs(gu_ref[t * n_g:(t + 1) * n_g, :])
        hd_ref[0:n_g, t:t + 1] = jnp.sum(ulo * h2[t:t + 1, :half] + uhi * h2[t:t + 1, half:], axis=-1, keepdims=True)
    hd_ref[n_g:PEER_PICKS, :] = hd_sc_ref[...]
    w_ref[...] = _gelu_erf(hd_ref[...]) * gate_ref[...]


def _peer_hidden(gu, hd_sc_blocks, gate_blocks, h2, tb):
    r = h2.shape[1]
    n_g = PEER_PICKS // 2
    blk = pl.BlockSpec((None, PEER_PICKS, tb), lambda gi, i: (i, 0, 0))
    return pl.pallas_call(
        functools.partial(_peer_hidden_kernel, tb=tb),
        grid=(1, r // tb),
        in_specs=[pl.BlockSpec((tb * n_g, D_MODEL // 2), lambda gi, i: (i, 0)),
                  pl.BlockSpec((None, n_g, tb), lambda gi, i: (i, 0, 0)), blk, _rows(tb, D_MODEL)],
        out_specs=blk,
        out_shape=jax.ShapeDtypeStruct((r // tb, PEER_PICKS, tb), F32),
        scratch_shapes=[pltpu.VMEM((PEER_PICKS, tb), F32)],
        compiler_params=_cparams("parallel", "parallel"),
        name="peer_hidden",
    )(gu, hd_sc_blocks, gate_blocks, h2)


def _peer_out_kernel(x1_ref, ff_ref, gtf_ref, l2g_ref, l2b_ref, o_ref, *, alpha):
    o_ref[...] = _layernorm(alpha * x1_ref[...] + gtf_ref[...] * ff_ref[...], l2g_ref[...], l2b_ref[...])


def _peer_out(x1, ff, mod, p, alpha, tb):
    g, r, _ = x1.shape
    return pl.pallas_call(
        functools.partial(_peer_out_kernel, alpha=alpha),
        grid=(g, r // tb),
        in_specs=[_rows(tb, D_MODEL), _rows(tb, D_MODEL), _mod(mod, tb, 5), _const((1, D_MODEL)), _const((1, D_MODEL))],
        out_specs=_rows(tb, D_MODEL),
        out_shape=jax.ShapeDtypeStruct((g, r, D_MODEL), F32),
        compiler_params=_cparams("parallel", "parallel"),
        name="peer_out_ln2",
    )(x1, ff, mod, _row2(p["ln2_g"]), _row2(p["ln2_b"]))


def _token_stage(x, mod, prev_fn, wkv_fn, attn_fn, p, alpha, tb, sequential):
    ps, pattn, pgate = _inproj(x, mod, p["ln_in_g"], p["ln_in_b"], p["w_in_bf16"], tb)
    r, lw, k, v, kk, kka, gl = _rwkv_prep(ps, prev_fn(), p, tb, sequential)
    y, wkv_new = wkv_fn(r, lw, k, v, kk, kka)
    ya = _rwkv_post(y, r, k, v, gl, p, tb)
    ob = attn_fn(pattn)
    x1, h2, scores_t = _merge(x, ya, ob, pgate, mod, p, alpha, tb)
    idx_t, gate_t = _topk(scores_t, TOPK_TB)
    return ps, pattn, wkv_new, x1, h2, idx_t, gate_t


def kernel(x_prompt, x_sample, state_wkv, state_shift, cache_k_win, cache_v_win, c_prompt, c_sample, ln_in_g, ln_in_b, w_ada, b_ada, w_in, mu_shift, rwkv_w0, rwkv_w2, rwkv_a0, rwkv_a2, rwkv_g2, rwkv_k_k, rwkv_k_a, rwkv_r_k, rwkv_gn_g, rwkv_gn_b, attn_sinks, w_pa, w_pb, w_o, ln1_g, ln1_b, peer_wq, peer_sub_keys, peer_u, peer_v, ln2_g, ln2_b):
    depth = w_in.shape[0]
    assert depth == 1, "single-layer trunk"
    alpha = (2.0 * depth) ** 0.25
    n_p, t_p, _ = x_prompt.shape
    n_s = x_sample.shape[0]
    p = dict(ln_in_g=ln_in_g, ln_in_b=ln_in_b, w_in_bf16=w_in[0].astype(BF16), mu_shift=mu_shift[0],
             rwkv_w0=rwkv_w0[0], rwkv_w2=rwkv_w2[0], rwkv_a0=rwkv_a0[0], rwkv_a2=rwkv_a2[0], rwkv_g2=rwkv_g2[0],
             rwkv_k_k=rwkv_k_k[0], rwkv_k_a=rwkv_k_a[0], rwkv_r_k=rwkv_r_k[0], rwkv_gn_g=rwkv_gn_g[0],
             rwkv_gn_b=rwkv_gn_b[0], w_pa=w_pa[0], w_pb=w_pb[0], w_o=w_o[0], ln1_g=ln1_g[0], ln1_b=ln1_b[0],
             peer_wq=peer_wq[0], peer_sub_keys=peer_sub_keys[0], ln2_g=ln2_g[0], ln2_b=ln2_b[0])
    sinks = attn_sinks[0]

    n_c = n_p + n_s
    pad = (-n_c) % 8
    c_all = jnp.concatenate([c_prompt, c_sample, jnp.zeros((pad, D_MODEL), F32)], axis=0)
    mod_all = _modulation(c_all, w_ada[0], b_ada[0])
    mod_p = mod_all[:n_p].reshape(n_p, 1, N_MOD * D_MODEL)
    mod_s = mod_all[n_p:n_c].reshape(1, n_s, N_MOD * D_MODEL)

    seg = min(PROMPT_SEGMENT, t_p)
    assert t_p % seg == 0
    first = min(PROMPT_FIRST_SEGMENT, seg)

    def segments(b):
        cuts = list(range(0, t_p + 1, seg))
        if b == 0 and first < seg:
            cuts.insert(1, first)
        return zip(cuts[:-1], cuts[1:])

    prompt_ids = [(b, lo, hi) for b in range(n_p) for lo, hi in segments(b)]
    assert all((hi - lo) % TOKEN_TB == 0 for _, lo, hi in prompt_ids)
    carry = {}

    def prompt_group(b, lo, hi):
        def prev_fn():
            return carry[b][0] if lo > 0 else jnp.zeros((1, 1, SHIFT_W), F32)

        def wkv_fn(r, lw, k, v, kk, kka):
            s0 = carry[b][1] if lo > 0 else jnp.zeros((1, H_A, HD_A, HD_A), F32)
            return _rwkv_chunk_scan(r, lw, k, v, kk, kka, s0)

        def attn_fn(pa):
            return _attn_band(pa, carry[b][2] if lo > 0 else None, sinks)

        return (x_prompt[b:b + 1, lo:hi], mod_p[b:b + 1], prev_fn, wkv_fn, attn_fn, TOKEN_TB, True)

    tu, tv = _pack_bf16_pairs(peer_u[0]), _pack_bf16_pairs(peer_v[0])

    def select(x, mod, prev_fn, wkv_fn, attn_fn, tb, sequential):
        ps, pattn, wkv_new, x1, h2, idx_t, gate_t = _token_stage(
            x, mod, prev_fn, wkv_fn, attn_fn, p, alpha, tb, sequential)
        r = x1.shape[1]
        idx = jnp.transpose(idx_t.reshape(PEER_PICKS, r))
        gate_blocks = jnp.transpose(gate_t.reshape(PEER_PICKS, r // PEER_TB, PEER_TB), (1, 0, 2))
        gu, hd_sc = _sc_key_stage(tu, idx, h2[0])
        hd_sc_blocks = jnp.transpose(hd_sc.reshape(r // PEER_TB, PEER_TB, PEER_PICKS // 2), (0, 2, 1))
        return ps, pattn, wkv_new, (gu, hd_sc_blocks, idx, gate_blocks, x1, h2, mod)

    def weigh(sel):
        gu, hd_sc_blocks, idx, gate_blocks, x1, h2, mod = sel
        w_blocks = _peer_hidden(gu, hd_sc_blocks, gate_blocks, h2, PEER_TB)
        wgt = jnp.transpose(w_blocks, (0, 2, 1)).reshape(-1, PEER_PICKS)
        return _sc_weighted_row_sum(tv, idx, wgt), x1, mod

    def finish(wsum):
        ff, x1, mod = wsum
        return _peer_out(x1, ff[None], mod, p, alpha, min(TOKEN_TB, x1.shape[1]))

    xs = x_sample.reshape(1, n_s, D_MODEL)

    def prev_s():
        return state_shift[0].reshape(1, n_s, SHIFT_W)

    def wkv_s(r, lw, k, v, kk, kka):
        sq = lambda z: z.reshape(n_s, D_A)
        y, s = _rwkv_step(state_wkv[0], sq(r), sq(lw), sq(k), sq(kk), sq(kka), sq(v), STEP_NB)
        return y.reshape(1, n_s, D_A), s

    def attn_s(pa):
        o = _attn_cache(pa.reshape(n_s, 1, ATTN_W), cache_k_win[0], cache_v_win[0], sinks, STEP_NB)
        return o.reshape(1, n_s, D_B)

    n_g = len(prompt_ids) + 1
    sel, wsum, y_l = [None] * n_g, [None] * n_g, [None] * n_g
    for step in range(n_g + 2):
        if step < n_g - 1:
            b, lo, hi = prompt_ids[step]
            ps, pattn, wkv_new, sel[step] = select(*prompt_group(b, lo, hi))
            carry[b] = (ps[:, -1:], wkv_new, pattn[:, -WINDOW:])
        elif step == n_g - 1:
            ps_s, pattn_s, wkv_s_new, sel[step] = select(xs, mod_s, prev_s, wkv_s, attn_s, min(TOKEN_TB, n_s), False)
        if 0 <= step - 1 < n_g:
            wsum[step - 1] = weigh(sel[step - 1])
        if 0 <= step - 2 < n_g:
            y_l[step - 2] = finish(wsum[step - 2])
    y_s = y_l[n_g - 1]
    y_p = jnp.concatenate(y_l[:n_g - 1], axis=1).reshape(n_p, t_p, D_MODEL)
    shift_p = jnp.concatenate([carry[b][0][:, 0] for b in range(n_p)], axis=0)
    pattn_p = jnp.concatenate([carry[b][2] for b in range(n_p)], axis=0)
    wkv_p_new = jnp.concatenate([carry[b][1] for b in range(n_p)], axis=0)

    kv = lambda pa, o: pa[..., o:o + H_KV * HD_B]
    ko, vo = D_B, D_B + H_KV * HD_B
    k_win_p = kv(pattn_p, ko)[:, -WINDOW:].reshape(n_p, WINDOW, H_KV, HD_B)
    v_win_p = kv(pattn_p, vo)[:, -WINDOW:].reshape(n_p, WINDOW, H_KV, HD_B)
    k_new_s = kv(pattn_s, ko).reshape(n_s, 1, H_KV, HD_B)
    v_new_s = kv(pattn_s, vo).reshape(n_s, 1, H_KV, HD_B)
    k_win_s = jnp.concatenate([cache_k_win[0], k_new_s], axis=1)[:, -WINDOW:]
    v_win_s = jnp.concatenate([cache_v_win[0], v_new_s], axis=1)[:, -WINDOW:]
    return (y_p, y_s.reshape(n_s, 1, D_MODEL), wkv_p_new[None], wkv_s_new[None],
            shift_p[None], ps_s.reshape(n_s, SHIFT_W)[None],
            k_win_p[None], k_win_s[None], v_win_p[None], v_win_s[None])
```

```python
import functools
import math

import jax
import jax.numpy as jnp
from jax import lax
from jax.experimental import pallas as pl
from jax.experimental.pallas import tpu as pltpu
from jax.experimental.pallas import tpu_sc as plsc

F32 = jnp.float32
BF16 = jnp.bfloat16
I32 = jnp.int32
U32 = jnp.uint32

D_MODEL = 1024
H_A, HD_A = 8, 64
D_A = H_A * HD_A
D_LORA_W, D_LORA_A, D_LORA_G = 64, 64, 128
GN_EPS = 64e-5
H_Q, H_KV, HD_B = 8, 2, 64
G_Q = H_Q // H_KV
D_B = H_Q * HD_B
WINDOW = 128
N_KEYS = 128
PEER_HEADS, PEER_TOPK, PEER_HALF = 8, 16, 128
PEER_PICKS = PEER_HEADS * PEER_TOPK
N_MOD = 6
LN_EPS = 1e-5
NEG_INF = -1e30
OFF_WD = 3 * D_A
OFF_AD = OFF_WD + D_LORA_W
OFF_GD = OFF_AD + D_LORA_A
SHIFT_W = OFF_GD + D_LORA_G
ATTN_W = D_B + 2 * H_KV * HD_B
GATE_W = 2 * D_MODEL
D_IN = SHIFT_W + ATTN_W + GATE_W

VMEM_LIMIT = 48 * 1024 * 1024
SUBLANES = 8
RWKV_CHUNK = 64
SC_GATHER_ROWS = 32
SC_IDX_ROWS = 32
SC_GATHER_BUFS = 6
SC_GATHER_AHEAD = 3
SC_ACC_AHEAD = 3
SC_ACC_TOKENS = 16
SC_ACC_PANELS = 4
TOKEN_TB = 256
TOPK_TB = 128
PEER_TB = 16
VALUE_TC_PICKS = 32
VALUE_TC_TB = 32
PROMPT_SEGMENT = 2048
PROMPT_FIRST_SEGMENT = 512
STEP_NB = 8


def _cparams(*sem):
    return pltpu.CompilerParams(dimension_semantics=sem, vmem_limit_bytes=VMEM_LIMIT)


def _layernorm(x, g, b):
    mu = jnp.mean(x, -1, keepdims=True)
    xc = x - mu
    var = jnp.mean(xc * xc, -1, keepdims=True)
    return xc * lax.rsqrt(var + LN_EPS) * g + b


def _split(x):
    hi = x.astype(BF16)
    lo = (x - hi.astype(F32)).astype(BF16)
    return hi, lo


_NN = (((1,), (0,)), ((), ()))
_NT = (((1,), (1,)), ((), ()))
_TN = (((0,), (0,)), ((), ()))


def _dot3(a, b, dims=_NN):
    ah, al = _split(a)
    bh, bl = _split(b)
    d = functools.partial(lax.dot_general, dimension_numbers=dims, preferred_element_type=F32)
    return d(ah, bh) + d(ah, bl) + d(al, bh)


def _dot_exact_lhs(a_bf16, b, dims=_NN):
    b1 = b.astype(BF16)
    r1 = b - b1.astype(F32)
    b2 = r1.astype(BF16)
    b3 = (r1 - b2.astype(F32)).astype(BF16)
    d = functools.partial(lax.dot_general, dimension_numbers=dims, preferred_element_type=F32)
    return d(a_bf16, b1) + d(a_bf16, b2) + d(a_bf16, b3)


def _dotb(a, b, dims=_NN):
    return lax.dot_general(a.astype(BF16), b.astype(BF16), dims, preferred_element_type=F32)


def _rows(tb, width, col=0):
    return pl.BlockSpec((None, tb, width), lambda g, i: (g, i, col))


def _mod(mod, tb, col):
    if mod.shape[1] == 1:
        return pl.BlockSpec((None, 1, D_MODEL), lambda g, i: (g, 0, col))
    return pl.BlockSpec((None, tb, D_MODEL), lambda g, i: (g, i, col))


def _const(shape):
    n = len(shape)
    return pl.BlockSpec(shape, lambda g, i: (0,) * n)


def _row2(p):
    return p.reshape(1, -1).astype(F32)


def _mod_kernel(c_ref, w_ref, b_ref, o_ref):
    c = c_ref[...]
    a = c * jax.nn.sigmoid(c)
    o_ref[...] = _dot3(a, w_ref[...]) + b_ref[...]


def _modulation(c, w_ada, b_ada):
    n = c.shape[0]
    tn = D_MODEL
    return pl.pallas_call(
        _mod_kernel,
        grid=(w_ada.shape[1] // tn,),
        in_specs=[pl.BlockSpec((n, D_MODEL), lambda j: (0, 0)),
                  pl.BlockSpec((D_MODEL, tn), lambda j: (0, j)),
                  pl.BlockSpec((1, tn), lambda j: (0, j))],
        out_specs=pl.BlockSpec((n, tn), lambda j: (0, j)),
        out_shape=jax.ShapeDtypeStruct((n, w_ada.shape[1]), F32),
        compiler_params=_cparams("arbitrary"),
        name="modulation",
    )(c, w_ada, b_ada.reshape(1, -1))


def _inproj_kernel(x_ref, sh_ref, sc_ref, g_ref, b_ref, w_ref, ps_ref, pa_ref, pg_ref):
    xn = _layernorm(x_ref[...], g_ref[...], b_ref[...])
    h = (xn * (1.0 + sc_ref[...]) + sh_ref[...]).astype(BF16)
    ps_ref[...] = jnp.dot(h, w_ref[:, :SHIFT_W], preferred_element_type=F32)
    pa_ref[...] = jnp.dot(h, w_ref[:, SHIFT_W:SHIFT_W + ATTN_W], preferred_element_type=F32)
    pg_ref[...] = jnp.dot(h, w_ref[:, SHIFT_W + ATTN_W:], preferred_element_type=F32)


def _inproj(x, mod, ln_g, ln_b, w_in_bf16, tb):
    g, r, _ = x.shape
    shp = lambda w: jax.ShapeDtypeStruct((g, r, w), F32)
    return pl.pallas_call(
        _inproj_kernel,
        grid=(g, r // tb),
        in_specs=[_rows(tb, D_MODEL), _mod(mod, tb, 0), _mod(mod, tb, 1),
                  _const((1, D_MODEL)), _const((1, D_MODEL)), _const((D_MODEL, D_IN))],
        out_specs=[_rows(tb, SHIFT_W), _rows(tb, ATTN_W), _rows(tb, GATE_W)],
        out_shape=[shp(SHIFT_W), shp(ATTN_W), shp(GATE_W)],
        compiler_params=_cparams("parallel", "parallel"),
        name="inproj",
    )(x, mod, mod, _row2(ln_g), _row2(ln_b), w_in_bf16)


def _softplus(x):
    return jnp.maximum(x, 0.0) + jnp.log1p(jnp.exp(-jnp.abs(x)))


def _rwkv_prep_kernel(ps_ref, prev_ref, first_ref, mu_ref, w0_ref, w2_ref, a0_ref, a2_ref, g2_ref, kk_w_ref, ka_w_ref,
                      hsum_ref, r_ref, lw_ref, k_ref, v_ref, kk_ref, kka_ref, g_ref, *, sequential):
    ps = ps_ref[...]
    if sequential:
        before = jnp.where(pl.program_id(1) == 0, first_ref[...], prev_ref[SUBLANES - 1:SUBLANES, :])
        row = lax.broadcasted_iota(I32, ps.shape, 0)
        prev = jnp.where(row == 0, before, pltpu.roll(ps, 1, 0))
    else:
        prev = prev_ref[...]
    xs = ps + (prev - ps) * mu_ref[...]
    r = xs[:, 0:D_A]
    k = xs[:, D_A:2 * D_A]
    v = xs[:, 2 * D_A:3 * D_A]
    wd = xs[:, OFF_WD:OFF_AD]
    ad = xs[:, OFF_AD:OFF_GD]
    gd = xs[:, OFF_GD:SHIFT_W]
    z = w0_ref[...] + _dot3(jnp.tanh(wd), w2_ref[...])
    w_log = -_softplus(-z) - 0.5
    a = jax.nn.sigmoid(a0_ref[...] + _dot3(ad, a2_ref[...]))
    kk = k * kk_w_ref[...]
    ss = _dot3(kk * kk, hsum_ref[...])
    kk = kk / jnp.maximum(jnp.sqrt(ss), 1e-12)
    r_ref[...] = r
    lw_ref[...] = -jnp.exp(w_log)
    k_ref[...] = k * (1.0 + (a - 1.0) * ka_w_ref[...])
    v_ref[...] = v
    kk_ref[...] = kk
    kka_ref[...] = kk * a
    g_ref[...] = _dot3(jax.nn.sigmoid(gd), g2_ref[...])


def _head_sum_matrix():
    h = jnp.arange(D_A) // HD_A
    return (h[:, None] == h[None, :]).astype(F32)


def _rwkv_prep(ps, prev, p, tb, sequential):
    g, r, _ = ps.shape
    shp = jax.ShapeDtypeStruct((g, r, D_A), F32)
    if sequential:
        per = tb // SUBLANES
        prev_args = (ps, prev)
        prev_specs = [pl.BlockSpec((None, SUBLANES, SHIFT_W), lambda gi, i: (gi, jnp.maximum(i * per - 1, 0), 0)),
                      pl.BlockSpec((None, 1, SHIFT_W), lambda gi, i: (gi, 0, 0))]
    else:
        prev_args = (prev, prev[:, :1])
        prev_specs = [_rows(tb, SHIFT_W), pl.BlockSpec((None, 1, SHIFT_W), lambda gi, i: (gi, 0, 0))]
    return pl.pallas_call(
        functools.partial(_rwkv_prep_kernel, sequential=sequential),
        grid=(g, r // tb),
        in_specs=[_rows(tb, SHIFT_W)] + prev_specs + [_const((1, SHIFT_W)),
                  _const((1, D_A)), _const((D_LORA_W, D_A)), _const((1, D_A)), _const((D_LORA_A, D_A)),
                  _const((D_LORA_G, D_A)), _const((1, D_A)), _const((1, D_A)), _const((D_A, D_A))],
        out_specs=[_rows(tb, D_A)] * 7,
        out_shape=[shp] * 7,
        compiler_params=_cparams("parallel", "parallel"),
        name="rwkv_prep",
    )(ps, *prev_args, _row2(p["mu_shift"]), _row2(p["rwkv_w0"]), p["rwkv_w2"], _row2(p["rwkv_a0"]), p["rwkv_a2"],
      p["rwkv_g2"], _row2(p["rwkv_k_k"]), _row2(p["rwkv_k_a"]), _head_sum_matrix())


def _rwkv_chunk_kernel(r_ref, lw_ref, k_ref, v_ref, kk_ref, kka_ref, s0_ref, y_ref, s_ref):
    c = RWKV_CHUNK

    @pl.when(pl.program_id(1) == 0)
    def _():
        s_ref[...] = s0_ref[...]

    row = lax.broadcasted_iota(I32, (c, c), 0)
    col = lax.broadcasted_iota(I32, (c, c), 1)
    tril = row >= col
    stril = row > col
    lw = lw_ref[...]
    cum = _dot_exact_lhs(tril.astype(BF16), lw)
    cum_end = cum[c - 1:c, :]
    g_inv = jnp.exp(-cum)
    g_end = jnp.exp(cum_end - cum)
    a_hat = -kk_ref[...] * jnp.exp(cum - lw)
    b_hat = kka_ref[...] * g_inv
    k_hat = k_ref[...] * g_inv
    r_til = r_ref[...] * jnp.exp(cum)
    b_end = kka_ref[...] * g_end
    k_end = k_ref[...] * g_end
    gam_end = jnp.exp(cum_end)
    v_all = v_ref[...]
    s_all = s_ref[...]
    n_steps = int(math.log2(c))
    heads = range(H_A)
    sl = [slice(h * HD_A, (h + 1) * HD_A) for h in heads]
    vh = [v_all[:, sl[h]] for h in heads]
    ar = [jnp.concatenate([a_hat[:, sl[h]], r_til[:, sl[h]]], axis=0) for h in heads]
    bk = [jnp.concatenate([b_hat[:, sl[h]], k_hat[:, sl[h]]], axis=0) for h in heads]
    x = [_dot3(ar[h], bk[h], _NT) for h in heads]
    ars = [_dot3(ar[h], s_all[h], _NT) for h in heads]
    a_ak = [jnp.where(stril, x[h][:c, c:], 0.0) for h in heads]
    n = [jnp.where(stril, x[h][:c, :c], 0.0) for h in heads]
    u = [ars[h][:c] + _dot3(a_ak[h], vh[h]) for h in heads]
    for it in range(n_steps):
        u = [u[h] + _dot3(n[h], u[h]) for h in heads]
        if it + 1 < n_steps:
            n = [_dot3(n[h], n[h]) for h in heads]
    uv = [jnp.concatenate([u[h], vh[h]], axis=0) for h in heads]
    a_r = [jnp.concatenate([jnp.where(tril, x[h][c:, :c], 0.0), jnp.where(tril, x[h][c:, c:], 0.0)], axis=1)
           for h in heads]
    y = [ars[h][c:] + _dot3(a_r[h], uv[h]) for h in heads]
    bke = [jnp.concatenate([b_end[:, sl[h]], k_end[:, sl[h]]], axis=0) for h in heads]
    s_new = [s_all[h] * gam_end[:, sl[h]] + _dot3(uv[h], bke[h], _TN) for h in heads]
    for h in heads:
        y_ref[:, sl[h]] = y[h]
        s_ref[h] = s_new[h]


def _rwkv_chunk_scan(r, lw, k, v, kk, kka, s0):
    n, t, _ = r.shape
    c = RWKV_CHUNK
    seq = pl.BlockSpec((None, c, D_A), lambda b, i: (b, i, 0))
    st = pl.BlockSpec((None, H_A, HD_A, HD_A), lambda b, i: (b, 0, 0, 0))
    return pl.pallas_call(
        _rwkv_chunk_kernel,
        grid=(n, t // c),
        in_specs=[seq] * 6 + [st],
        out_specs=[seq, st],
        out_shape=[jax.ShapeDtypeStruct((n, t, D_A), F32), jax.ShapeDtypeStruct((n, H_A, HD_A, HD_A), F32)],
        compiler_params=_cparams("parallel", "arbitrary"),
        name="rwkv_chunk_scan",
    )(r, lw, k, v, kk, kka, s0)


def _rwkv_step_kernel(s_ref, r_ref, lw_ref, k_ref, kk_ref, kka_ref, v_ref, y_ref, so_ref):
    s = s_ref[...]
    sa = jnp.sum(s * (-kk_ref[...]), axis=-1, keepdims=True)
    s = s * jnp.exp(lw_ref[...]) + sa * kka_ref[...] + v_ref[...] * k_ref[...]
    so_ref[...] = s
    y_ref[...] = jnp.sum(s * r_ref[...], axis=-1, keepdims=True)


def _rwkv_step(s0, r, lw, k, kk, kka, v, nb):
    n = s0.shape[0]
    key = lambda z: z.reshape(n, H_A, 1, HD_A)
    st = pl.BlockSpec((nb, H_A, HD_A, HD_A), lambda i: (i, 0, 0, 0))
    ks = pl.BlockSpec((nb, H_A, 1, HD_A), lambda i: (i, 0, 0, 0))
    vs = pl.BlockSpec((nb, H_A, HD_A, 1), lambda i: (i, 0, 0, 0))
    y, s = pl.pallas_call(
        _rwkv_step_kernel,
        grid=(n // nb,),
        in_specs=[st, ks, ks, ks, ks, ks, vs],
        out_specs=[vs, st],
        out_shape=[jax.ShapeDtypeStruct((n, H_A, HD_A, 1), F32), jax.ShapeDtypeStruct(s0.shape, F32)],
        compiler_params=_cparams("parallel"),
        name="rwkv_step",
    )(s0, key(r), key(lw), key(k), key(kk), key(kka), v.reshape(n, H_A, HD_A, 1))
    return y.reshape(n, D_A), s


def _rwkv_post_kernel(y_ref, r_ref, k_ref, v_ref, g_ref, gn_g_ref, gn_b_ref, rk_ref, hsum_ref, o_ref):
    y = y_ref[...]
    hs = hsum_ref[...]
    mu = _dot3(y, hs) * (1.0 / HD_A)
    yc = y - mu
    var = _dot3(yc * yc, hs) * (1.0 / HD_A)
    yn = yc * lax.rsqrt(var + GN_EPS) * gn_g_ref[...] + gn_b_ref[...]
    bonus = _dot3(r_ref[...] * k_ref[...] * rk_ref[...], hs) * v_ref[...]
    o_ref[...] = (yn + bonus) * g_ref[...]


def _rwkv_post(y, r, k, v, g, p, tb):
    gg, rr, _ = y.shape
    return pl.pallas_call(
        _rwkv_post_kernel,
        grid=(gg, rr // tb),
        in_specs=[_rows(tb, D_A)] * 5 + [_const((1, D_A))] * 3 + [_const((D_A, D_A))],
        out_specs=_rows(tb, D_A),
        out_shape=jax.ShapeDtypeStruct((gg, rr, D_A), F32),
        compiler_params=_cparams("parallel", "parallel"),
        name="rwkv_post",
    )(y, r, k, v, g, _row2(p["rwkv_gn_g"]), _row2(p["rwkv_gn_b"]), _row2(p["rwkv_r_k"]), _head_sum_matrix())


def _sink_softmax(s, sink):
    m = jnp.maximum(jnp.max(s, axis=-1, keepdims=True), sink)
    p = jnp.exp(s - m)
    den = jnp.sum(p, axis=-1, keepdims=True) + jnp.exp(sink - m)
    return p / den


def _attn_band_kernel(cur_ref, prev_ref, carry_ref, sink_ref, o_ref, *, has_carry):
    blk = WINDOW
    i = pl.program_id(1)
    cur = cur_ref[...]
    prev = jnp.where(i == 0, carry_ref[...], prev_ref[...])
    qi = lax.broadcasted_iota(I32, (G_Q * blk, 2 * blk), 0) % blk
    kj = lax.broadcasted_iota(I32, (G_Q * blk, 2 * blk), 1)
    rel = blk + qi - kj
    valid = (rel >= 0) & (rel <= WINDOW)
    if not has_carry:
        valid = valid & ((kj >= blk) | (i > 0))
    relf = rel.astype(F32)
    gidx = lax.broadcasted_iota(I32, (G_Q * blk, 1), 0) // blk
    for kvh in range(H_KV):
        q4 = jnp.concatenate([cur[:, (kvh * G_Q + g) * HD_B:(kvh * G_Q + g + 1) * HD_B] for g in range(G_Q)], axis=0)
        ko = D_B + kvh * HD_B
        vo = D_B + H_KV * HD_B + kvh * HD_B
        kmat = jnp.concatenate([prev[:, ko:ko + HD_B], cur[:, ko:ko + HD_B]], axis=0)
        vmat = jnp.concatenate([prev[:, vo:vo + HD_B], cur[:, vo:vo + HD_B]], axis=0)
        slope = jnp.zeros((G_Q * blk, 1), F32)
        sink = jnp.zeros((G_Q * blk, 1), F32)
        for g in range(G_Q):
            hq = kvh * G_Q + g
            slope = jnp.where(gidx == g, 2.0 ** (-8.0 * (hq + 1) / H_Q), slope)
            sink = jnp.where(gidx == g, sink_ref[hq], sink)
        s = _dotb(q4, kmat, _NT) * (HD_B ** -0.5)
        s = jnp.where(valid, s - slope * relf, NEG_INF)
        p = _sink_softmax(s, sink)
        o = _dotb(p, vmat)
        for g in range(G_Q):
            hq = kvh * G_Q + g
            o_ref[:, hq * HD_B:(hq + 1) * HD_B] = o[g * blk:(g + 1) * blk]


def _attn_band(pattn, carry, sinks):
    n, t, _ = pattn.shape
    blk = WINDOW
    has_carry = carry is not None
    if not has_carry:
        carry = jnp.zeros((n, blk, ATTN_W), F32)
    return pl.pallas_call(
        functools.partial(_attn_band_kernel, has_carry=has_carry),
        grid=(n, t // blk),
        in_specs=[pl.BlockSpec((None, blk, ATTN_W), lambda b, i: (b, i, 0)),
                  pl.BlockSpec((None, blk, ATTN_W), lambda b, i: (b, jnp.maximum(i - 1, 0), 0)),
                  pl.BlockSpec((None, blk, ATTN_W), lambda b, i: (b, 0, 0)),
                  pl.BlockSpec(memory_space=pltpu.SMEM)],
        out_specs=pl.BlockSpec((None, blk, D_B), lambda b, i: (b, i, 0)),
        out_shape=jax.ShapeDtypeStruct((n, t, D_B), F32),
        compiler_params=_cparams("parallel", "parallel"),
        name="attn_band",
    )(pattn, pattn, carry, sinks.astype(F32))


def _attn_cache_kernel(cur_ref, kc_ref, vc_ref, sink_ref, o_ref, *, nb):
    relc = (WINDOW - lax.broadcasted_iota(I32, (G_Q, WINDOW), 1)).astype(F32)
    gidx = lax.broadcasted_iota(I32, (G_Q, 1), 0)
    for b in range(nb):
        cur = cur_ref[b]
        for kvh in range(H_KV):
            q4 = jnp.concatenate([cur[:, (kvh * G_Q + g) * HD_B:(kvh * G_Q + g + 1) * HD_B] for g in range(G_Q)], axis=0)
            ko = D_B + kvh * HD_B
            vo = D_B + H_KV * HD_B + kvh * HD_B
            k_new = cur[:, ko:ko + HD_B]
            v_new = cur[:, vo:vo + HD_B]
            kc = kc_ref[b, :, kvh * HD_B:(kvh + 1) * HD_B]
            vc = vc_ref[b, :, kvh * HD_B:(kvh + 1) * HD_B]
            slope = jnp.zeros((G_Q, 1), F32)
            sink = jnp.zeros((G_Q, 1), F32)
            for g in range(G_Q):
                hq = kvh * G_Q + g
                slope = jnp.where(gidx == g, 2.0 ** (-8.0 * (hq + 1) / H_Q), slope)
                sink = jnp.where(gidx == g, sink_ref[hq], sink)
            scale = HD_B ** -0.5
            sc = _dotb(q4, kc, _NT) * scale - slope * relc
            sn = jnp.sum(q4.astype(BF16).astype(F32) * k_new.astype(BF16).astype(F32), axis=-1, keepdims=True) * scale
            m = jnp.maximum(jnp.maximum(jnp.max(sc, axis=-1, keepdims=True), sn), sink)
            pc = jnp.exp(sc - m)
            pn = jnp.exp(sn - m)
            den = jnp.sum(pc, axis=-1, keepdims=True) + pn + jnp.exp(sink - m)
            o = (_dotb(pc / den, vc) + (pn / den).astype(BF16).astype(F32) * v_new.astype(BF16).astype(F32))
            for g in range(G_Q):
                hq = kvh * G_Q + g
                o_ref[b, :, hq * HD_B:(hq + 1) * HD_B] = o[g:g + 1]


def _attn_cache(pattn, k_buf, v_buf, sinks, nb):
    n = pattn.shape[0]
    kc = k_buf.reshape(n, WINDOW, H_KV * HD_B)
    vc = v_buf.reshape(n, WINDOW, H_KV * HD_B)
    return pl.pallas_call(
        functools.partial(_attn_cache_kernel, nb=nb),
        grid=(n // nb,),
        in_specs=[pl.BlockSpec((nb, 1, ATTN_W), lambda i: (i, 0, 0)),
                  pl.BlockSpec((nb, WINDOW, H_KV * HD_B), lambda i: (i, 0, 0)),
                  pl.BlockSpec((nb, WINDOW, H_KV * HD_B), lambda i: (i, 0, 0)),
                  pl.BlockSpec(memory_space=pltpu.SMEM)],
        out_specs=pl.BlockSpec((nb, 1, D_B), lambda i: (i, 0, 0)),
        out_shape=jax.ShapeDtypeStruct((n, 1, D_B), F32),
        compiler_params=_cparams("parallel"),
        name="attn_cache",
    )(pattn, kc, vc, sinks.astype(F32))


def _merge_kernel(x_ref, ya_ref, ob_ref, pg_ref, gtm_ref, shf_ref, scf_ref, lng_ref, lnb_ref, l1g_ref, l1b_ref,
                  wpa_ref, wpb_ref, wo_ref, wq_ref, sk_ref, x1_ref, st_ref, *, alpha):
    ya = jnp.dot(ya_ref[...].astype(BF16), wpa_ref[...], preferred_element_type=F32)
    yb = jnp.dot(ob_ref[...].astype(BF16), wpb_ref[...], preferred_element_type=F32)
    pg = pg_ref[...]
    merged = jax.nn.sigmoid(pg[:, :D_MODEL]) * ya + jax.nn.sigmoid(pg[:, D_MODEL:]) * yb
    mix = jnp.dot(merged.astype(BF16), wo_ref[...], preferred_element_type=F32)
    xn = _layernorm(x_ref[...], lng_ref[...], lnb_ref[...])
    x1 = _layernorm(alpha * xn + gtm_ref[...] * mix, l1g_ref[...], l1b_ref[...])
    x1_ref[...] = x1
    h2 = x1 * (1.0 + scf_ref[...]) + shf_ref[...]
    q = jnp.dot(h2.astype(BF16), wq_ref[...], preferred_element_type=F32)
    for hc in range(2 * PEER_HEADS):
        st_ref[hc] = _dot3(sk_ref[hc % 2], q[:, hc * PEER_HALF:(hc + 1) * PEER_HALF], _NT)


def _merge(x, ya, ob, pg, mod, p, alpha, tb):
    g, r, _ = x.shape
    return pl.pallas_call(
        functools.partial(_merge_kernel, alpha=alpha),
        grid=(g, r // tb),
        in_specs=[_rows(tb, D_MODEL), _rows(tb, D_A), _rows(tb, D_B), _rows(tb, GATE_W),
                  _mod(mod, tb, 2), _mod(mod, tb, 3), _mod(mod, tb, 4)]
                 + [_const((1, D_MODEL))] * 4
                 + [_const((D_A, D_MODEL)), _const((D_B, D_MODEL)), _const((D_MODEL, D_MODEL)),
                    _const((D_MODEL, 2 * PEER_HEADS * PEER_HALF)), _const((2, N_KEYS, PEER_HALF))],
        out_specs=[_rows(tb, D_MODEL),
                   pl.BlockSpec((None, 2 * PEER_HEADS, N_KEYS, tb), lambda gi, i: (gi, 0, 0, i))],
        out_shape=[jax.ShapeDtypeStruct((g, r, D_MODEL), F32),
                   jax.ShapeDtypeStruct((g, 2 * PEER_HEADS, N_KEYS, r), F32)],
        compiler_params=_cparams("parallel", "parallel"),
        name="merge_ln1_peer_scores",
    )(x, ya, ob, pg, mod, mod, mod, _row2(p["ln_in_g"]), _row2(p["ln_in_b"]), _row2(p["ln1_g"]), _row2(p["ln1_b"]),
      p["w_pa"].astype(BF16), p["w_pb"].astype(BF16), p["w_o"].astype(BF16), p["peer_wq"].astype(BF16),
      p["peer_sub_keys"])


def _extract_top(problems, n_rows, tb):
    rio = lax.broadcasted_iota(I32, (n_rows, tb), 0).astype(F32)
    vals = [v for v, _ in problems]
    tops = [([], [], []) for _ in problems]
    for _ in range(PEER_TOPK):
        for n, (_, payload) in enumerate(problems):
            m = jnp.max(vals[n], axis=0, keepdims=True)
            i = jnp.min(jnp.where(vals[n] == m, rio, float(n_rows)), axis=0, keepdims=True)
            sel = rio == i
            tops[n][0].append(m)
            tops[n][1].append(i)
            if payload is not None:
                tops[n][2].append(jnp.max(jnp.where(sel, payload, -1.0), axis=0, keepdims=True))
            vals[n] = jnp.where(sel, -jnp.inf, vals[n])
    cat = lambda z: jnp.concatenate(z, axis=0) if z else None
    return [(cat(v), cat(i), cat(pl_)) for v, i, pl_ in tops]


def _pair_candidates(v1, i1, v2, i2, tb):
    k = PEER_TOPK
    sub = 8
    eid = lambda a0, a1, b0, b1: i1[a0:a1] * float(N_KEYS) + i2[b0:b1]
    vals = [v1[0:1] + v2, v1[1:2] + v2[0:sub]]
    ids = [eid(0, 1, 0, k), eid(1, 2, 0, sub)]
    brow = lax.broadcasted_iota(I32, (sub, tb), 0)
    for a in range(2, sub):
        vals.append(jnp.where(brow < k // (a + 1), v1[a:a + 1] + v2[0:sub], -jnp.inf))
        ids.append(eid(a, a + 1, 0, sub))
    vals.append(v1[sub:k] + v2[0:1])
    ids.append(eid(sub, k, 0, 1))
    return jnp.concatenate(vals, axis=0), jnp.concatenate(ids, axis=0)


def _topk_kernel(s_ref, idx_ref, gate_ref):
    tb = s_ref.shape[-1]

    def head_pair(hp, carry):
        tops = _extract_top([(s_ref[4 * hp + n], None) for n in range(4)], N_KEYS, tb)
        cands = [_pair_candidates(tops[2 * n][0], tops[2 * n][1], tops[2 * n + 1][0], tops[2 * n + 1][1], tb)
                 for n in range(2)]
        picked = _extract_top(cands, cands[0][0].shape[0], tb)
        for n, (sc, _, ex) in enumerate(picked):
            pexp = jnp.exp(sc - sc[0:1])
            idx_ref[2 * hp + n] = ex.astype(I32)
            gate_ref[2 * hp + n] = pexp / jnp.sum(pexp, axis=0, keepdims=True)
        return carry

    lax.fori_loop(0, PEER_HEADS // 2, head_pair, 0)


def _topk(scores_t, tb):
    g, _, _, r = scores_t.shape
    out = pl.BlockSpec((None, PEER_HEADS, PEER_TOPK, tb), lambda gi, i: (gi, 0, 0, i))
    return pl.pallas_call(
        _topk_kernel,
        grid=(g, r // tb),
        in_specs=[pl.BlockSpec((None, 2 * PEER_HEADS, N_KEYS, tb), lambda gi, i: (gi, 0, 0, i))],
        out_specs=[out, out],
        out_shape=[jax.ShapeDtypeStruct((g, PEER_HEADS, PEER_TOPK, r), I32),
                   jax.ShapeDtypeStruct((g, PEER_HEADS, PEER_TOPK, r), F32)],
        compiler_params=_cparams("parallel", "parallel"),
        name="peer_topk",
    )(scores_t)


def _sc_mesh():
    info = plsc.get_sparse_core_info()
    mesh = plsc.VectorSubcoreMesh(core_axis_name="c", subcore_axis_name="s")
    return info.num_cores, info.num_subcores, info.num_lanes, mesh


def _sc_gather_rows(table, idx):
    nc, ns, _, mesh = _sc_mesh()
    nw = nc * ns
    ni = idx.shape[0]
    w = table.shape[1]
    rr = SC_GATHER_ROWS
    per_w = ni // nw
    n_chunks = per_w // rr
    ki = min(SC_IDX_ROWS, n_chunks)
    n_outer = n_chunks // ki
    assert per_w * nw == ni and n_chunks * rr == per_w and n_outer * ki == n_chunks
    nb, ahead = SC_GATHER_BUFS, SC_GATHER_AHEAD
    buf = pltpu.VMEM((rr, w), table.dtype)

    @functools.partial(
        pl.kernel, mesh=mesh, out_type=jax.ShapeDtypeStruct((ni, w), table.dtype),
        scratch_types=[pltpu.VMEM((ki, rr), I32)] + [buf] * nb + [pltpu.SemaphoreType.DMA] * (2 * nb),
        name="peer_row_gather",
    )
    def gather(tab_hbm, idx_hbm, out_hbm, idx_v, *scratch):
        wid = lax.axis_index("s") * nc + lax.axis_index("c")
        base = wid * per_w
        bufs, gsem, wsem = scratch[:nb], scratch[nb:2 * nb], scratch[2 * nb:]

        @pl.loop(0, n_outer)
        def _(o):
            pltpu.sync_copy(idx_hbm.at[wid, pl.ds(o * ki, ki)], idx_v)

            def start_gather(j):
                return pltpu.async_copy(tab_hbm.at[idx_v.at[j]], bufs[j % nb], gsem[j % nb])

            def start_write(j):
                dst = pl.ds(base + (o * ki + j) * rr, rr)
                return pltpu.async_copy(bufs[j % nb], out_hbm.at[dst], wsem[j % nb])

            gathers = {j: start_gather(j) for j in range(min(ahead, ki))}
            writes = {}
            for j in range(ki):
                if j + ahead < ki:
                    if j + ahead - nb >= 0:
                        writes.pop(j + ahead - nb).wait()
                    gathers[j + ahead] = start_gather(j + ahead)
                gathers.pop(j).wait()
                writes[j] = start_write(j)
            for j in sorted(writes):
                writes[j].wait()

    return gather(table, idx.reshape(nw, n_chunks, rr))


def _sc_weighted_row_sum(table, idx, wgt, pick0):
    nc, ns, lanes, mesh = _sc_mesh()
    nw = nc * ns
    n_tok, picks = idx.shape
    ww = table.shape[1]
    rr = SC_GATHER_ROWS
    nq_all = picks // rr
    q0 = pick0 // rr
    nq = nq_all - q0
    tpw = n_tok // nw
    kt = min(SC_ACC_TOKENS, tpw)
    n_outer = tpw // kt
    pw = ww // SC_ACC_PANELS
    nv = pw // lanes
    ahead = min(SC_ACC_AHEAD, nq - 1)
    assert tpw * nw == n_tok and n_outer * kt == tpw and nq_all * rr == picks and q0 * rr == pick0 and nv * lanes == pw
    cp = pltpu.CompilerParams(needs_layout_passes=False)
    buf = pltpu.VMEM((rr, ww), table.dtype)

    @functools.partial(
        pl.kernel, mesh=mesh, out_type=jax.ShapeDtypeStruct((n_tok, 2 * ww), F32),
        scratch_types=[pltpu.VMEM((kt * nq_all, rr), I32), pltpu.VMEM((kt, picks), F32), pltpu.VMEM((kt, 2 * ww), F32)]
                      + [buf] * nq + [pltpu.SemaphoreType.DMA] * nq,
        compiler_params=cp, name="peer_weighted_row_sum",
    )
    def kern(tab_hbm, idx_hbm, w_hbm, out_hbm, idx_v, w_v, acc_v, *scratch):
        wid = lax.axis_index("s") * nc + lax.axis_index("c")
        rows, sems = scratch[:nq], scratch[nq:]

        def chunk_copy(t, q):
            return pltpu.make_async_copy(tab_hbm.at[idx_v.at[t * nq_all + q0 + q]], rows[q], sems[q])

        @pl.loop(0, n_outer)
        def _(o):
            tok0 = wid * tpw + o * kt
            pltpu.sync_copy(idx_hbm.at[pl.ds(tok0 * nq_all, kt * nq_all)], idx_v)
            pltpu.sync_copy(w_hbm.at[pl.ds(tok0, kt)], w_v)
            for q in range(ahead):
                chunk_copy(0, q).start()

            @pl.loop(0, kt)
            def _(t):
                tvec = jnp.full((lanes,), t, I32)
                for q in range(nq):
                    b = q
                    if q + ahead < nq:
                        chunk_copy(t, q + ahead).start()
                    else:
                        @pl.when(t + 1 < kt)
                        def _():
                            chunk_copy(t + 1, q + ahead - nq).start()
                    chunk_copy(t, q).wait()
                    for pan in range(SC_ACC_PANELS):
                        lo_at = lambda c: pl.ds(pan * pw + c * lanes, lanes)
                        hi_at = lambda c: pl.ds(ww + pan * pw + c * lanes, lanes)

                        def row_body(j, acc):
                            wj = plsc.load_gather(w_v, [tvec, jnp.full((lanes,), pick0 + q * rr, I32) + j])
                            new = []
                            for c in range(nv):
                                word = rows[b][j, lo_at(c)]
                                new.append(acc[2 * c] + wj * plsc.bitcast(word << 16, F32))
                                new.append(acc[2 * c + 1] + wj * plsc.bitcast(word & jnp.uint32(0xFFFF0000), F32))
                            return tuple(new)

                        if q == 0:
                            init = tuple(jnp.zeros((lanes,), F32) for _ in range(2 * nv))
                        else:
                            init = tuple(acc_v[t, at(c)] for c in range(nv) for at in (lo_at, hi_at))
                        acc = lax.fori_loop(0, rr, row_body, init)
                        for c in range(nv):
                            acc_v[t, lo_at(c)] = acc[2 * c]
                            acc_v[t, hi_at(c)] = acc[2 * c + 1]

            pltpu.sync_copy(acc_v, out_hbm.at[pl.ds(tok0, kt)])

    return kern(table, idx.reshape(n_tok * nq_all, rr), wgt)


def _pack_bf16_pairs(t):
    half = t.shape[1] // 2
    b = lax.bitcast_convert_type(t.astype(BF16), jnp.uint16).astype(U32)
    return b[:, :half] | (b[:, half:] << 16)


def _gelu_erf(x):
    return 0.5 * x * (1.0 + lax.erf(x * (2.0 ** -0.5)))


def _unpack_pairs(words):
    lo = pltpu.bitcast(words << 16, F32)
    hi = pltpu.bitcast(words & jnp.uint32(0xFFFF0000), F32)
    return lo, hi


def _peer_hidden_kernel(gu_ref, gate_ref, x1_ref, shf_ref, scf_ref, w_ref, hd_ref, *, tb):
    half = D_MODEL // 2
    h2 = x1_ref[...] * (1.0 + scf_ref[...]) + shf_ref[...]
    for t in range(tb):
        ulo, uhi = _unpack_pairs(gu_ref[t * PEER_PICKS:(t + 1) * PEER_PICKS, :])
        hd_ref[:, t:t + 1] = jnp.sum(ulo * h2[t:t + 1, :half] + uhi * h2[t:t + 1, half:], axis=-1, keepdims=True)
    w_ref[...] = _gelu_erf(hd_ref[...]) * gate_ref[...]


def _peer_hidden(gu, gate_blocks, x1, mod, tb):
    r = x1.shape[1]
    blk = pl.BlockSpec((None, PEER_PICKS, tb), lambda gi, i: (i, 0, 0))
    return pl.pallas_call(
        functools.partial(_peer_hidden_kernel, tb=tb),
        grid=(1, r // tb),
        in_specs=[pl.BlockSpec((tb * PEER_PICKS, D_MODEL // 2), lambda gi, i: (i, 0)), blk,
                  _rows(tb, D_MODEL), _mod(mod, tb, 3), _mod(mod, tb, 4)],
        out_specs=blk,
        out_shape=jax.ShapeDtypeStruct((r // tb, PEER_PICKS, tb), F32),
        scratch_shapes=[pltpu.VMEM((PEER_PICKS, tb), F32)],
        compiler_params=_cparams("parallel", "parallel"),
        name="peer_hidden",
    )(gu, gate_blocks, x1, mod, mod)


def _peer_value_tc_kernel(idx_ref, w_ref, tab_hbm, o_ref, tab_v, sem, *, tb, picks):
    @pl.when(pl.program_id(0) == 0)
    def _():
        load = pltpu.make_async_copy(tab_hbm, tab_v, sem)
        load.start()
        load.wait()

    slab = tab_v.shape[1:]
    for t in range(tb):
        lo = jnp.zeros(slab, F32)
        hi = jnp.zeros(slab, F32)
        for e in range(picks):
            word = tab_v[idx_ref[t, e]]
            wt = w_ref[t, e]
            wlo, whi = _unpack_pairs(word)
            lo = lo + wt * wlo
            hi = hi + wt * whi
        o_ref[t, :, pl.ds(0, 128)] = lo
        o_ref[t, :, pl.ds(128, 128)] = hi


def _peer_value_tc(table3, idx, wgt, tb):
    r, picks = idx.shape
    v, s, _ = table3.shape
    part = pl.pallas_call(
        functools.partial(_peer_value_tc_kernel, tb=tb, picks=picks),
        grid=(r // tb,),
        in_specs=[pl.BlockSpec((tb, picks), lambda i: (i, 0), memory_space=pltpu.SMEM),
                  pl.BlockSpec((tb, picks), lambda i: (i, 0), memory_space=pltpu.SMEM),
                  pl.BlockSpec(memory_space=pl.ANY)],
        out_specs=pl.BlockSpec((tb, s, 256), lambda i: (i, 0, 0)),
        out_shape=jax.ShapeDtypeStruct((r, s, 256), F32),
        scratch_shapes=[pltpu.VMEM((v, s, 128), table3.dtype), pltpu.SemaphoreType.DMA],
        compiler_params=_cparams("arbitrary"),
        name="peer_value_tc",
    )(idx, wgt, table3)
    return jnp.concatenate([part[:, :, :128].reshape(r, s * 128), part[:, :, 128:].reshape(r, s * 128)], axis=1)


def _peer_out_kernel(x1_ref, ff_a_ref, ff_b_ref, gtf_ref, l2g_ref, l2b_ref, o_ref, *, alpha):
    ff = ff_a_ref[...] + ff_b_ref[...]
    o_ref[...] = _layernorm(alpha * x1_ref[...] + gtf_ref[...] * ff, l2g_ref[...], l2b_ref[...])


def _peer_out(x1, ff_a, ff_b, mod, p, alpha, tb):
    g, r, _ = x1.shape
    return pl.pallas_call(
        functools.partial(_peer_out_kernel, alpha=alpha),
        grid=(g, r // tb),
        in_specs=[_rows(tb, D_MODEL)] * 3 + [_mod(mod, tb, 5), _const((1, D_MODEL)), _const((1, D_MODEL))],
        out_specs=_rows(tb, D_MODEL),
        out_shape=jax.ShapeDtypeStruct((g, r, D_MODEL), F32),
        compiler_params=_cparams("parallel", "parallel"),
        name="peer_out_ln2",
    )(x1, ff_a, ff_b, mod, _row2(p["ln2_g"]), _row2(p["ln2_b"]))


def _token_stage(x, mod, prev_fn, wkv_fn, attn_fn, p, alpha, tb, sequential):
    ps, pattn, pgate = _inproj(x, mod, p["ln_in_g"], p["ln_in_b"], p["w_in_bf16"], tb)
    r, lw, k, v, kk, kka, gl = _rwkv_prep(ps, prev_fn(), p, tb, sequential)
    y, wkv_new = wkv_fn(r, lw, k, v, kk, kka)
    ya = _rwkv_post(y, r, k, v, gl, p, tb)
    ob = attn_fn(pattn)
    x1, scores_t = _merge(x, ya, ob, pgate, mod, p, alpha, tb)
    idx_t, gate_t = _topk(scores_t, TOPK_TB)
    return ps, pattn, wkv_new, x1, idx_t, gate_t


def kernel(x_prompt, x_sample, state_wkv, state_shift, cache_k_win, cache_v_win, c_prompt, c_sample, ln_in_g, ln_in_b, w_ada, b_ada, w_in, mu_shift, rwkv_w0, rwkv_w2, rwkv_a0, rwkv_a2, rwkv_g2, rwkv_k_k, rwkv_k_a, rwkv_r_k, rwkv_gn_g, rwkv_gn_b, attn_sinks, w_pa, w_pb, w_o, ln1_g, ln1_b, peer_wq, peer_sub_keys, peer_u, peer_v, ln2_g, ln2_b):
    depth = w_in.shape[0]
    assert depth == 1, "single-layer trunk"
    alpha = (2.0 * depth) ** 0.25
    n_p, t_p, _ = x_prompt.shape
    n_s = x_sample.shape[0]
    p = dict(ln_in_g=ln_in_g, ln_in_b=ln_in_b, w_in_bf16=w_in[0].astype(BF16), mu_shift=mu_shift[0],
             rwkv_w0=rwkv_w0[0], rwkv_w2=rwkv_w2[0], rwkv_a0=rwkv_a0[0], rwkv_a2=rwkv_a2[0], rwkv_g2=rwkv_g2[0],
             rwkv_k_k=rwkv_k_k[0], rwkv_k_a=rwkv_k_a[0], rwkv_r_k=rwkv_r_k[0], rwkv_gn_g=rwkv_gn_g[0],
             rwkv_gn_b=rwkv_gn_b[0], w_pa=w_pa[0], w_pb=w_pb[0], w_o=w_o[0], ln1_g=ln1_g[0], ln1_b=ln1_b[0],
             peer_wq=peer_wq[0], peer_sub_keys=peer_sub_keys[0], ln2_g=ln2_g[0], ln2_b=ln2_b[0])
    sinks = attn_sinks[0]

    n_c = n_p + n_s
    pad = (-n_c) % 8
    c_all = jnp.concatenate([c_prompt, c_sample, jnp.zeros((pad, D_MODEL), F32)], axis=0)
    mod_all = _modulation(c_all, w_ada[0], b_ada[0])
    mod_p = mod_all[:n_p].reshape(n_p, 1, N_MOD * D_MODEL)
    mod_s = mod_all[n_p:n_c].reshape(1, n_s, N_MOD * D_MODEL)

    seg = min(PROMPT_SEGMENT, t_p)
    assert t_p % seg == 0
    first = min(PROMPT_FIRST_SEGMENT, seg)

    def segments(b):
        cuts = list(range(0, t_p + 1, seg))
        if b == 0 and first < seg:
            cuts.insert(1, first)
        return zip(cuts[:-1], cuts[1:])

    prompt_ids = [(b, lo, hi) for b in range(n_p) for lo, hi in segments(b)]
    assert all((hi - lo) % TOKEN_TB == 0 for _, lo, hi in prompt_ids)
    carry = {}

    def prompt_group(b, lo, hi):
        def prev_fn():
            return carry[b][0] if lo > 0 else jnp.zeros((1, 1, SHIFT_W), F32)

        def wkv_fn(r, lw, k, v, kk, kka):
            s0 = carry[b][1] if lo > 0 else jnp.zeros((1, H_A, HD_A, HD_A), F32)
            return _rwkv_chunk_scan(r, lw, k, v, kk, kka, s0)

        def attn_fn(pa):
            return _attn_band(pa, carry[b][2] if lo > 0 else None, sinks)

        return (x_prompt[b:b + 1, lo:hi], mod_p[b:b + 1], prev_fn, wkv_fn, attn_fn, TOKEN_TB, True)

    tu, tv = _pack_bf16_pairs(peer_u[0]), _pack_bf16_pairs(peer_v[0])

    def select(x, mod, prev_fn, wkv_fn, attn_fn, tb, sequential):
        ps, pattn, wkv_new, x1, idx_t, gate_t = _token_stage(x, mod, prev_fn, wkv_fn, attn_fn, p, alpha, tb, sequential)
        r = x1.shape[1]
        idx = jnp.transpose(idx_t.reshape(PEER_PICKS, r))
        gate_blocks = jnp.transpose(gate_t.reshape(PEER_PICKS, r // PEER_TB, PEER_TB), (1, 0, 2))
        gu = _sc_gather_rows(tu, idx.reshape(-1))
        return ps, pattn, wkv_new, (gu, idx, gate_blocks, x1, mod)

    tv_slabs = tv.reshape(tv.shape[0], -1, 128)

    def weigh(sel):
        gu, idx, gate_blocks, x1, mod = sel
        w_blocks = _peer_hidden(gu, gate_blocks, x1, mod, PEER_TB)
        wgt = jnp.transpose(w_blocks, (0, 2, 1)).reshape(-1, PEER_PICKS)
        x = VALUE_TC_PICKS
        ff_sc = _sc_weighted_row_sum(tv, idx, wgt, x)
        ff_tc = _peer_value_tc(tv_slabs, idx[:, :x], wgt[:, :x], min(VALUE_TC_TB, idx.shape[0]))
        return ff_sc, ff_tc, x1, mod

    def finish(wsum):
        ff_sc, ff_tc, x1, mod = wsum
        return _peer_out(x1, ff_sc[None], ff_tc[None], mod, p, alpha, min(TOKEN_TB, x1.shape[1]))

    xs = x_sample.reshape(1, n_s, D_MODEL)

    def prev_s():
        return state_shift[0].reshape(1, n_s, SHIFT_W)

    def wkv_s(r, lw, k, v, kk, kka):
        sq = lambda z: z.reshape(n_s, D_A)
        y, s = _rwkv_step(state_wkv[0], sq(r), sq(lw), sq(k), sq(kk), sq(kka), sq(v), STEP_NB)
        return y.reshape(1, n_s, D_A), s

    def attn_s(pa):
        o = _attn_cache(pa.reshape(n_s, 1, ATTN_W), cache_k_win[0], cache_v_win[0], sinks, STEP_NB)
        return o.reshape(1, n_s, D_B)

    n_g = len(prompt_ids) + 1
    sel, wsum, y_l = [None] * n_g, [None] * n_g, [None] * n_g
    for step in range(n_g + 2):
        if step < n_g - 1:
            b, lo, hi = prompt_ids[step]
            ps, pattn, wkv_new, sel[step] = select(*prompt_group(b, lo, hi))
            carry[b] = (ps[:, -1:], wkv_new, pattn[:, -WINDOW:])
        elif step == n_g - 1:
            ps_s, pattn_s, wkv_s_new, sel[step] = select(xs, mod_s, prev_s, wkv_s, attn_s, min(TOKEN_TB, n_s), False)
        if 0 <= step - 1 < n_g:
            wsum[step - 1] = weigh(sel[step - 1])
        if 0 <= step - 2 < n_g:
            y_l[step - 2] = finish(wsum[step - 2])
    y_s = y_l[n_g - 1]
    y_p = jnp.concatenate(y_l[:n_g - 1], axis=1).reshape(n_p, t_p, D_MODEL)
    shift_p = jnp.concatenate([carry[b][0][:, 0] for b in range(n_p)], axis=0)
    pattn_p = jnp.concatenate([carry[b][2] for b in range(n_p)], axis=0)
    wkv_p_new = jnp.concatenate([carry[b][1] for b in range(n_p)], axis=0)

    kv = lambda pa, o: pa[..., o:o + H_KV * HD_B]
    ko, vo = D_B, D_B + H_KV * HD_B
    k_win_p = kv(pattn_p, ko)[:, -WINDOW:].reshape(n_p, WINDOW, H_KV, HD_B)
    v_win_p = kv(pattn_p, vo)[:, -WINDOW:].reshape(n_p, WINDOW, H_KV, HD_B)
    k_new_s = kv(pattn_s, ko).reshape(n_s, 1, H_KV, HD_B)
    v_new_s = kv(pattn_s, vo).reshape(n_s, 1, H_KV, HD_B)
    k_win_s = jnp.concatenate([cache_k_win[0], k_new_s], axis=1)[:, -WINDOW:]
    v_win_s = jnp.concatenate([cache_v_win[0], v_new_s], axis=1)[:, -WINDOW:]
    return (y_p, y_s.reshape(n_s, 1, D_MODEL), wkv_p_new[None], wkv_s_new[None],
            shift_p[None], ps_s.reshape(n_s, SHIFT_W)[None],
            k_win_p[None], k_win_s[None], v_win_p[None], v_win_s[None])
```

```python
import functools
import math

import jax
import jax.numpy as jnp
from jax import lax
from jax.experimental import pallas as pl
from jax.experimental.pallas import tpu as pltpu
from jax.experimental.pallas import tpu_sc as plsc

F32 = jnp.float32
BF16 = jnp.bfloat16
I32 = jnp.int32
U32 = jnp.uint32

D_MODEL = 1024
H_A, HD_A = 8, 64
D_A = H_A * HD_A
D_LORA_W, D_LORA_A, D_LORA_G = 64, 64, 128
GN_EPS = 64e-5
H_Q, H_KV, HD_B = 8, 2, 64
G_Q = H_Q // H_KV
D_B = H_Q * HD_B
WINDOW = 128
N_KEYS = 128
PEER_HEADS, PEER_TOPK, PEER_HALF = 8, 16, 128
PEER_PICKS = PEER_HEADS * PEER_TOPK
N_MOD = 6
LN_EPS = 1e-5
NEG_INF = -1e30
OFF_WD = 3 * D_A
OFF_AD = OFF_WD + D_LORA_W
OFF_GD = OFF_AD + D_LORA_A
SHIFT_W = OFF_GD + D_LORA_G
ATTN_W = D_B + 2 * H_KV * HD_B
GATE_W = 2 * D_MODEL
D_IN = SHIFT_W + ATTN_W + GATE_W

VMEM_LIMIT = 48 * 1024 * 1024
SUBLANES = 8
RWKV_CHUNK = 64
SC_GATHER_ROWS = 32
SC_ACC_AHEAD = 3
SC_ACC_TOKENS = 16
SC_DOT_TOKENS = 8
SC_ACC_PANELS = 4
TOKEN_TB = 256
TOPK_TB = 128
PROMPT_SEGMENT = 2048
PROMPT_FIRST_SEGMENT = 512
STEP_NB = 8


def _cparams(*sem):
    return pltpu.CompilerParams(dimension_semantics=sem, vmem_limit_bytes=VMEM_LIMIT)


def _layernorm(x, g, b):
    mu = jnp.mean(x, -1, keepdims=True)
    xc = x - mu
    var = jnp.mean(xc * xc, -1, keepdims=True)
    return xc * lax.rsqrt(var + LN_EPS) * g + b


def _split(x):
    hi = x.astype(BF16)
    lo = (x - hi.astype(F32)).astype(BF16)
    return hi, lo


_NN = (((1,), (0,)), ((), ()))
_NT = (((1,), (1,)), ((), ()))
_TN = (((0,), (0,)), ((), ()))


def _dot3(a, b, dims=_NN):
    ah, al = _split(a)
    bh, bl = _split(b)
    d = functools.partial(lax.dot_general, dimension_numbers=dims, preferred_element_type=F32)
    return d(ah, bh) + d(ah, bl) + d(al, bh)


def _dot_exact_lhs(a_bf16, b, dims=_NN):
    b1 = b.astype(BF16)
    r1 = b - b1.astype(F32)
    b2 = r1.astype(BF16)
    b3 = (r1 - b2.astype(F32)).astype(BF16)
    d = functools.partial(lax.dot_general, dimension_numbers=dims, preferred_element_type=F32)
    return d(a_bf16, b1) + d(a_bf16, b2) + d(a_bf16, b3)


def _dotb(a, b, dims=_NN):
    return lax.dot_general(a.astype(BF16), b.astype(BF16), dims, preferred_element_type=F32)


def _rows(tb, width, col=0):
    return pl.BlockSpec((None, tb, width), lambda g, i: (g, i, col))


def _mod(mod, tb, col):
    if mod.shape[1] == 1:
        return pl.BlockSpec((None, 1, D_MODEL), lambda g, i: (g, 0, col))
    return pl.BlockSpec((None, tb, D_MODEL), lambda g, i: (g, i, col))


def _const(shape):
    n = len(shape)
    return pl.BlockSpec(shape, lambda g, i: (0,) * n)


def _row2(p):
    return p.reshape(1, -1).astype(F32)


def _mod_kernel(c_ref, w_ref, b_ref, o_ref):
    c = c_ref[...]
    a = c * jax.nn.sigmoid(c)
    o_ref[...] = _dot3(a, w_ref[...]) + b_ref[...]


def _modulation(c, w_ada, b_ada):
    n = c.shape[0]
    tn = D_MODEL
    return pl.pallas_call(
        _mod_kernel,
        grid=(w_ada.shape[1] // tn,),
        in_specs=[pl.BlockSpec((n, D_MODEL), lambda j: (0, 0)),
                  pl.BlockSpec((D_MODEL, tn), lambda j: (0, j)),
                  pl.BlockSpec((1, tn), lambda j: (0, j))],
        out_specs=pl.BlockSpec((n, tn), lambda j: (0, j)),
        out_shape=jax.ShapeDtypeStruct((n, w_ada.shape[1]), F32),
        compiler_params=_cparams("arbitrary"),
        name="modulation",
    )(c, w_ada, b_ada.reshape(1, -1))


def _inproj_kernel(x_ref, sh_ref, sc_ref, g_ref, b_ref, w_ref, ps_ref, pa_ref, pg_ref):
    xn = _layernorm(x_ref[...], g_ref[...], b_ref[...])
    h = (xn * (1.0 + sc_ref[...]) + sh_ref[...]).astype(BF16)
    ps_ref[...] = jnp.dot(h, w_ref[:, :SHIFT_W], preferred_element_type=F32)
    pa_ref[...] = jnp.dot(h, w_ref[:, SHIFT_W:SHIFT_W + ATTN_W], preferred_element_type=F32)
    pg_ref[...] = jnp.dot(h, w_ref[:, SHIFT_W + ATTN_W:], preferred_element_type=F32)


def _inproj(x, mod, ln_g, ln_b, w_in_bf16, tb):
    g, r, _ = x.shape
    shp = lambda w: jax.ShapeDtypeStruct((g, r, w), F32)
    return pl.pallas_call(
        _inproj_kernel,
        grid=(g, r // tb),
        in_specs=[_rows(tb, D_MODEL), _mod(mod, tb, 0), _mod(mod, tb, 1),
                  _const((1, D_MODEL)), _const((1, D_MODEL)), _const((D_MODEL, D_IN))],
        out_specs=[_rows(tb, SHIFT_W), _rows(tb, ATTN_W), _rows(tb, GATE_W)],
        out_shape=[shp(SHIFT_W), shp(ATTN_W), shp(GATE_W)],
        compiler_params=_cparams("parallel", "parallel"),
        name="inproj",
    )(x, mod, mod, _row2(ln_g), _row2(ln_b), w_in_bf16)


def _softplus(x):
    return jnp.maximum(x, 0.0) + jnp.log1p(jnp.exp(-jnp.abs(x)))


def _rwkv_prep_kernel(ps_ref, prev_ref, first_ref, mu_ref, w0_ref, w2_ref, a0_ref, a2_ref, g2_ref, kk_w_ref, ka_w_ref,
                      hsum_ref, r_ref, lw_ref, k_ref, v_ref, kk_ref, kka_ref, g_ref, *, sequential):
    ps = ps_ref[...]
    if sequential:
        before = jnp.where(pl.program_id(1) == 0, first_ref[...], prev_ref[SUBLANES - 1:SUBLANES, :])
        row = lax.broadcasted_iota(I32, ps.shape, 0)
        prev = jnp.where(row == 0, before, pltpu.roll(ps, 1, 0))
    else:
        prev = prev_ref[...]
    xs = ps + (prev - ps) * mu_ref[...]
    r = xs[:, 0:D_A]
    k = xs[:, D_A:2 * D_A]
    v = xs[:, 2 * D_A:3 * D_A]
    wd = xs[:, OFF_WD:OFF_AD]
    ad = xs[:, OFF_AD:OFF_GD]
    gd = xs[:, OFF_GD:SHIFT_W]
    z = w0_ref[...] + _dot3(jnp.tanh(wd), w2_ref[...])
    w_log = -_softplus(-z) - 0.5
    a = jax.nn.sigmoid(a0_ref[...] + _dot3(ad, a2_ref[...]))
    kk = k * kk_w_ref[...]
    ss = _dot3(kk * kk, hsum_ref[...])
    kk = kk / jnp.maximum(jnp.sqrt(ss), 1e-12)
    r_ref[...] = r
    lw_ref[...] = -jnp.exp(w_log)
    k_ref[...] = k * (1.0 + (a - 1.0) * ka_w_ref[...])
    v_ref[...] = v
    kk_ref[...] = kk
    kka_ref[...] = kk * a
    g_ref[...] = _dot3(jax.nn.sigmoid(gd), g2_ref[...])


def _head_sum_matrix():
    h = jnp.arange(D_A) // HD_A
    return (h[:, None] == h[None, :]).astype(F32)


def _rwkv_prep(ps, prev, p, tb, sequential):
    g, r, _ = ps.shape
    shp = jax.ShapeDtypeStruct((g, r, D_A), F32)
    if sequential:
        per = tb // SUBLANES
        prev_args = (ps, prev)
        prev_specs = [pl.BlockSpec((None, SUBLANES, SHIFT_W), lambda gi, i: (gi, jnp.maximum(i * per - 1, 0), 0)),
                      pl.BlockSpec((None, 1, SHIFT_W), lambda gi, i: (gi, 0, 0))]
    else:
        prev_args = (prev, prev[:, :1])
        prev_specs = [_rows(tb, SHIFT_W), pl.BlockSpec((None, 1, SHIFT_W), lambda gi, i: (gi, 0, 0))]
    return pl.pallas_call(
        functools.partial(_rwkv_prep_kernel, sequential=sequential),
        grid=(g, r // tb),
        in_specs=[_rows(tb, SHIFT_W)] + prev_specs + [_const((1, SHIFT_W)),
                  _const((1, D_A)), _const((D_LORA_W, D_A)), _const((1, D_A)), _const((D_LORA_A, D_A)),
                  _const((D_LORA_G, D_A)), _const((1, D_A)), _const((1, D_A)), _const((D_A, D_A))],
        out_specs=[_rows(tb, D_A)] * 7,
        out_shape=[shp] * 7,
        compiler_params=_cparams("parallel", "parallel"),
        name="rwkv_prep",
    )(ps, *prev_args, _row2(p["mu_shift"]), _row2(p["rwkv_w0"]), p["rwkv_w2"], _row2(p["rwkv_a0"]), p["rwkv_a2"],
      p["rwkv_g2"], _row2(p["rwkv_k_k"]), _row2(p["rwkv_k_a"]), _head_sum_matrix())


def _rwkv_chunk_kernel(r_ref, lw_ref, k_ref, v_ref, kk_ref, kka_ref, s0_ref, y_ref, s_ref):
    c = RWKV_CHUNK

    @pl.when(pl.program_id(1) == 0)
    def _():
        s_ref[...] = s0_ref[...]

    row = lax.broadcasted_iota(I32, (c, c), 0)
    col = lax.broadcasted_iota(I32, (c, c), 1)
    tril = row >= col
    stril = row > col
    lw = lw_ref[...]
    cum = _dot_exact_lhs(tril.astype(BF16), lw)
    cum_end = cum[c - 1:c, :]
    g_inv = jnp.exp(-cum)
    g_end = jnp.exp(cum_end - cum)
    a_hat = -kk_ref[...] * jnp.exp(cum - lw)
    b_hat = kka_ref[...] * g_inv
    k_hat = k_ref[...] * g_inv
    r_til = r_ref[...] * jnp.exp(cum)
    b_end = kka_ref[...] * g_end
    k_end = k_ref[...] * g_end
    gam_end = jnp.exp(cum_end)
    v_all = v_ref[...]
    s_all = s_ref[...]
    n_steps = int(math.log2(c))
    heads = range(H_A)
    sl = [slice(h * HD_A, (h + 1) * HD_A) for h in heads]
    vh = [v_all[:, sl[h]] for h in heads]
    ar = [jnp.concatenate([a_hat[:, sl[h]], r_til[:, sl[h]]], axis=0) for h in heads]
    bk = [jnp.concatenate([b_hat[:, sl[h]], k_hat[:, sl[h]]], axis=0) for h in heads]
    x = [_dot3(ar[h], bk[h], _NT) for h in heads]
    ars = [_dot3(ar[h], s_all[h], _NT) for h in heads]
    a_ak = [jnp.where(stril, x[h][:c, c:], 0.0) for h in heads]
    n = [jnp.where(stril, x[h][:c, :c], 0.0) for h in heads]
    u = [ars[h][:c] + _dot3(a_ak[h], vh[h]) for h in heads]
    for it in range(n_steps):
        u = [u[h] + _dot3(n[h], u[h]) for h in heads]
        if it + 1 < n_steps:
            n = [_dot3(n[h], n[h]) for h in heads]
    uv = [jnp.concatenate([u[h], vh[h]], axis=0) for h in heads]
    a_r = [jnp.concatenate([jnp.where(tril, x[h][c:, :c], 0.0), jnp.where(tril, x[h][c:, c:], 0.0)], axis=1)
           for h in heads]
    y = [ars[h][c:] + _dot3(a_r[h], uv[h]) for h in heads]
    bke = [jnp.concatenate([b_end[:, sl[h]], k_end[:, sl[h]]], axis=0) for h in heads]
    s_new = [s_all[h] * gam_end[:, sl[h]] + _dot3(uv[h], bke[h], _TN) for h in heads]
    for h in heads:
        y_ref[:, sl[h]] = y[h]
        s_ref[h] = s_new[h]


def _rwkv_chunk_scan(r, lw, k, v, kk, kka, s0):
    n, t, _ = r.shape
    c = RWKV_CHUNK
    seq = pl.BlockSpec((None, c, D_A), lambda b, i: (b, i, 0))
    st = pl.BlockSpec((None, H_A, HD_A, HD_A), lambda b, i: (b, 0, 0, 0))
    return pl.pallas_call(
        _rwkv_chunk_kernel,
        grid=(n, t // c),
        in_specs=[seq] * 6 + [st],
        out_specs=[seq, st],
        out_shape=[jax.ShapeDtypeStruct((n, t, D_A), F32), jax.ShapeDtypeStruct((n, H_A, HD_A, HD_A), F32)],
        compiler_params=_cparams("parallel", "arbitrary"),
        name="rwkv_chunk_scan",
    )(r, lw, k, v, kk, kka, s0)


def _rwkv_step_kernel(s_ref, r_ref, lw_ref, k_ref, kk_ref, kka_ref, v_ref, y_ref, so_ref):
    s = s_ref[...]
    sa = jnp.sum(s * (-kk_ref[...]), axis=-1, keepdims=True)
    s = s * jnp.exp(lw_ref[...]) + sa * kka_ref[...] + v_ref[...] * k_ref[...]
    so_ref[...] = s
    y_ref[...] = jnp.sum(s * r_ref[...], axis=-1, keepdims=True)


def _rwkv_step(s0, r, lw, k, kk, kka, v, nb):
    n = s0.shape[0]
    key = lambda z: z.reshape(n, H_A, 1, HD_A)
    st = pl.BlockSpec((nb, H_A, HD_A, HD_A), lambda i: (i, 0, 0, 0))
    ks = pl.BlockSpec((nb, H_A, 1, HD_A), lambda i: (i, 0, 0, 0))
    vs = pl.BlockSpec((nb, H_A, HD_A, 1), lambda i: (i, 0, 0, 0))
    y, s = pl.pallas_call(
        _rwkv_step_kernel,
        grid=(n // nb,),
        in_specs=[st, ks, ks, ks, ks, ks, vs],
        out_specs=[vs, st],
        out_shape=[jax.ShapeDtypeStruct((n, H_A, HD_A, 1), F32), jax.ShapeDtypeStruct(s0.shape, F32)],
        compiler_params=_cparams("parallel"),
        name="rwkv_step",
    )(s0, key(r), key(lw), key(k), key(kk), key(kka), v.reshape(n, H_A, HD_A, 1))
    return y.reshape(n, D_A), s


def _rwkv_post_kernel(y_ref, r_ref, k_ref, v_ref, g_ref, gn_g_ref, gn_b_ref, rk_ref, hsum_ref, o_ref):
    y = y_ref[...]
    hs = hsum_ref[...]
    mu = _dot3(y, hs) * (1.0 / HD_A)
    yc = y - mu
    var = _dot3(yc * yc, hs) * (1.0 / HD_A)
    yn = yc * lax.rsqrt(var + GN_EPS) * gn_g_ref[...] + gn_b_ref[...]
    bonus = _dot3(r_ref[...] * k_ref[...] * rk_ref[...], hs) * v_ref[...]
    o_ref[...] = (yn + bonus) * g_ref[...]


def _rwkv_post(y, r, k, v, g, p, tb):
    gg, rr, _ = y.shape
    return pl.pallas_call(
        _rwkv_post_kernel,
        grid=(gg, rr // tb),
        in_specs=[_rows(tb, D_A)] * 5 + [_const((1, D_A))] * 3 + [_const((D_A, D_A))],
        out_specs=_rows(tb, D_A),
        out_shape=jax.ShapeDtypeStruct((gg, rr, D_A), F32),
        compiler_params=_cparams("parallel", "parallel"),
        name="rwkv_post",
    )(y, r, k, v, g, _row2(p["rwkv_gn_g"]), _row2(p["rwkv_gn_b"]), _row2(p["rwkv_r_k"]), _head_sum_matrix())


def _sink_softmax(s, sink):
    m = jnp.maximum(jnp.max(s, axis=-1, keepdims=True), sink)
    p = jnp.exp(s - m)
    den = jnp.sum(p, axis=-1, keepdims=True) + jnp.exp(sink - m)
    return p / den


def _attn_band_kernel(cur_ref, prev_ref, carry_ref, sink_ref, o_ref, *, has_carry):
    blk = WINDOW
    i = pl.program_id(1)
    cur = cur_ref[...]
    prev = jnp.where(i == 0, carry_ref[...], prev_ref[...])
    qi = lax.broadcasted_iota(I32, (G_Q * blk, 2 * blk), 0) % blk
    kj = lax.broadcasted_iota(I32, (G_Q * blk, 2 * blk), 1)
    rel = blk + qi - kj
    valid = (rel >= 0) & (rel <= WINDOW)
    if not has_carry:
        valid = valid & ((kj >= blk) | (i > 0))
    relf = rel.astype(F32)
    gidx = lax.broadcasted_iota(I32, (G_Q * blk, 1), 0) // blk
    for kvh in range(H_KV):
        q4 = jnp.concatenate([cur[:, (kvh * G_Q + g) * HD_B:(kvh * G_Q + g + 1) * HD_B] for g in range(G_Q)], axis=0)
        ko = D_B + kvh * HD_B
        vo = D_B + H_KV * HD_B + kvh * HD_B
        kmat = jnp.concatenate([prev[:, ko:ko + HD_B], cur[:, ko:ko + HD_B]], axis=0)
        vmat = jnp.concatenate([prev[:, vo:vo + HD_B], cur[:, vo:vo + HD_B]], axis=0)
        slope = jnp.zeros((G_Q * blk, 1), F32)
        sink = jnp.zeros((G_Q * blk, 1), F32)
        for g in range(G_Q):
            hq = kvh * G_Q + g
            slope = jnp.where(gidx == g, 2.0 ** (-8.0 * (hq + 1) / H_Q), slope)
            sink = jnp.where(gidx == g, sink_ref[hq], sink)
        s = _dotb(q4, kmat, _NT) * (HD_B ** -0.5)
        s = jnp.where(valid, s - slope * relf, NEG_INF)
        p = _sink_softmax(s, sink)
        o = _dotb(p, vmat)
        for g in range(G_Q):
            hq = kvh * G_Q + g
            o_ref[:, hq * HD_B:(hq + 1) * HD_B] = o[g * blk:(g + 1) * blk]


def _attn_band(pattn, carry, sinks):
    n, t, _ = pattn.shape
    blk = WINDOW
    has_carry = carry is not None
    if not has_carry:
        carry = jnp.zeros((n, blk, ATTN_W), F32)
    return pl.pallas_call(
        functools.partial(_attn_band_kernel, has_carry=has_carry),
        grid=(n, t // blk),
        in_specs=[pl.BlockSpec((None, blk, ATTN_W), lambda b, i: (b, i, 0)),
                  pl.BlockSpec((None, blk, ATTN_W), lambda b, i: (b, jnp.maximum(i - 1, 0), 0)),
                  pl.BlockSpec((None, blk, ATTN_W), lambda b, i: (b, 0, 0)),
                  pl.BlockSpec(memory_space=pltpu.SMEM)],
        out_specs=pl.BlockSpec((None, blk, D_B), lambda b, i: (b, i, 0)),
        out_shape=jax.ShapeDtypeStruct((n, t, D_B), F32),
        compiler_params=_cparams("parallel", "parallel"),
        name="attn_band",
    )(pattn, pattn, carry, sinks.astype(F32))


def _attn_cache_kernel(cur_ref, kc_ref, vc_ref, sink_ref, o_ref, *, nb):
    relc = (WINDOW - lax.broadcasted_iota(I32, (G_Q, WINDOW), 1)).astype(F32)
    gidx = lax.broadcasted_iota(I32, (G_Q, 1), 0)
    for b in range(nb):
        cur = cur_ref[b]
        for kvh in range(H_KV):
            q4 = jnp.concatenate([cur[:, (kvh * G_Q + g) * HD_B:(kvh * G_Q + g + 1) * HD_B] for g in range(G_Q)], axis=0)
            ko = D_B + kvh * HD_B
            vo = D_B + H_KV * HD_B + kvh * HD_B
            k_new = cur[:, ko:ko + HD_B]
            v_new = cur[:, vo:vo + HD_B]
            kc = kc_ref[b, :, kvh * HD_B:(kvh + 1) * HD_B]
            vc = vc_ref[b, :, kvh * HD_B:(kvh + 1) * HD_B]
            slope = jnp.zeros((G_Q, 1), F32)
            sink = jnp.zeros((G_Q, 1), F32)
            for g in range(G_Q):
                hq = kvh * G_Q + g
                slope = jnp.where(gidx == g, 2.0 ** (-8.0 * (hq + 1) / H_Q), slope)
                sink = jnp.where(gidx == g, sink_ref[hq], sink)
            scale = HD_B ** -0.5
            sc = _dotb(q4, kc, _NT) * scale - slope * relc
            sn = jnp.sum(q4.astype(BF16).astype(F32) * k_new.astype(BF16).astype(F32), axis=-1, keepdims=True) * scale
            m = jnp.maximum(jnp.maximum(jnp.max(sc, axis=-1, keepdims=True), sn), sink)
            pc = jnp.exp(sc - m)
            pn = jnp.exp(sn - m)
            den = jnp.sum(pc, axis=-1, keepdims=True) + pn + jnp.exp(sink - m)
            o = (_dotb(pc / den, vc) + (pn / den).astype(BF16).astype(F32) * v_new.astype(BF16).astype(F32))
            for g in range(G_Q):
                hq = kvh * G_Q + g
                o_ref[b, :, hq * HD_B:(hq + 1) * HD_B] = o[g:g + 1]


def _attn_cache(pattn, k_buf, v_buf, sinks, nb):
    n = pattn.shape[0]
    kc = k_buf.reshape(n, WINDOW, H_KV * HD_B)
    vc = v_buf.reshape(n, WINDOW, H_KV * HD_B)
    return pl.pallas_call(
        functools.partial(_attn_cache_kernel, nb=nb),
        grid=(n // nb,),
        in_specs=[pl.BlockSpec((nb, 1, ATTN_W), lambda i: (i, 0, 0)),
                  pl.BlockSpec((nb, WINDOW, H_KV * HD_B), lambda i: (i, 0, 0)),
                  pl.BlockSpec((nb, WINDOW, H_KV * HD_B), lambda i: (i, 0, 0)),
                  pl.BlockSpec(memory_space=pltpu.SMEM)],
        out_specs=pl.BlockSpec((nb, 1, D_B), lambda i: (i, 0, 0)),
        out_shape=jax.ShapeDtypeStruct((n, 1, D_B), F32),
        compiler_params=_cparams("parallel"),
        name="attn_cache",
    )(pattn, kc, vc, sinks.astype(F32))


def _merge_kernel(x_ref, ya_ref, ob_ref, pg_ref, gtm_ref, shf_ref, scf_ref, lng_ref, lnb_ref, l1g_ref, l1b_ref,
                  wpa_ref, wpb_ref, wo_ref, wq_ref, sk_ref, x1_ref, h2_ref, st_ref, *, alpha):
    ya = jnp.dot(ya_ref[...].astype(BF16), wpa_ref[...], preferred_element_type=F32)
    yb = jnp.dot(ob_ref[...].astype(BF16), wpb_ref[...], preferred_element_type=F32)
    pg = pg_ref[...]
    merged = jax.nn.sigmoid(pg[:, :D_MODEL]) * ya + jax.nn.sigmoid(pg[:, D_MODEL:]) * yb
    mix = jnp.dot(merged.astype(BF16), wo_ref[...], preferred_element_type=F32)
    xn = _layernorm(x_ref[...], lng_ref[...], lnb_ref[...])
    x1 = _layernorm(alpha * xn + gtm_ref[...] * mix, l1g_ref[...], l1b_ref[...])
    x1_ref[...] = x1
    h2 = x1 * (1.0 + scf_ref[...]) + shf_ref[...]
    h2_ref[...] = h2
    q = jnp.dot(h2.astype(BF16), wq_ref[...], preferred_element_type=F32)
    for hc in range(2 * PEER_HEADS):
        st_ref[hc] = _dot3(sk_ref[hc % 2], q[:, hc * PEER_HALF:(hc + 1) * PEER_HALF], _NT)


def _merge(x, ya, ob, pg, mod, p, alpha, tb):
    g, r, _ = x.shape
    return pl.pallas_call(
        functools.partial(_merge_kernel, alpha=alpha),
        grid=(g, r // tb),
        in_specs=[_rows(tb, D_MODEL), _rows(tb, D_A), _rows(tb, D_B), _rows(tb, GATE_W),
                  _mod(mod, tb, 2), _mod(mod, tb, 3), _mod(mod, tb, 4)]
                 + [_const((1, D_MODEL))] * 4
                 + [_const((D_A, D_MODEL)), _const((D_B, D_MODEL)), _const((D_MODEL, D_MODEL)),
                    _const((D_MODEL, 2 * PEER_HEADS * PEER_HALF)), _const((2, N_KEYS, PEER_HALF))],
        out_specs=[_rows(tb, D_MODEL), _rows(tb, D_MODEL),
                   pl.BlockSpec((None, 2 * PEER_HEADS, N_KEYS, tb), lambda gi, i: (gi, 0, 0, i))],
        out_shape=[jax.ShapeDtypeStruct((g, r, D_MODEL), F32), jax.ShapeDtypeStruct((g, r, D_MODEL), F32),
                   jax.ShapeDtypeStruct((g, 2 * PEER_HEADS, N_KEYS, r), F32)],
        compiler_params=_cparams("parallel", "parallel"),
        name="merge_ln1_peer_scores",
    )(x, ya, ob, pg, mod, mod, mod, _row2(p["ln_in_g"]), _row2(p["ln_in_b"]), _row2(p["ln1_g"]), _row2(p["ln1_b"]),
      p["w_pa"].astype(BF16), p["w_pb"].astype(BF16), p["w_o"].astype(BF16), p["peer_wq"].astype(BF16),
      p["peer_sub_keys"])


def _extract_top(problems, n_rows, tb):
    rio = lax.broadcasted_iota(I32, (n_rows, tb), 0).astype(F32)
    vals = [v for v, _ in problems]
    tops = [([], [], []) for _ in problems]
    for _ in range(PEER_TOPK):
        for n, (_, payload) in enumerate(problems):
            m = jnp.max(vals[n], axis=0, keepdims=True)
            i = jnp.min(jnp.where(vals[n] == m, rio, float(n_rows)), axis=0, keepdims=True)
            sel = rio == i
            tops[n][0].append(m)
            tops[n][1].append(i)
            if payload is not None:
                tops[n][2].append(jnp.max(jnp.where(sel, payload, -1.0), axis=0, keepdims=True))
            vals[n] = jnp.where(sel, -jnp.inf, vals[n])
    cat = lambda z: jnp.concatenate(z, axis=0) if z else None
    return [(cat(v), cat(i), cat(pl_)) for v, i, pl_ in tops]


def _pair_candidates(v1, i1, v2, i2, tb):
    k = PEER_TOPK
    sub = 8
    eid = lambda a0, a1, b0, b1: i1[a0:a1] * float(N_KEYS) + i2[b0:b1]
    vals = [v1[0:1] + v2, v1[1:2] + v2[0:sub]]
    ids = [eid(0, 1, 0, k), eid(1, 2, 0, sub)]
    brow = lax.broadcasted_iota(I32, (sub, tb), 0)
    for a in range(2, sub):
        vals.append(jnp.where(brow < k // (a + 1), v1[a:a + 1] + v2[0:sub], -jnp.inf))
        ids.append(eid(a, a + 1, 0, sub))
    vals.append(v1[sub:k] + v2[0:1])
    ids.append(eid(sub, k, 0, 1))
    return jnp.concatenate(vals, axis=0), jnp.concatenate(ids, axis=0)


def _topk_kernel(s_ref, idx_ref, gate_ref):
    tb = s_ref.shape[-1]

    def head_pair(hp, carry):
        tops = _extract_top([(s_ref[4 * hp + n], None) for n in range(4)], N_KEYS, tb)
        cands = [_pair_candidates(tops[2 * n][0], tops[2 * n][1], tops[2 * n + 1][0], tops[2 * n + 1][1], tb)
                 for n in range(2)]
        picked = _extract_top(cands, cands[0][0].shape[0], tb)
        for n, (sc, _, ex) in enumerate(picked):
            pexp = jnp.exp(sc - sc[0:1])
            idx_ref[2 * hp + n] = ex.astype(I32)
            gate_ref[2 * hp + n] = pexp / jnp.sum(pexp, axis=0, keepdims=True)
        return carry

    lax.fori_loop(0, PEER_HEADS // 2, head_pair, 0)


def _topk(scores_t, tb):
    g, _, _, r = scores_t.shape
    out = pl.BlockSpec((None, PEER_HEADS, PEER_TOPK, tb), lambda gi, i: (gi, 0, 0, i))
    return pl.pallas_call(
        _topk_kernel,
        grid=(g, r // tb),
        in_specs=[pl.BlockSpec((None, 2 * PEER_HEADS, N_KEYS, tb), lambda gi, i: (gi, 0, 0, i))],
        out_specs=[out, out],
        out_shape=[jax.ShapeDtypeStruct((g, PEER_HEADS, PEER_TOPK, r), I32),
                   jax.ShapeDtypeStruct((g, PEER_HEADS, PEER_TOPK, r), F32)],
        compiler_params=_cparams("parallel", "parallel"),
        name="peer_topk",
    )(scores_t)


def _sc_mesh():
    info = plsc.get_sparse_core_info()
    mesh = plsc.VectorSubcoreMesh(core_axis_name="c", subcore_axis_name="s")
    return info.num_cores, info.num_subcores, info.num_lanes, mesh


def _sc_row_dots(table, idx, x):
    nc, ns, lanes, mesh = _sc_mesh()
    nw = nc * ns
    n_tok, picks = idx.shape
    ww = table.shape[1]
    rr = SC_GATHER_ROWS
    nq = picks // rr
    tpw = n_tok // nw
    kt = min(SC_DOT_TOKENS, tpw)
    n_outer = tpw // kt
    npan = SC_ACC_PANELS
    pw = ww // npan
    nv = pw // lanes
    ahead = SC_ACC_AHEAD
    assert tpw * nw == n_tok and n_outer * kt == tpw and nq * rr == picks and ahead < nq and nv * lanes == pw
    assert npan * lanes <= 128
    cp = pltpu.CompilerParams(needs_layout_passes=False)
    buf = pltpu.VMEM((rr, ww), table.dtype)

    @functools.partial(
        pl.kernel, mesh=mesh, out_type=jax.ShapeDtypeStruct((n_tok * picks,), F32),
        scratch_types=[pltpu.VMEM((kt * nq, rr), I32), pltpu.VMEM((kt, 2 * ww), F32), pltpu.VMEM((picks, 128), F32),
                       pltpu.VMEM((kt * picks,), F32)] + [buf] * nq + [pltpu.SemaphoreType.DMA] * nq,
        compiler_params=cp, name="peer_row_dots",
    )
    def kern(tab_hbm, idx_hbm, x_hbm, out_hbm, idx_v, x_v, part_v, hd_v, *scratch):
        wid = lax.axis_index("s") * nc + lax.axis_index("c")
        rows, sems = scratch[:nq], scratch[nq:]
        last_lane = lax.iota(I32, lanes) == lanes - 1

        def chunk_copy(t, q):
            return pltpu.make_async_copy(tab_hbm.at[idx_v.at[t * nq + q]], rows[q], sems[q])

        @pl.loop(0, n_outer)
        def _(o):
            tok0 = wid * tpw + o * kt
            pltpu.sync_copy(idx_hbm.at[pl.ds(tok0 * nq, kt * nq)], idx_v)
            pltpu.sync_copy(x_hbm.at[pl.ds(tok0, kt)], x_v)
            for q in range(ahead):
                chunk_copy(0, q).start()

            @pl.loop(0, kt)
            def _(t):
                for q in range(nq):
                    if q + ahead < nq:
                        chunk_copy(t, q + ahead).start()
                    else:
                        @pl.when(t + 1 < kt)
                        def _():
                            chunk_copy(t + 1, q + ahead - nq).start()
                    chunk_copy(t, q).wait()
                    for pan in range(npan):
                        lo_at = lambda v: pl.ds(pan * pw + v * lanes, lanes)
                        hi_at = lambda v: pl.ds(ww + pan * pw + v * lanes, lanes)
                        xlo = [x_v[t, lo_at(v)] for v in range(nv)]
                        xhi = [x_v[t, hi_at(v)] for v in range(nv)]

                        @plsc.parallel_loop(0, rr, unroll=2)
                        def _(j):
                            terms = []
                            for v in range(nv):
                                word = rows[q][j, lo_at(v)]
                                terms.append(plsc.bitcast(word << 16, F32) * xlo[v])
                                terms.append(plsc.bitcast(word & jnp.uint32(0xFFFF0000), F32) * xhi[v])
                            while len(terms) > 1:
                                terms = [terms[i] + terms[i + 1] for i in range(0, len(terms), 2)]
                            part_v[q * rr + j, pl.ds(pan * lanes, lanes)] = terms[0]

                @plsc.parallel_loop(0, picks, unroll=2)
                def _(e):
                    s = part_v[e, pl.ds(0, lanes)]
                    for pan in range(1, npan):
                        s = s + part_v[e, pl.ds(pan * lanes, lanes)]
                    slot = jnp.full((lanes,), t * picks, I32) + e
                    plsc.store_scatter(hd_v, [slot], plsc.cumsum(s), mask=last_lane)

            pltpu.sync_copy(hd_v, out_hbm.at[pl.ds(tok0 * picks, kt * picks)])

    return kern(table, idx.reshape(n_tok * nq, rr), x).reshape(n_tok, picks)


def _sc_weighted_row_sum(table, idx, wgt):
    nc, ns, lanes, mesh = _sc_mesh()
    nw = nc * ns
    n_tok, picks = idx.shape
    ww = table.shape[1]
    rr = SC_GATHER_ROWS
    nq = picks // rr
    tpw = n_tok // nw
    kt = min(SC_ACC_TOKENS, tpw)
    n_outer = tpw // kt
    pw = ww // SC_ACC_PANELS
    nv = pw // lanes
    ahead = SC_ACC_AHEAD
    assert tpw * nw == n_tok and n_outer * kt == tpw and nq * rr == picks and ahead < nq and nv * lanes == pw
    cp = pltpu.CompilerParams(needs_layout_passes=False)
    buf = pltpu.VMEM((rr, ww), table.dtype)

    @functools.partial(
        pl.kernel, mesh=mesh, out_type=jax.ShapeDtypeStruct((n_tok, 2 * ww), F32),
        scratch_types=[pltpu.VMEM((kt * nq, rr), I32), pltpu.VMEM((kt, picks), F32), pltpu.VMEM((kt, 2 * ww), F32)]
                      + [buf] * nq + [pltpu.SemaphoreType.DMA] * nq,
        compiler_params=cp, name="peer_weighted_row_sum",
    )
    def kern(tab_hbm, idx_hbm, w_hbm, out_hbm, idx_v, w_v, acc_v, *scratch):
        wid = lax.axis_index("s") * nc + lax.axis_index("c")
        rows, sems = scratch[:nq], scratch[nq:]

        def chunk_copy(t, q):
            return pltpu.make_async_copy(tab_hbm.at[idx_v.at[t * nq + q]], rows[q], sems[q])

        @pl.loop(0, n_outer)
        def _(o):
            tok0 = wid * tpw + o * kt
            pltpu.sync_copy(idx_hbm.at[pl.ds(tok0 * nq, kt * nq)], idx_v)
            pltpu.sync_copy(w_hbm.at[pl.ds(tok0, kt)], w_v)
            for q in range(ahead):
                chunk_copy(0, q).start()

            @pl.loop(0, kt)
            def _(t):
                tvec = jnp.full((lanes,), t, I32)
                for q in range(nq):
                    b = q
                    if q + ahead < nq:
                        chunk_copy(t, q + ahead).start()
                    else:
                        @pl.when(t + 1 < kt)
                        def _():
                            chunk_copy(t + 1, q + ahead - nq).start()
                    chunk_copy(t, q).wait()
                    for pan in range(SC_ACC_PANELS):
                        lo_at = lambda c: pl.ds(pan * pw + c * lanes, lanes)
                        hi_at = lambda c: pl.ds(ww + pan * pw + c * lanes, lanes)

                        def row_body(j, acc):
                            wj = plsc.load_gather(w_v, [tvec, jnp.full((lanes,), q * rr, I32) + j])
                            new = []
                            for c in range(nv):
                                word = rows[b][j, lo_at(c)]
                                new.append(acc[2 * c] + wj * plsc.bitcast(word << 16, F32))
                                new.append(acc[2 * c + 1] + wj * plsc.bitcast(word & jnp.uint32(0xFFFF0000), F32))
                            return tuple(new)

                        if q == 0:
                            init = tuple(jnp.zeros((lanes,), F32) for _ in range(2 * nv))
                        else:
                            init = tuple(acc_v[t, at(c)] for c in range(nv) for at in (lo_at, hi_at))
                        acc = lax.fori_loop(0, rr, row_body, init)
                        for c in range(nv):
                            acc_v[t, lo_at(c)] = acc[2 * c]
                            acc_v[t, hi_at(c)] = acc[2 * c + 1]

            pltpu.sync_copy(acc_v, out_hbm.at[pl.ds(tok0, kt)])

    return kern(table, idx.reshape(n_tok * nq, rr), wgt)


def _pack_bf16_pairs(t):
    half = t.shape[1] // 2
    b = lax.bitcast_convert_type(t.astype(BF16), jnp.uint16).astype(U32)
    return b[:, :half] | (b[:, half:] << 16)


def _gelu_erf(x):
    return 0.5 * x * (1.0 + lax.erf(x * (2.0 ** -0.5)))


def _peer_weights_kernel(hd_ref, gate_ref, w_ref):
    w_ref[...] = _gelu_erf(hd_ref[...]) * gate_ref[...]


def _peer_weights(hd, gate, tb):
    g, r, _ = hd.shape
    return pl.pallas_call(
        _peer_weights_kernel,
        grid=(g, r // tb),
        in_specs=[_rows(tb, PEER_PICKS), _rows(tb, PEER_PICKS)],
        out_specs=_rows(tb, PEER_PICKS),
        out_shape=jax.ShapeDtypeStruct((g, r, PEER_PICKS), F32),
        compiler_params=_cparams("parallel", "parallel"),
        name="peer_weights",
    )(hd, gate)


def _peer_out_kernel(x1_ref, ff_ref, gtf_ref, l2g_ref, l2b_ref, o_ref, *, alpha):
    o_ref[...] = _layernorm(alpha * x1_ref[...] + gtf_ref[...] * ff_ref[...], l2g_ref[...], l2b_ref[...])


def _peer_out(x1, ff, mod, p, alpha, tb):
    g, r, _ = x1.shape
    return pl.pallas_call(
        functools.partial(_peer_out_kernel, alpha=alpha),
        grid=(g, r // tb),
        in_specs=[_rows(tb, D_MODEL), _rows(tb, D_MODEL), _mod(mod, tb, 5), _const((1, D_MODEL)), _const((1, D_MODEL))],
        out_specs=_rows(tb, D_MODEL),
        out_shape=jax.ShapeDtypeStruct((g, r, D_MODEL), F32),
        compiler_params=_cparams("parallel", "parallel"),
        name="peer_out_ln2",
    )(x1, ff, mod, _row2(p["ln2_g"]), _row2(p["ln2_b"]))


def _token_stage(x, mod, prev_fn, wkv_fn, attn_fn, p, alpha, tb, sequential):
    ps, pattn, pgate = _inproj(x, mod, p["ln_in_g"], p["ln_in_b"], p["w_in_bf16"], tb)
    r, lw, k, v, kk, kka, gl = _rwkv_prep(ps, prev_fn(), p, tb, sequential)
    y, wkv_new = wkv_fn(r, lw, k, v, kk, kka)
    ya = _rwkv_post(y, r, k, v, gl, p, tb)
    ob = attn_fn(pattn)
    x1, h2, scores_t = _merge(x, ya, ob, pgate, mod, p, alpha, tb)
    idx_t, gate_t = _topk(scores_t, TOPK_TB)
    return ps, pattn, wkv_new, x1, h2, idx_t, gate_t


def kernel(x_prompt, x_sample, state_wkv, state_shift, cache_k_win, cache_v_win, c_prompt, c_sample, ln_in_g, ln_in_b, w_ada, b_ada, w_in, mu_shift, rwkv_w0, rwkv_w2, rwkv_a0, rwkv_a2, rwkv_g2, rwkv_k_k, rwkv_k_a, rwkv_r_k, rwkv_gn_g, rwkv_gn_b, attn_sinks, w_pa, w_pb, w_o, ln1_g, ln1_b, peer_wq, peer_sub_keys, peer_u, peer_v, ln2_g, ln2_b):
    depth = w_in.shape[0]
    assert depth == 1, "single-layer trunk"
    alpha = (2.0 * depth) ** 0.25
    n_p, t_p, _ = x_prompt.shape
    n_s = x_sample.shape[0]
    p = dict(ln_in_g=ln_in_g, ln_in_b=ln_in_b, w_in_bf16=w_in[0].astype(BF16), mu_shift=mu_shift[0],
             rwkv_w0=rwkv_w0[0], rwkv_w2=rwkv_w2[0], rwkv_a0=rwkv_a0[0], rwkv_a2=rwkv_a2[0], rwkv_g2=rwkv_g2[0],
             rwkv_k_k=rwkv_k_k[0], rwkv_k_a=rwkv_k_a[0], rwkv_r_k=rwkv_r_k[0], rwkv_gn_g=rwkv_gn_g[0],
             rwkv_gn_b=rwkv_gn_b[0], w_pa=w_pa[0], w_pb=w_pb[0], w_o=w_o[0], ln1_g=ln1_g[0], ln1_b=ln1_b[0],
             peer_wq=peer_wq[0], peer_sub_keys=peer_sub_keys[0], ln2_g=ln2_g[0], ln2_b=ln2_b[0])
    sinks = attn_sinks[0]

    n_c = n_p + n_s
    pad = (-n_c) % 8
    c_all = jnp.concatenate([c_prompt, c_sample, jnp.zeros((pad, D_MODEL), F32)], axis=0)
    mod_all = _modulation(c_all, w_ada[0], b_ada[0])
    mod_p = mod_all[:n_p].reshape(n_p, 1, N_MOD * D_MODEL)
    mod_s = mod_all[n_p:n_c].reshape(1, n_s, N_MOD * D_MODEL)

    seg = min(PROMPT_SEGMENT, t_p)
    assert t_p % seg == 0
    first = min(PROMPT_FIRST_SEGMENT, seg)

    def segments(b):
        cuts = list(range(0, t_p + 1, seg))
        if b == 0 and first < seg:
            cuts.insert(1, first)
        return zip(cuts[:-1], cuts[1:])

    prompt_ids = [(b, lo, hi) for b in range(n_p) for lo, hi in segments(b)]
    assert all((hi - lo) % TOKEN_TB == 0 for _, lo, hi in prompt_ids)
    carry = {}

    def prompt_group(b, lo, hi):
        def prev_fn():
            return carry[b][0] if lo > 0 else jnp.zeros((1, 1, SHIFT_W), F32)

        def wkv_fn(r, lw, k, v, kk, kka):
            s0 = carry[b][1] if lo > 0 else jnp.zeros((1, H_A, HD_A, HD_A), F32)
            return _rwkv_chunk_scan(r, lw, k, v, kk, kka, s0)

        def attn_fn(pa):
            return _attn_band(pa, carry[b][2] if lo > 0 else None, sinks)

        return (x_prompt[b:b + 1, lo:hi], mod_p[b:b + 1], prev_fn, wkv_fn, attn_fn, TOKEN_TB, True)

    tu, tv = _pack_bf16_pairs(peer_u[0]), _pack_bf16_pairs(peer_v[0])

    def select(x, mod, prev_fn, wkv_fn, attn_fn, tb, sequential):
        ps, pattn, wkv_new, x1, h2, idx_t, gate_t = _token_stage(
            x, mod, prev_fn, wkv_fn, attn_fn, p, alpha, tb, sequential)
        r = x1.shape[1]
        idx = jnp.transpose(idx_t.reshape(PEER_PICKS, r))
        gate = jnp.transpose(gate_t.reshape(PEER_PICKS, r))[None]
        hd = _sc_row_dots(tu, idx, h2[0])
        return ps, pattn, wkv_new, (hd, idx, gate, x1, mod)

    def weigh(sel):
        hd, idx, gate, x1, mod = sel
        wgt = _peer_weights(hd[None], gate, min(TOKEN_TB, idx.shape[0]))
        return _sc_weighted_row_sum(tv, idx, wgt[0]), x1, mod

    def finish(wsum):
        ff, x1, mod = wsum
        return _peer_out(x1, ff[None], mod, p, alpha, min(TOKEN_TB, x1.shape[1]))

    xs = x_sample.reshape(1, n_s, D_MODEL)

    def prev_s():
        return state_shift[0].reshape(1, n_s, SHIFT_W)

    def wkv_s(r, lw, k, v, kk, kka):
        sq = lambda z: z.reshape(n_s, D_A)
        y, s = _rwkv_step(state_wkv[0], sq(r), sq(lw), sq(k), sq(kk), sq(kka), sq(v), STEP_NB)
        return y.reshape(1, n_s, D_A), s

    def attn_s(pa):
        o = _attn_cache(pa.reshape(n_s, 1, ATTN_W), cache_k_win[0], cache_v_win[0], sinks, STEP_NB)
        return o.reshape(1, n_s, D_B)

    n_g = len(prompt_ids) + 1
    sel, wsum, y_l = [None] * n_g, [None] * n_g, [None] * n_g
    for step in range(n_g + 2):
        if step < n_g - 1:
            b, lo, hi = prompt_ids[step]
            ps, pattn, wkv_new, sel[step] = select(*prompt_group(b, lo, hi))
            carry[b] = (ps[:, -1:], wkv_new, pattn[:, -WINDOW:])
        elif step == n_g - 1:
            ps_s, pattn_s, wkv_s_new, sel[step] = select(xs, mod_s, prev_s, wkv_s, attn_s, min(TOKEN_TB, n_s), False)
        if 0 <= step - 1 < n_g:
            wsum[step - 1] = weigh(sel[step - 1])
        if 0 <= step - 2 < n_g:
            y_l[step - 2] = finish(wsum[step - 2])
    y_s = y_l[n_g - 1]
    y_p = jnp.concatenate(y_l[:n_g - 1], axis=1).reshape(n_p, t_p, D_MODEL)
    shift_p = jnp.concatenate([carry[b][0][:, 0] for b in range(n_p)], axis=0)
    pattn_p = jnp.concatenate([carry[b][2] for b in range(n_p)], axis=0)
    wkv_p_new = jnp.concatenate([carry[b][1] for b in range(n_p)], axis=0)

    kv = lambda pa, o: pa[..., o:o + H_KV * HD_B]
    ko, vo = D_B, D_B + H_KV * HD_B
    k_win_p = kv(pattn_p, ko)[:, -WINDOW:].reshape(n_p, WINDOW, H_KV, HD_B)
    v_win_p = kv(pattn_p, vo)[:, -WINDOW:].reshape(n_p, WINDOW, H_KV, HD_B)
    k_new_s = kv(pattn_s, ko).reshape(n_s, 1, H_KV, HD_B)
    v_new_s = kv(pattn_s, vo).reshape(n_s, 1, H_KV, HD_B)
    k_win_s = jnp.concatenate([cache_k_win[0], k_new_s], axis=1)[:, -WINDOW:]
    v_win_s = jnp.concatenate([cache_v_win[0], v_new_s], axis=1)[:, -WINDOW:]
    return (y_p, y_s.reshape(n_s, 1, D_MODEL), wkv_p_new[None], wkv_s_new[None],
            shift_p[None], ps_s.reshape(n_s, SHIFT_W)[None],
            k_win_p[None], k_win_s[None], v_win_p[None], v_win_s[None])
```

```python
import functools
import math

import jax
import jax.numpy as jnp
from jax import lax
from jax.experimental import pallas as pl
from jax.experimental.pallas import tpu as pltpu
from jax.experimental.pallas import tpu_sc as plsc

F32 = jnp.float32
BF16 = jnp.bfloat16
I32 = jnp.int32
U32 = jnp.uint32

D_MODEL = 1024
H_A, HD_A = 8, 64
D_A = H_A * HD_A
D_LORA_W, D_LORA_A, D_LORA_G = 64, 64, 128
GN_EPS = 64e-5
H_Q, H_KV, HD_B = 8, 2, 64
G_Q = H_Q // H_KV
D_B = H_Q * HD_B
WINDOW = 128
N_KEYS = 128
PEER_HEADS, PEER_TOPK, PEER_HALF = 8, 16, 128
PEER_PICKS = PEER_HEADS * PEER_TOPK
N_MOD = 6
LN_EPS = 1e-5
NEG_INF = -1e30
OFF_WD = 3 * D_A
OFF_AD = OFF_WD + D_LORA_W
OFF_GD = OFF_AD + D_LORA_A
SHIFT_W = OFF_GD + D_LORA_G
ATTN_W = D_B + 2 * H_KV * HD_B
GATE_W = 2 * D_MODEL
D_IN = SHIFT_W + ATTN_W + GATE_W

VMEM_LIMIT = 48 * 1024 * 1024
SUBLANES = 8
RWKV_CHUNK = 64
SC_GATHER_ROWS = 32
SC_ACC_AHEAD = 3
SC_ACC_TOKENS = 16
SC_DOT_TOKENS = 16
SC_ACC_PANELS = 4
TOKEN_TB = 256
TOPK_TB = 128
VALUE_TC_PICKS = 64
VALUE_TC_TB = 16
PROMPT_SEGMENT = 2048
PROMPT_FIRST_SEGMENT = 512
STEP_NB = 8


def _cparams(*sem):
    return pltpu.CompilerParams(dimension_semantics=sem, vmem_limit_bytes=VMEM_LIMIT)


def _layernorm(x, g, b):
    mu = jnp.mean(x, -1, keepdims=True)
    xc = x - mu
    var = jnp.mean(xc * xc, -1, keepdims=True)
    return xc * lax.rsqrt(var + LN_EPS) * g + b


def _split(x):
    hi = x.astype(BF16)
    lo = (x - hi.astype(F32)).astype(BF16)
    return hi, lo


_NN = (((1,), (0,)), ((), ()))
_NT = (((1,), (1,)), ((), ()))
_TN = (((0,), (0,)), ((), ()))


def _dot3(a, b, dims=_NN):
    ah, al = _split(a)
    bh, bl = _split(b)
    d = functools.partial(lax.dot_general, dimension_numbers=dims, preferred_element_type=F32)
    return d(ah, bh) + d(ah, bl) + d(al, bh)


def _dot_exact_lhs(a_bf16, b, dims=_NN):
    b1 = b.astype(BF16)
    r1 = b - b1.astype(F32)
    b2 = r1.astype(BF16)
    b3 = (r1 - b2.astype(F32)).astype(BF16)
    d = functools.partial(lax.dot_general, dimension_numbers=dims, preferred_element_type=F32)
    return d(a_bf16, b1) + d(a_bf16, b2) + d(a_bf16, b3)


def _dotb(a, b, dims=_NN):
    return lax.dot_general(a.astype(BF16), b.astype(BF16), dims, preferred_element_type=F32)


def _rows(tb, width, col=0):
    return pl.BlockSpec((None, tb, width), lambda g, i: (g, i, col))


def _mod(mod, tb, col):
    if mod.shape[1] == 1:
        return pl.BlockSpec((None, 1, D_MODEL), lambda g, i: (g, 0, col))
    return pl.BlockSpec((None, tb, D_MODEL), lambda g, i: (g, i, col))


def _const(shape):
    n = len(shape)
    return pl.BlockSpec(shape, lambda g, i: (0,) * n)


def _row2(p):
    return p.reshape(1, -1).astype(F32)


def _mod_kernel(c_ref, w_ref, b_ref, o_ref):
    c = c_ref[...]
    a = c * jax.nn.sigmoid(c)
    o_ref[...] = _dot3(a, w_ref[...]) + b_ref[...]


def _modulation(c, w_ada, b_ada):
    n = c.shape[0]
    tn = D_MODEL
    return pl.pallas_call(
        _mod_kernel,
        grid=(w_ada.shape[1] // tn,),
        in_specs=[pl.BlockSpec((n, D_MODEL), lambda j: (0, 0)),
                  pl.BlockSpec((D_MODEL, tn), lambda j: (0, j)),
                  pl.BlockSpec((1, tn), lambda j: (0, j))],
        out_specs=pl.BlockSpec((n, tn), lambda j: (0, j)),
        out_shape=jax.ShapeDtypeStruct((n, w_ada.shape[1]), F32),
        compiler_params=_cparams("arbitrary"),
        name="modulation",
    )(c, w_ada, b_ada.reshape(1, -1))


def _inproj_kernel(x_ref, sh_ref, sc_ref, g_ref, b_ref, w_ref, ps_ref, pa_ref, pg_ref):
    xn = _layernorm(x_ref[...], g_ref[...], b_ref[...])
    h = (xn * (1.0 + sc_ref[...]) + sh_ref[...]).astype(BF16)
    ps_ref[...] = jnp.dot(h, w_ref[:, :SHIFT_W], preferred_element_type=F32)
    pa_ref[...] = jnp.dot(h, w_ref[:, SHIFT_W:SHIFT_W + ATTN_W], preferred_element_type=F32)
    pg_ref[...] = jnp.dot(h, w_ref[:, SHIFT_W + ATTN_W:], preferred_element_type=F32)


def _inproj(x, mod, ln_g, ln_b, w_in_bf16, tb):
    g, r, _ = x.shape
    shp = lambda w: jax.ShapeDtypeStruct((g, r, w), F32)
    return pl.pallas_call(
        _inproj_kernel,
        grid=(g, r // tb),
        in_specs=[_rows(tb, D_MODEL), _mod(mod, tb, 0), _mod(mod, tb, 1),
                  _const((1, D_MODEL)), _const((1, D_MODEL)), _const((D_MODEL, D_IN))],
        out_specs=[_rows(tb, SHIFT_W), _rows(tb, ATTN_W), _rows(tb, GATE_W)],
        out_shape=[shp(SHIFT_W), shp(ATTN_W), shp(GATE_W)],
        compiler_params=_cparams("parallel", "parallel"),
        name="inproj",
    )(x, mod, mod, _row2(ln_g), _row2(ln_b), w_in_bf16)


def _softplus(x):
    return jnp.maximum(x, 0.0) + jnp.log1p(jnp.exp(-jnp.abs(x)))


def _rwkv_prep_kernel(ps_ref, prev_ref, first_ref, mu_ref, w0_ref, w2_ref, a0_ref, a2_ref, g2_ref, kk_w_ref, ka_w_ref,
                      hsum_ref, r_ref, lw_ref, k_ref, v_ref, kk_ref, kka_ref, g_ref, *, sequential):
    ps = ps_ref[...]
    if sequential:
        before = jnp.where(pl.program_id(1) == 0, first_ref[...], prev_ref[SUBLANES - 1:SUBLANES, :])
        row = lax.broadcasted_iota(I32, ps.shape, 0)
        prev = jnp.where(row == 0, before, pltpu.roll(ps, 1, 0))
    else:
        prev = prev_ref[...]
    xs = ps + (prev - ps) * mu_ref[...]
    r = xs[:, 0:D_A]
    k = xs[:, D_A:2 * D_A]
    v = xs[:, 2 * D_A:3 * D_A]
    wd = xs[:, OFF_WD:OFF_AD]
    ad = xs[:, OFF_AD:OFF_GD]
    gd = xs[:, OFF_GD:SHIFT_W]
    z = w0_ref[...] + _dot3(jnp.tanh(wd), w2_ref[...])
    w_log = -_softplus(-z) - 0.5
    a = jax.nn.sigmoid(a0_ref[...] + _dot3(ad, a2_ref[...]))
    kk = k * kk_w_ref[...]
    ss = _dot3(kk * kk, hsum_ref[...])
    kk = kk / jnp.maximum(jnp.sqrt(ss), 1e-12)
    r_ref[...] = r
    lw_ref[...] = -jnp.exp(w_log)
    k_ref[...] = k * (1.0 + (a - 1.0) * ka_w_ref[...])
    v_ref[...] = v
    kk_ref[...] = kk
    kka_ref[...] = kk * a
    g_ref[...] = _dot3(jax.nn.sigmoid(gd), g2_ref[...])


def _head_sum_matrix():
    h = jnp.arange(D_A) // HD_A
    return (h[:, None] == h[None, :]).astype(F32)


def _rwkv_prep(ps, prev, p, tb, sequential):
    g, r, _ = ps.shape
    shp = jax.ShapeDtypeStruct((g, r, D_A), F32)
    if sequential:
        per = tb // SUBLANES
        prev_args = (ps, prev)
        prev_specs = [pl.BlockSpec((None, SUBLANES, SHIFT_W), lambda gi, i: (gi, jnp.maximum(i * per - 1, 0), 0)),
                      pl.BlockSpec((None, 1, SHIFT_W), lambda gi, i: (gi, 0, 0))]
    else:
        prev_args = (prev, prev[:, :1])
        prev_specs = [_rows(tb, SHIFT_W), pl.BlockSpec((None, 1, SHIFT_W), lambda gi, i: (gi, 0, 0))]
    return pl.pallas_call(
        functools.partial(_rwkv_prep_kernel, sequential=sequential),
        grid=(g, r // tb),
        in_specs=[_rows(tb, SHIFT_W)] + prev_specs + [_const((1, SHIFT_W)),
                  _const((1, D_A)), _const((D_LORA_W, D_A)), _const((1, D_A)), _const((D_LORA_A, D_A)),
                  _const((D_LORA_G, D_A)), _const((1, D_A)), _const((1, D_A)), _const((D_A, D_A))],
        out_specs=[_rows(tb, D_A)] * 7,
        out_shape=[shp] * 7,
        compiler_params=_cparams("parallel", "parallel"),
        name="rwkv_prep",
    )(ps, *prev_args, _row2(p["mu_shift"]), _row2(p["rwkv_w0"]), p["rwkv_w2"], _row2(p["rwkv_a0"]), p["rwkv_a2"],
      p["rwkv_g2"], _row2(p["rwkv_k_k"]), _row2(p["rwkv_k_a"]), _head_sum_matrix())


def _rwkv_chunk_kernel(r_ref, lw_ref, k_ref, v_ref, kk_ref, kka_ref, s0_ref, y_ref, s_ref):
    c = RWKV_CHUNK

    @pl.when(pl.program_id(1) == 0)
    def _():
        s_ref[...] = s0_ref[...]

    row = lax.broadcasted_iota(I32, (c, c), 0)
    col = lax.broadcasted_iota(I32, (c, c), 1)
    tril = row >= col
    stril = row > col
    lw = lw_ref[...]
    cum = _dot_exact_lhs(tril.astype(BF16), lw)
    cum_end = cum[c - 1:c, :]
    g_inv = jnp.exp(-cum)
    g_end = jnp.exp(cum_end - cum)
    a_hat = -kk_ref[...] * jnp.exp(cum - lw)
    b_hat = kka_ref[...] * g_inv
    k_hat = k_ref[...] * g_inv
    r_til = r_ref[...] * jnp.exp(cum)
    b_end = kka_ref[...] * g_end
    k_end = k_ref[...] * g_end
    gam_end = jnp.exp(cum_end)
    v_all = v_ref[...]
    s_all = s_ref[...]
    n_steps = int(math.log2(c))
    heads = range(H_A)
    sl = [slice(h * HD_A, (h + 1) * HD_A) for h in heads]
    vh = [v_all[:, sl[h]] for h in heads]
    ar = [jnp.concatenate([a_hat[:, sl[h]], r_til[:, sl[h]]], axis=0) for h in heads]
    bk = [jnp.concatenate([b_hat[:, sl[h]], k_hat[:, sl[h]]], axis=0) for h in heads]
    x = [_dot3(ar[h], bk[h], _NT) for h in heads]
    ars = [_dot3(ar[h], s_all[h], _NT) for h in heads]
    a_ak = [jnp.where(stril, x[h][:c, c:], 0.0) for h in heads]
    n = [jnp.where(stril, x[h][:c, :c], 0.0) for h in heads]
    u = [ars[h][:c] + _dot3(a_ak[h], vh[h]) for h in heads]
    for it in range(n_steps):
        u = [u[h] + _dot3(n[h], u[h]) for h in heads]
        if it + 1 < n_steps:
            n = [_dot3(n[h], n[h]) for h in heads]
    uv = [jnp.concatenate([u[h], vh[h]], axis=0) for h in heads]
    a_r = [jnp.concatenate([jnp.where(tril, x[h][c:, :c], 0.0), jnp.where(tril, x[h][c:, c:], 0.0)], axis=1)
           for h in heads]
    y = [ars[h][c:] + _dot3(a_r[h], uv[h]) for h in heads]
    bke = [jnp.concatenate([b_end[:, sl[h]], k_end[:, sl[h]]], axis=0) for h in heads]
    s_new = [s_all[h] * gam_end[:, sl[h]] + _dot3(uv[h], bke[h], _TN) for h in heads]
    for h in heads:
        y_ref[:, sl[h]] = y[h]
        s_ref[h] = s_new[h]


def _rwkv_chunk_scan(r, lw, k, v, kk, kka, s0):
    n, t, _ = r.shape
    c = RWKV_CHUNK
    seq = pl.BlockSpec((None, c, D_A), lambda b, i: (b, i, 0))
    st = pl.BlockSpec((None, H_A, HD_A, HD_A), lambda b, i: (b, 0, 0, 0))
    return pl.pallas_call(
        _rwkv_chunk_kernel,
        grid=(n, t // c),
        in_specs=[seq] * 6 + [st],
        out_specs=[seq, st],
        out_shape=[jax.ShapeDtypeStruct((n, t, D_A), F32), jax.ShapeDtypeStruct((n, H_A, HD_A, HD_A), F32)],
        compiler_params=_cparams("parallel", "arbitrary"),
        name="rwkv_chunk_scan",
    )(r, lw, k, v, kk, kka, s0)


def _rwkv_step_kernel(s_ref, r_ref, lw_ref, k_ref, kk_ref, kka_ref, v_ref, y_ref, so_ref):
    s = s_ref[...]
    sa = jnp.sum(s * (-kk_ref[...]), axis=-1, keepdims=True)
    s = s * jnp.exp(lw_ref[...]) + sa * kka_ref[...] + v_ref[...] * k_ref[...]
    so_ref[...] = s
    y_ref[...] = jnp.sum(s * r_ref[...], axis=-1, keepdims=True)


def _rwkv_step(s0, r, lw, k, kk, kka, v, nb):
    n = s0.shape[0]
    key = lambda z: z.reshape(n, H_A, 1, HD_A)
    st = pl.BlockSpec((nb, H_A, HD_A, HD_A), lambda i: (i, 0, 0, 0))
    ks = pl.BlockSpec((nb, H_A, 1, HD_A), lambda i: (i, 0, 0, 0))
    vs = pl.BlockSpec((nb, H_A, HD_A, 1), lambda i: (i, 0, 0, 0))
    y, s = pl.pallas_call(
        _rwkv_step_kernel,
        grid=(n // nb,),
        in_specs=[st, ks, ks, ks, ks, ks, vs],
        out_specs=[vs, st],
        out_shape=[jax.ShapeDtypeStruct((n, H_A, HD_A, 1), F32), jax.ShapeDtypeStruct(s0.shape, F32)],
        compiler_params=_cparams("parallel"),
        name="rwkv_step",
    )(s0, key(r), key(lw), key(k), key(kk), key(kka), v.reshape(n, H_A, HD_A, 1))
    return y.reshape(n, D_A), s


def _rwkv_post_kernel(y_ref, r_ref, k_ref, v_ref, g_ref, gn_g_ref, gn_b_ref, rk_ref, hsum_ref, o_ref):
    y = y_ref[...]
    hs = hsum_ref[...]
    mu = _dot3(y, hs) * (1.0 / HD_A)
    yc = y - mu
    var = _dot3(yc * yc, hs) * (1.0 / HD_A)
    yn = yc * lax.rsqrt(var + GN_EPS) * gn_g_ref[...] + gn_b_ref[...]
    bonus = _dot3(r_ref[...] * k_ref[...] * rk_ref[...], hs) * v_ref[...]
    o_ref[...] = (yn + bonus) * g_ref[...]


def _rwkv_post(y, r, k, v, g, p, tb):
    gg, rr, _ = y.shape
    return pl.pallas_call(
        _rwkv_post_kernel,
        grid=(gg, rr // tb),
        in_specs=[_rows(tb, D_A)] * 5 + [_const((1, D_A))] * 3 + [_const((D_A, D_A))],
        out_specs=_rows(tb, D_A),
        out_shape=jax.ShapeDtypeStruct((gg, rr, D_A), F32),
        compiler_params=_cparams("parallel", "parallel"),
        name="rwkv_post",
    )(y, r, k, v, g, _row2(p["rwkv_gn_g"]), _row2(p["rwkv_gn_b"]), _row2(p["rwkv_r_k"]), _head_sum_matrix())


def _sink_softmax(s, sink):
    m = jnp.maximum(jnp.max(s, axis=-1, keepdims=True), sink)
    p = jnp.exp(s - m)
    den = jnp.sum(p, axis=-1, keepdims=True) + jnp.exp(sink - m)
    return p / den


def _attn_band_kernel(cur_ref, prev_ref, carry_ref, sink_ref, o_ref, *, has_carry):
    blk = WINDOW
    i = pl.program_id(1)
    cur = cur_ref[...]
    prev = jnp.where(i == 0, carry_ref[...], prev_ref[...])
    qi = lax.broadcasted_iota(I32, (G_Q * blk, 2 * blk), 0) % blk
    kj = lax.broadcasted_iota(I32, (G_Q * blk, 2 * blk), 1)
    rel = blk + qi - kj
    valid = (rel >= 0) & (rel <= WINDOW)
    if not has_carry:
        valid = valid & ((kj >= blk) | (i > 0))
    relf = rel.astype(F32)
    gidx = lax.broadcasted_iota(I32, (G_Q * blk, 1), 0) // blk
    for kvh in range(H_KV):
        q4 = jnp.concatenate([cur[:, (kvh * G_Q + g) * HD_B:(kvh * G_Q + g + 1) * HD_B] for g in range(G_Q)], axis=0)
        ko = D_B + kvh * HD_B
        vo = D_B + H_KV * HD_B + kvh * HD_B
        kmat = jnp.concatenate([prev[:, ko:ko + HD_B], cur[:, ko:ko + HD_B]], axis=0)
        vmat = jnp.concatenate([prev[:, vo:vo + HD_B], cur[:, vo:vo + HD_B]], axis=0)
        slope = jnp.zeros((G_Q * blk, 1), F32)
        sink = jnp.zeros((G_Q * blk, 1), F32)
        for g in range(G_Q):
            hq = kvh * G_Q + g
            slope = jnp.where(gidx == g, 2.0 ** (-8.0 * (hq + 1) / H_Q), slope)
            sink = jnp.where(gidx == g, sink_ref[hq], sink)
        s = _dotb(q4, kmat, _NT) * (HD_B ** -0.5)
        s = jnp.where(valid, s - slope * relf, NEG_INF)
        p = _sink_softmax(s, sink)
        o = _dotb(p, vmat)
        for g in range(G_Q):
            hq = kvh * G_Q + g
            o_ref[:, hq * HD_B:(hq + 1) * HD_B] = o[g * blk:(g + 1) * blk]


def _attn_band(pattn, carry, sinks):
    n, t, _ = pattn.shape
    blk = WINDOW
    has_carry = carry is not None
    if not has_carry:
        carry = jnp.zeros((n, blk, ATTN_W), F32)
    return pl.pallas_call(
        functools.partial(_attn_band_kernel, has_carry=has_carry),
        grid=(n, t // blk),
        in_specs=[pl.BlockSpec((None, blk, ATTN_W), lambda b, i: (b, i, 0)),
                  pl.BlockSpec((None, blk, ATTN_W), lambda b, i: (b, jnp.maximum(i - 1, 0), 0)),
                  pl.BlockSpec((None, blk, ATTN_W), lambda b, i: (b, 0, 0)),
                  pl.BlockSpec(memory_space=pltpu.SMEM)],
        out_specs=pl.BlockSpec((None, blk, D_B), lambda b, i: (b, i, 0)),
        out_shape=jax.ShapeDtypeStruct((n, t, D_B), F32),
        compiler_params=_cparams("parallel", "parallel"),
        name="attn_band",
    )(pattn, pattn, carry, sinks.astype(F32))


def _attn_cache_kernel(cur_ref, kc_ref, vc_ref, sink_ref, o_ref, *, nb):
    relc = (WINDOW - lax.broadcasted_iota(I32, (G_Q, WINDOW), 1)).astype(F32)
    gidx = lax.broadcasted_iota(I32, (G_Q, 1), 0)
    for b in range(nb):
        cur = cur_ref[b]
        for kvh in range(H_KV):
            q4 = jnp.concatenate([cur[:, (kvh * G_Q + g) * HD_B:(kvh * G_Q + g + 1) * HD_B] for g in range(G_Q)], axis=0)
            ko = D_B + kvh * HD_B
            vo = D_B + H_KV * HD_B + kvh * HD_B
            k_new = cur[:, ko:ko + HD_B]
            v_new = cur[:, vo:vo + HD_B]
            kc = kc_ref[b, :, kvh * HD_B:(kvh + 1) * HD_B]
            vc = vc_ref[b, :, kvh * HD_B:(kvh + 1) * HD_B]
            slope = jnp.zeros((G_Q, 1), F32)
            sink = jnp.zeros((G_Q, 1), F32)
            for g in range(G_Q):
                hq = kvh * G_Q + g
                slope = jnp.where(gidx == g, 2.0 ** (-8.0 * (hq + 1) / H_Q), slope)
                sink = jnp.where(gidx == g, sink_ref[hq], sink)
            scale = HD_B ** -0.5
            sc = _dotb(q4, kc, _NT) * scale - slope * relc
            sn = jnp.sum(q4.astype(BF16).astype(F32) * k_new.astype(BF16).astype(F32), axis=-1, keepdims=True) * scale
            m = jnp.maximum(jnp.maximum(jnp.max(sc, axis=-1, keepdims=True), sn), sink)
            pc = jnp.exp(sc - m)
            pn = jnp.exp(sn - m)
            den = jnp.sum(pc, axis=-1, keepdims=True) + pn + jnp.exp(sink - m)
            o = (_dotb(pc / den, vc) + (pn / den).astype(BF16).astype(F32) * v_new.astype(BF16).astype(F32))
            for g in range(G_Q):
                hq = kvh * G_Q + g
                o_ref[b, :, hq * HD_B:(hq + 1) * HD_B] = o[g:g + 1]


def _attn_cache(pattn, k_buf, v_buf, sinks, nb):
    n = pattn.shape[0]
    kc = k_buf.reshape(n, WINDOW, H_KV * HD_B)
    vc = v_buf.reshape(n, WINDOW, H_KV * HD_B)
    return pl.pallas_call(
        functools.partial(_attn_cache_kernel, nb=nb),
        grid=(n // nb,),
        in_specs=[pl.BlockSpec((nb, 1, ATTN_W), lambda i: (i, 0, 0)),
                  pl.BlockSpec((nb, WINDOW, H_KV * HD_B), lambda i: (i, 0, 0)),
                  pl.BlockSpec((nb, WINDOW, H_KV * HD_B), lambda i: (i, 0, 0)),
                  pl.BlockSpec(memory_space=pltpu.SMEM)],
        out_specs=pl.BlockSpec((nb, 1, D_B), lambda i: (i, 0, 0)),
        out_shape=jax.ShapeDtypeStruct((n, 1, D_B), F32),
        compiler_params=_cparams("parallel"),
        name="attn_cache",
    )(pattn, kc, vc, sinks.astype(F32))


def _merge_kernel(x_ref, ya_ref, ob_ref, pg_ref, gtm_ref, shf_ref, scf_ref, lng_ref, lnb_ref, l1g_ref, l1b_ref,
                  wpa_ref, wpb_ref, wo_ref, wq_ref, sk_ref, x1_ref, h2_ref, st_ref, *, alpha):
    ya = jnp.dot(ya_ref[...].astype(BF16), wpa_ref[...], preferred_element_type=F32)
    yb = jnp.dot(ob_ref[...].astype(BF16), wpb_ref[...], preferred_element_type=F32)
    pg = pg_ref[...]
    merged = jax.nn.sigmoid(pg[:, :D_MODEL]) * ya + jax.nn.sigmoid(pg[:, D_MODEL:]) * yb
    mix = jnp.dot(merged.astype(BF16), wo_ref[...], preferred_element_type=F32)
    xn = _layernorm(x_ref[...], lng_ref[...], lnb_ref[...])
    x1 = _layernorm(alpha * xn + gtm_ref[...] * mix, l1g_ref[...], l1b_ref[...])
    x1_ref[...] = x1
    h2 = x1 * (1.0 + scf_ref[...]) + shf_ref[...]
    h2_ref[...] = h2
    q = jnp.dot(h2.astype(BF16), wq_ref[...], preferred_element_type=F32)
    for hc in range(2 * PEER_HEADS):
        st_ref[hc] = _dot3(sk_ref[hc % 2], q[:, hc * PEER_HALF:(hc + 1) * PEER_HALF], _NT)


def _merge(x, ya, ob, pg, mod, p, alpha, tb):
    g, r, _ = x.shape
    return pl.pallas_call(
        functools.partial(_merge_kernel, alpha=alpha),
        grid=(g, r // tb),
        in_specs=[_rows(tb, D_MODEL), _rows(tb, D_A), _rows(tb, D_B), _rows(tb, GATE_W),
                  _mod(mod, tb, 2), _mod(mod, tb, 3), _mod(mod, tb, 4)]
                 + [_const((1, D_MODEL))] * 4
                 + [_const((D_A, D_MODEL)), _const((D_B, D_MODEL)), _const((D_MODEL, D_MODEL)),
                    _const((D_MODEL, 2 * PEER_HEADS * PEER_HALF)), _const((2, N_KEYS, PEER_HALF))],
        out_specs=[_rows(tb, D_MODEL), _rows(tb, D_MODEL),
                   pl.BlockSpec((None, 2 * PEER_HEADS, N_KEYS, tb), lambda gi, i: (gi, 0, 0, i))],
        out_shape=[jax.ShapeDtypeStruct((g, r, D_MODEL), F32), jax.ShapeDtypeStruct((g, r, D_MODEL), F32),
                   jax.ShapeDtypeStruct((g, 2 * PEER_HEADS, N_KEYS, r), F32)],
        compiler_params=_cparams("parallel", "parallel"),
        name="merge_ln1_peer_scores",
    )(x, ya, ob, pg, mod, mod, mod, _row2(p["ln_in_g"]), _row2(p["ln_in_b"]), _row2(p["ln1_g"]), _row2(p["ln1_b"]),
      p["w_pa"].astype(BF16), p["w_pb"].astype(BF16), p["w_o"].astype(BF16), p["peer_wq"].astype(BF16),
      p["peer_sub_keys"])


def _extract_top(problems, n_rows, tb):
    rio = lax.broadcasted_iota(I32, (n_rows, tb), 0).astype(F32)
    vals = [v for v, _ in problems]
    tops = [([], [], []) for _ in problems]
    for _ in range(PEER_TOPK):
        for n, (_, payload) in enumerate(problems):
            m = jnp.max(vals[n], axis=0, keepdims=True)
            i = jnp.min(jnp.where(vals[n] == m, rio, float(n_rows)), axis=0, keepdims=True)
            sel = rio == i
            tops[n][0].append(m)
            tops[n][1].append(i)
            if payload is not None:
                tops[n][2].append(jnp.max(jnp.where(sel, payload, -1.0), axis=0, keepdims=True))
            vals[n] = jnp.where(sel, -jnp.inf, vals[n])
    cat = lambda z: jnp.concatenate(z, axis=0) if z else None
    return [(cat(v), cat(i), cat(pl_)) for v, i, pl_ in tops]


def _pair_candidates(v1, i1, v2, i2, tb):
    k = PEER_TOPK
    sub = 8
    eid = lambda a0, a1, b0, b1: i1[a0:a1] * float(N_KEYS) + i2[b0:b1]
    vals = [v1[0:1] + v2, v1[1:2] + v2[0:sub]]
    ids = [eid(0, 1, 0, k), eid(1, 2, 0, sub)]
    brow = lax.broadcasted_iota(I32, (sub, tb), 0)
    for a in range(2, sub):
        vals.append(jnp.where(brow < k // (a + 1), v1[a:a + 1] + v2[0:sub], -jnp.inf))
        ids.append(eid(a, a + 1, 0, sub))
    vals.append(v1[sub:k] + v2[0:1])
    ids.append(eid(sub, k, 0, 1))
    return jnp.concatenate(vals, axis=0), jnp.concatenate(ids, axis=0)


def _topk_kernel(s_ref, idx_ref, gate_ref):
    tb = s_ref.shape[-1]

    def head_pair(hp, carry):
        tops = _extract_top([(s_ref[4 * hp + n], None) for n in range(4)], N_KEYS, tb)
        cands = [_pair_candidates(tops[2 * n][0], tops[2 * n][1], tops[2 * n + 1][0], tops[2 * n + 1][1], tb)
                 for n in range(2)]
        picked = _extract_top(cands, cands[0][0].shape[0], tb)
        for n, (sc, _, ex) in enumerate(picked):
            pexp = jnp.exp(sc - sc[0:1])
            idx_ref[2 * hp + n] = ex.astype(I32)
            gate_ref[2 * hp + n] = pexp / jnp.sum(pexp, axis=0, keepdims=True)
        return carry

    lax.fori_loop(0, PEER_HEADS // 2, head_pair, 0)


def _topk(scores_t, tb):
    g, _, _, r = scores_t.shape
    out = pl.BlockSpec((None, PEER_HEADS, PEER_TOPK, tb), lambda gi, i: (gi, 0, 0, i))
    return pl.pallas_call(
        _topk_kernel,
        grid=(g, r // tb),
        in_specs=[pl.BlockSpec((None, 2 * PEER_HEADS, N_KEYS, tb), lambda gi, i: (gi, 0, 0, i))],
        out_specs=[out, out],
        out_shape=[jax.ShapeDtypeStruct((g, PEER_HEADS, PEER_TOPK, r), I32),
                   jax.ShapeDtypeStruct((g, PEER_HEADS, PEER_TOPK, r), F32)],
        compiler_params=_cparams("parallel", "parallel"),
        name="peer_topk",
    )(scores_t)


def _sc_mesh():
    info = plsc.get_sparse_core_info()
    mesh = plsc.VectorSubcoreMesh(core_axis_name="c", subcore_axis_name="s")
    return info.num_cores, info.num_subcores, info.num_lanes, mesh


def _sc_row_dots(table, idx, x):
    nc, ns, lanes, mesh = _sc_mesh()
    nw = nc * ns
    n_tok, picks = idx.shape
    ww = table.shape[1]
    rr = SC_GATHER_ROWS
    nq = picks // rr
    tpw = n_tok // nw
    kt = min(SC_DOT_TOKENS, tpw)
    n_outer = tpw // kt
    npan = SC_ACC_PANELS
    pw = ww // npan
    nv = pw // lanes
    ahead = SC_ACC_AHEAD
    assert tpw * nw == n_tok and n_outer * kt == tpw and nq * rr == picks and ahead < nq and nv * lanes == pw
    assert npan * lanes <= 128
    cp = pltpu.CompilerParams(needs_layout_passes=False)
    buf = pltpu.VMEM((rr, ww), table.dtype)

    @functools.partial(
        pl.kernel, mesh=mesh, out_type=jax.ShapeDtypeStruct((n_tok * picks,), F32),
        scratch_types=[pltpu.VMEM((kt * nq, rr), I32), pltpu.VMEM((kt, 2 * ww), F32), pltpu.VMEM((picks, 128), F32),
                       pltpu.VMEM((kt * picks,), F32)] + [buf] * nq + [pltpu.SemaphoreType.DMA] * nq,
        compiler_params=cp, name="peer_row_dots",
    )
    def kern(tab_hbm, idx_hbm, x_hbm, out_hbm, idx_v, x_v, part_v, hd_v, *scratch):
        wid = lax.axis_index("s") * nc + lax.axis_index("c")
        rows, sems = scratch[:nq], scratch[nq:]
        last_lane = lax.iota(I32, lanes) == lanes - 1

        def chunk_copy(t, q):
            return pltpu.make_async_copy(tab_hbm.at[idx_v.at[t * nq + q]], rows[q], sems[q])

        @pl.loop(0, n_outer)
        def _(o):
            tok0 = wid * tpw + o * kt
            pltpu.sync_copy(idx_hbm.at[pl.ds(tok0 * nq, kt * nq)], idx_v)
            pltpu.sync_copy(x_hbm.at[pl.ds(tok0, kt)], x_v)
            for q in range(ahead):
                chunk_copy(0, q).start()

            @pl.loop(0, kt)
            def _(t):
                for q in range(nq):
                    if q + ahead < nq:
                        chunk_copy(t, q + ahead).start()
                    else:
                        @pl.when(t + 1 < kt)
                        def _():
                            chunk_copy(t + 1, q + ahead - nq).start()
                    chunk_copy(t, q).wait()
                    for pan in range(npan):
                        lo_at = lambda v: pl.ds(pan * pw + v * lanes, lanes)
                        hi_at = lambda v: pl.ds(ww + pan * pw + v * lanes, lanes)
                        xlo = [x_v[t, lo_at(v)] for v in range(nv)]
                        xhi = [x_v[t, hi_at(v)] for v in range(nv)]

                        @plsc.parallel_loop(0, rr, unroll=2)
                        def _(j):
                            terms = []
                            for v in range(nv):
                                word = rows[q][j, lo_at(v)]
                                terms.append(plsc.bitcast(word << 16, F32) * xlo[v])
                                terms.append(plsc.bitcast(word & jnp.uint32(0xFFFF0000), F32) * xhi[v])
                            while len(terms) > 1:
                                terms = [terms[i] + terms[i + 1] for i in range(0, len(terms), 2)]
                            part_v[q * rr + j, pl.ds(pan * lanes, lanes)] = terms[0]

                @plsc.parallel_loop(0, picks, unroll=2)
                def _(e):
                    s = part_v[e, pl.ds(0, lanes)]
                    for pan in range(1, npan):
                        s = s + part_v[e, pl.ds(pan * lanes, lanes)]
                    slot = jnp.full((lanes,), t * picks, I32) + e
                    plsc.store_scatter(hd_v, [slot], plsc.cumsum(s), mask=last_lane)

            pltpu.sync_copy(hd_v, out_hbm.at[pl.ds(tok0 * picks, kt * picks)])

    return kern(table, idx.reshape(n_tok * nq, rr), x).reshape(n_tok, picks)


def _sc_weighted_row_sum(table, idx, wgt, pick0):
    nc, ns, lanes, mesh = _sc_mesh()
    nw = nc * ns
    n_tok, picks = idx.shape
    ww = table.shape[1]
    rr = SC_GATHER_ROWS
    nq_all = picks // rr
    q0 = pick0 // rr
    nq = nq_all - q0
    tpw = n_tok // nw
    kt = min(SC_ACC_TOKENS, tpw)
    n_outer = tpw // kt
    pw = ww // SC_ACC_PANELS
    nv = pw // lanes
    ahead = min(SC_ACC_AHEAD, nq - 1)
    assert tpw * nw == n_tok and n_outer * kt == tpw and nq_all * rr == picks and q0 * rr == pick0 and nv * lanes == pw
    assert ahead >= 1
    cp = pltpu.CompilerParams(needs_layout_passes=False)
    buf = pltpu.VMEM((rr, ww), table.dtype)

    @functools.partial(
        pl.kernel, mesh=mesh, out_type=jax.ShapeDtypeStruct((n_tok, 2 * ww), F32),
        scratch_types=[pltpu.VMEM((kt * nq_all, rr), I32), pltpu.VMEM((kt, picks), F32), pltpu.VMEM((kt, 2 * ww), F32)]
                      + [buf] * nq + [pltpu.SemaphoreType.DMA] * nq,
        compiler_params=cp, name="peer_weighted_row_sum",
    )
    def kern(tab_hbm, idx_hbm, w_hbm, out_hbm, idx_v, w_v, acc_v, *scratch):
        wid = lax.axis_index("s") * nc + lax.axis_index("c")
        rows, sems = scratch[:nq], scratch[nq:]

        def chunk_copy(t, q):
            return pltpu.make_async_copy(tab_hbm.at[idx_v.at[t * nq_all + q0 + q]], rows[q], sems[q])

        @pl.loop(0, n_outer)
        def _(o):
            tok0 = wid * tpw + o * kt
            pltpu.sync_copy(idx_hbm.at[pl.ds(tok0 * nq_all, kt * nq_all)], idx_v)
            pltpu.sync_copy(w_hbm.at[pl.ds(tok0, kt)], w_v)
            for q in range(ahead):
                chunk_copy(0, q).start()

            @pl.loop(0, kt)
            def _(t):
                tvec = jnp.full((lanes,), t, I32)
                for q in range(nq):
                    b = q
                    if q + ahead < nq:
                        chunk_copy(t, q + ahead).start()
                    else:
                        @pl.when(t + 1 < kt)
                        def _():
                            chunk_copy(t + 1, q + ahead - nq).start()
                    chunk_copy(t, q).wait()
                    for pan in range(SC_ACC_PANELS):
                        lo_at = lambda c: pl.ds(pan * pw + c * lanes, lanes)
                        hi_at = lambda c: pl.ds(ww + pan * pw + c * lanes, lanes)

                        def row_body(j, acc):
                            wj = plsc.load_gather(w_v, [tvec, jnp.full((lanes,), pick0 + q * rr, I32) + j])
                            new = []
                            for c in range(nv):
                                word = rows[b][j, lo_at(c)]
                                new.append(acc[2 * c] + wj * plsc.bitcast(word << 16, F32))
                                new.append(acc[2 * c + 1] + wj * plsc.bitcast(word & jnp.uint32(0xFFFF0000), F32))
                            return tuple(new)

                        if q == 0:
                            init = tuple(jnp.zeros((lanes,), F32) for _ in range(2 * nv))
                        else:
                            init = tuple(acc_v[t, at(c)] for c in range(nv) for at in (lo_at, hi_at))
                        acc = lax.fori_loop(0, rr, row_body, init)
                        for c in range(nv):
                            acc_v[t, lo_at(c)] = acc[2 * c]
                            acc_v[t, hi_at(c)] = acc[2 * c + 1]

            pltpu.sync_copy(acc_v, out_hbm.at[pl.ds(tok0, kt)])

    return kern(table, idx.reshape(n_tok * nq_all, rr), wgt)


def _pack_bf16_pairs(t):
    half = t.shape[1] // 2
    b = lax.bitcast_convert_type(t.astype(BF16), jnp.uint16).astype(U32)
    return b[:, :half] | (b[:, half:] << 16)


def _gelu_erf(x):
    return 0.5 * x * (1.0 + lax.erf(x * (2.0 ** -0.5)))


def _peer_weights_kernel(hd_ref, gate_ref, w_ref):
    w_ref[...] = _gelu_erf(hd_ref[...]) * gate_ref[...]


def _peer_weights(hd, gate, tb):
    g, r, _ = hd.shape
    return pl.pallas_call(
        _peer_weights_kernel,
        grid=(g, r // tb),
        in_specs=[_rows(tb, PEER_PICKS), _rows(tb, PEER_PICKS)],
        out_specs=_rows(tb, PEER_PICKS),
        out_shape=jax.ShapeDtypeStruct((g, r, PEER_PICKS), F32),
        compiler_params=_cparams("parallel", "parallel"),
        name="peer_weights",
    )(hd, gate)


def _peer_value_tc_kernel(idx_ref, w_ref, tab_hbm, o_ref, tab_v, sem, *, tb, picks):
    @pl.when(pl.program_id(0) == 0)
    def _():
        load = pltpu.make_async_copy(tab_hbm, tab_v, sem)
        load.start()
        load.wait()

    slab = tab_v.shape[1:]
    for t in range(tb):
        lo = jnp.zeros(slab, F32)
        hi = jnp.zeros(slab, F32)
        for e in range(picks):
            word = tab_v[idx_ref[t, e]]
            wt = w_ref[t, e]
            lo = lo + wt * pltpu.bitcast(word << 16, F32)
            hi = hi + wt * pltpu.bitcast(word & jnp.uint32(0xFFFF0000), F32)
        o_ref[t, :, pl.ds(0, 128)] = lo
        o_ref[t, :, pl.ds(128, 128)] = hi


def _peer_value_tc(table3, idx, wgt, tb):
    r, picks = idx.shape
    v, s, _ = table3.shape
    part = pl.pallas_call(
        functools.partial(_peer_value_tc_kernel, tb=tb, picks=picks),
        grid=(r // tb,),
        in_specs=[pl.BlockSpec((tb, picks), lambda i: (i, 0), memory_space=pltpu.SMEM),
                  pl.BlockSpec((tb, picks), lambda i: (i, 0), memory_space=pltpu.SMEM),
                  pl.BlockSpec(memory_space=pl.ANY)],
        out_specs=pl.BlockSpec((tb, s, 256), lambda i: (i, 0, 0)),
        out_shape=jax.ShapeDtypeStruct((r, s, 256), F32),
        scratch_shapes=[pltpu.VMEM((v, s, 128), table3.dtype), pltpu.SemaphoreType.DMA],
        compiler_params=_cparams("arbitrary"),
        name="peer_value_tc",
    )(idx, wgt, table3)
    return jnp.concatenate([part[:, :, :128].reshape(r, s * 128), part[:, :, 128:].reshape(r, s * 128)], axis=1)


def _peer_out_kernel(x1_ref, ff_a_ref, ff_b_ref, gtf_ref, l2g_ref, l2b_ref, o_ref, *, alpha):
    ff = ff_a_ref[...] + ff_b_ref[...]
    o_ref[...] = _layernorm(alpha * x1_ref[...] + gtf_ref[...] * ff, l2g_ref[...], l2b_ref[...])


def _peer_out(x1, ff_a, ff_b, mod, p, alpha, tb):
    g, r, _ = x1.shape
    return pl.pallas_call(
        functools.partial(_peer_out_kernel, alpha=alpha),
        grid=(g, r // tb),
        in_specs=[_rows(tb, D_MODEL)] * 3 + [_mod(mod, tb, 5), _const((1, D_MODEL)), _const((1, D_MODEL))],
        out_specs=_rows(tb, D_MODEL),
        out_shape=jax.ShapeDtypeStruct((g, r, D_MODEL), F32),
        compiler_params=_cparams("parallel", "parallel"),
        name="peer_out_ln2",
    )(x1, ff_a, ff_b, mod, _row2(p["ln2_g"]), _row2(p["ln2_b"]))


def _token_stage(x, mod, prev_fn, wkv_fn, attn_fn, p, alpha, tb, sequential):
    ps, pattn, pgate = _inproj(x, mod, p["ln_in_g"], p["ln_in_b"], p["w_in_bf16"], tb)
    r, lw, k, v, kk, kka, gl = _rwkv_prep(ps, prev_fn(), p, tb, sequential)
    y, wkv_new = wkv_fn(r, lw, k, v, kk, kka)
    ya = _rwkv_post(y, r, k, v, gl, p, tb)
    ob = attn_fn(pattn)
    x1, h2, scores_t = _merge(x, ya, ob, pgate, mod, p, alpha, tb)
    idx_t, gate_t = _topk(scores_t, TOPK_TB)
    return ps, pattn, wkv_new, x1, h2, idx_t, gate_t


def kernel(x_prompt, x_sample, state_wkv, state_shift, cache_k_win, cache_v_win, c_prompt, c_sample, ln_in_g, ln_in_b, w_ada, b_ada, w_in, mu_shift, rwkv_w0, rwkv_w2, rwkv_a0, rwkv_a2, rwkv_g2, rwkv_k_k, rwkv_k_a, rwkv_r_k, rwkv_gn_g, rwkv_gn_b, attn_sinks, w_pa, w_pb, w_o, ln1_g, ln1_b, peer_wq, peer_sub_keys, peer_u, peer_v, ln2_g, ln2_b):
    depth = w_in.shape[0]
    assert depth == 1, "single-layer trunk"
    alpha = (2.0 * depth) ** 0.25
    n_p, t_p, _ = x_prompt.shape
    n_s = x_sample.shape[0]
    p = dict(ln_in_g=ln_in_g, ln_in_b=ln_in_b, w_in_bf16=w_in[0].astype(BF16), mu_shift=mu_shift[0],
             rwkv_w0=rwkv_w0[0], rwkv_w2=rwkv_w2[0], rwkv_a0=rwkv_a0[0], rwkv_a2=rwkv_a2[0], rwkv_g2=rwkv_g2[0],
             rwkv_k_k=rwkv_k_k[0], rwkv_k_a=rwkv_k_a[0], rwkv_r_k=rwkv_r_k[0], rwkv_gn_g=rwkv_gn_g[0],
             rwkv_gn_b=rwkv_gn_b[0], w_pa=w_pa[0], w_pb=w_pb[0], w_o=w_o[0], ln1_g=ln1_g[0], ln1_b=ln1_b[0],
             peer_wq=peer_wq[0], peer_sub_keys=peer_sub_keys[0], ln2_g=ln2_g[0], ln2_b=ln2_b[0])
    sinks = attn_sinks[0]

    n_c = n_p + n_s
    pad = (-n_c) % 8
    c_all = jnp.concatenate([c_prompt, c_sample, jnp.zeros((pad, D_MODEL), F32)], axis=0)
    mod_all = _modulation(c_all, w_ada[0], b_ada[0])
    mod_p = mod_all[:n_p].reshape(n_p, 1, N_MOD * D_MODEL)
    mod_s = mod_all[n_p:n_c].reshape(1, n_s, N_MOD * D_MODEL)

    seg = min(PROMPT_SEGMENT, t_p)
    assert t_p % seg == 0
    first = min(PROMPT_FIRST_SEGMENT, seg)

    def segments(b):
        cuts = list(range(0, t_p + 1, seg))
        if b == 0 and first < seg:
            cuts.insert(1, first)
        return zip(cuts[:-1], cuts[1:])

    prompt_ids = [(b, lo, hi) for b in range(n_p) for lo, hi in segments(b)]
    assert all((hi - lo) % TOKEN_TB == 0 for _, lo, hi in prompt_ids)
    carry = {}

    def prompt_group(b, lo, hi):
        def prev_fn():
            return carry[b][0] if lo > 0 else jnp.zeros((1, 1, SHIFT_W), F32)

        def wkv_fn(r, lw, k, v, kk, kka):
            s0 = carry[b][1] if lo > 0 else jnp.zeros((1, H_A, HD_A, HD_A), F32)
            return _rwkv_chunk_scan(r, lw, k, v, kk, kka, s0)

        def attn_fn(pa):
            return _attn_band(pa, carry[b][2] if lo > 0 else None, sinks)

        return (x_prompt[b:b + 1, lo:hi], mod_p[b:b + 1], prev_fn, wkv_fn, attn_fn, TOKEN_TB, True)

    tu, tv = _pack_bf16_pairs(peer_u[0]), _pack_bf16_pairs(peer_v[0])

    def select(x, mod, prev_fn, wkv_fn, attn_fn, tb, sequential):
        ps, pattn, wkv_new, x1, h2, idx_t, gate_t = _token_stage(
            x, mod, prev_fn, wkv_fn, attn_fn, p, alpha, tb, sequential)
        r = x1.shape[1]
        idx = jnp.transpose(idx_t.reshape(PEER_PICKS, r))
        gate = jnp.transpose(gate_t.reshape(PEER_PICKS, r))[None]
        hd = _sc_row_dots(tu, idx, h2[0])
        return ps, pattn, wkv_new, (hd, idx, gate, x1, mod)

    tv_slabs = tv.reshape(tv.shape[0], -1, 128)

    def weigh(sel):
        hd, idx, gate, x1, mod = sel
        wgt = _peer_weights(hd[None], gate, min(TOKEN_TB, idx.shape[0]))[0]
        x = VALUE_TC_PICKS
        ff_sc = _sc_weighted_row_sum(tv, idx, wgt, x)
        ff_tc = _peer_value_tc(tv_slabs, idx[:, :x], wgt[:, :x], min(VALUE_TC_TB, idx.shape[0]))
        return ff_sc, ff_tc, x1, mod

    def finish(wsum):
        ff_sc, ff_tc, x1, mod = wsum
        return _peer_out(x1, ff_sc[None], ff_tc[None], mod, p, alpha, min(TOKEN_TB, x1.shape[1]))

    xs = x_sample.reshape(1, n_s, D_MODEL)

    def prev_s():
        return state_shift[0].reshape(1, n_s, SHIFT_W)

    def wkv_s(r, lw, k, v, kk, kka):
        sq = lambda z: z.reshape(n_s, D_A)
        y, s = _rwkv_step(state_wkv[0], sq(r), sq(lw), sq(k), sq(kk), sq(kka), sq(v), STEP_NB)
        return y.reshape(1, n_s, D_A), s

    def attn_s(pa):
        o = _attn_cache(pa.reshape(n_s, 1, ATTN_W), cache_k_win[0], cache_v_win[0], sinks, STEP_NB)
        return o.reshape(1, n_s, D_B)

    n_g = len(prompt_ids) + 1
    sel, wsum, y_l = [None] * n_g, [None] * n_g, [None] * n_g
    for step in range(n_g + 2):
        if step < n_g - 1:
            b, lo, hi = prompt_ids[step]
            ps, pattn, wkv_new, sel[step] = select(*prompt_group(b, lo, hi))
            carry[b] = (ps[:, -1:], wkv_new, pattn[:, -WINDOW:])
        elif step == n_g - 1:
            ps_s, pattn_s, wkv_s_new, sel[step] = select(xs, mod_s, prev_s, wkv_s, attn_s, min(TOKEN_TB, n_s), False)
        if 0 <= step - 1 < n_g:
            wsum[step - 1] = weigh(sel[step - 1])
        if 0 <= step - 2 < n_g:
            y_l[step - 2] = finish(wsum[step - 2])
    y_s = y_l[n_g - 1]
    y_p = jnp.concatenate(y_l[:n_g - 1], axis=1).reshape(n_p, t_p, D_MODEL)
    shift_p = jnp.concatenate([carry[b][0][:, 0] for b in range(n_p)], axis=0)
    pattn_p = jnp.concatenate([carry[b][2] for b in range(n_p)], axis=0)
    wkv_p_new = jnp.concatenate([carry[b][1] for b in range(n_p)], axis=0)

    kv = lambda pa, o: pa[..., o:o + H_KV * HD_B]
    ko, vo = D_B, D_B + H_KV * HD_B
    k_win_p = kv(pattn_p, ko)[:, -WINDOW:].reshape(n_p, WINDOW, H_KV, HD_B)
    v_win_p = kv(pattn_p, vo)[:, -WINDOW:].reshape(n_p, WINDOW, H_KV, HD_B)
    k_new_s = kv(pattn_s, ko).reshape(n_s, 1, H_KV, HD_B)
    v_new_s = kv(pattn_s, vo).reshape(n_s, 1, H_KV, HD_B)
    k_win_s = jnp.concatenate([cache_k_win[0], k_new_s], axis=1)[:, -WINDOW:]
    v_win_s = jnp.concatenate([cache_v_win[0], v_new_s], axis=1)[:, -WINDOW:]
    return (y_p, y_s.reshape(n_s, 1, D_MODEL), wkv_p_new[None], wkv_s_new[None],
            shift_p[None], ps_s.reshape(n_s, SHIFT_W)[None],
            k_win_p[None], k_win_s[None], v_win_p[None], v_win_s[None])
```

```python
import functools
import math

import jax
import jax.numpy as jnp
from jax import lax
from jax.experimental import pallas as pl
from jax.experimental.pallas import tpu as pltpu
from jax.experimental.pallas import tpu_sc as plsc

F32 = jnp.float32
BF16 = jnp.bfloat16
I32 = jnp.int32
U32 = jnp.uint32

D_MODEL = 1024
H_A, HD_A = 8, 64
D_A = H_A * HD_A
D_LORA_W, D_LORA_A, D_LORA_G = 64, 64, 128
GN_EPS = 64e-5
H_Q, H_KV, HD_B = 8, 2, 64
G_Q = H_Q // H_KV
D_B = H_Q * HD_B
WINDOW = 128
N_KEYS = 128
PEER_HEADS, PEER_TOPK, PEER_HALF = 8, 16, 128
PEER_PICKS = PEER_HEADS * PEER_TOPK
N_MOD = 6
LN_EPS = 1e-5
NEG_INF = -1e30
OFF_WD = 3 * D_A
OFF_AD = OFF_WD + D_LORA_W
OFF_GD = OFF_AD + D_LORA_A
SHIFT_W = OFF_GD + D_LORA_G
ATTN_W = D_B + 2 * H_KV * HD_B
GATE_W = 2 * D_MODEL
D_IN = SHIFT_W + ATTN_W + GATE_W

VMEM_LIMIT = 48 * 1024 * 1024
SUBLANES = 8
RWKV_CHUNK = 64
SC_GATHER_ROWS = 32
SC_ACC_AHEAD = 3
SC_ACC_TOKENS = 16
SC_DOT_TOKENS = 16
SC_ACC_PANELS = 4
TOKEN_TB = 256
TOPK_TB = 128
VALUE_TC_PICKS = 64
VALUE_TC_TB = 16
PROMPT_SEGMENT = 2048
PROMPT_FIRST_SEGMENT = 512
STEP_NB = 8


def _cparams(*sem):
    return pltpu.CompilerParams(dimension_semantics=sem, vmem_limit_bytes=VMEM_LIMIT)


def _layernorm(x, g, b):
    mu = jnp.mean(x, -1, keepdims=True)
    xc = x - mu
    var = jnp.mean(xc * xc, -1, keepdims=True)
    return xc * lax.rsqrt(var + LN_EPS) * g + b


def _split(x):
    hi = x.astype(BF16)
    lo = (x - hi.astype(F32)).astype(BF16)
    return hi, lo


_NN = (((1,), (0,)), ((), ()))
_NT = (((1,), (1,)), ((), ()))
_TN = (((0,), (0,)), ((), ()))


def _dot3(a, b, dims=_NN):
    ah, al = _split(a)
    bh, bl = _split(b)
    d = functools.partial(lax.dot_general, dimension_numbers=dims, preferred_element_type=F32)
    return d(ah, bh) + d(ah, bl) + d(al, bh)


def _dot_exact_lhs(a_bf16, b, dims=_NN):
    b1 = b.astype(BF16)
    r1 = b - b1.astype(F32)
    b2 = r1.astype(BF16)
    b3 = (r1 - b2.astype(F32)).astype(BF16)
    d = functools.partial(lax.dot_general, dimension_numbers=dims, preferred_element_type=F32)
    return d(a_bf16, b1) + d(a_bf16, b2) + d(a_bf16, b3)


def _dotb(a, b, dims=_NN):
    return lax.dot_general(a.astype(BF16), b.astype(BF16), dims, preferred_element_type=F32)


def _rows(tb, width, col=0):
    return pl.BlockSpec((None, tb, width), lambda g, i: (g, i, col))


def _mod(mod, tb, col):
    if mod.shape[1] == 1:
        return pl.BlockSpec((None, 1, D_MODEL), lambda g, i: (g, 0, col))
    return pl.BlockSpec((None, tb, D_MODEL), lambda g, i: (g, i, col))


def _const(shape):
    n = len(shape)
    return pl.BlockSpec(shape, lambda g, i: (0,) * n)


def _row2(p):
    return p.reshape(1, -1).astype(F32)


def _mod_kernel(c_ref, w_ref, b_ref, o_ref):
    c = c_ref[...]
    a = c * jax.nn.sigmoid(c)
    o_ref[...] = _dot3(a, w_ref[...]) + b_ref[...]


def _modulation(c, w_ada, b_ada):
    n = c.shape[0]
    tn = D_MODEL
    return pl.pallas_call(
        _mod_kernel,
        grid=(w_ada.shape[1] // tn,),
        in_specs=[pl.BlockSpec((n, D_MODEL), lambda j: (0, 0)),
                  pl.BlockSpec((D_MODEL, tn), lambda j: (0, j)),
                  pl.BlockSpec((1, tn), lambda j: (0, j))],
        out_specs=pl.BlockSpec((n, tn), lambda j: (0, j)),
        out_shape=jax.ShapeDtypeStruct((n, w_ada.shape[1]), F32),
        compiler_params=_cparams("arbitrary"),
        name="modulation",
    )(c, w_ada, b_ada.reshape(1, -1))


def _inproj_kernel(x_ref, sh_ref, sc_ref, g_ref, b_ref, w_ref, ps_ref, pa_ref, pg_ref):
    xn = _layernorm(x_ref[...], g_ref[...], b_ref[...])
    h = (xn * (1.0 + sc_ref[...]) + sh_ref[...]).astype(BF16)
    ps_ref[...] = jnp.dot(h, w_ref[:, :SHIFT_W], preferred_element_type=F32)
    pa_ref[...] = jnp.dot(h, w_ref[:, SHIFT_W:SHIFT_W + ATTN_W], preferred_element_type=F32)
    pg_ref[...] = jnp.dot(h, w_ref[:, SHIFT_W + ATTN_W:], preferred_element_type=F32)


def _inproj(x, mod, ln_g, ln_b, w_in_bf16, tb):
    g, r, _ = x.shape
    shp = lambda w: jax.ShapeDtypeStruct((g, r, w), F32)
    return pl.pallas_call(
        _inproj_kernel,
        grid=(g, r // tb),
        in_specs=[_rows(tb, D_MODEL), _mod(mod, tb, 0), _mod(mod, tb, 1),
                  _const((1, D_MODEL)), _const((1, D_MODEL)), _const((D_MODEL, D_IN))],
        out_specs=[_rows(tb, SHIFT_W), _rows(tb, ATTN_W), _rows(tb, GATE_W)],
        out_shape=[shp(SHIFT_W), shp(ATTN_W), shp(GATE_W)],
        compiler_params=_cparams("parallel", "parallel"),
        name="inproj",
    )(x, mod, mod, _row2(ln_g), _row2(ln_b), w_in_bf16)


def _softplus(x):
    return jnp.maximum(x, 0.0) + jnp.log1p(jnp.exp(-jnp.abs(x)))


def _rwkv_prep_kernel(ps_ref, prev_ref, first_ref, mu_ref, w0_ref, w2_ref, a0_ref, a2_ref, g2_ref, kk_w_ref, ka_w_ref,
                      hsum_ref, r_ref, lw_ref, k_ref, v_ref, kk_ref, kka_ref, g_ref, *, sequential):
    ps = ps_ref[...]
    if sequential:
        before = jnp.where(pl.program_id(1) == 0, first_ref[...], prev_ref[SUBLANES - 1:SUBLANES, :])
        row = lax.broadcasted_iota(I32, ps.shape, 0)
        prev = jnp.where(row == 0, before, pltpu.roll(ps, 1, 0))
    else:
        prev = prev_ref[...]
    xs = ps + (prev - ps) * mu_ref[...]
    r = xs[:, 0:D_A]
    k = xs[:, D_A:2 * D_A]
    v = xs[:, 2 * D_A:3 * D_A]
    wd = xs[:, OFF_WD:OFF_AD]
    ad = xs[:, OFF_AD:OFF_GD]
    gd = xs[:, OFF_GD:SHIFT_W]
    z = w0_ref[...] + _dot3(jnp.tanh(wd), w2_ref[...])
    w_log = -_softplus(-z) - 0.5
    a = jax.nn.sigmoid(a0_ref[...] + _dot3(ad, a2_ref[...]))
    kk = k * kk_w_ref[...]
    ss = _dot3(kk * kk, hsum_ref[...])
    kk = kk / jnp.maximum(jnp.sqrt(ss), 1e-12)
    r_ref[...] = r
    lw_ref[...] = -jnp.exp(w_log)
    k_ref[...] = k * (1.0 + (a - 1.0) * ka_w_ref[...])
    v_ref[...] = v
    kk_ref[...] = kk
    kka_ref[...] = kk * a
    g_ref[...] = _dot3(jax.nn.sigmoid(gd), g2_ref[...])


def _head_sum_matrix():
    h = jnp.arange(D_A) // HD_A
    return (h[:, None] == h[None, :]).astype(F32)


def _rwkv_prep(ps, prev, p, tb, sequential):
    g, r, _ = ps.shape
    shp = jax.ShapeDtypeStruct((g, r, D_A), F32)
    if sequential:
        per = tb // SUBLANES
        prev_args = (ps, prev)
        prev_specs = [pl.BlockSpec((None, SUBLANES, SHIFT_W), lambda gi, i: (gi, jnp.maximum(i * per - 1, 0), 0)),
                      pl.BlockSpec((None, 1, SHIFT_W), lambda gi, i: (gi, 0, 0))]
    else:
        prev_args = (prev, prev[:, :1])
        prev_specs = [_rows(tb, SHIFT_W), pl.BlockSpec((None, 1, SHIFT_W), lambda gi, i: (gi, 0, 0))]
    return pl.pallas_call(
        functools.partial(_rwkv_prep_kernel, sequential=sequential),
        grid=(g, r // tb),
        in_specs=[_rows(tb, SHIFT_W)] + prev_specs + [_const((1, SHIFT_W)),
                  _const((1, D_A)), _const((D_LORA_W, D_A)), _const((1, D_A)), _const((D_LORA_A, D_A)),
                  _const((D_LORA_G, D_A)), _const((1, D_A)), _const((1, D_A)), _const((D_A, D_A))],
        out_specs=[_rows(tb, D_A)] * 7,
        out_shape=[shp] * 7,
        compiler_params=_cparams("parallel", "parallel"),
        name="rwkv_prep",
    )(ps, *prev_args, _row2(p["mu_shift"]), _row2(p["rwkv_w0"]), p["rwkv_w2"], _row2(p["rwkv_a0"]), p["rwkv_a2"],
      p["rwkv_g2"], _row2(p["rwkv_k_k"]), _row2(p["rwkv_k_a"]), _head_sum_matrix())


def _rwkv_chunk_kernel(r_ref, lw_ref, k_ref, v_ref, kk_ref, kka_ref, s0_ref, y_ref, s_ref):
    c = RWKV_CHUNK

    @pl.when(pl.program_id(1) == 0)
    def _():
        s_ref[...] = s0_ref[...]

    row = lax.broadcasted_iota(I32, (c, c), 0)
    col = lax.broadcasted_iota(I32, (c, c), 1)
    tril = row >= col
    stril = row > col
    lw = lw_ref[...]
    cum = _dot_exact_lhs(tril.astype(BF16), lw)
    cum_end = cum[c - 1:c, :]
    g_inv = jnp.exp(-cum)
    g_end = jnp.exp(cum_end - cum)
    a_hat = -kk_ref[...] * jnp.exp(cum - lw)
    b_hat = kka_ref[...] * g_inv
    k_hat = k_ref[...] * g_inv
    r_til = r_ref[...] * jnp.exp(cum)
    b_end = kka_ref[...] * g_end
    k_end = k_ref[...] * g_end
    gam_end = jnp.exp(cum_end)
    v_all = v_ref[...]
    s_all = s_ref[...]
    n_steps = int(math.log2(c))
    heads = range(H_A)
    sl = [slice(h * HD_A, (h + 1) * HD_A) for h in heads]
    vh = [v_all[:, sl[h]] for h in heads]
    ar = [jnp.concatenate([a_hat[:, sl[h]], r_til[:, sl[h]]], axis=0) for h in heads]
    bk = [jnp.concatenate([b_hat[:, sl[h]], k_hat[:, sl[h]]], axis=0) for h in heads]
    x = [_dot3(ar[h], bk[h], _NT) for h in heads]
    ars = [_dot3(ar[h], s_all[h], _NT) for h in heads]
    a_ak = [jnp.where(stril, x[h][:c, c:], 0.0) for h in heads]
    n = [jnp.where(stril, x[h][:c, :c], 0.0) for h in heads]
    u = [ars[h][:c] + _dot3(a_ak[h], vh[h]) for h in heads]
    for it in range(n_steps):
        u = [u[h] + _dot3(n[h], u[h]) for h in heads]
        if it + 1 < n_steps:
            n = [_dot3(n[h], n[h]) for h in heads]
    uv = [jnp.concatenate([u[h], vh[h]], axis=0) for h in heads]
    a_r = [jnp.concatenate([jnp.where(tril, x[h][c:, :c], 0.0), jnp.where(tril, x[h][c:, c:], 0.0)], axis=1)
           for h in heads]
    y = [ars[h][c:] + _dot3(a_r[h], uv[h]) for h in heads]
    bke = [jnp.concatenate([b_end[:, sl[h]], k_end[:, sl[h]]], axis=0) for h in heads]
    s_new = [s_all[h] * gam_end[:, sl[h]] + _dot3(uv[h], bke[h], _TN) for h in heads]
    for h in heads:
        y_ref[:, sl[h]] = y[h]
        s_ref[h] = s_new[h]


def _rwkv_chunk_scan(r, lw, k, v, kk, kka, s0):
    n, t, _ = r.shape
    c = RWKV_CHUNK
    seq = pl.BlockSpec((None, c, D_A), lambda b, i: (b, i, 0))
    st = pl.BlockSpec((None, H_A, HD_A, HD_A), lambda b, i: (b, 0, 0, 0))
    return pl.pallas_call(
        _rwkv_chunk_kernel,
        grid=(n, t // c),
        in_specs=[seq] * 6 + [st],
        out_specs=[seq, st],
        out_shape=[jax.ShapeDtypeStruct((n, t, D_A), F32), jax.ShapeDtypeStruct((n, H_A, HD_A, HD_A), F32)],
        compiler_params=_cparams("parallel", "arbitrary"),
        name="rwkv_chunk_scan",
    )(r, lw, k, v, kk, kka, s0)


def _rwkv_step_kernel(s_ref, r_ref, lw_ref, k_ref, kk_ref, kka_ref, v_ref, y_ref, so_ref):
    s = s_ref[...]
    sa = jnp.sum(s * (-kk_ref[...]), axis=-1, keepdims=True)
    s = s * jnp.exp(lw_ref[...]) + sa * kka_ref[...] + v_ref[...] * k_ref[...]
    so_ref[...] = s
    y_ref[...] = jnp.sum(s * r_ref[...], axis=-1, keepdims=True)


def _rwkv_step(s0, r, lw, k, kk, kka, v, nb):
    n = s0.shape[0]
    key = lambda z: z.reshape(n, H_A, 1, HD_A)
    st = pl.BlockSpec((nb, H_A, HD_A, HD_A), lambda i: (i, 0, 0, 0))
    ks = pl.BlockSpec((nb, H_A, 1, HD_A), lambda i: (i, 0, 0, 0))
    vs = pl.BlockSpec((nb, H_A, HD_A, 1), lambda i: (i, 0, 0, 0))
    y, s = pl.pallas_call(
        _rwkv_step_kernel,
        grid=(n // nb,),
        in_specs=[st, ks, ks, ks, ks, ks, vs],
        out_specs=[vs, st],
        out_shape=[jax.ShapeDtypeStruct((n, H_A, HD_A, 1), F32), jax.ShapeDtypeStruct(s0.shape, F32)],
        compiler_params=_cparams("parallel"),
        name="rwkv_step",
    )(s0, key(r), key(lw), key(k), key(kk), key(kka), v.reshape(n, H_A, HD_A, 1))
    return y.reshape(n, D_A), s


def _rwkv_post_kernel(y_ref, r_ref, k_ref, v_ref, g_ref, gn_g_ref, gn_b_ref, rk_ref, hsum_ref, o_ref):
    y = y_ref[...]
    hs = hsum_ref[...]
    mu = _dot3(y, hs) * (1.0 / HD_A)
    yc = y - mu
    var = _dot3(yc * yc, hs) * (1.0 / HD_A)
    yn = yc * lax.rsqrt(var + GN_EPS) * gn_g_ref[...] + gn_b_ref[...]
    bonus = _dot3(r_ref[...] * k_ref[...] * rk_ref[...], hs) * v_ref[...]
    o_ref[...] = (yn + bonus) * g_ref[...]


def _rwkv_post(y, r, k, v, g, p, tb):
    gg, rr, _ = y.shape
    return pl.pallas_call(
        _rwkv_post_kernel,
        grid=(gg, rr // tb),
        in_specs=[_rows(tb, D_A)] * 5 + [_const((1, D_A))] * 3 + [_const((D_A, D_A))],
        out_specs=_rows(tb, D_A),
        out_shape=jax.ShapeDtypeStruct((gg, rr, D_A), F32),
        compiler_params=_cparams("parallel", "parallel"),
        name="rwkv_post",
    )(y, r, k, v, g, _row2(p["rwkv_gn_g"]), _row2(p["rwkv_gn_b"]), _row2(p["rwkv_r_k"]), _head_sum_matrix())


def _sink_softmax(s, sink):
    m = jnp.maximum(jnp.max(s, axis=-1, keepdims=True), sink)
    p = jnp.exp(s - m)
    den = jnp.sum(p, axis=-1, keepdims=True) + jnp.exp(sink - m)
    return p / den


def _attn_band_kernel(cur_ref, prev_ref, carry_ref, sink_ref, o_ref, *, has_carry):
    blk = WINDOW
    i = pl.program_id(1)
    cur = cur_ref[...]
    prev = jnp.where(i == 0, carry_ref[...], prev_ref[...])
    qi = lax.broadcasted_iota(I32, (G_Q * blk, 2 * blk), 0) % blk
    kj = lax.broadcasted_iota(I32, (G_Q * blk, 2 * blk), 1)
    rel = blk + qi - kj
    valid = (rel >= 0) & (rel <= WINDOW)
    if not has_carry:
        valid = valid & ((kj >= blk) | (i > 0))
    relf = rel.astype(F32)
    gidx = lax.broadcasted_iota(I32, (G_Q * blk, 1), 0) // blk
    for kvh in range(H_KV):
        q4 = jnp.concatenate([cur[:, (kvh * G_Q + g) * HD_B:(kvh * G_Q + g + 1) * HD_B] for g in range(G_Q)], axis=0)
        ko = D_B + kvh * HD_B
        vo = D_B + H_KV * HD_B + kvh * HD_B
        kmat = jnp.concatenate([prev[:, ko:ko + HD_B], cur[:, ko:ko + HD_B]], axis=0)
        vmat = jnp.concatenate([prev[:, vo:vo + HD_B], cur[:, vo:vo + HD_B]], axis=0)
        slope = jnp.zeros((G_Q * blk, 1), F32)
        sink = jnp.zeros((G_Q * blk, 1), F32)
        for g in range(G_Q):
            hq = kvh * G_Q + g
            slope = jnp.where(gidx == g, 2.0 ** (-8.0 * (hq + 1) / H_Q), slope)
            sink = jnp.where(gidx == g, sink_ref[hq], sink)
        s = _dotb(q4, kmat, _NT) * (HD_B ** -0.5)
        s = jnp.where(valid, s - slope * relf, NEG_INF)
        p = _sink_softmax(s, sink)
        o = _dotb(p, vmat)
        for g in range(G_Q):
            hq = kvh * G_Q + g
            o_ref[:, hq * HD_B:(hq + 1) * HD_B] = o[g * blk:(g + 1) * blk]


def _attn_band(pattn, carry, sinks):
    n, t, _ = pattn.shape
    blk = WINDOW
    has_carry = carry is not None
    if not has_carry:
        carry = jnp.zeros((n, blk, ATTN_W), F32)
    return pl.pallas_call(
        functools.partial(_attn_band_kernel, has_carry=has_carry),
        grid=(n, t // blk),
        in_specs=[pl.BlockSpec((None, blk, ATTN_W), lambda b, i: (b, i, 0)),
                  pl.BlockSpec((None, blk, ATTN_W), lambda b, i: (b, jnp.maximum(i - 1, 0), 0)),
                  pl.BlockSpec((None, blk, ATTN_W), lambda b, i: (b, 0, 0)),
                  pl.BlockSpec(memory_space=pltpu.SMEM)],
        out_specs=pl.BlockSpec((None, blk, D_B), lambda b, i: (b, i, 0)),
        out_shape=jax.ShapeDtypeStruct((n, t, D_B), F32),
        compiler_params=_cparams("parallel", "parallel"),
        name="attn_band",
    )(pattn, pattn, carry, sinks.astype(F32))


def _attn_cache_kernel(cur_ref, kc_ref, vc_ref, sink_ref, o_ref, *, nb):
    relc = (WINDOW - lax.broadcasted_iota(I32, (G_Q, WINDOW), 1)).astype(F32)
    gidx = lax.broadcasted_iota(I32, (G_Q, 1), 0)
    for b in range(nb):
        cur = cur_ref[b]
        for kvh in range(H_KV):
            q4 = jnp.concatenate([cur[:, (kvh * G_Q + g) * HD_B:(kvh * G_Q + g + 1) * HD_B] for g in range(G_Q)], axis=0)
            ko = D_B + kvh * HD_B
            vo = D_B + H_KV * HD_B + kvh * HD_B
            k_new = cur[:, ko:ko + HD_B]
            v_new = cur[:, vo:vo + HD_B]
            kc = kc_ref[b, :, kvh * HD_B:(kvh + 1) * HD_B]
            vc = vc_ref[b, :, kvh * HD_B:(kvh + 1) * HD_B]
            slope = jnp.zeros((G_Q, 1), F32)
            sink = jnp.zeros((G_Q, 1), F32)
            for g in range(G_Q):
                hq = kvh * G_Q + g
                slope = jnp.where(gidx == g, 2.0 ** (-8.0 * (hq + 1) / H_Q), slope)
                sink = jnp.where(gidx == g, sink_ref[hq], sink)
            scale = HD_B ** -0.5
            sc = _dotb(q4, kc, _NT) * scale - slope * relc
            sn = jnp.sum(q4.astype(BF16).astype(F32) * k_new.astype(BF16).astype(F32), axis=-1, keepdims=True) * scale
            m = jnp.maximum(jnp.maximum(jnp.max(sc, axis=-1, keepdims=True), sn), sink)
            pc = jnp.exp(sc - m)
            pn = jnp.exp(sn - m)
            den = jnp.sum(pc, axis=-1, keepdims=True) + pn + jnp.exp(sink - m)
            o = (_dotb(pc / den, vc) + (pn / den).astype(BF16).astype(F32) * v_new.astype(BF16).astype(F32))
            for g in range(G_Q):
                hq = kvh * G_Q + g
                o_ref[b, :, hq * HD_B:(hq + 1) * HD_B] = o[g:g + 1]


def _attn_cache(pattn, k_buf, v_buf, sinks, nb):
    n = pattn.shape[0]
    kc = k_buf.reshape(n, WINDOW, H_KV * HD_B)
    vc = v_buf.reshape(n, WINDOW, H_KV * HD_B)
    return pl.pallas_call(
        functools.partial(_attn_cache_kernel, nb=nb),
        grid=(n // nb,),
        in_specs=[pl.BlockSpec((nb, 1, ATTN_W), lambda i: (i, 0, 0)),
                  pl.BlockSpec((nb, WINDOW, H_KV * HD_B), lambda i: (i, 0, 0)),
                  pl.BlockSpec((nb, WINDOW, H_KV * HD_B), lambda i: (i, 0, 0)),
                  pl.BlockSpec(memory_space=pltpu.SMEM)],
        out_specs=pl.BlockSpec((nb, 1, D_B), lambda i: (i, 0, 0)),
        out_shape=jax.ShapeDtypeStruct((n, 1, D_B), F32),
        compiler_params=_cparams("parallel"),
        name="attn_cache",
    )(pattn, kc, vc, sinks.astype(F32))


def _merge_kernel(x_ref, ya_ref, ob_ref, pg_ref, gtm_ref, shf_ref, scf_ref, lng_ref, lnb_ref, l1g_ref, l1b_ref,
                  wpa_ref, wpb_ref, wo_ref, wq_ref, sk_ref, x1_ref, h2_ref, st_ref, *, alpha):
    ya = jnp.dot(ya_ref[...].astype(BF16), wpa_ref[...], preferred_element_type=F32)
    yb = jnp.dot(ob_ref[...].astype(BF16), wpb_ref[...], preferred_element_type=F32)
    pg = pg_ref[...]
    merged = jax.nn.sigmoid(pg[:, :D_MODEL]) * ya + jax.nn.sigmoid(pg[:, D_MODEL:]) * yb
    mix = jnp.dot(merged.astype(BF16), wo_ref[...], preferred_element_type=F32)
    xn = _layernorm(x_ref[...], lng_ref[...], lnb_ref[...])
    x1 = _layernorm(alpha * xn + gtm_ref[...] * mix, l1g_ref[...], l1b_ref[...])
    x1_ref[...] = x1
    h2 = x1 * (1.0 + scf_ref[...]) + shf_ref[...]
    h2_ref[...] = h2
    q = jnp.dot(h2.astype(BF16), wq_ref[...], preferred_element_type=F32)
    for hc in range(2 * PEER_HEADS):
        st_ref[hc] = _dot3(sk_ref[hc % 2], q[:, hc * PEER_HALF:(hc + 1) * PEER_HALF], _NT)


def _merge(x, ya, ob, pg, mod, p, alpha, tb):
    g, r, _ = x.shape
    return pl.pallas_call(
        functools.partial(_merge_kernel, alpha=alpha),
        grid=(g, r // tb),
        in_specs=[_rows(tb, D_MODEL), _rows(tb, D_A), _rows(tb, D_B), _rows(tb, GATE_W),
                  _mod(mod, tb, 2), _mod(mod, tb, 3), _mod(mod, tb, 4)]
                 + [_const((1, D_MODEL))] * 4
                 + [_const((D_A, D_MODEL)), _const((D_B, D_MODEL)), _const((D_MODEL, D_MODEL)),
                    _const((D_MODEL, 2 * PEER_HEADS * PEER_HALF)), _const((2, N_KEYS, PEER_HALF))],
        out_specs=[_rows(tb, D_MODEL), _rows(tb, D_MODEL),
                   pl.BlockSpec((None, 2 * PEER_HEADS, N_KEYS, tb), lambda gi, i: (gi, 0, 0, i))],
        out_shape=[jax.ShapeDtypeStruct((g, r, D_MODEL), F32), jax.ShapeDtypeStruct((g, r, D_MODEL), F32),
                   jax.ShapeDtypeStruct((g, 2 * PEER_HEADS, N_KEYS, r), F32)],
        compiler_params=_cparams("parallel", "parallel"),
        name="merge_ln1_peer_scores",
    )(x, ya, ob, pg, mod, mod, mod, _row2(p["ln_in_g"]), _row2(p["ln_in_b"]), _row2(p["ln1_g"]), _row2(p["ln1_b"]),
      p["w_pa"].astype(BF16), p["w_pb"].astype(BF16), p["w_o"].astype(BF16), p["peer_wq"].astype(BF16),
      p["peer_sub_keys"])


def _extract_top(problems, n_rows, tb):
    rio = lax.broadcasted_iota(I32, (n_rows, tb), 0).astype(F32)
    vals = [v for v, _ in problems]
    tops = [([], [], []) for _ in problems]
    for _ in range(PEER_TOPK):
        for n, (_, payload) in enumerate(problems):
            m = jnp.max(vals[n], axis=0, keepdims=True)
            i = jnp.min(jnp.where(vals[n] == m, rio, float(n_rows)), axis=0, keepdims=True)
            sel = rio == i
            tops[n][0].append(m)
            tops[n][1].append(i)
            if payload is not None:
                tops[n][2].append(jnp.max(jnp.where(sel, payload, -1.0), axis=0, keepdims=True))
            vals[n] = jnp.where(sel, -jnp.inf, vals[n])
    cat = lambda z: jnp.concatenate(z, axis=0) if z else None
    return [(cat(v), cat(i), cat(pl_)) for v, i, pl_ in tops]


def _pair_candidates(v1, i1, v2, i2, tb):
    k = PEER_TOPK
    sub = 8
    eid = lambda a0, a1, b0, b1: i1[a0:a1] * float(N_KEYS) + i2[b0:b1]
    vals = [v1[0:1] + v2, v1[1:2] + v2[0:sub]]
    ids = [eid(0, 1, 0, k), eid(1, 2, 0, sub)]
    brow = lax.broadcasted_iota(I32, (sub, tb), 0)
    for a in range(2, sub):
        vals.append(jnp.where(brow < k // (a + 1), v1[a:a + 1] + v2[0:sub], -jnp.inf))
        ids.append(eid(a, a + 1, 0, sub))
    vals.append(v1[sub:k] + v2[0:1])
    ids.append(eid(sub, k, 0, 1))
    return jnp.concatenate(vals, axis=0), jnp.concatenate(ids, axis=0)


def _topk_kernel(s_ref, idx_ref, gate_ref, idx_t, gate_t):
    tb = s_ref.shape[-1]

    def head_pair(hp, carry):
        tops = _extract_top([(s_ref[4 * hp + n], None) for n in range(4)], N_KEYS, tb)
        cands = [_pair_candidates(tops[2 * n][0], tops[2 * n][1], tops[2 * n + 1][0], tops[2 * n + 1][1], tb)
                 for n in range(2)]
        picked = _extract_top(cands, cands[0][0].shape[0], tb)
        for n, (sc, _, ex) in enumerate(picked):
            pexp = jnp.exp(sc - sc[0:1])
            rows = pl.ds(pl.multiple_of((2 * hp + n) * PEER_TOPK, PEER_TOPK), PEER_TOPK)
            idx_t[rows, :] = ex
            gate_t[rows, :] = pexp / jnp.sum(pexp, axis=0, keepdims=True)
        return carry

    lax.fori_loop(0, PEER_HEADS // 2, head_pair, 0)
    idx_ref[...] = jnp.transpose(idx_t[...]).astype(I32)
    gate_ref[...] = jnp.transpose(gate_t[...])


def _topk(scores_t, tb):
    g, _, _, r = scores_t.shape
    out = pl.BlockSpec((None, tb, PEER_PICKS), lambda gi, i: (gi, i, 0))
    return pl.pallas_call(
        _topk_kernel,
        grid=(g, r // tb),
        in_specs=[pl.BlockSpec((None, 2 * PEER_HEADS, N_KEYS, tb), lambda gi, i: (gi, 0, 0, i))],
        out_specs=[out, out],
        out_shape=[jax.ShapeDtypeStruct((g, r, PEER_PICKS), I32), jax.ShapeDtypeStruct((g, r, PEER_PICKS), F32)],
        scratch_shapes=[pltpu.VMEM((PEER_PICKS, tb), F32), pltpu.VMEM((PEER_PICKS, tb), F32)],
        compiler_params=_cparams("parallel", "parallel"),
        name="peer_topk",
    )(scores_t)


def _sc_mesh():
    info = plsc.get_sparse_core_info()
    mesh = plsc.VectorSubcoreMesh(core_axis_name="c", subcore_axis_name="s")
    return info.num_cores, info.num_subcores, info.num_lanes, mesh


def _sc_row_dots(table, idx, x):
    nc, ns, lanes, mesh = _sc_mesh()
    nw = nc * ns
    n_tok, picks = idx.shape
    ww = table.shape[1]
    rr = SC_GATHER_ROWS
    nq = picks // rr
    tpw = n_tok // nw
    kt = min(SC_DOT_TOKENS, tpw)
    n_outer = tpw // kt
    npan = SC_ACC_PANELS
    pw = ww // npan
    nv = pw // lanes
    ahead = SC_ACC_AHEAD
    assert tpw * nw == n_tok and n_outer * kt == tpw and nq * rr == picks and ahead < nq and nv * lanes == pw
    assert npan * lanes <= 128
    cp = pltpu.CompilerParams(needs_layout_passes=False)
    buf = pltpu.VMEM((rr, ww), table.dtype)

    @functools.partial(
        pl.kernel, mesh=mesh, out_type=jax.ShapeDtypeStruct((n_tok * picks,), F32),
        scratch_types=[pltpu.VMEM((kt * nq, rr), I32), pltpu.VMEM((kt, 2 * ww), F32), pltpu.VMEM((picks, 128), F32),
                       pltpu.VMEM((kt * picks,), F32)] + [buf] * nq + [pltpu.SemaphoreType.DMA] * nq,
        compiler_params=cp, name="peer_row_dots",
    )
    def kern(tab_hbm, idx_hbm, x_hbm, out_hbm, idx_v, x_v, part_v, hd_v, *scratch):
        wid = lax.axis_index("s") * nc + lax.axis_index("c")
        rows, sems = scratch[:nq], scratch[nq:]
        last_lane = lax.iota(I32, lanes) == lanes - 1

        def chunk_copy(t, q):
            return pltpu.make_async_copy(tab_hbm.at[idx_v.at[t * nq + q]], rows[q], sems[q])

        @pl.loop(0, n_outer)
        def _(o):
            tok0 = wid * tpw + o * kt
            pltpu.sync_copy(idx_hbm.at[pl.ds(tok0 * nq, kt * nq)], idx_v)
            pltpu.sync_copy(x_hbm.at[pl.ds(tok0, kt)], x_v)
            for q in range(ahead):
                chunk_copy(0, q).start()

            @pl.loop(0, kt)
            def _(t):
                for q in range(nq):
                    if q + ahead < nq:
                        chunk_copy(t, q + ahead).start()
                    else:
                        @pl.when(t + 1 < kt)
                        def _():
                            chunk_copy(t + 1, q + ahead - nq).start()
                    chunk_copy(t, q).wait()
                    for pan in range(npan):
                        lo_at = lambda v: pl.ds(pan * pw + v * lanes, lanes)
                        hi_at = lambda v: pl.ds(ww + pan * pw + v * lanes, lanes)
                        xlo = [x_v[t, lo_at(v)] for v in range(nv)]
                        xhi = [x_v[t, hi_at(v)] for v in range(nv)]

                        @plsc.parallel_loop(0, rr, unroll=2)
                        def _(j):
                            terms = []
                            for v in range(nv):
                                word = rows[q][j, lo_at(v)]
                                terms.append(plsc.bitcast(word << 16, F32) * xlo[v])
                                terms.append(plsc.bitcast(word & jnp.uint32(0xFFFF0000), F32) * xhi[v])
                            while len(terms) > 1:
                                terms = [terms[i] + terms[i + 1] for i in range(0, len(terms), 2)]
                            part_v[q * rr + j, pl.ds(pan * lanes, lanes)] = terms[0]

                @plsc.parallel_loop(0, picks, unroll=2)
                def _(e):
                    s = part_v[e, pl.ds(0, lanes)]
                    for pan in range(1, npan):
                        s = s + part_v[e, pl.ds(pan * lanes, lanes)]
                    slot = jnp.full((lanes,), t * picks, I32) + e
                    plsc.store_scatter(hd_v, [slot], plsc.cumsum(s), mask=last_lane)

            pltpu.sync_copy(hd_v, out_hbm.at[pl.ds(tok0 * picks, kt * picks)])

    return kern(table, idx.reshape(n_tok * nq, rr), x).reshape(n_tok, picks)


def _sc_weighted_row_sum(table, idx, wgt, pick0):
    nc, ns, lanes, mesh = _sc_mesh()
    nw = nc * ns
    n_tok, picks = idx.shape
    ww = table.shape[1]
    rr = SC_GATHER_ROWS
    nq_all = picks // rr
    q0 = pick0 // rr
    nq = nq_all - q0
    tpw = n_tok // nw
    kt = min(SC_ACC_TOKENS, tpw)
    n_outer = tpw // kt
    pw = ww // SC_ACC_PANELS
    nv = pw // lanes
    ahead = min(SC_ACC_AHEAD, nq - 1)
    assert tpw * nw == n_tok and n_outer * kt == tpw and nq_all * rr == picks and q0 * rr == pick0 and nv * lanes == pw
    assert ahead >= 1
    cp = pltpu.CompilerParams(needs_layout_passes=False)
    buf = pltpu.VMEM((rr, ww), table.dtype)

    @functools.partial(
        pl.kernel, mesh=mesh, out_type=jax.ShapeDtypeStruct((n_tok, 2 * ww), F32),
        scratch_types=[pltpu.VMEM((kt * nq_all, rr), I32), pltpu.VMEM((kt, picks), F32), pltpu.VMEM((kt, 2 * ww), F32)]
                      + [buf] * nq + [pltpu.SemaphoreType.DMA] * nq,
        compiler_params=cp, name="peer_weighted_row_sum",
    )
    def kern(tab_hbm, idx_hbm, w_hbm, out_hbm, idx_v, w_v, acc_v, *scratch):
        wid = lax.axis_index("s") * nc + lax.axis_index("c")
        rows, sems = scratch[:nq], scratch[nq:]

        def chunk_copy(t, q):
            return pltpu.make_async_copy(tab_hbm.at[idx_v.at[t * nq_all + q0 + q]], rows[q], sems[q])

        @pl.loop(0, n_outer)
        def _(o):
            tok0 = wid * tpw + o * kt
            pltpu.sync_copy(idx_hbm.at[pl.ds(tok0 * nq_all, kt * nq_all)], idx_v)
            pltpu.sync_copy(w_hbm.at[pl.ds(tok0, kt)], w_v)
            for q in range(ahead):
                chunk_copy(0, q).start()

            @pl.loop(0, kt)
            def _(t):
                tvec = jnp.full((lanes,), t, I32)
                for q in range(nq):
                    b = q
                    if q + ahead < nq:
                        chunk_copy(t, q + ahead).start()
                    else:
                        @pl.when(t + 1 < kt)
                        def _():
                            chunk_copy(t + 1, q + ahead - nq).start()
                    chunk_copy(t, q).wait()
                    for pan in range(SC_ACC_PANELS):
                        lo_at = lambda c: pl.ds(pan * pw + c * lanes, lanes)
                        hi_at = lambda c: pl.ds(ww + pan * pw + c * lanes, lanes)

                        def row_body(j, acc):
                            wj = plsc.load_gather(w_v, [tvec, jnp.full((lanes,), pick0 + q * rr, I32) + j])
                            new = []
                            for c in range(nv):
                                word = rows[b][j, lo_at(c)]
                                new.append(acc[2 * c] + wj * plsc.bitcast(word << 16, F32))
                                new.append(acc[2 * c + 1] + wj * plsc.bitcast(word & jnp.uint32(0xFFFF0000), F32))
                            return tuple(new)

                        if q == 0:
                            init = tuple(jnp.zeros((lanes,), F32) for _ in range(2 * nv))
                        else:
                            init = tuple(acc_v[t, at(c)] for c in range(nv) for at in (lo_at, hi_at))
                        acc = lax.fori_loop(0, rr, row_body, init)
                        for c in range(nv):
                            acc_v[t, lo_at(c)] = acc[2 * c]
                            acc_v[t, hi_at(c)] = acc[2 * c + 1]

            pltpu.sync_copy(acc_v, out_hbm.at[pl.ds(tok0, kt)])

    return kern(table, idx.reshape(n_tok * nq_all, rr), wgt)


def _pack_bf16_pairs(t):
    half = t.shape[1] // 2
    b = lax.bitcast_convert_type(t.astype(BF16), jnp.uint16).astype(U32)
    return b[:, :half] | (b[:, half:] << 16)


def _gelu_erf(x):
    return 0.5 * x * (1.0 + lax.erf(x * (2.0 ** -0.5)))


def _peer_weights_kernel(hd_ref, gate_ref, w_ref):
    w_ref[...] = _gelu_erf(hd_ref[...]) * gate_ref[...]


def _peer_weights(hd, gate, tb):
    g, r, _ = hd.shape
    return pl.pallas_call(
        _peer_weights_kernel,
        grid=(g, r // tb),
        in_specs=[_rows(tb, PEER_PICKS), _rows(tb, PEER_PICKS)],
        out_specs=_rows(tb, PEER_PICKS),
        out_shape=jax.ShapeDtypeStruct((g, r, PEER_PICKS), F32),
        compiler_params=_cparams("parallel", "parallel"),
        name="peer_weights",
    )(hd, gate)


def _peer_value_tc_kernel(idx_ref, w_ref, tab_hbm, o_ref, tab_v, sem, *, tb, picks):
    @pl.when(pl.program_id(0) == 0)
    def _():
        load = pltpu.make_async_copy(tab_hbm, tab_v, sem)
        load.start()
        load.wait()

    slab = tab_v.shape[1:]
    for t in range(tb):
        lo = jnp.zeros(slab, F32)
        hi = jnp.zeros(slab, F32)
        for e in range(picks):
            word = tab_v[idx_ref[t, e]]
            wt = w_ref[t, e]
            lo = lo + wt * pltpu.bitcast(word << 16, F32)
            hi = hi + wt * pltpu.bitcast(word & jnp.uint32(0xFFFF0000), F32)
        o_ref[t, :, pl.ds(0, 128)] = lo
        o_ref[t, :, pl.ds(128, 128)] = hi


def _peer_value_tc(table3, idx, wgt, tb):
    r, picks = idx.shape
    v, s, _ = table3.shape
    part = pl.pallas_call(
        functools.partial(_peer_value_tc_kernel, tb=tb, picks=picks),
        grid=(r // tb,),
        in_specs=[pl.BlockSpec((tb, picks), lambda i: (i, 0), memory_space=pltpu.SMEM),
                  pl.BlockSpec((tb, picks), lambda i: (i, 0), memory_space=pltpu.SMEM),
                  pl.BlockSpec(memory_space=pl.ANY)],
        out_specs=pl.BlockSpec((tb, s, 256), lambda i: (i, 0, 0)),
        out_shape=jax.ShapeDtypeStruct((r, s, 256), F32),
        scratch_shapes=[pltpu.VMEM((v, s, 128), table3.dtype), pltpu.SemaphoreType.DMA],
        compiler_params=_cparams("arbitrary"),
        name="peer_value_tc",
    )(idx, wgt, table3)
    return jnp.concatenate([part[:, :, :128].reshape(r, s * 128), part[:, :, 128:].reshape(r, s * 128)], axis=1)


def _peer_out_kernel(x1_ref, ff_a_ref, ff_b_ref, gtf_ref, l2g_ref, l2b_ref, o_ref, *, alpha):
    ff = ff_a_ref[...] + ff_b_ref[...]
    o_ref[...] = _layernorm(alpha * x1_ref[...] + gtf_ref[...] * ff, l2g_ref[...], l2b_ref[...])


def _peer_out(x1, ff_a, ff_b, mod, p, alpha, tb):
    g, r, _ = x1.shape
    return pl.pallas_call(
        functools.partial(_peer_out_kernel, alpha=alpha),
        grid=(g, r // tb),
        in_specs=[_rows(tb, D_MODEL)] * 3 + [_mod(mod, tb, 5), _const((1, D_MODEL)), _const((1, D_MODEL))],
        out_specs=_rows(tb, D_MODEL),
        out_shape=jax.ShapeDtypeStruct((g, r, D_MODEL), F32),
        compiler_params=_cparams("parallel", "parallel"),
        name="peer_out_ln2",
    )(x1, ff_a, ff_b, mod, _row2(p["ln2_g"]), _row2(p["ln2_b"]))


def _token_stage(x, mod, prev_fn, wkv_fn, attn_fn, p, alpha, tb, sequential):
    ps, pattn, pgate = _inproj(x, mod, p["ln_in_g"], p["ln_in_b"], p["w_in_bf16"], tb)
    r, lw, k, v, kk, kka, gl = _rwkv_prep(ps, prev_fn(), p, tb, sequential)
    y, wkv_new = wkv_fn(r, lw, k, v, kk, kka)
    ya = _rwkv_post(y, r, k, v, gl, p, tb)
    ob = attn_fn(pattn)
    x1, h2, scores_t = _merge(x, ya, ob, pgate, mod, p, alpha, tb)
    idx_t, gate_t = _topk(scores_t, TOPK_TB)
    return ps, pattn, wkv_new, x1, h2, idx_t, gate_t


def kernel(x_prompt, x_sample, state_wkv, state_shift, cache_k_win, cache_v_win, c_prompt, c_sample, ln_in_g, ln_in_b, w_ada, b_ada, w_in, mu_shift, rwkv_w0, rwkv_w2, rwkv_a0, rwkv_a2, rwkv_g2, rwkv_k_k, rwkv_k_a, rwkv_r_k, rwkv_gn_g, rwkv_gn_b, attn_sinks, w_pa, w_pb, w_o, ln1_g, ln1_b, peer_wq, peer_sub_keys, peer_u, peer_v, ln2_g, ln2_b):
    depth = w_in.shape[0]
    assert depth == 1, "single-layer trunk"
    alpha = (2.0 * depth) ** 0.25
    n_p, t_p, _ = x_prompt.shape
    n_s = x_sample.shape[0]
    p = dict(ln_in_g=ln_in_g, ln_in_b=ln_in_b, w_in_bf16=w_in[0].astype(BF16), mu_shift=mu_shift[0],
             rwkv_w0=rwkv_w0[0], rwkv_w2=rwkv_w2[0], rwkv_a0=rwkv_a0[0], rwkv_a2=rwkv_a2[0], rwkv_g2=rwkv_g2[0],
             rwkv_k_k=rwkv_k_k[0], rwkv_k_a=rwkv_k_a[0], rwkv_r_k=rwkv_r_k[0], rwkv_gn_g=rwkv_gn_g[0],
             rwkv_gn_b=rwkv_gn_b[0], w_pa=w_pa[0], w_pb=w_pb[0], w_o=w_o[0], ln1_g=ln1_g[0], ln1_b=ln1_b[0],
             peer_wq=peer_wq[0], peer_sub_keys=peer_sub_keys[0], ln2_g=ln2_g[0], ln2_b=ln2_b[0])
    sinks = attn_sinks[0]

    n_c = n_p + n_s
    pad = (-n_c) % 8
    c_all = jnp.concatenate([c_prompt, c_sample, jnp.zeros((pad, D_MODEL), F32)], axis=0)
    mod_all = _modulation(c_all, w_ada[0], b_ada[0])
    mod_p = mod_all[:n_p].reshape(n_p, 1, N_MOD * D_MODEL)
    mod_s = mod_all[n_p:n_c].reshape(1, n_s, N_MOD * D_MODEL)

    seg = min(PROMPT_SEGMENT, t_p)
    assert t_p % seg == 0
    first = min(PROMPT_FIRST_SEGMENT, seg)

    def segments(b):
        cuts = list(range(0, t_p + 1, seg))
        if b == 0 and first < seg:
            cuts.insert(1, first)
        return zip(cuts[:-1], cuts[1:])

    prompt_ids = [(b, lo, hi) for b in range(n_p) for lo, hi in segments(b)]
    assert all((hi - lo) % TOKEN_TB == 0 for _, lo, hi in prompt_ids)
    carry = {}

    def prompt_group(b, lo, hi):
        def prev_fn():
            return carry[b][0] if lo > 0 else jnp.zeros((1, 1, SHIFT_W), F32)

        def wkv_fn(r, lw, k, v, kk, kka):
            s0 = carry[b][1] if lo > 0 else jnp.zeros((1, H_A, HD_A, HD_A), F32)
            return _rwkv_chunk_scan(r, lw, k, v, kk, kka, s0)

        def attn_fn(pa):
            return _attn_band(pa, carry[b][2] if lo > 0 else None, sinks)

        return (x_prompt[b:b + 1, lo:hi], mod_p[b:b + 1], prev_fn, wkv_fn, attn_fn, TOKEN_TB, True)

    tu, tv = _pack_bf16_pairs(peer_u[0]), _pack_bf16_pairs(peer_v[0])

    def select(x, mod, prev_fn, wkv_fn, attn_fn, tb, sequential):
        ps, pattn, wkv_new, x1, h2, idx_rows, gate = _token_stage(
            x, mod, prev_fn, wkv_fn, attn_fn, p, alpha, tb, sequential)
        idx = idx_rows[0]
        hd = _sc_row_dots(tu, idx, h2[0])
        return ps, pattn, wkv_new, (hd, idx, gate, x1, mod)

    tv_slabs = tv.reshape(tv.shape[0], -1, 128)

    def weigh(sel):
        hd, idx, gate, x1, mod = sel
        wgt = _peer_weights(hd[None], gate, min(TOKEN_TB, idx.shape[0]))[0]
        x = VALUE_TC_PICKS
        ff_sc = _sc_weighted_row_sum(tv, idx, wgt, x)
        ff_tc = _peer_value_tc(tv_slabs, idx[:, :x], wgt[:, :x], min(VALUE_TC_TB, idx.shape[0]))
        return ff_sc, ff_tc, x1, mod

    def finish(wsum):
        ff_sc, ff_tc, x1, mod = wsum
        return _peer_out(x1, ff_sc[None], ff_tc[None], mod, p, alpha, min(TOKEN_TB, x1.shape[1]))

    xs = x_sample.reshape(1, n_s, D_MODEL)

    def prev_s():
        return state_shift[0].reshape(1, n_s, SHIFT_W)

    def wkv_s(r, lw, k, v, kk, kka):
        sq = lambda z: z.reshape(n_s, D_A)
        y, s = _rwkv_step(state_wkv[0], sq(r), sq(lw), sq(k), sq(kk), sq(kka), sq(v), STEP_NB)
        return y.reshape(1, n_s, D_A), s

    def attn_s(pa):
        o = _attn_cache(pa.reshape(n_s, 1, ATTN_W), cache_k_win[0], cache_v_win[0], sinks, STEP_NB)
        return o.reshape(1, n_s, D_B)

    n_g = len(prompt_ids) + 1
    sel, wsum, y_l = [None] * n_g, [None] * n_g, [None] * n_g
    for step in range(n_g + 2):
        if step < n_g - 1:
            b, lo, hi = prompt_ids[step]
            ps, pattn, wkv_new, sel[step] = select(*prompt_group(b, lo, hi))
            carry[b] = (ps[:, -1:], wkv_new, pattn[:, -WINDOW:])
        elif step == n_g - 1:
            ps_s, pattn_s, wkv_s_new, sel[step] = select(xs, mod_s, prev_s, wkv_s, attn_s, min(TOKEN_TB, n_s), False)
        if 0 <= step - 1 < n_g:
            wsum[step - 1] = weigh(sel[step - 1])
        if 0 <= step - 2 < n_g:
            y_l[step - 2] = finish(wsum[step - 2])
    y_s = y_l[n_g - 1]
    y_p = jnp.concatenate(y_l[:n_g - 1], axis=1).reshape(n_p, t_p, D_MODEL)
    shift_p = jnp.concatenate([carry[b][0][:, 0] for b in range(n_p)], axis=0)
    pattn_p = jnp.concatenate([carry[b][2] for b in range(n_p)], axis=0)
    wkv_p_new = jnp.concatenate([carry[b][1] for b in range(n_p)], axis=0)

    kv = lambda pa, o: pa[..., o:o + H_KV * HD_B]
    ko, vo = D_B, D_B + H_KV * HD_B
    k_win_p = kv(pattn_p, ko)[:, -WINDOW:].reshape(n_p, WINDOW, H_KV, HD_B)
    v_win_p = kv(pattn_p, vo)[:, -WINDOW:].reshape(n_p, WINDOW, H_KV, HD_B)
    k_new_s = kv(pattn_s, ko).reshape(n_s, 1, H_KV, HD_B)
    v_new_s = kv(pattn_s, vo).reshape(n_s, 1, H_KV, HD_B)
    k_win_s = jnp.concatenate([cache_k_win[0], k_new_s], axis=1)[:, -WINDOW:]
    v_win_s = jnp.concatenate([cache_v_win[0], v_new_s], axis=1)[:, -WINDOW:]
    return (y_p, y_s.reshape(n_s, 1, D_MODEL), wkv_p_new[None], wkv_s_new[None],
            shift_p[None], ps_s.reshape(n_s, SHIFT_W)[None],
            k_win_p[None], k_win_s[None], v_win_p[None], v_win_s[None])
```

```python
import functools
import math

import jax
import jax.numpy as jnp
from jax import lax
from jax.experimental import pallas as pl
from jax.experimental.pallas import tpu as pltpu
from jax.experimental.pallas import tpu_sc as plsc

F32 = jnp.float32
BF16 = jnp.bfloat16
I32 = jnp.int32
U32 = jnp.uint32

D_MODEL = 1024
H_A, HD_A = 8, 64
D_A = H_A * HD_A
D_LORA_W, D_LORA_A, D_LORA_G = 64, 64, 128
GN_EPS = 64e-5
H_Q, H_KV, HD_B = 8, 2, 64
G_Q = H_Q // H_KV
D_B = H_Q * HD_B
WINDOW = 128
N_KEYS = 128
PEER_HEADS, PEER_TOPK, PEER_HALF = 8, 16, 128
PEER_PICKS = PEER_HEADS * PEER_TOPK
N_MOD = 6
LN_EPS = 1e-5
NEG_INF = -1e30
OFF_WD = 3 * D_A
OFF_AD = OFF_WD + D_LORA_W
OFF_GD = OFF_AD + D_LORA_A
SHIFT_W = OFF_GD + D_LORA_G
ATTN_W = D_B + 2 * H_KV * HD_B
GATE_W = 2 * D_MODEL
D_IN = SHIFT_W + ATTN_W + GATE_W

VMEM_LIMIT = 48 * 1024 * 1024
SUBLANES = 8
RWKV_CHUNK = 64
SC_GATHER_ROWS = 32
SC_ACC_AHEAD = 3
SC_ACC_TOKENS = 16
SC_DOT_TOKENS = 16
SC_ACC_PANELS = 4
TOKEN_TB = 256
TOPK_TB = 128
VALUE_TC_PICKS = 64
VALUE_TC_TB = 16
PROMPT_SEGMENT = 2048
PROMPT_FIRST_SEGMENT = 512
STEP_NB = 8


def _cparams(*sem):
    return pltpu.CompilerParams(dimension_semantics=sem, vmem_limit_bytes=VMEM_LIMIT)


def _layernorm(x, g, b):
    mu = jnp.mean(x, -1, keepdims=True)
    xc = x - mu
    var = jnp.mean(xc * xc, -1, keepdims=True)
    return xc * lax.rsqrt(var + LN_EPS) * g + b


def _split(x):
    hi = x.astype(BF16)
    lo = (x - hi.astype(F32)).astype(BF16)
    return hi, lo


_NN = (((1,), (0,)), ((), ()))
_NT = (((1,), (1,)), ((), ()))
_TN = (((0,), (0,)), ((), ()))


def _dot3(a, b, dims=_NN):
    ah, al = _split(a)
    bh, bl = _split(b)
    d = functools.partial(lax.dot_general, dimension_numbers=dims, preferred_element_type=F32)
    return d(ah, bh) + d(ah, bl) + d(al, bh)


def _dot_exact_lhs(a_bf16, b, dims=_NN):
    b1 = b.astype(BF16)
    r1 = b - b1.astype(F32)
    b2 = r1.astype(BF16)
    b3 = (r1 - b2.astype(F32)).astype(BF16)
    d = functools.partial(lax.dot_general, dimension_numbers=dims, preferred_element_type=F32)
    return d(a_bf16, b1) + d(a_bf16, b2) + d(a_bf16, b3)


def _dotb(a, b, dims=_NN):
    return lax.dot_general(a.astype(BF16), b.astype(BF16), dims, preferred_element_type=F32)


def _rows(tb, width, col=0):
    return pl.BlockSpec((None, tb, width), lambda g, i: (g, i, col))


def _mod(mod, tb, col):
    if mod.shape[1] == 1:
        return pl.BlockSpec((None, 1, D_MODEL), lambda g, i: (g, 0, col))
    return pl.BlockSpec((None, tb, D_MODEL), lambda g, i: (g, i, col))


def _const(shape):
    n = len(shape)
    return pl.BlockSpec(shape, lambda g, i: (0,) * n)


def _row2(p):
    return p.reshape(1, -1).astype(F32)


def _mod_kernel(c_ref, w_ref, b_ref, o_ref):
    c = c_ref[...]
    a = c * jax.nn.sigmoid(c)
    o_ref[...] = _dot3(a, w_ref[...]) + b_ref[...]


def _modulation(c, w_ada, b_ada):
    n = c.shape[0]
    tn = D_MODEL
    return pl.pallas_call(
        _mod_kernel,
        grid=(w_ada.shape[1] // tn,),
        in_specs=[pl.BlockSpec((n, D_MODEL), lambda j: (0, 0)),
                  pl.BlockSpec((D_MODEL, tn), lambda j: (0, j)),
                  pl.BlockSpec((1, tn), lambda j: (0, j))],
        out_specs=pl.BlockSpec((n, tn), lambda j: (0, j)),
        out_shape=jax.ShapeDtypeStruct((n, w_ada.shape[1]), F32),
        compiler_params=_cparams("arbitrary"),
        name="modulation",
    )(c, w_ada, b_ada.reshape(1, -1))


def _inproj_kernel(x_ref, sh_ref, sc_ref, g_ref, b_ref, w_ref, ps_ref, pa_ref, pg_ref):
    xn = _layernorm(x_ref[...], g_ref[...], b_ref[...])
    h = (xn * (1.0 + sc_ref[...]) + sh_ref[...]).astype(BF16)
    ps_ref[...] = jnp.dot(h, w_ref[:, :SHIFT_W], preferred_element_type=F32)
    pa_ref[...] = jnp.dot(h, w_ref[:, SHIFT_W:SHIFT_W + ATTN_W], preferred_element_type=F32)
    pg_ref[...] = jnp.dot(h, w_ref[:, SHIFT_W + ATTN_W:], preferred_element_type=F32)


def _inproj(x, mod, ln_g, ln_b, w_in_bf16, tb):
    g, r, _ = x.shape
    shp = lambda w: jax.ShapeDtypeStruct((g, r, w), F32)
    return pl.pallas_call(
        _inproj_kernel,
        grid=(g, r // tb),
        in_specs=[_rows(tb, D_MODEL), _mod(mod, tb, 0), _mod(mod, tb, 1),
                  _const((1, D_MODEL)), _const((1, D_MODEL)), _const((D_MODEL, D_IN))],
        out_specs=[_rows(tb, SHIFT_W), _rows(tb, ATTN_W), _rows(tb, GATE_W)],
        out_shape=[shp(SHIFT_W), shp(ATTN_W), shp(GATE_W)],
        compiler_params=_cparams("parallel", "parallel"),
        name="inproj",
    )(x, mod, mod, _row2(ln_g), _row2(ln_b), w_in_bf16)


def _softplus(x):
    return jnp.maximum(x, 0.0) + jnp.log1p(jnp.exp(-jnp.abs(x)))


def _rwkv_prep_kernel(ps_ref, prev_ref, first_ref, mu_ref, w0_ref, w2_ref, a0_ref, a2_ref, g2_ref, kk_w_ref, ka_w_ref,
                      hsum_ref, r_ref, lw_ref, k_ref, v_ref, kk_ref, kka_ref, g_ref, *, sequential):
    ps = ps_ref[...]
    if sequential:
        before = jnp.where(pl.program_id(1) == 0, first_ref[...], prev_ref[SUBLANES - 1:SUBLANES, :])
        row = lax.broadcasted_iota(I32, ps.shape, 0)
        prev = jnp.where(row == 0, before, pltpu.roll(ps, 1, 0))
    else:
        prev = prev_ref[...]
    xs = ps + (prev - ps) * mu_ref[...]
    r = xs[:, 0:D_A]
    k = xs[:, D_A:2 * D_A]
    v = xs[:, 2 * D_A:3 * D_A]
    wd = xs[:, OFF_WD:OFF_AD]
    ad = xs[:, OFF_AD:OFF_GD]
    gd = xs[:, OFF_GD:SHIFT_W]
    z = w0_ref[...] + _dot3(jnp.tanh(wd), w2_ref[...])
    w_log = -_softplus(-z) - 0.5
    a = jax.nn.sigmoid(a0_ref[...] + _dot3(ad, a2_ref[...]))
    kk = k * kk_w_ref[...]
    ss = _dot3(kk * kk, hsum_ref[...])
    kk = kk / jnp.maximum(jnp.sqrt(ss), 1e-12)
    r_ref[...] = r
    lw_ref[...] = -jnp.exp(w_log)
    k_ref[...] = k * (1.0 + (a - 1.0) * ka_w_ref[...])
    v_ref[...] = v
    kk_ref[...] = kk
    kka_ref[...] = kk * a
    g_ref[...] = _dot3(jax.nn.sigmoid(gd), g2_ref[...])


def _head_sum_matrix():
    h = jnp.arange(D_A) // HD_A
    return (h[:, None] == h[None, :]).astype(F32)


def _rwkv_prep(ps, prev, p, tb, sequential):
    g, r, _ = ps.shape
    shp = jax.ShapeDtypeStruct((g, r, D_A), F32)
    if sequential:
        per = tb // SUBLANES
        prev_args = (ps, prev)
        prev_specs = [pl.BlockSpec((None, SUBLANES, SHIFT_W), lambda gi, i: (gi, jnp.maximum(i * per - 1, 0), 0)),
                      pl.BlockSpec((None, 1, SHIFT_W), lambda gi, i: (gi, 0, 0))]
    else:
        prev_args = (prev, prev[:, :1])
        prev_specs = [_rows(tb, SHIFT_W), pl.BlockSpec((None, 1, SHIFT_W), lambda gi, i: (gi, 0, 0))]
    return pl.pallas_call(
        functools.partial(_rwkv_prep_kernel, sequential=sequential),
        grid=(g, r // tb),
        in_specs=[_rows(tb, SHIFT_W)] + prev_specs + [_const((1, SHIFT_W)),
                  _const((1, D_A)), _const((D_LORA_W, D_A)), _const((1, D_A)), _const((D_LORA_A, D_A)),
                  _const((D_LORA_G, D_A)), _const((1, D_A)), _const((1, D_A)), _const((D_A, D_A))],
        out_specs=[_rows(tb, D_A)] * 7,
        out_shape=[shp] * 7,
        compiler_params=_cparams("parallel", "parallel"),
        name="rwkv_prep",
    )(ps, *prev_args, _row2(p["mu_shift"]), _row2(p["rwkv_w0"]), p["rwkv_w2"], _row2(p["rwkv_a0"]), p["rwkv_a2"],
      p["rwkv_g2"], _row2(p["rwkv_k_k"]), _row2(p["rwkv_k_a"]), _head_sum_matrix())


def _rwkv_chunk_kernel(r_ref, lw_ref, k_ref, v_ref, kk_ref, kka_ref, s0_ref, y_ref, s_ref):
    c = RWKV_CHUNK

    @pl.when(pl.program_id(1) == 0)
    def _():
        s_ref[...] = s0_ref[...]

    row = lax.broadcasted_iota(I32, (c, c), 0)
    col = lax.broadcasted_iota(I32, (c, c), 1)
    tril = row >= col
    stril = row > col
    lw = lw_ref[...]
    cum = _dot_exact_lhs(tril.astype(BF16), lw)
    cum_end = cum[c - 1:c, :]
    g_inv = jnp.exp(-cum)
    g_end = jnp.exp(cum_end - cum)
    a_hat = -kk_ref[...] * jnp.exp(cum - lw)
    b_hat = kka_ref[...] * g_inv
    k_hat = k_ref[...] * g_inv
    r_til = r_ref[...] * jnp.exp(cum)
    b_end = kka_ref[...] * g_end
    k_end = k_ref[...] * g_end
    gam_end = jnp.exp(cum_end)
    v_all = v_ref[...]
    s_all = s_ref[...]
    n_steps = int(math.log2(c))
    heads = range(H_A)
    sl = [slice(h * HD_A, (h + 1) * HD_A) for h in heads]
    vh = [v_all[:, sl[h]] for h in heads]
    ar = [jnp.concatenate([a_hat[:, sl[h]], r_til[:, sl[h]]], axis=0) for h in heads]
    bk = [jnp.concatenate([b_hat[:, sl[h]], k_hat[:, sl[h]]], axis=0) for h in heads]
    x = [_dot3(ar[h], bk[h], _NT) for h in heads]
    ars = [_dot3(ar[h], s_all[h], _NT) for h in heads]
    a_ak = [jnp.where(stril, x[h][:c, c:], 0.0) for h in heads]
    n = [jnp.where(stril, x[h][:c, :c], 0.0) for h in heads]
    u = [ars[h][:c] + _dot3(a_ak[h], vh[h]) for h in heads]
    for it in range(n_steps):
        u = [u[h] + _dotb(n[h], u[h]) for h in heads]
        if it + 1 < n_steps:
            n = [_dotb(n[h], n[h]) for h in heads]
    uv = [jnp.concatenate([u[h], vh[h]], axis=0) for h in heads]
    a_r = [jnp.concatenate([jnp.where(tril, x[h][c:, :c], 0.0), jnp.where(tril, x[h][c:, c:], 0.0)], axis=1)
           for h in heads]
    y = [ars[h][c:] + _dot3(a_r[h], uv[h]) for h in heads]
    bke = [jnp.concatenate([b_end[:, sl[h]], k_end[:, sl[h]]], axis=0) for h in heads]
    s_new = [s_all[h] * gam_end[:, sl[h]] + _dot3(uv[h], bke[h], _TN) for h in heads]
    for h in heads:
        y_ref[:, sl[h]] = y[h]
        s_ref[h] = s_new[h]


def _rwkv_chunk_scan(r, lw, k, v, kk, kka, s0):
    n, t, _ = r.shape
    c = RWKV_CHUNK
    seq = pl.BlockSpec((None, c, D_A), lambda b, i: (b, i, 0))
    st = pl.BlockSpec((None, H_A, HD_A, HD_A), lambda b, i: (b, 0, 0, 0))
    return pl.pallas_call(
        _rwkv_chunk_kernel,
        grid=(n, t // c),
        in_specs=[seq] * 6 + [st],
        out_specs=[seq, st],
        out_shape=[jax.ShapeDtypeStruct((n, t, D_A), F32), jax.ShapeDtypeStruct((n, H_A, HD_A, HD_A), F32)],
        compiler_params=_cparams("parallel", "arbitrary"),
        name="rwkv_chunk_scan",
    )(r, lw, k, v, kk, kka, s0)


def _rwkv_step_kernel(s_ref, r_ref, lw_ref, k_ref, kk_ref, kka_ref, v_ref, y_ref, so_ref):
    s = s_ref[...]
    sa = jnp.sum(s * (-kk_ref[...]), axis=-1, keepdims=True)
    s = s * jnp.exp(lw_ref[...]) + sa * kka_ref[...] + v_ref[...] * k_ref[...]
    so_ref[...] = s
    y_ref[...] = jnp.sum(s * r_ref[...], axis=-1, keepdims=True)


def _rwkv_step(s0, r, lw, k, kk, kka, v, nb):
    n = s0.shape[0]
    key = lambda z: z.reshape(n, H_A, 1, HD_A)
    st = pl.BlockSpec((nb, H_A, HD_A, HD_A), lambda i: (i, 0, 0, 0))
    ks = pl.BlockSpec((nb, H_A, 1, HD_A), lambda i: (i, 0, 0, 0))
    vs = pl.BlockSpec((nb, H_A, HD_A, 1), lambda i: (i, 0, 0, 0))
    y, s = pl.pallas_call(
        _rwkv_step_kernel,
        grid=(n // nb,),
        in_specs=[st, ks, ks, ks, ks, ks, vs],
        out_specs=[vs, st],
        out_shape=[jax.ShapeDtypeStruct((n, H_A, HD_A, 1), F32), jax.ShapeDtypeStruct(s0.shape, F32)],
        compiler_params=_cparams("parallel"),
        name="rwkv_step",
    )(s0, key(r), key(lw), key(k), key(kk), key(kka), v.reshape(n, H_A, HD_A, 1))
    return y.reshape(n, D_A), s


def _rwkv_post_kernel(y_ref, r_ref, k_ref, v_ref, g_ref, gn_g_ref, gn_b_ref, rk_ref, hsum_ref, o_ref):
    y = y_ref[...]
    hs = hsum_ref[...]
    mu = _dot3(y, hs) * (1.0 / HD_A)
    yc = y - mu
    var = _dot3(yc * yc, hs) * (1.0 / HD_A)
    yn = yc * lax.rsqrt(var + GN_EPS) * gn_g_ref[...] + gn_b_ref[...]
    bonus = _dot3(r_ref[...] * k_ref[...] * rk_ref[...], hs) * v_ref[...]
    o_ref[...] = (yn + bonus) * g_ref[...]


def _rwkv_post(y, r, k, v, g, p, tb):
    gg, rr, _ = y.shape
    return pl.pallas_call(
        _rwkv_post_kernel,
        grid=(gg, rr // tb),
        in_specs=[_rows(tb, D_A)] * 5 + [_const((1, D_A))] * 3 + [_const((D_A, D_A))],
        out_specs=_rows(tb, D_A),
        out_shape=jax.ShapeDtypeStruct((gg, rr, D_A), F32),
        compiler_params=_cparams("parallel", "parallel"),
        name="rwkv_post",
    )(y, r, k, v, g, _row2(p["rwkv_gn_g"]), _row2(p["rwkv_gn_b"]), _row2(p["rwkv_r_k"]), _head_sum_matrix())


def _sink_softmax(s, sink):
    m = jnp.maximum(jnp.max(s, axis=-1, keepdims=True), sink)
    p = jnp.exp(s - m)
    den = jnp.sum(p, axis=-1, keepdims=True) + jnp.exp(sink - m)
    return p / den


def _attn_band_kernel(cur_ref, prev_ref, carry_ref, sink_ref, o_ref, *, has_carry):
    blk = WINDOW
    i = pl.program_id(1)
    cur = cur_ref[...]
    prev = jnp.where(i == 0, carry_ref[...], prev_ref[...])
    qi = lax.broadcasted_iota(I32, (G_Q * blk, 2 * blk), 0) % blk
    kj = lax.broadcasted_iota(I32, (G_Q * blk, 2 * blk), 1)
    rel = blk + qi - kj
    valid = (rel >= 0) & (rel <= WINDOW)
    if not has_carry:
        valid = valid & ((kj >= blk) | (i > 0))
    relf = rel.astype(F32)
    gidx = lax.broadcasted_iota(I32, (G_Q * blk, 1), 0) // blk
    for kvh in range(H_KV):
        q4 = jnp.concatenate([cur[:, (kvh * G_Q + g) * HD_B:(kvh * G_Q + g + 1) * HD_B] for g in range(G_Q)], axis=0)
        ko = D_B + kvh * HD_B
        vo = D_B + H_KV * HD_B + kvh * HD_B
        kmat = jnp.concatenate([prev[:, ko:ko + HD_B], cur[:, ko:ko + HD_B]], axis=0)
        vmat = jnp.concatenate([prev[:, vo:vo + HD_B], cur[:, vo:vo + HD_B]], axis=0)
        slope = jnp.zeros((G_Q * blk, 1), F32)
        sink = jnp.zeros((G_Q * blk, 1), F32)
        for g in range(G_Q):
            hq = kvh * G_Q + g
            slope = jnp.where(gidx == g, 2.0 ** (-8.0 * (hq + 1) / H_Q), slope)
            sink = jnp.where(gidx == g, sink_ref[hq], sink)
        s = _dotb(q4, kmat, _NT) * (HD_B ** -0.5)
        s = jnp.where(valid, s - slope * relf, NEG_INF)
        p = _sink_softmax(s, sink)
        o = _dotb(p, vmat)
        for g in range(G_Q):
            hq = kvh * G_Q + g
            o_ref[:, hq * HD_B:(hq + 1) * HD_B] = o[g * blk:(g + 1) * blk]


def _attn_band(pattn, carry, sinks):
    n, t, _ = pattn.shape
    blk = WINDOW
    has_carry = carry is not None
    if not has_carry:
        carry = jnp.zeros((n, blk, ATTN_W), F32)
    return pl.pallas_call(
        functools.partial(_attn_band_kernel, has_carry=has_carry),
        grid=(n, t // blk),
        in_specs=[pl.BlockSpec((None, blk, ATTN_W), lambda b, i: (b, i, 0)),
                  pl.BlockSpec((None, blk, ATTN_W), lambda b, i: (b, jnp.maximum(i - 1, 0), 0)),
                  pl.BlockSpec((None, blk, ATTN_W), lambda b, i: (b, 0, 0)),
                  pl.BlockSpec(memory_space=pltpu.SMEM)],
        out_specs=pl.BlockSpec((None, blk, D_B), lambda b, i: (b, i, 0)),
        out_shape=jax.ShapeDtypeStruct((n, t, D_B), F32),
        compiler_params=_cparams("parallel", "parallel"),
        name="attn_band",
    )(pattn, pattn, carry, sinks.astype(F32))


def _attn_cache_kernel(cur_ref, kc_ref, vc_ref, sink_ref, o_ref, *, nb):
    relc = (WINDOW - lax.broadcasted_iota(I32, (G_Q, WINDOW), 1)).astype(F32)
    gidx = lax.broadcasted_iota(I32, (G_Q, 1), 0)
    for b in range(nb):
        cur = cur_ref[b]
        for kvh in range(H_KV):
            q4 = jnp.concatenate([cur[:, (kvh * G_Q + g) * HD_B:(kvh * G_Q + g + 1) * HD_B] for g in range(G_Q)], axis=0)
            ko = D_B + kvh * HD_B
            vo = D_B + H_KV * HD_B + kvh * HD_B
            k_new = cur[:, ko:ko + HD_B]
            v_new = cur[:, vo:vo + HD_B]
            kc = kc_ref[b, :, kvh * HD_B:(kvh + 1) * HD_B]
            vc = vc_ref[b, :, kvh * HD_B:(kvh + 1) * HD_B]
            slope = jnp.zeros((G_Q, 1), F32)
            sink = jnp.zeros((G_Q, 1), F32)
            for g in range(G_Q):
                hq = kvh * G_Q + g
                slope = jnp.where(gidx == g, 2.0 ** (-8.0 * (hq + 1) / H_Q), slope)
                sink = jnp.where(gidx == g, sink_ref[hq], sink)
            scale = HD_B ** -0.5
            sc = _dotb(q4, kc, _NT) * scale - slope * relc
            sn = jnp.sum(q4.astype(BF16).astype(F32) * k_new.astype(BF16).astype(F32), axis=-1, keepdims=True) * scale
            m = jnp.maximum(jnp.maximum(jnp.max(sc, axis=-1, keepdims=True), sn), sink)
            pc = jnp.exp(sc - m)
            pn = jnp.exp(sn - m)
            den = jnp.sum(pc, axis=-1, keepdims=True) + pn + jnp.exp(sink - m)
            o = (_dotb(pc / den, vc) + (pn / den).astype(BF16).astype(F32) * v_new.astype(BF16).astype(F32))
            for g in range(G_Q):
                hq = kvh * G_Q + g
                o_ref[b, :, hq * HD_B:(hq + 1) * HD_B] = o[g:g + 1]


def _attn_cache(pattn, k_buf, v_buf, sinks, nb):
    n = pattn.shape[0]
    kc = k_buf.reshape(n, WINDOW, H_KV * HD_B)
    vc = v_buf.reshape(n, WINDOW, H_KV * HD_B)
    return pl.pallas_call(
        functools.partial(_attn_cache_kernel, nb=nb),
        grid=(n // nb,),
        in_specs=[pl.BlockSpec((nb, 1, ATTN_W), lambda i: (i, 0, 0)),
                  pl.BlockSpec((nb, WINDOW, H_KV * HD_B), lambda i: (i, 0, 0)),
                  pl.BlockSpec((nb, WINDOW, H_KV * HD_B), lambda i: (i, 0, 0)),
                  pl.BlockSpec(memory_space=pltpu.SMEM)],
        out_specs=pl.BlockSpec((nb, 1, D_B), lambda i: (i, 0, 0)),
        out_shape=jax.ShapeDtypeStruct((n, 1, D_B), F32),
        compiler_params=_cparams("parallel"),
        name="attn_cache",
    )(pattn, kc, vc, sinks.astype(F32))


def _merge_kernel(x_ref, ya_ref, ob_ref, pg_ref, gtm_ref, shf_ref, scf_ref, lng_ref, lnb_ref, l1g_ref, l1b_ref,
                  wpa_ref, wpb_ref, wo_ref, wq_ref, sk_ref, x1_ref, h2_ref, st_ref, *, alpha):
    ya = jnp.dot(ya_ref[...].astype(BF16), wpa_ref[...], preferred_element_type=F32)
    yb = jnp.dot(ob_ref[...].astype(BF16), wpb_ref[...], preferred_element_type=F32)
    pg = pg_ref[...]
    merged = jax.nn.sigmoid(pg[:, :D_MODEL]) * ya + jax.nn.sigmoid(pg[:, D_MODEL:]) * yb
    mix = jnp.dot(merged.astype(BF16), wo_ref[...], preferred_element_type=F32)
    xn = _layernorm(x_ref[...], lng_ref[...], lnb_ref[...])
    x1 = _layernorm(alpha * xn + gtm_ref[...] * mix, l1g_ref[...], l1b_ref[...])
    x1_ref[...] = x1
    h2 = x1 * (1.0 + scf_ref[...]) + shf_ref[...]
    h2_ref[...] = h2
    q = jnp.dot(h2.astype(BF16), wq_ref[...], preferred_element_type=F32)
    for hc in range(2 * PEER_HEADS):
        st_ref[hc] = _dot3(sk_ref[hc % 2], q[:, hc * PEER_HALF:(hc + 1) * PEER_HALF], _NT)


def _merge(x, ya, ob, pg, mod, p, alpha, tb):
    g, r, _ = x.shape
    return pl.pallas_call(
        functools.partial(_merge_kernel, alpha=alpha),
        grid=(g, r // tb),
        in_specs=[_rows(tb, D_MODEL), _rows(tb, D_A), _rows(tb, D_B), _rows(tb, GATE_W),
                  _mod(mod, tb, 2), _mod(mod, tb, 3), _mod(mod, tb, 4)]
                 + [_const((1, D_MODEL))] * 4
                 + [_const((D_A, D_MODEL)), _const((D_B, D_MODEL)), _const((D_MODEL, D_MODEL)),
                    _const((D_MODEL, 2 * PEER_HEADS * PEER_HALF)), _const((2, N_KEYS, PEER_HALF))],
        out_specs=[_rows(tb, D_MODEL), _rows(tb, D_MODEL),
                   pl.BlockSpec((None, 2 * PEER_HEADS, N_KEYS, tb), lambda gi, i: (gi, 0, 0, i))],
        out_shape=[jax.ShapeDtypeStruct((g, r, D_MODEL), F32), jax.ShapeDtypeStruct((g, r, D_MODEL), F32),
                   jax.ShapeDtypeStruct((g, 2 * PEER_HEADS, N_KEYS, r), F32)],
        compiler_params=_cparams("parallel", "parallel"),
        name="merge_ln1_peer_scores",
    )(x, ya, ob, pg, mod, mod, mod, _row2(p["ln_in_g"]), _row2(p["ln_in_b"]), _row2(p["ln1_g"]), _row2(p["ln1_b"]),
      p["w_pa"].astype(BF16), p["w_pb"].astype(BF16), p["w_o"].astype(BF16), p["peer_wq"].astype(BF16),
      p["peer_sub_keys"])


def _extract_top(problems, n_rows, tb):
    rio = lax.broadcasted_iota(I32, (n_rows, tb), 0).astype(F32)
    vals = [v for v, _ in problems]
    tops = [([], [], []) for _ in problems]
    for _ in range(PEER_TOPK):
        for n, (_, payload) in enumerate(problems):
            m = jnp.max(vals[n], axis=0, keepdims=True)
            i = jnp.min(jnp.where(vals[n] == m, rio, float(n_rows)), axis=0, keepdims=True)
            sel = rio == i
            tops[n][0].append(m)
            tops[n][1].append(i)
            if payload is not None:
                tops[n][2].append(jnp.max(jnp.where(sel, payload, -1.0), axis=0, keepdims=True))
            vals[n] = jnp.where(sel, -jnp.inf, vals[n])
    cat = lambda z: jnp.concatenate(z, axis=0) if z else None
    return [(cat(v), cat(i), cat(pl_)) for v, i, pl_ in tops]


def _pair_candidates(v1, i1, v2, i2, tb):
    k = PEER_TOPK
    sub = 8
    eid = lambda a0, a1, b0, b1: i1[a0:a1] * float(N_KEYS) + i2[b0:b1]
    vals = [v1[0:1] + v2, v1[1:2] + v2[0:sub]]
    ids = [eid(0, 1, 0, k), eid(1, 2, 0, sub)]
    brow = lax.broadcasted_iota(I32, (sub, tb), 0)
    for a in range(2, sub):
        vals.append(jnp.where(brow < k // (a + 1), v1[a:a + 1] + v2[0:sub], -jnp.inf))
        ids.append(eid(a, a + 1, 0, sub))
    vals.append(v1[sub:k] + v2[0:1])
    ids.append(eid(sub, k, 0, 1))
    return jnp.concatenate(vals, axis=0), jnp.concatenate(ids, axis=0)


def _topk_kernel(s_ref, idx_ref, gate_ref, idx_t, gate_t):
    tb = s_ref.shape[-1]

    def head_pair(hp, carry):
        tops = _extract_top([(s_ref[4 * hp + n], None) for n in range(4)], N_KEYS, tb)
        cands = [_pair_candidates(tops[2 * n][0], tops[2 * n][1], tops[2 * n + 1][0], tops[2 * n + 1][1], tb)
                 for n in range(2)]
        picked = _extract_top(cands, cands[0][0].shape[0], tb)
        for n, (sc, _, ex) in enumerate(picked):
            pexp = jnp.exp(sc - sc[0:1])
            rows = pl.ds(pl.multiple_of((2 * hp + n) * PEER_TOPK, PEER_TOPK), PEER_TOPK)
            idx_t[rows, :] = ex
            gate_t[rows, :] = pexp / jnp.sum(pexp, axis=0, keepdims=True)
        return carry

    lax.fori_loop(0, PEER_HEADS // 2, head_pair, 0)
    idx_ref[...] = jnp.transpose(idx_t[...]).astype(I32)
    gate_ref[...] = jnp.transpose(gate_t[...])


def _topk(scores_t, tb):
    g, _, _, r = scores_t.shape
    out = pl.BlockSpec((None, tb, PEER_PICKS), lambda gi, i: (gi, i, 0))
    return pl.pallas_call(
        _topk_kernel,
        grid=(g, r // tb),
        in_specs=[pl.BlockSpec((None, 2 * PEER_HEADS, N_KEYS, tb), lambda gi, i: (gi, 0, 0, i))],
        out_specs=[out, out],
        out_shape=[jax.ShapeDtypeStruct((g, r, PEER_PICKS), I32), jax.ShapeDtypeStruct((g, r, PEER_PICKS), F32)],
        scratch_shapes=[pltpu.VMEM((PEER_PICKS, tb), F32), pltpu.VMEM((PEER_PICKS, tb), F32)],
        compiler_params=_cparams("parallel", "parallel"),
        name="peer_topk",
    )(scores_t)


def _sc_mesh():
    info = plsc.get_sparse_core_info()
    mesh = plsc.VectorSubcoreMesh(core_axis_name="c", subcore_axis_name="s")
    return info.num_cores, info.num_subcores, info.num_lanes, mesh


def _sc_row_dots(table, idx, x):
    nc, ns, lanes, mesh = _sc_mesh()
    nw = nc * ns
    n_tok, picks = idx.shape
    ww = table.shape[1]
    rr = SC_GATHER_ROWS
    nq = picks // rr
    tpw = n_tok // nw
    kt = min(SC_DOT_TOKENS, tpw)
    n_outer = tpw // kt
    npan = SC_ACC_PANELS
    pw = ww // npan
    nv = pw // lanes
    ahead = SC_ACC_AHEAD
    assert tpw * nw == n_tok and n_outer * kt == tpw and nq * rr == picks and ahead < nq and nv * lanes == pw
    assert npan * lanes <= 128
    cp = pltpu.CompilerParams(needs_layout_passes=False)
    buf = pltpu.VMEM((rr, ww), table.dtype)

    @functools.partial(
        pl.kernel, mesh=mesh, out_type=jax.ShapeDtypeStruct((n_tok * picks,), F32),
        scratch_types=[pltpu.VMEM((kt * nq, rr), I32), pltpu.VMEM((kt, 2 * ww), F32), pltpu.VMEM((picks, 128), F32),
                       pltpu.VMEM((kt * picks,), F32)] + [buf] * nq + [pltpu.SemaphoreType.DMA] * nq,
        compiler_params=cp, name="peer_row_dots",
    )
    def kern(tab_hbm, idx_hbm, x_hbm, out_hbm, idx_v, x_v, part_v, hd_v, *scratch):
        wid = lax.axis_index("s") * nc + lax.axis_index("c")
        rows, sems = scratch[:nq], scratch[nq:]
        last_lane = lax.iota(I32, lanes) == lanes - 1

        def chunk_copy(t, q):
            return pltpu.make_async_copy(tab_hbm.at[idx_v.at[t * nq + q]], rows[q], sems[q])

        @pl.loop(0, n_outer)
        def _(o):
            tok0 = wid * tpw + o * kt
            pltpu.sync_copy(idx_hbm.at[pl.ds(tok0 * nq, kt * nq)], idx_v)
            pltpu.sync_copy(x_hbm.at[pl.ds(tok0, kt)], x_v)
            for q in range(ahead):
                chunk_copy(0, q).start()

            @pl.loop(0, kt)
            def _(t):
                for q in range(nq):
                    if q + ahead < nq:
                        chunk_copy(t, q + ahead).start()
                    else:
                        @pl.when(t + 1 < kt)
                        def _():
                            chunk_copy(t + 1, q + ahead - nq).start()
                    chunk_copy(t, q).wait()
                    for pan in range(npan):
                        lo_at = lambda v: pl.ds(pan * pw + v * lanes, lanes)
                        hi_at = lambda v: pl.ds(ww + pan * pw + v * lanes, lanes)
                        xlo = [x_v[t, lo_at(v)] for v in range(nv)]
                        xhi = [x_v[t, hi_at(v)] for v in range(nv)]

                        @plsc.parallel_loop(0, rr, unroll=2)
                        def _(j):
                            terms = []
                            for v in range(nv):
                                word = rows[q][j, lo_at(v)]
                                terms.append(plsc.bitcast(word << 16, F32) * xlo[v])
                                terms.append(plsc.bitcast(word & jnp.uint32(0xFFFF0000), F32) * xhi[v])
                            while len(terms) > 1:
                                terms = [terms[i] + terms[i + 1] for i in range(0, len(terms), 2)]
                            part_v[q * rr + j, pl.ds(pan * lanes, lanes)] = terms[0]

                @plsc.parallel_loop(0, picks, unroll=2)
                def _(e):
                    s = part_v[e, pl.ds(0, lanes)]
                    for pan in range(1, npan):
                        s = s + part_v[e, pl.ds(pan * lanes, lanes)]
                    slot = jnp.full((lanes,), t * picks, I32) + e
                    plsc.store_scatter(hd_v, [slot], plsc.cumsum(s), mask=last_lane)

            pltpu.sync_copy(hd_v, out_hbm.at[pl.ds(tok0 * picks, kt * picks)])

    return kern(table, idx.reshape(n_tok * nq, rr), x).reshape(n_tok, picks)


def _sc_weighted_row_sum(table, idx, wgt, pick0):
    nc, ns, lanes, mesh = _sc_mesh()
    nw = nc * ns
    n_tok, picks = idx.shape
    ww = table.shape[1]
    rr = SC_GATHER_ROWS
    nq_all = picks // rr
    q0 = pick0 // rr
    nq = nq_all - q0
    tpw = n_tok // nw
    kt = min(SC_ACC_TOKENS, tpw)
    n_outer = tpw // kt
    pw = ww // SC_ACC_PANELS
    nv = pw // lanes
    ahead = min(SC_ACC_AHEAD, nq - 1)
    assert tpw * nw == n_tok and n_outer * kt == tpw and nq_all * rr == picks and q0 * rr == pick0 and nv * lanes == pw
    assert ahead >= 1
    cp = pltpu.CompilerParams(needs_layout_passes=False)
    buf = pltpu.VMEM((rr, ww), table.dtype)

    @functools.partial(
        pl.kernel, mesh=mesh, out_type=jax.ShapeDtypeStruct((n_tok, 2 * ww), F32),
        scratch_types=[pltpu.VMEM((kt * nq_all, rr), I32), pltpu.VMEM((kt, picks), F32), pltpu.VMEM((kt, 2 * ww), F32)]
                      + [buf] * nq + [pltpu.SemaphoreType.DMA] * nq,
        compiler_params=cp, name="peer_weighted_row_sum",
    )
    def kern(tab_hbm, idx_hbm, w_hbm, out_hbm, idx_v, w_v, acc_v, *scratch):
        wid = lax.axis_index("s") * nc + lax.axis_index("c")
        rows, sems = scratch[:nq], scratch[nq:]

        def chunk_copy(t, q):
            return pltpu.make_async_copy(tab_hbm.at[idx_v.at[t * nq_all + q0 + q]], rows[q], sems[q])

        @pl.loop(0, n_outer)
        def _(o):
            tok0 = wid * tpw + o * kt
            pltpu.sync_copy(idx_hbm.at[pl.ds(tok0 * nq_all, kt * nq_all)], idx_v)
            pltpu.sync_copy(w_hbm.at[pl.ds(tok0, kt)], w_v)
            for q in range(ahead):
                chunk_copy(0, q).start()

            @pl.loop(0, kt)
            def _(t):
                tvec = jnp.full((lanes,), t, I32)
                for q in range(nq):
                    b = q
                    if q + ahead < nq:
                        chunk_copy(t, q + ahead).start()
                    else:
                        @pl.when(t + 1 < kt)
                        def _():
                            chunk_copy(t + 1, q + ahead - nq).start()
                    chunk_copy(t, q).wait()
                    for pan in range(SC_ACC_PANELS):
                        lo_at = lambda c: pl.ds(pan * pw + c * lanes, lanes)
                        hi_at = lambda c: pl.ds(ww + pan * pw + c * lanes, lanes)

                        def row_body(j, acc):
                            wj = plsc.load_gather(w_v, [tvec, jnp.full((lanes,), pick0 + q * rr, I32) + j])
                            new = []
                            for c in range(nv):
                                word = rows[b][j, lo_at(c)]
                                new.append(acc[2 * c] + wj * plsc.bitcast(word << 16, F32))
                                new.append(acc[2 * c + 1] + wj * plsc.bitcast(word & jnp.uint32(0xFFFF0000), F32))
                            return tuple(new)

                        if q == 0:
                            init = tuple(jnp.zeros((lanes,), F32) for _ in range(2 * nv))
                        else:
                            init = tuple(acc_v[t, at(c)] for c in range(nv) for at in (lo_at, hi_at))
                        acc = lax.fori_loop(0, rr, row_body, init)
                        for c in range(nv):
                            acc_v[t, lo_at(c)] = acc[2 * c]
                            acc_v[t, hi_at(c)] = acc[2 * c + 1]

            pltpu.sync_copy(acc_v, out_hbm.at[pl.ds(tok0, kt)])

    return kern(table, idx.reshape(n_tok * nq_all, rr), wgt)


def _pack_bf16_pairs(t):
    half = t.shape[1] // 2
    b = lax.bitcast_convert_type(t.astype(BF16), jnp.uint16).astype(U32)
    return b[:, :half] | (b[:, half:] << 16)


def _gelu_erf(x):
    return 0.5 * x * (1.0 + lax.erf(x * (2.0 ** -0.5)))


def _peer_weights_kernel(hd_ref, gate_ref, w_ref):
    w_ref[...] = _gelu_erf(hd_ref[...]) * gate_ref[...]


def _peer_weights(hd, gate, tb):
    g, r, _ = hd.shape
    return pl.pallas_call(
        _peer_weights_kernel,
        grid=(g, r // tb),
        in_specs=[_rows(tb, PEER_PICKS), _rows(tb, PEER_PICKS)],
        out_specs=_rows(tb, PEER_PICKS),
        out_shape=jax.ShapeDtypeStruct((g, r, PEER_PICKS), F32),
        compiler_params=_cparams("parallel", "parallel"),
        name="peer_weights",
    )(hd, gate)


def _peer_value_tc_kernel(idx_ref, w_ref, tab_hbm, o_ref, tab_v, sem, *, tb, picks):
    @pl.when(pl.program_id(0) == 0)
    def _():
        load = pltpu.make_async_copy(tab_hbm, tab_v, sem)
        load.start()
        load.wait()

    slab = tab_v.shape[1:]
    for t in range(tb):
        lo = jnp.zeros(slab, F32)
        hi = jnp.zeros(slab, F32)
        for e in range(picks):
            word = tab_v[idx_ref[t, e]]
            wt = w_ref[t, e]
            lo = lo + wt * pltpu.bitcast(word << 16, F32)
            hi = hi + wt * pltpu.bitcast(word & jnp.uint32(0xFFFF0000), F32)
        o_ref[t, :, pl.ds(0, 128)] = lo
        o_ref[t, :, pl.ds(128, 128)] = hi


def _peer_value_tc(table3, idx, wgt, tb):
    r, picks = idx.shape
    v, s, _ = table3.shape
    part = pl.pallas_call(
        functools.partial(_peer_value_tc_kernel, tb=tb, picks=picks),
        grid=(r // tb,),
        in_specs=[pl.BlockSpec((tb, picks), lambda i: (i, 0), memory_space=pltpu.SMEM),
                  pl.BlockSpec((tb, picks), lambda i: (i, 0), memory_space=pltpu.SMEM),
                  pl.BlockSpec(memory_space=pl.ANY)],
        out_specs=pl.BlockSpec((tb, s, 256), lambda i: (i, 0, 0)),
        out_shape=jax.ShapeDtypeStruct((r, s, 256), F32),
        scratch_shapes=[pltpu.VMEM((v, s, 128), table3.dtype), pltpu.SemaphoreType.DMA],
        compiler_params=_cparams("arbitrary"),
        name="peer_value_tc",
    )(idx, wgt, table3)
    return jnp.concatenate([part[:, :, :128].reshape(r, s * 128), part[:, :, 128:].reshape(r, s * 128)], axis=1)


def _peer_out_kernel(x1_ref, ff_a_ref, ff_b_ref, gtf_ref, l2g_ref, l2b_ref, o_ref, *, alpha):
    ff = ff_a_ref[...] + ff_b_ref[...]
    o_ref[...] = _layernorm(alpha * x1_ref[...] + gtf_ref[...] * ff, l2g_ref[...], l2b_ref[...])


def _peer_out(x1, ff_a, ff_b, mod, p, alpha, tb):
    g, r, _ = x1.shape
    return pl.pallas_call(
        functools.partial(_peer_out_kernel, alpha=alpha),
        grid=(g, r // tb),
        in_specs=[_rows(tb, D_MODEL)] * 3 + [_mod(mod, tb, 5), _const((1, D_MODEL)), _const((1, D_MODEL))],
        out_specs=_rows(tb, D_MODEL),
        out_shape=jax.ShapeDtypeStruct((g, r, D_MODEL), F32),
        compiler_params=_cparams("parallel", "parallel"),
        name="peer_out_ln2",
    )(x1, ff_a, ff_b, mod, _row2(p["ln2_g"]), _row2(p["ln2_b"]))


def _token_stage(x, mod, prev_fn, wkv_fn, attn_fn, p, alpha, tb, sequential):
    ps, pattn, pgate = _inproj(x, mod, p["ln_in_g"], p["ln_in_b"], p["w_in_bf16"], tb)
    r, lw, k, v, kk, kka, gl = _rwkv_prep(ps, prev_fn(), p, tb, sequential)
    y, wkv_new = wkv_fn(r, lw, k, v, kk, kka)
    ya = _rwkv_post(y, r, k, v, gl, p, tb)
    ob = attn_fn(pattn)
    x1, h2, scores_t = _merge(x, ya, ob, pgate, mod, p, alpha, tb)
    idx_t, gate_t = _topk(scores_t, TOPK_TB)
    return ps, pattn, wkv_new, x1, h2, idx_t, gate_t


def kernel(x_prompt, x_sample, state_wkv, state_shift, cache_k_win, cache_v_win, c_prompt, c_sample, ln_in_g, ln_in_b, w_ada, b_ada, w_in, mu_shift, rwkv_w0, rwkv_w2, rwkv_a0, rwkv_a2, rwkv_g2, rwkv_k_k, rwkv_k_a, rwkv_r_k, rwkv_gn_g, rwkv_gn_b, attn_sinks, w_pa, w_pb, w_o, ln1_g, ln1_b, peer_wq, peer_sub_keys, peer_u, peer_v, ln2_g, ln2_b):
    depth = w_in.shape[0]
    assert depth == 1, "single-layer trunk"
    alpha = (2.0 * depth) ** 0.25
    n_p, t_p, _ = x_prompt.shape
    n_s = x_sample.shape[0]
    p = dict(ln_in_g=ln_in_g, ln_in_b=ln_in_b, w_in_bf16=w_in[0].astype(BF16), mu_shift=mu_shift[0],
             rwkv_w0=rwkv_w0[0], rwkv_w2=rwkv_w2[0], rwkv_a0=rwkv_a0[0], rwkv_a2=rwkv_a2[0], rwkv_g2=rwkv_g2[0],
             rwkv_k_k=rwkv_k_k[0], rwkv_k_a=rwkv_k_a[0], rwkv_r_k=rwkv_r_k[0], rwkv_gn_g=rwkv_gn_g[0],
             rwkv_gn_b=rwkv_gn_b[0], w_pa=w_pa[0], w_pb=w_pb[0], w_o=w_o[0], ln1_g=ln1_g[0], ln1_b=ln1_b[0],
             peer_wq=peer_wq[0], peer_sub_keys=peer_sub_keys[0], ln2_g=ln2_g[0], ln2_b=ln2_b[0])
    sinks = attn_sinks[0]

    n_c = n_p + n_s
    pad = (-n_c) % 8
    c_all = jnp.concatenate([c_prompt, c_sample, jnp.zeros((pad, D_MODEL), F32)], axis=0)
    mod_all = _modulation(c_all, w_ada[0], b_ada[0])
    mod_p = mod_all[:n_p].reshape(n_p, 1, N_MOD * D_MODEL)
    mod_s = mod_all[n_p:n_c].reshape(1, n_s, N_MOD * D_MODEL)

    seg = min(PROMPT_SEGMENT, t_p)
    assert t_p % seg == 0
    first = min(PROMPT_FIRST_SEGMENT, seg)

    def segments(b):
        cuts = list(range(0, t_p + 1, seg))
        if b == 0 and first < seg:
            cuts.insert(1, first)
        return zip(cuts[:-1], cuts[1:])

    prompt_ids = [(b, lo, hi) for b in range(n_p) for lo, hi in segments(b)]
    assert all((hi - lo) % TOKEN_TB == 0 for _, lo, hi in prompt_ids)
    carry = {}

    def prompt_group(b, lo, hi):
        def prev_fn():
            return carry[b][0] if lo > 0 else jnp.zeros((1, 1, SHIFT_W), F32)

        def wkv_fn(r, lw, k, v, kk, kka):
            s0 = carry[b][1] if lo > 0 else jnp.zeros((1, H_A, HD_A, HD_A), F32)
            return _rwkv_chunk_scan(r, lw, k, v, kk, kka, s0)

        def attn_fn(pa):
            return _attn_band(pa, carry[b][2] if lo > 0 else None, sinks)

        return (x_prompt[b:b + 1, lo:hi], mod_p[b:b + 1], prev_fn, wkv_fn, attn_fn, TOKEN_TB, True)

    tu, tv = _pack_bf16_pairs(peer_u[0]), _pack_bf16_pairs(peer_v[0])

    def select(x, mod, prev_fn, wkv_fn, attn_fn, tb, sequential):
        ps, pattn, wkv_new, x1, h2, idx_rows, gate = _token_stage(
            x, mod, prev_fn, wkv_fn, attn_fn, p, alpha, tb, sequential)
        idx = idx_rows[0]
        hd = _sc_row_dots(tu, idx, h2[0])
        return ps, pattn, wkv_new, (hd, idx, gate, x1, mod)

    tv_slabs = tv.reshape(tv.shape[0], -1, 128)

    def weigh(sel):
        hd, idx, gate, x1, mod = sel
        wgt = _peer_weights(hd[None], gate, min(TOKEN_TB, idx.shape[0]))[0]
        x = VALUE_TC_PICKS
        ff_sc = _sc_weighted_row_sum(tv, idx, wgt, x)
        ff_tc = _peer_value_tc(tv_slabs, idx[:, :x], wgt[:, :x], min(VALUE_TC_TB, idx.shape[0]))
        return ff_sc, ff_tc, x1, mod

    def finish(wsum):
        ff_sc, ff_tc, x1, mod = wsum
        return _peer_out(x1, ff_sc[None], ff_tc[None], mod, p, alpha, min(TOKEN_TB, x1.shape[1]))

    xs = x_sample.reshape(1, n_s, D_MODEL)

    def prev_s():
        return state_shift[0].reshape(1, n_s, SHIFT_W)

    def wkv_s(r, lw, k, v, kk, kka):
        sq = lambda z: z.reshape(n_s, D_A)
        y, s = _rwkv_step(state_wkv[0], sq(r), sq(lw), sq(k), sq(kk), sq(kka), sq(v), STEP_NB)
        return y.reshape(1, n_s, D_A), s

    def attn_s(pa):
        o = _attn_cache(pa.reshape(n_s, 1, ATTN_W), cache_k_win[0], cache_v_win[0], sinks, STEP_NB)
        return o.reshape(1, n_s, D_B)

    n_g = len(prompt_ids) + 1
    sel, wsum, y_l = [None] * n_g, [None] * n_g, [None] * n_g
    for step in range(n_g + 2):
        if step < n_g - 1:
            b, lo, hi = prompt_ids[step]
            ps, pattn, wkv_new, sel[step] = select(*prompt_group(b, lo, hi))
            carry[b] = (ps[:, -1:], wkv_new, pattn[:, -WINDOW:])
        elif step == n_g - 1:
            ps_s, pattn_s, wkv_s_new, sel[step] = select(xs, mod_s, prev_s, wkv_s, attn_s, min(TOKEN_TB, n_s), False)
        if 0 <= step - 1 < n_g:
            wsum[step - 1] = weigh(sel[step - 1])
        if 0 <= step - 2 < n_g:
            y_l[step - 2] = finish(wsum[step - 2])
    y_s = y_l[n_g - 1]
    y_p = jnp.concatenate(y_l[:n_g - 1], axis=1).reshape(n_p, t_p, D_MODEL)
    shift_p = jnp.concatenate([carry[b][0][:, 0] for b in range(n_p)], axis=0)
    pattn_p = jnp.concatenate([carry[b][2] for b in range(n_p)], axis=0)
    wkv_p_new = jnp.concatenate([carry[b][1] for b in range(n_p)], axis=0)

    kv = lambda pa, o: pa[..., o:o + H_KV * HD_B]
    ko, vo = D_B, D_B + H_KV * HD_B
    k_win_p = kv(pattn_p, ko)[:, -WINDOW:].reshape(n_p, WINDOW, H_KV, HD_B)
    v_win_p = kv(pattn_p, vo)[:, -WINDOW:].reshape(n_p, WINDOW, H_KV, HD_B)
    k_new_s = kv(pattn_s, ko).reshape(n_s, 1, H_KV, HD_B)
    v_new_s = kv(pattn_s, vo).reshape(n_s, 1, H_KV, HD_B)
    k_win_s = jnp.concatenate([cache_k_win[0], k_new_s], axis=1)[:, -WINDOW:]
    v_win_s = jnp.concatenate([cache_v_win[0], v_new_s], axis=1)[:, -WINDOW:]
    return (y_p, y_s.reshape(n_s, 1, D_MODEL), wkv_p_new[None], wkv_s_new[None],
            shift_p[None], ps_s.reshape(n_s, SHIFT_W)[None],
            k_win_p[None], k_win_s[None], v_win_p[None], v_win_s[None])
```
